```python
import functools
import jax
import jax.numpy as jnp
from jax import lax
import numpy as np

D_MODEL = 2048
BATCH = 2
SEQ = 4096
DEPTH = 1
DEC_BATCH = 128
DEC_SEQ = 4
PAST_LEN = 2048
PAGE_SIZE = 128

HEAD_DIM = 128
N_NSA_HEADS = D_MODEL // (2 * HEAD_DIM)
N_NSA_GROUPS = 2
NSA_HPG = N_NSA_HEADS // N_NSA_GROUPS
N_FOX_HEADS = D_MODEL // (2 * HEAD_DIM)
CMP_BLOCK = 32
CMP_STRIDE = 16
CMP_RATIO = CMP_BLOCK // CMP_STRIDE
SEL_BLOCK = 64
SEL_TOPK = 16
N_LOCAL_BLOCKS = 2
WINDOW = 512
N_NSA_BRANCH = 3
N_MERGE = 2
Q_BLOCK = 128
CONV_WIDTH = 3
D_FF = 11 * D_MODEL // 4
ROPE_THETA = 10000.0
RMS_EPS = 1e-6
FORGET_BIAS_INIT = 4.0
FORCE_BONUS = 1e4
NEG_INF = -1e30

NSA_Q_W = N_NSA_HEADS * HEAD_DIM
NSA_KV_W = 2 * N_NSA_GROUPS * HEAD_DIM
NSA_GATE_W = N_NSA_HEADS * N_NSA_BRANCH
FOX_QKV_W = 3 * N_FOX_HEADS * HEAD_DIM
FOX_F_W = N_FOX_HEADS
MERGE_W = N_MERGE * D_MODEL
IN_SPLITS = (NSA_Q_W, NSA_KV_W, NSA_KV_W, NSA_KV_W, NSA_GATE_W, FOX_QKV_W, FOX_F_W, MERGE_W)
D_IN = sum(IN_SPLITS)

kernel_name = 'nsa_fox_hybrid_decode_step'


def _rmsnorm(x, g):
    xf = x.astype(jnp.float32)
    y = xf * lax.rsqrt(jnp.mean(xf * xf, axis=-1, keepdims=True) + RMS_EPS)
    return (y * g.astype(jnp.float32)).astype(x.dtype)


def _rope(x, pos):
    half = HEAD_DIM // 2
    inv_freq = ROPE_THETA ** (-jnp.arange(half, dtype=jnp.float32) / half)
    ang = pos.astype(jnp.float32)[:, None] * inv_freq[None, :]
    cos, sin = jnp.cos(ang)[:, None, :], jnp.sin(ang)[:, None, :]
    xf = x.astype(jnp.float32)
    x1, x2 = xf[..., :half], xf[..., half:]
    return jnp.concatenate([x1 * cos - x2 * sin, x2 * cos + x1 * sin], axis=-1).astype(x.dtype)


def _masked_softmax(s, mask):
    p = jax.nn.softmax(jnp.where(mask, s, NEG_INF), axis=-1)
    return jnp.where(mask, p, 0.0)


def _qblock(t):
    return Q_BLOCK if t % Q_BLOCK == 0 else t


def _project(h, pos, w_in, b_fgt):
    b, t, _ = h.shape
    offs = [int(o) for o in np.cumsum(IN_SPLITS)[:-1]]
    q_n, kv_c, kv_s, kv_w, g_n, qkv_f, f_f, g_m = jnp.split(h @ w_in, offs, axis=-1)
    q_n = _rope(q_n.reshape(b, t, N_NSA_HEADS, HEAD_DIM), pos)

    def kv_rows(piece):
        piece = piece.reshape(b, t, 2, N_NSA_GROUPS, HEAD_DIM)
        return jnp.stack([_rope(piece[:, :, 0], pos), piece[:, :, 1]], axis=2)

    kv_c, kv_s, kv_w = kv_rows(kv_c), kv_rows(kv_s), kv_rows(kv_w)
    g_n = jax.nn.sigmoid(g_n.reshape(b, t, N_NSA_HEADS, N_NSA_BRANCH))
    qkv_f = qkv_f.reshape(b, t, 3, N_FOX_HEADS, HEAD_DIM)
    q_f, kv_f = qkv_f[:, :, 0], qkv_f[:, :, 1:]
    logf = jax.nn.log_sigmoid((f_f + b_fgt).astype(jnp.float32))
    g_m = jax.nn.sigmoid(g_m.reshape(b, t, N_MERGE, D_MODEL))
    return q_n, kv_c, kv_s, kv_w, g_n, q_f, kv_f, logf, g_m


def _compress(x, w1, pe, w2):
    s = x.shape[0]
    n_chunk = -(-s // CMP_STRIDE)
    n_cmp = n_chunk - CMP_RATIO + 1
    xc = jnp.pad(x, ((0, n_chunk * CMP_STRIDE - s), (0, 0), (0, 0)))
    xc = xc.reshape(n_chunk, CMP_STRIDE, N_NSA_GROUPS, HEAD_DIM)
    part = jnp.einsum('cigd,ride->rcge', xc, w1.reshape(CMP_RATIO, CMP_STRIDE, HEAD_DIM, HEAD_DIM))
    hid = jnp.einsum('id,ide->e', pe, w1)
    for r in range(CMP_RATIO):
        hid = hid + part[r, r:r + n_cmp]
    return jax.nn.gelu(hid) @ w2


def _nsa_block(q, tpos, gates, ck, cv, ks, vs, kw, vw, wpos):
    qb = q.shape[0]
    s_len = ks.shape[0]
    n_cmp = ck.shape[0]
    qg = q.reshape(qb, N_NSA_GROUPS, NSA_HPG, HEAD_DIM)
    scale = HEAD_DIM ** -0.5
    c_start = jnp.arange(n_cmp) * CMP_STRIDE
    mask_c = (c_start + CMP_BLOCK - 1)[None, :] <= tpos[:, None]
    s_c = jnp.einsum('tghd,cgd->tghc', qg, ck, preferred_element_type=jnp.float32) * scale
    p_c = _masked_softmax(s_c, mask_c[:, None, None, :])
    o_c = jnp.einsum('tghc,cgd->tghd', p_c.astype(cv.dtype), cv)
    n_sel = -(-s_len // SEL_BLOCK)
    b_start = jnp.arange(n_sel) * SEL_BLOCK
    cover = ((c_start[:, None] < b_start[None, :] + SEL_BLOCK)
             & (c_start[:, None] + CMP_BLOCK > b_start[None, :])).astype(jnp.float32)
    imp = jnp.einsum('tghc,cb->tgb', p_c, cover)
    cur = (tpos // SEL_BLOCK)[:, None]
    bidx = jnp.arange(n_sel)[None, :]
    valid = bidx <= cur
    forced = (bidx == 0) | (valid & (bidx > cur - N_LOCAL_BLOCKS))
    score = jnp.where(valid[:, None, :], imp + FORCE_BONUS * forced[:, None, :], NEG_INF)
    _, sel = lax.top_k(score, min(SEL_TOPK, n_sel))
    n_k = sel.shape[-1]
    pad = n_sel * SEL_BLOCK - s_len

    def blocks(a):
        a = jnp.pad(a, ((0, pad), (0, 0), (0, 0)))
        return a.reshape(n_sel, SEL_BLOCK, N_NSA_GROUPS, HEAD_DIM).transpose(2, 0, 1, 3)

    gid = jnp.arange(N_NSA_GROUPS)[None, :, None]
    gk = blocks(ks)[gid, sel]
    gv = blocks(vs)[gid, sel]
    spos = sel[..., None] * SEL_BLOCK + jnp.arange(SEL_BLOCK)
    mask_s = (spos <= tpos[:, None, None, None]).reshape(qb, N_NSA_GROUPS, 1, n_k * SEL_BLOCK)
    s_s = jnp.einsum('tghd,tgnkd->tghnk', qg, gk, preferred_element_type=jnp.float32) * scale
    p_s = _masked_softmax(s_s.reshape(qb, N_NSA_GROUPS, NSA_HPG, n_k * SEL_BLOCK), mask_s)
    p_s = p_s.reshape(qb, N_NSA_GROUPS, NSA_HPG, n_k, SEL_BLOCK)
    o_s = jnp.einsum('tghnk,tgnkd->tghd', p_s.astype(gv.dtype), gv)
    mask_w = ((wpos[None, :] <= tpos[:, None]) & (wpos[None, :] > tpos[:, None] - WINDOW)
              & (wpos[None, :] >= 0))
    s_w = jnp.einsum('tghd,sgd->tghs', qg, kw, preferred_element_type=jnp.float32) * scale
    p_w = _masked_softmax(s_w, mask_w[:, None, None, :])
    o_w = jnp.einsum('tghs,sgd->tghd', p_w.astype(vw.dtype), vw)
    g = gates.reshape(qb, N_NSA_GROUPS, NSA_HPG, N_NSA_BRANCH)
    out = g[..., 0:1] * o_c + g[..., 1:2] * o_s + g[..., 2:3] * o_w
    return out.reshape(qb, N_NSA_HEADS, HEAD_DIM)


def _fox_block(q, fq, tpos, k, v, fk, kpos):
    s = jnp.einsum('thd,shd->hts', q, k, preferred_element_type=jnp.float32) * HEAD_DIM ** -0.5
    s = s + (fq.T[:, :, None] - fk.T[:, None, :])
    mask = (kpos[None, :] <= tpos[:, None])[None]
    p = _masked_softmax(s, mask)
    return jnp.einsum('hts,shd->thd', p.astype(v.dtype), v)


def _mixers_prompt(q_n, kv_c, kv_s, kv_w, g_n, q_f, kv_f, logf, *, cmp_k, cmp_v):
    b_n, t_n = q_n.shape[:2]
    qb = _qblock(t_n)
    n_qb = t_n // qb
    ck = jax.vmap(lambda a: _compress(a, *cmp_k))(kv_c[:, :, 0])
    cv = jax.vmap(lambda a: _compress(a, *cmp_v))(kv_c[:, :, 1])
    k_sel, v_sel = kv_s[:, :, 0], kv_s[:, :, 1]
    kv_w_pad = jnp.pad(kv_w, ((0, 0), (WINDOW, 0), (0, 0), (0, 0), (0, 0)))
    k_fox, v_fox = kv_f[:, :, 0], kv_f[:, :, 1]
    f_cum = jnp.cumsum(logf, axis=1)
    kpos = jnp.arange(t_n)

    def step(i):
        b = i // n_qb
        t0 = (i % n_qb) * qb
        tpos = t0 + jnp.arange(qb)
        rows = lambda a: lax.dynamic_slice_in_dim(a[b], t0, qb, axis=0)
        band = lax.dynamic_slice_in_dim(kv_w_pad[b], t0, WINDOW + qb, axis=0)
        wpos = t0 - WINDOW + jnp.arange(WINDOW + qb)
        o_n = _nsa_block(rows(q_n), tpos, rows(g_n), ck[b], cv[b], k_sel[b], v_sel[b],
                         band[:, 0], band[:, 1], wpos)
        o_f = _fox_block(rows(q_f), rows(f_cum), tpos, k_fox[b], v_fox[b], f_cum[b], kpos)
        return o_n, o_f

    o_n, o_f = lax.map(step, jnp.arange(b_n * n_qb))
    return o_n.reshape(b_n, t_n, -1), o_f.reshape(b_n, t_n, -1)


def _mixers_sample(q_n, kv_c, kv_s, kv_w, g_n, q_f, kv_f, logf, *, layer, win_buf, page_table,
                   cache_cmp, cache_sel, cache_fox, cache_logf, cmp_k, cmp_v):
    b_n, t_n = q_n.shape[:2]
    past = page_table.shape[1] * cache_cmp.shape[2]
    n_buf = win_buf.shape[1]
    tpos = past + jnp.arange(t_n)
    kpos = jnp.arange(past + t_n)
    wpos = past - n_buf + jnp.arange(n_buf + t_n)

    def rows(cache, pt):
        c = cache[layer, pt]
        return c.reshape((past,) + c.shape[2:])

    def step(xs):
        pt, qn, kc, ksl, kwn, gn, wb, qf, kf, lf = xs
        kvc = jnp.concatenate([rows(cache_cmp, pt).astype(kc.dtype), kc], axis=0)
        kvs = jnp.concatenate([rows(cache_sel, pt).astype(ksl.dtype), ksl], axis=0)
        kvw = jnp.concatenate([wb.astype(kwn.dtype), kwn], axis=0)
        ck = _compress(kvc[:, 0], *cmp_k)
        cv = _compress(kvc[:, 1], *cmp_v)
        o_n = _nsa_block(qn, tpos, gn, ck, cv, kvs[:, 0], kvs[:, 1], kvw[:, 0], kvw[:, 1], wpos)
        kvf = jnp.concatenate([rows(cache_fox, pt).astype(kf.dtype), kf], axis=0)
        f_cum = jnp.cumsum(jnp.concatenate([rows(cache_logf, pt).astype(jnp.float32), lf], axis=0), axis=0)
        o_f = _fox_block(qf, f_cum[past:], tpos, kvf[:, 0], kvf[:, 1], f_cum, kpos)
        return o_n, o_f

    o_n, o_f = lax.map(step, (page_table, q_n, kv_c, kv_s, kv_w, g_n, win_buf, q_f, kv_f, logf))
    return o_n.reshape(b_n, t_n, -1), o_f.reshape(b_n, t_n, -1)


def _layer(x, pos, mixer_fn, conv_prev, prm):
    h = _rmsnorm(x, prm['g_pre_mix'])
    q_n, kv_c, kv_s, kv_w, g_n, q_f, kv_f, logf, g_m = _project(h, pos, prm['w_in'], prm['b_fgt'])
    o_n, o_f = mixer_fn(q_n, kv_c, kv_s, kv_w, g_n, q_f, kv_f, logf)
    merged = g_m[:, :, 0] * (o_n @ prm['w_nsa_o']) + g_m[:, :, 1] * (o_f @ prm['w_fox_o'])
    x = x + _rmsnorm(merged @ prm['w_out'], prm['g_post_mix'])
    h2 = _rmsnorm(x, prm['g_pre_ffn'])
    gate, up = jnp.split(h2 @ prm['w_up'], 2, axis=-1)
    t = x.shape[1]
    gfull = jnp.concatenate([conv_prev.astype(gate.dtype), gate], axis=1)
    w_conv = prm['w_conv']
    gc = prm['b_conv'] + sum(w_conv[i] * gfull[:, i:i + t] for i in range(CONV_WIDTH))
    act = jax.nn.gelu(gc, approximate=True) * up
    x = x + _rmsnorm(act @ prm['w_down'], prm['g_post_ffn'])
    return x, (kv_c, kv_s, kv_w, kv_f, logf), gfull[:, t:]


def setup_inputs(seed: int = 0) -> dict:
    key = jax.random.key(seed)
    keys = iter(jax.random.split(key, 40))

    def nrm(shape, scale):
        return scale * jax.random.normal(next(keys), shape, jnp.float32)

    n_pages = PAST_LEN // PAGE_SIZE
    n_used = DEC_BATCH * n_pages
    n_pool = (5 * n_used + 3) // 4
    n_buf = min(WINDOW, PAST_LEN)
    G, H, HD = N_NSA_GROUPS, N_FOX_HEADS, HEAD_DIM
    page_table = jax.random.permutation(next(keys), n_pool)[:n_used].reshape(DEC_BATCH, n_pages).astype(jnp.int32)
    return {
        'x_prompt': nrm((BATCH, SEQ, D_MODEL), 1.0),
        'x_sample': nrm((DEC_BATCH, DEC_SEQ, D_MODEL), 1.0),
        'cache_nsa_cmp_kv': nrm((DEPTH, n_pool, PAGE_SIZE, 2, G, HD), 1.0),
        'cache_nsa_sel_kv': nrm((DEPTH, n_pool, PAGE_SIZE, 2, G, HD), 1.0),
        'cache_nsa_win_kv': nrm((DEPTH, DEC_BATCH, n_buf, 2, G, HD), 1.0),
        'cache_fox_kv': nrm((DEPTH, n_pool, PAGE_SIZE, 2, H, HD), 1.0),
        'cache_fox_logf': jax.nn.log_sigmoid(FORGET_BIAS_INIT + nrm((DEPTH, n_pool, PAGE_SIZE, H), 1.0)),
        'state_ffn_conv': nrm((DEPTH, DEC_BATCH, CONV_WIDTH - 1, D_FF), 1.0),
        'page_table': page_table,
        'g_pre_mix': 1.0 + nrm((DEPTH, D_MODEL), 0.05),
        'w_in': nrm((DEPTH, D_MODEL, D_IN), D_MODEL ** -0.5),
        'b_fgt': FORGET_BIAS_INIT + nrm((DEPTH, H), 0.5),
        'w_cmp_k1': nrm((DEPTH, CMP_BLOCK, HD, HD), (CMP_BLOCK * HD) ** -0.5),
        'pe_cmp_k': nrm((DEPTH, CMP_BLOCK, HD), 0.1),
        'w_cmp_k2': nrm((DEPTH, HD, HD), HD ** -0.5),
        'w_cmp_v1': nrm((DEPTH, CMP_BLOCK, HD, HD), (CMP_BLOCK * HD) ** -0.5),
        'pe_cmp_v': nrm((DEPTH, CMP_BLOCK, HD), 0.1),
        'w_cmp_v2': nrm((DEPTH, HD, HD), HD ** -0.5),
        'w_nsa_o': nrm((DEPTH, N_NSA_HEADS * HD, D_MODEL), (N_NSA_HEADS * HD) ** -0.5),
        'w_fox_o': nrm((DEPTH, N_FOX_HEADS * HD, D_MODEL), (N_FOX_HEADS * HD) ** -0.5),
        'w_out': nrm((DEPTH, D_MODEL, D_MODEL), D_MODEL ** -0.5),
        'g_post_mix': 1.0 + nrm((DEPTH, D_MODEL), 0.05),
        'g_pre_ffn': 1.0 + nrm((DEPTH, D_MODEL), 0.05),
        'w_up': nrm((DEPTH, D_MODEL, 2 * D_FF), D_MODEL ** -0.5),
        'w_conv': nrm((DEPTH, CONV_WIDTH, D_FF), CONV_WIDTH ** -0.5),
        'b_conv': nrm((DEPTH, D_FF), 0.01),
        'w_down': nrm((DEPTH, D_FF, D_MODEL), D_FF ** -0.5),
        'g_post_ffn': 1.0 + nrm((DEPTH, D_MODEL), 0.05),
    }


def reference(x_prompt, x_sample, cache_nsa_cmp_kv, cache_nsa_sel_kv, cache_nsa_win_kv,
              cache_fox_kv, cache_fox_logf, state_ffn_conv, page_table,
              g_pre_mix, w_in, b_fgt, w_cmp_k1, pe_cmp_k, w_cmp_k2, w_cmp_v1, pe_cmp_v, w_cmp_v2,
              w_nsa_o, w_fox_o, w_out, g_post_mix, g_pre_ffn, w_up, w_conv, b_conv, w_down, g_post_ffn):
    b_p, t_p = x_prompt.shape[:2]
    t_s = x_sample.shape[1]
    past = page_table.shape[1] * cache_nsa_cmp_kv.shape[2]
    pos_p = jnp.arange(t_p)
    pos_s = past + jnp.arange(t_s)
    y_p, y_s = x_prompt, x_sample
    new = {k: [] for k in ('cmp_p', 'cmp_s', 'sel_p', 'sel_s', 'win_p', 'win_s',
                           'fox_p', 'fox_s', 'lf_p', 'lf_s', 'conv_p', 'conv_s')}
    for l in range(DEPTH):
        prm = dict(g_pre_mix=g_pre_mix[l], w_in=w_in[l], b_fgt=b_fgt[l], w_nsa_o=w_nsa_o[l],
                   w_fox_o=w_fox_o[l], w_out=w_out[l], g_post_mix=g_post_mix[l],
                   g_pre_ffn=g_pre_ffn[l], w_up=w_up[l], w_conv=w_conv[l], b_conv=b_conv[l],
                   w_down=w_down[l], g_post_ffn=g_post_ffn[l])
        cmp_k = (w_cmp_k1[l], pe_cmp_k[l], w_cmp_k2[l])
        cmp_v = (w_cmp_v1[l], pe_cmp_v[l], w_cmp_v2[l])
        conv0 = jnp.zeros((b_p, CONV_WIDTH - 1, D_FF), x_prompt.dtype)
        mix_p = functools.partial(_mixers_prompt, cmp_k=cmp_k, cmp_v=cmp_v)
        y_p, (kc, ksl, kw, kf, lf), conv_p = _layer(y_p, pos_p, mix_p, conv0, prm)
        new['cmp_p'].append(kc)
        new['sel_p'].append(ksl)
        new['win_p'].append(kw[:, -min(WINDOW, t_p):])
        new['fox_p'].append(kf)
        new['lf_p'].append(lf)
        new['conv_p'].append(conv_p)
        win_buf = cache_nsa_win_kv[l]
        mix_s = functools.partial(_mixers_sample, layer=l, win_buf=win_buf, page_table=page_table,
                                  cache_cmp=cache_nsa_cmp_kv, cache_sel=cache_nsa_sel_kv,
                                  cache_fox=cache_fox_kv, cache_logf=cache_fox_logf,
                                  cmp_k=cmp_k, cmp_v=cmp_v)
        y_s, (kc, ksl, kw, kf, lf), conv_s = _layer(y_s, pos_s, mix_s, state_ffn_conv[l], prm)
        n_win = min(WINDOW, win_buf.shape[1] + t_s)
        new['cmp_s'].append(kc)
        new['sel_s'].append(ksl)
        new['win_s'].append(jnp.concatenate([win_buf.astype(kw.dtype), kw], axis=1)[:, -n_win:])
        new['fox_s'].append(kf)
        new['lf_s'].append(lf)
        new['conv_s'].append(conv_s)
    st = {k: jnp.stack(v) for k, v in new.items()}
    return (y_p, y_s, st['cmp_p'], st['cmp_s'], st['sel_p'], st['sel_s'], st['win_p'], st['win_s'],
            st['fox_p'], st['fox_s'], st['lf_p'], st['lf_s'], st['conv_p'], st['conv_s'])
```

```python
import functools

import numpy as np
import jax
import jax.numpy as jnp
from jax import lax
from jax.experimental import pallas as pl
from jax.experimental.pallas import tpu as pltpu

F32 = jnp.float32
BF16 = jnp.bfloat16

HEAD_DIM = 128
N_NSA_HEADS = 8
N_NSA_GROUPS = 2
NSA_HPG = N_NSA_HEADS // N_NSA_GROUPS
N_FOX_HEADS = 8
CMP_BLOCK = 32
CMP_STRIDE = 16
CMP_RATIO = CMP_BLOCK // CMP_STRIDE
SEL_BLOCK = 64
SEL_TOPK = 16
N_LOCAL_BLOCKS = 2
WINDOW = 512
CONV_WIDTH = 3
ROPE_THETA = 10000.0
RMS_EPS = 1e-6
FORCE_BONUS = 1e4
NEG_INF = -1e30
QK_SCALE = HEAD_DIM ** -0.5

N_GATE_COLS = N_NSA_HEADS * 3
LOGF_COL0 = N_GATE_COLS
LANES = 128
VMEM_LIMIT = 56 * 1024 * 1024

PROJ_TN = 512
KV_W = 2 * N_NSA_GROUPS * HEAD_DIM
NSA_Q_W = N_NSA_HEADS * HEAD_DIM
FOX_W = N_FOX_HEADS * HEAD_DIM


def _cparams(sem):
    return pltpu.CompilerParams(dimension_semantics=sem, vmem_limit_bytes=VMEM_LIMIT)


def _dot(a, b):
    return jnp.dot(a, b, preferred_element_type=F32)


def _dot_nt(a, b):
    return lax.dot_general(a, b, (((1,), (1,)), ((), ())), preferred_element_type=F32)


def _rms(x, g):
    return x * lax.rsqrt(jnp.mean(x * x, axis=-1, keepdims=True) + RMS_EPS) * g


def _masked_softmax(s, mask):
    sm = jnp.where(mask, s, NEG_INF)
    m = jnp.max(sm, axis=-1, keepdims=True)
    e = jnp.where(mask, jnp.exp(sm - m), 0.0)
    l = jnp.sum(e, axis=-1, keepdims=True)
    return e / jnp.where(l > 0.0, l, 1.0)


def _split3(x):
    hi = x.astype(BF16)
    r = x - hi.astype(F32)
    mid = r.astype(BF16)
    lo = (r - mid.astype(F32)).astype(BF16)
    return hi, mid, lo


def _topk_mask(score, k):
    lane = lax.broadcasted_iota(jnp.int32, score.shape, 1).astype(F32)

    def body(_, carry):
        sc, sel = carry
        m = jnp.max(sc, axis=-1, keepdims=True)
        idx = jnp.min(jnp.where(sc == m, lane, 1e9), axis=-1, keepdims=True)
        hit = lane == idx
        return jnp.where(hit, -jnp.inf, sc), jnp.where(hit, 1.0, sel)

    _, sel = lax.fori_loop(0, k, body, (score, jnp.zeros_like(score)))
    return sel


def _sel_scores(imp, tpos, n_sel):
    bidx = lax.broadcasted_iota(jnp.int32, imp.shape, 1)
    cur = jnp.right_shift(tpos, 6)
    valid = bidx <= cur
    forced = (bidx == 0) | (valid & (bidx > cur - N_LOCAL_BLOCKS))
    score = jnp.where(valid, jnp.where(forced, imp + FORCE_BONUS, imp), NEG_INF)
    return jnp.where(bidx < n_sel, score, -jnp.inf)


def _proj_kernel(x_ref, g_ref, cos_ref, sin_ref, w_ref, ws_ref, bf_ref,
                 qn_ref, kvc_ref, kvs_ref, kvw_ref, qf_ref, kvf_ref, gm_ref, sm_ref,
                 kvsb_ref, kvwb_ref, kvfb_ref, h_scr):
    j = pl.program_id(1)

    @pl.when(j == 0)
    def _():
        x = x_ref[...]
        y = x * lax.rsqrt(jnp.mean(x * x, axis=-1, keepdims=True) + RMS_EPS)
        h = (y * g_ref[...]).astype(BF16)
        h_scr[...] = h
        s = _dot(h, ws_ref[...])
        lane = lax.broadcasted_iota(jnp.int32, s.shape, 1)
        z = s + bf_ref[...]
        lf = jnp.minimum(z, 0.0) - jnp.log1p(jnp.exp(-jnp.abs(z)))
        sm_ref[...] = jnp.where(lane < N_GATE_COLS, jax.nn.sigmoid(s),
                                jnp.where(lane < LOGF_COL0 + N_FOX_HEADS, lf, 0.0))

    acc = _dot(h_scr[...], w_ref[...])
    cos = cos_ref[...]
    sin = sin_ref[...]

    def head(k):
        return acc[:, k * HEAD_DIM:(k + 1) * HEAD_DIM]

    def rope(a):
        return a * cos + pltpu.roll(a, HEAD_DIM // 2, axis=1) * sin

    def kv_rows(ref, bref):
        for k in range(N_NSA_GROUPS):
            r = rope(head(k))
            ref[:, k * HEAD_DIM:(k + 1) * HEAD_DIM] = r
            if bref is not None:
                bref[:, k * HEAD_DIM:(k + 1) * HEAD_DIM] = r.astype(BF16)
        v = acc[:, N_NSA_GROUPS * HEAD_DIM:]
        ref[:, N_NSA_GROUPS * HEAD_DIM:] = v
        if bref is not None:
            bref[:, N_NSA_GROUPS * HEAD_DIM:] = v.astype(BF16)

    @pl.when(j < 2)
    def _():
        for k in range(PROJ_TN // HEAD_DIM):
            qn_ref[:, k * HEAD_DIM:(k + 1) * HEAD_DIM] = (rope(head(k)) * QK_SCALE).astype(BF16)

    @pl.when(j == 2)
    def _():
        kv_rows(kvc_ref, None)

    @pl.when(j == 3)
    def _():
        kv_rows(kvs_ref, kvsb_ref)

    @pl.when(j == 4)
    def _():
        kv_rows(kvw_ref, kvwb_ref)

    @pl.when((j >= 5) & (j < 7))
    def _():
        qf_ref[...] = (acc * QK_SCALE).astype(BF16)

    @pl.when((j >= 7) & (j < 11))
    def _():
        kvf_ref[...] = acc
        kvfb_ref[...] = acc.astype(BF16)

    @pl.when(j >= 11)
    def _():
        gm_ref[...] = jax.nn.sigmoid(acc)


def _project(x2, g, cos2, sin2, w_main, w_small, bf_row):
    n, d = x2.shape
    tm = 512
    n_j = w_main.shape[1] // PROJ_TN
    tn = PROJ_TN

    def clip(lo, hi):
        return lambda i, j: (i, jnp.clip(j - lo, 0, hi - lo))

    row = lambda i, j: (i, 0)
    out_shape = (
        jax.ShapeDtypeStruct((n, NSA_Q_W), BF16),
        jax.ShapeDtypeStruct((n, KV_W), F32),
        jax.ShapeDtypeStruct((n, KV_W), F32),
        jax.ShapeDtypeStruct((n, KV_W), F32),
        jax.ShapeDtypeStruct((n, FOX_W), BF16),
        jax.ShapeDtypeStruct((n, 2 * FOX_W), F32),
        jax.ShapeDtypeStruct((n, 2 * d), F32),
        jax.ShapeDtypeStruct((n, LANES), F32),
        jax.ShapeDtypeStruct((n, KV_W), BF16),
        jax.ShapeDtypeStruct((n, KV_W), BF16),
        jax.ShapeDtypeStruct((n, 2 * FOX_W), BF16),
    )
    out_specs = (
        pl.BlockSpec((tm, tn), clip(0, 1)),
        pl.BlockSpec((tm, tn), row),
        pl.BlockSpec((tm, tn), row),
        pl.BlockSpec((tm, tn), row),
        pl.BlockSpec((tm, tn), clip(5, 6)),
        pl.BlockSpec((tm, tn), clip(7, 10)),
        pl.BlockSpec((tm, tn), clip(11, 18)),
        pl.BlockSpec((tm, LANES), row),
        pl.BlockSpec((tm, tn), row),
        pl.BlockSpec((tm, tn), row),
        pl.BlockSpec((tm, tn), clip(7, 10)),
    )
    in_specs = [
        pl.BlockSpec((tm, d), row),
        pl.BlockSpec((1, d), lambda i, j: (0, 0)),
        pl.BlockSpec((tm, LANES), row),
        pl.BlockSpec((tm, LANES), row),
        pl.BlockSpec((d, tn), lambda i, j: (0, j)),
        pl.BlockSpec((d, LANES), lambda i, j: (0, 0)),
        pl.BlockSpec((1, LANES), lambda i, j: (0, 0)),
    ]
    return pl.pallas_call(
        _proj_kernel,
        grid=(n // tm, n_j),
        in_specs=in_specs,
        out_specs=out_specs,
        out_shape=out_shape,
        scratch_shapes=[pltpu.VMEM((tm, d), BF16)],
        compiler_params=_cparams(("arbitrary", "arbitrary")),
        name="proj",
    )(x2, g, cos2, sin2, w_main, w_small, bf_row)


def _compress_tail(xc, w1, pe8, w2):
    n = xc.shape[0]
    part = _dot(xc, w1)
    pp = _dot(pe8.astype(BF16), w1)
    pe_term = pp[0:1, :HEAD_DIM] + pp[1:2, HEAD_DIM:]
    hid = pe_term + part[:, :HEAD_DIM] + pltpu.roll(part[:, HEAD_DIM:], n - 1, axis=0)
    return _dot(jax.nn.gelu(hid, approximate=True).astype(BF16), w2)


def _cmp_prompt_kernel(x_ref, w1_ref, pe_ref, w2_ref, o_ref):
    n = x_ref.shape[0] // CMP_STRIDE
    xc = jnp.concatenate([x_ref[pl.ds(i, n, stride=CMP_STRIDE), :] for i in range(CMP_STRIDE)],
                         axis=1).astype(BF16)
    o_ref[0, 0] = _compress_tail(xc, w1_ref[0], pe_ref[0], w2_ref[0]).astype(BF16)


def _compress_prompt(kvc, b_n, t_n, w1cat, pe8, w2):
    n = t_n // CMP_STRIDE
    return pl.pallas_call(
        _cmp_prompt_kernel,
        grid=(b_n, 2 * N_NSA_GROUPS),
        in_specs=[
            pl.BlockSpec((t_n, HEAD_DIM), lambda b, s: (b, s)),
            pl.BlockSpec((1, CMP_STRIDE * HEAD_DIM, 2 * HEAD_DIM), lambda b, s: (s // N_NSA_GROUPS, 0, 0)),
            pl.BlockSpec((1, 8, CMP_STRIDE * HEAD_DIM), lambda b, s: (s // N_NSA_GROUPS, 0, 0)),
            pl.BlockSpec((1, HEAD_DIM, HEAD_DIM), lambda b, s: (s // N_NSA_GROUPS, 0, 0)),
        ],
        out_specs=pl.BlockSpec((1, 1, n, HEAD_DIM), lambda b, s: (b, s, 0, 0)),
        out_shape=jax.ShapeDtypeStruct((b_n, 2 * N_NSA_GROUPS, n, HEAD_DIM), BF16),
        compiler_params=_cparams(("arbitrary", "arbitrary")),
        name="cmp_prompt",
    )(kvc, w1cat, pe8, w2)


def _fcum_kernel(x_ref, fcol_ref, frow_ref, carry_scr):
    i = pl.program_id(1)

    @pl.when(i == 0)
    def _():
        carry_scr[...] = jnp.zeros_like(carry_scr)

    x = x_ref[...]
    tb = x.shape[0]
    r = lax.broadcasted_iota(jnp.int32, (tb, tb), 0)
    c = lax.broadcasted_iota(jnp.int32, (tb, tb), 1)
    tri = jnp.where(r >= c, 1.0, 0.0).astype(BF16)
    hi, mid, lo = _split3(x)
    cs = _dot(tri, hi) + _dot(tri, mid) + _dot(tri, lo) + carry_scr[0:1, :]
    fcol_ref[...] = cs
    carry_scr[...] = jnp.broadcast_to(cs[tb - 1:tb, :], carry_scr.shape)
    frow_ref[0] = cs.T[LOGF_COL0:LOGF_COL0 + N_FOX_HEADS, :]


def _fcum_prompt(sm, b_n, t_n):
    tb = 512
    nb = t_n // tb
    return pl.pallas_call(
        _fcum_kernel,
        grid=(b_n, nb),
        in_specs=[pl.BlockSpec((tb, LANES), lambda b, i: (b * nb + i, 0))],
        out_specs=(pl.BlockSpec((tb, LANES), lambda b, i: (b * nb + i, 0)),
                   pl.BlockSpec((1, N_FOX_HEADS, tb), lambda b, i: (b, 0, i))),
        out_shape=(jax.ShapeDtypeStruct((b_n * t_n, LANES), F32),
                   jax.ShapeDtypeStruct((b_n, N_FOX_HEADS, t_n), F32)),
        scratch_shapes=[pltpu.VMEM((8, LANES), F32)],
        compiler_params=_cparams(("arbitrary", "arbitrary")),
        name="fcum_prompt",
    )(sm)


NSA_TQ = 128
NSA_TK = 512


def _tile_rows(a, reps):
    return jnp.concatenate([a] * reps, axis=0)


def _nsa_prompt_kernel(q_ref, sm_ref, ck_ref, ks_ref, kw_ref, cover_ref, expand_ref, o_ref, *, t_n, n_cmp, n_sel):
    i = pl.program_id(1)
    tq = NSA_TQ
    t0 = i * tq
    tpos = t0 + lax.broadcasted_iota(jnp.int32, (tq, 1), 0)
    gates = sm_ref[...]
    n_ck = ck_ref.shape[2]
    band = WINDOW + tq
    w0 = pl.multiple_of(jnp.maximum(t0 - WINDOW, 0), tq)
    n_tiles = (t0 + tq + NSA_TK - 1) // NSA_TK

    for g in range(N_NSA_GROUPS):
        q = jnp.concatenate([q_ref[:, (g * NSA_HPG + h) * HEAD_DIM:(g * NSA_HPG + h + 1) * HEAD_DIM]
                             for h in range(NSA_HPG)], axis=0)
        ck = ck_ref[0, g]
        cv = ck_ref[0, N_NSA_GROUPS + g]
        s_c = _dot_nt(q, ck)
        cidx = lax.broadcasted_iota(jnp.int32, (tq, n_ck), 1)
        mc = jnp.where((cidx * CMP_STRIDE + CMP_BLOCK - 1 <= tpos) & (cidx < n_cmp), 1.0, 0.0)
        p_c = _masked_softmax(s_c, _tile_rows(mc, NSA_HPG) > 0.5)
        p_cb = p_c.astype(BF16)
        o_c = _dot(p_cb, cv)
        imp4 = _dot(p_cb, cover_ref[...])
        imp = imp4[0:tq] + imp4[tq:2 * tq] + imp4[2 * tq:3 * tq] + imp4[3 * tq:4 * tq]
        selm = _topk_mask(_sel_scores(imp, tpos, n_sel), min(SEL_TOPK, n_sel)).astype(BF16)

        def sel_body(kt, carry, q=q, selm=selm, g=g):
            m, l, acc = carry
            k0 = pl.multiple_of(kt * NSA_TK, NSA_TK)
            k = ks_ref[pl.ds(k0, NSA_TK), g * HEAD_DIM:(g + 1) * HEAD_DIM]
            v = ks_ref[pl.ds(k0, NSA_TK), (N_NSA_GROUPS + g) * HEAD_DIM:(N_NSA_GROUPS + g + 1) * HEAD_DIM]
            s = _dot_nt(q, k)
            selx = _dot(selm, expand_ref[:, pl.ds(k0, NSA_TK)])
            kpos = k0 + lax.broadcasted_iota(jnp.int32, (tq, NSA_TK), 1)
            mk = jnp.where((selx > 0.5) & (kpos <= tpos), 1.0, 0.0)
            mk4 = _tile_rows(mk, NSA_HPG) > 0.5
            sm_ = jnp.where(mk4, s, NEG_INF)
            m_new = jnp.maximum(m, jnp.max(sm_, axis=-1, keepdims=True))
            alpha = jnp.exp(m - m_new)
            p = jnp.where(mk4, jnp.exp(sm_ - m_new), 0.0)
            l = alpha * l + jnp.sum(p, axis=-1, keepdims=True)
            acc = alpha * acc + _dot(p.astype(BF16), v)
            return m_new, l, acc

        init = (jnp.full((NSA_HPG * tq, 1), -jnp.inf, F32), jnp.zeros((NSA_HPG * tq, 1), F32),
                jnp.zeros((NSA_HPG * tq, HEAD_DIM), F32))
        _, l_s, acc_s = lax.fori_loop(0, n_tiles, sel_body, init)
        o_s = acc_s / jnp.where(l_s > 0.0, l_s, 1.0)

        kw = kw_ref[pl.ds(w0, band), g * HEAD_DIM:(g + 1) * HEAD_DIM]
        vw = kw_ref[pl.ds(w0, band), (N_NSA_GROUPS + g) * HEAD_DIM:(N_NSA_GROUPS + g + 1) * HEAD_DIM]
        s_w = _dot_nt(q, kw)
        wpos = w0 + lax.broadcasted_iota(jnp.int32, (tq, band), 1)
        mw = jnp.where((wpos <= tpos) & (wpos > tpos - WINDOW), 1.0, 0.0)
        p_w = _masked_softmax(s_w, _tile_rows(mw, NSA_HPG) > 0.5)
        o_w = _dot(p_w.astype(BF16), vw)

        for h in range(NSA_HPG):
            hh = g * NSA_HPG + h
            rows = slice(h * tq, (h + 1) * tq)
            out = (gates[:, 3 * hh:3 * hh + 1] * o_c[rows] + gates[:, 3 * hh + 1:3 * hh + 2] * o_s[rows]
                   + gates[:, 3 * hh + 2:3 * hh + 3] * o_w[rows])
            o_ref[:, hh * HEAD_DIM:(hh + 1) * HEAD_DIM] = out.astype(BF16)


def _cover_matrix(n_rows, n_cmp, n_sel):
    c = np.arange(n_rows)[:, None] * CMP_STRIDE
    b = np.arange(LANES)[None, :] * SEL_BLOCK
    m = (c < b + SEL_BLOCK) & (c + CMP_BLOCK > b) & (np.arange(n_rows)[:, None] < n_cmp) & (np.arange(LANES)[None, :] < n_sel)
    return jnp.asarray(m.astype(np.float32), dtype=BF16)


def _expand_matrix(n_keys):
    m = (np.arange(n_keys)[None, :] // SEL_BLOCK) == np.arange(LANES)[:, None]
    return jnp.asarray(m.astype(np.float32), dtype=BF16)


def _nsa_prompt(qn, sm, ckv, kvs_b, kvw_b, b_n, t_n):
    nq = t_n // NSA_TQ
    n_cmp = t_n // CMP_STRIDE - CMP_RATIO + 1
    n_sel = t_n // SEL_BLOCK
    n_ck = ckv.shape[2]
    cover = _cover_matrix(n_ck, n_cmp, n_sel)
    expand = _expand_matrix(t_n)
    kern = functools.partial(_nsa_prompt_kernel, t_n=t_n, n_cmp=n_cmp, n_sel=n_sel)
    return pl.pallas_call(
        kern,
        grid=(b_n, nq),
        in_specs=[
            pl.BlockSpec((NSA_TQ, NSA_Q_W), lambda b, i: (b * nq + i, 0)),
            pl.BlockSpec((NSA_TQ, LANES), lambda b, i: (b * nq + i, 0)),
            pl.BlockSpec((1, 2 * N_NSA_GROUPS, n_ck, HEAD_DIM), lambda b, i: (b, 0, 0, 0)),
            pl.BlockSpec((t_n, KV_W), lambda b, i: (b, 0)),
            pl.BlockSpec((t_n, KV_W), lambda b, i: (b, 0)),
            pl.BlockSpec((n_ck, LANES), lambda b, i: (0, 0)),
            pl.BlockSpec((LANES, t_n), lambda b, i: (0, 0)),
        ],
        out_specs=pl.BlockSpec((NSA_TQ, NSA_Q_W), lambda b, i: (b * nq + i, 0)),
        out_shape=jax.ShapeDtypeStruct((b_n * t_n, NSA_Q_W), BF16),
        compiler_params=_cparams(("arbitrary", "arbitrary")),
        name="nsa_prompt",
    )(qn, sm, ckv, kvs_b, kvw_b, cover, expand)


FOX_TQ = 256
FOX_TK = 512


def _fox_prompt_kernel(q_ref, k_ref, v_ref, fcol_ref, frow_ref, o_ref):
    h = pl.program_id(1)
    i = pl.program_id(2)
    tq, tk = FOX_TQ, FOX_TK
    t0 = i * tq
    q = q_ref[...]
    lane = lax.broadcasted_iota(jnp.int32, (tq, LANES), 1)
    fq = jnp.sum(jnp.where(lane == LOGF_COL0 + h, fcol_ref[...], 0.0), axis=-1, keepdims=True)
    tpos = t0 + lax.broadcasted_iota(jnp.int32, (tq, 1), 0)
    n_tiles = (t0 + tq + tk - 1) // tk

    def body(kt, carry):
        m, l, acc = carry
        k0 = pl.multiple_of(kt * tk, tk)
        k = k_ref[pl.ds(k0, tk), :]
        v = v_ref[pl.ds(k0, tk), :]
        fk = frow_ref[0, pl.ds(h, 1), pl.ds(k0, tk)]
        s = _dot_nt(q, k) + (fq - fk)
        kpos = k0 + lax.broadcasted_iota(jnp.int32, (tq, tk), 1)
        mk = kpos <= tpos
        sm_ = jnp.where(mk, s, NEG_INF)
        m_new = jnp.maximum(m, jnp.max(sm_, axis=-1, keepdims=True))
        alpha = jnp.exp(m - m_new)
        p = jnp.where(mk, jnp.exp(sm_ - m_new), 0.0)
        l = alpha * l + jnp.sum(p, axis=-1, keepdims=True)
        acc = alpha * acc + _dot(p.astype(BF16), v)
        return m_new, l, acc

    init = (jnp.full((tq, 1), -jnp.inf, F32), jnp.zeros((tq, 1), F32), jnp.zeros((tq, HEAD_DIM), F32))
    _, l, acc = lax.fori_loop(0, n_tiles, body, init)
    o_ref[...] = (acc / jnp.where(l > 0.0, l, 1.0)).astype(BF16)


def _fox_prompt(qf, kvf_b, fcol, frow, b_n, t_n):
    nq = t_n // FOX_TQ
    nh = N_FOX_HEADS
    return pl.pallas_call(
        _fox_prompt_kernel,
        grid=(b_n, nh, nq),
        in_specs=[
            pl.BlockSpec((FOX_TQ, HEAD_DIM), lambda b, h, i: (b * nq + i, h)),
            pl.BlockSpec((t_n, HEAD_DIM), lambda b, h, i: (b, h)),
            pl.BlockSpec((t_n, HEAD_DIM), lambda b, h, i: (b, nh + h)),
            pl.BlockSpec((FOX_TQ, LANES), lambda b, h, i: (b * nq + i, 0)),
            pl.BlockSpec((1, nh, t_n), lambda b, h, i: (b, 0, 0)),
        ],
        out_specs=pl.BlockSpec((FOX_TQ, HEAD_DIM), lambda b, h, i: (b * nq + i, h)),
        out_shape=jax.ShapeDtypeStruct((b_n * t_n, FOX_W), BF16),
        compiler_params=_cparams(("arbitrary", "arbitrary", "arbitrary")),
        name="fox_prompt",
    )(qf, kvf_b, kvf_b, fcol, frow)


TOK_PAD = 8
NEW_PAD = 128


def _with_new(past, new8):
    pad = jnp.zeros((NEW_PAD - new8.shape[0], new8.shape[1]), new8.dtype)
    return jnp.concatenate([past, new8, pad], axis=0)


def _nsa_sample_kernel(pt_ref, *refs, n_pages, page, n_buf, n_tok):
    cmp_pages = refs[:n_pages]
    sel_pages = refs[n_pages:2 * n_pages]
    (win_ref, kvs_new_ref, kvw_new_ref, q_ref, sm_ref, w1k_ref, w1v_ref, pek_ref, pev_ref,
     w2k_ref, w2v_ref, cover_ref, expand_ref, o_ref) = refs[2 * n_pages:]
    del pt_ref
    past = n_pages * page
    n_slab = 2 * N_NSA_GROUPS
    chunks_per_page = page // CMP_STRIDE
    n_chunk = n_pages * chunks_per_page
    n_cmp = (past + n_tok + CMP_STRIDE - 1) // CMP_STRIDE - CMP_RATIO + 1
    n_sel = (past + n_tok + SEL_BLOCK - 1) // SEL_BLOCK
    tp = TOK_PAD
    tpos = past + lax.broadcasted_iota(jnp.int32, (tp, 1), 0)
    gates = sm_ref[0]
    kvs_new = kvs_new_ref[0]
    kvw_new = kvw_new_ref[0]

    def compress(slab, w1_ref, pe_ref, w2_ref):
        cols = []
        for i in range(CMP_STRIDE):
            cols.append(jnp.concatenate(
                [cmp_pages[p][pl.ds(i * n_slab + slab, chunks_per_page, stride=CMP_STRIDE * n_slab), :]
                 for p in range(n_pages)], axis=0))
        xc = jnp.concatenate(cols, axis=1).astype(BF16)
        return _compress_tail(xc, w1_ref[...], pe_ref[...], w2_ref[...]).astype(BF16)

    def sel_rows(slab):
        return jnp.concatenate([sel_pages[p][pl.ds(slab, page, stride=n_slab), :] for p in range(n_pages)], axis=0)

    for g in range(N_NSA_GROUPS):
        q = jnp.concatenate([q_ref[0, :, (g * NSA_HPG + h) * HEAD_DIM:(g * NSA_HPG + h + 1) * HEAD_DIM]
                             for h in range(NSA_HPG)], axis=0)
        ck = compress(g, w1k_ref, pek_ref, w2k_ref)
        cv = compress(N_NSA_GROUPS + g, w1v_ref, pev_ref, w2v_ref)
        s_c = _dot_nt(q, ck)
        cidx = lax.broadcasted_iota(jnp.int32, (tp, n_chunk), 1)
        mc = jnp.where((cidx * CMP_STRIDE + CMP_BLOCK - 1 <= tpos) & (cidx < n_cmp), 1.0, 0.0)
        p_c = _masked_softmax(s_c, _tile_rows(mc, NSA_HPG) > 0.5)
        p_cb = p_c.astype(BF16)
        o_c = _dot(p_cb, cv)
        imp4 = _dot(p_cb, cover_ref[...])
        imp = imp4[0:tp] + imp4[tp:2 * tp] + imp4[2 * tp:3 * tp] + imp4[3 * tp:4 * tp]
        selm = _topk_mask(_sel_scores(imp, tpos, n_sel), min(SEL_TOPK, n_sel)).astype(BF16)

        n_keys = past + NEW_PAD
        k_all = _with_new(sel_rows(g), kvs_new[:, g * HEAD_DIM:(g + 1) * HEAD_DIM]).astype(BF16)
        v_all = _with_new(sel_rows(N_NSA_GROUPS + g),
                          kvs_new[:, (N_NSA_GROUPS + g) * HEAD_DIM:(N_NSA_GROUPS + g + 1) * HEAD_DIM]).astype(BF16)
        s_s = _dot_nt(q, k_all)
        selx = _dot(selm, expand_ref[...])
        kpos = lax.broadcasted_iota(jnp.int32, (tp, n_keys), 1)
        ms = jnp.where((selx > 0.5) & (kpos <= tpos) & (kpos < past + n_tok), 1.0, 0.0)
        p_s = _masked_softmax(s_s, _tile_rows(ms, NSA_HPG) > 0.5)
        o_s = _dot(p_s.astype(BF16), v_all)

        kw_all = _with_new(win_ref[pl.ds(g, n_buf, stride=n_slab), :],
                           kvw_new[:, g * HEAD_DIM:(g + 1) * HEAD_DIM]).astype(BF16)
        vw_all = _with_new(win_ref[pl.ds(N_NSA_GROUPS + g, n_buf, stride=n_slab), :],
                           kvw_new[:, (N_NSA_GROUPS + g) * HEAD_DIM:(N_NSA_GROUPS + g + 1) * HEAD_DIM]).astype(BF16)
        s_w = _dot_nt(q, kw_all)
        wpos = past - n_buf + lax.broadcasted_iota(jnp.int32, (tp, n_buf + NEW_PAD), 1)
        mw = jnp.where((wpos <= tpos) & (wpos > tpos - WINDOW) & (wpos < past + n_tok), 1.0, 0.0)
        p_w = _masked_softmax(s_w, _tile_rows(mw, NSA_HPG) > 0.5)
        o_w = _dot(p_w.astype(BF16), vw_all)

        for h in range(NSA_HPG):
            hh = g * NSA_HPG + h
            rows = slice(h * tp, (h + 1) * tp)
            out = (gates[:, 3 * hh:3 * hh + 1] * o_c[rows] + gates[:, 3 * hh + 1:3 * hh + 2] * o_s[rows]
                   + gates[:, 3 * hh + 2:3 * hh + 3] * o_w[rows])
            o_ref[0, :, hh * HEAD_DIM:(hh + 1) * HEAD_DIM] = out.astype(BF16)


def _pad_tokens(a, n_seq, n_tok):
    a = a.reshape(n_seq, n_tok, a.shape[-1])
    return jnp.pad(a, ((0, 0), (0, TOK_PAD - n_tok), (0, 0)))


def _nsa_sample(page_table, cache_cmp, cache_sel, win_buf, kvs_new, kvw_new, qn, sm, cmp_w, n_tok):
    n_seq, n_pages = page_table.shape
    page = cache_cmp.shape[1]
    n_slab = 2 * N_NSA_GROUPS
    n_buf = win_buf.shape[1]
    past = n_pages * page
    cmp2 = cache_cmp.reshape(-1, HEAD_DIM)
    sel2 = cache_sel.reshape(-1, HEAD_DIM)
    win2 = win_buf.reshape(-1, HEAD_DIM)
    n_chunk = past // CMP_STRIDE
    n_cmp = (past + n_tok + CMP_STRIDE - 1) // CMP_STRIDE - CMP_RATIO + 1
    n_sel = (past + n_tok + SEL_BLOCK - 1) // SEL_BLOCK
    cover = _cover_matrix(n_chunk, min(n_cmp, n_chunk), n_sel)
    expand = _expand_matrix(past + NEW_PAD)
    w1k, pek, w2k, w1v, pev, w2v = cmp_w

    def page_spec(p):
        return pl.BlockSpec((page * n_slab, HEAD_DIM), lambda b, pt, p=p: (pt[b, p], 0))

    const2 = lambda b, pt: (0, 0)
    seq3 = lambda b, pt: (b, 0, 0)
    in_specs = ([page_spec(p) for p in range(n_pages)] + [page_spec(p) for p in range(n_pages)] + [
        pl.BlockSpec((n_buf * n_slab, HEAD_DIM), lambda b, pt: (b, 0)),
        pl.BlockSpec((1, TOK_PAD, KV_W), seq3),
        pl.BlockSpec((1, TOK_PAD, KV_W), seq3),
        pl.BlockSpec((1, TOK_PAD, NSA_Q_W), seq3),
        pl.BlockSpec((1, TOK_PAD, LANES), seq3),
        pl.BlockSpec(w1k.shape, const2),
        pl.BlockSpec(w1v.shape, const2),
        pl.BlockSpec(pek.shape, const2),
        pl.BlockSpec(pev.shape, const2),
        pl.BlockSpec(w2k.shape, const2),
        pl.BlockSpec(w2v.shape, const2),
        pl.BlockSpec(cover.shape, const2),
        pl.BlockSpec(expand.shape, const2),
    ])
    kern = functools.partial(_nsa_sample_kernel, n_pages=n_pages, page=page, n_buf=n_buf, n_tok=n_tok)
    grid_spec = pltpu.PrefetchScalarGridSpec(
        num_scalar_prefetch=1, grid=(n_seq,), in_specs=in_specs,
        out_specs=pl.BlockSpec((1, TOK_PAD, NSA_Q_W), seq3))
    return pl.pallas_call(
        kern,
        grid_spec=grid_spec,
        out_shape=jax.ShapeDtypeStruct((n_seq, TOK_PAD, NSA_Q_W), BF16),
        compiler_params=_cparams(("arbitrary",)),
        name="nsa_sample",
    )(page_table, *([cmp2] * n_pages), *([sel2] * n_pages), win2,
      _pad_tokens(kvs_new, n_seq, n_tok), _pad_tokens(kvw_new, n_seq, n_tok),
      _pad_tokens(qn, n_seq, n_tok), _pad_tokens(sm, n_seq, n_tok),
      w1k, w1v, pek, pev, w2k, w2v, cover, expand)


def _fox_sample_kernel(pt_ref, *refs, n_pages, page, n_tok):
    kv_pages = refs[:n_pages]
    lf_pages = refs[n_pages:2 * n_pages]
    kvf_new_ref, q_ref, sm_ref, lft_ref, o_ref = refs[2 * n_pages:]
    del pt_ref
    nh = N_FOX_HEADS
    past = n_pages * page
    tp = TOK_PAD
    n_keys = past + NEW_PAD
    rows_pp = 2 * nh

    lf = jnp.concatenate([lf_pages[p][0] for p in range(n_pages)], axis=0)
    r = lax.broadcasted_iota(jnp.int32, (page, page), 0)
    c = lax.broadcasted_iota(jnp.int32, (page, page), 1)
    upper = jnp.where(r <= c, 1.0, 0.0).astype(BF16)
    hi, mid, lo = _split3(lf)
    cs = _dot(hi, upper) + _dot(mid, upper) + _dot(lo, upper)
    n_r = n_pages * nh
    rr = lax.broadcasted_iota(jnp.int32, (n_r, n_r), 0)
    cc = lax.broadcasted_iota(jnp.int32, (n_r, n_r), 1)
    assert nh & (nh - 1) == 0
    sh_h = nh.bit_length() - 1
    earlier = jnp.where(((rr & (nh - 1)) == (cc & (nh - 1))) & (jnp.right_shift(cc, sh_h) < jnp.right_shift(rr, sh_h)),
                        1.0, 0.0).astype(BF16)
    tot = jnp.broadcast_to(cs[:, page - 1:page], cs.shape)
    th, tm_, tl = _split3(tot)
    off = _dot(earlier, th) + _dot(earlier, tm_) + _dot(earlier, tl)
    f_past = cs + off

    sm = sm_ref[0]
    r8 = lax.broadcasted_iota(jnp.int32, (tp, tp), 0)
    c8 = lax.broadcasted_iota(jnp.int32, (tp, tp), 1)
    low8 = jnp.where((r8 >= c8) & (c8 < n_tok), 1.0, 0.0).astype(BF16)
    sh, sm2, sl = _split3(sm)
    c_col = _dot(low8, sh) + _dot(low8, sm2) + _dot(low8, sl)
    lft = lft_ref[0]
    rl = lax.broadcasted_iota(jnp.int32, (LANES, LANES), 0)
    cl = lax.broadcasted_iota(jnp.int32, (LANES, LANES), 1)
    upl = jnp.where((rl <= cl) & (rl < n_tok), 1.0, 0.0).astype(BF16)
    lh, lm, ll = _split3(lft)
    c_row = _dot(lh, upl) + _dot(lm, upl) + _dot(ll, upl)

    tpos = past + lax.broadcasted_iota(jnp.int32, (tp, 1), 0)
    kpos = lax.broadcasted_iota(jnp.int32, (tp, n_keys), 1)
    mask = (kpos <= tpos) & (kpos < past + n_tok)
    lane = lax.broadcasted_iota(jnp.int32, (tp, LANES), 1)
    kvf_new = kvf_new_ref[0]

    for h in range(nh):
        q = q_ref[0, :, h * HEAD_DIM:(h + 1) * HEAD_DIM]
        k_past = jnp.concatenate([kv_pages[p][pl.ds(h, page, stride=rows_pp), :] for p in range(n_pages)], axis=0)
        v_past = jnp.concatenate([kv_pages[p][pl.ds(nh + h, page, stride=rows_pp), :] for p in range(n_pages)], axis=0)
        k_all = _with_new(k_past, kvf_new[:, h * HEAD_DIM:(h + 1) * HEAD_DIM]).astype(BF16)
        v_all = _with_new(v_past, kvf_new[:, (nh + h) * HEAD_DIM:(nh + h + 1) * HEAD_DIM]).astype(BF16)
        f_tot = f_past[(n_pages - 1) * nh + h:(n_pages - 1) * nh + h + 1, page - 1:page]
        f_row = jnp.concatenate([f_tot - f_past[p * nh + h:p * nh + h + 1, :] for p in range(n_pages)]
                                + [-c_row[h:h + 1, :]], axis=1)
        fq = jnp.sum(jnp.where(lane == LOGF_COL0 + h, c_col, 0.0), axis=-1, keepdims=True)
        s = _dot_nt(q, k_all) + (fq + f_row)
        p_ = _masked_softmax(s, mask)
        o_ref[0, :, h * HEAD_DIM:(h + 1) * HEAD_DIM] = _dot(p_.astype(BF16), v_all).astype(BF16)


def _fox_sample(page_table, cache_fox, cache_logf, kvf_new, qf, sm, n_tok):
    n_seq, n_pages = page_table.shape
    page = cache_fox.shape[1]
    nh = N_FOX_HEADS
    kv2 = cache_fox.reshape(-1, HEAD_DIM)
    lf_t = jnp.swapaxes(cache_logf, 1, 2)
    lf_new = sm[:, LOGF_COL0:LOGF_COL0 + nh].reshape(n_seq, n_tok, nh)
    lft = jnp.pad(jnp.swapaxes(lf_new, 1, 2), ((0, 0), (0, 0), (0, LANES - n_tok)))

    seq3 = lambda b, pt: (b, 0, 0)
    in_specs = ([pl.BlockSpec((page * 2 * nh, HEAD_DIM), lambda b, pt, p=p: (pt[b, p], 0)) for p in range(n_pages)]
                + [pl.BlockSpec((1, nh, page), lambda b, pt, p=p: (pt[b, p], 0, 0)) for p in range(n_pages)]
                + [pl.BlockSpec((1, TOK_PAD, 2 * FOX_W), seq3),
                   pl.BlockSpec((1, TOK_PAD, FOX_W), seq3),
                   pl.BlockSpec((1, TOK_PAD, LANES), seq3),
                   pl.BlockSpec((1, nh, LANES), seq3)])
    kern = functools.partial(_fox_sample_kernel, n_pages=n_pages, page=page, n_tok=n_tok)
    grid_spec = pltpu.PrefetchScalarGridSpec(
        num_scalar_prefetch=1, grid=(n_seq,), in_specs=in_specs,
        out_specs=pl.BlockSpec((1, TOK_PAD, FOX_W), seq3))
    return pl.pallas_call(
        kern,
        grid_spec=grid_spec,
        out_shape=jax.ShapeDtypeStruct((n_seq, TOK_PAD, FOX_W), BF16),
        compiler_params=_cparams(("arbitrary",)),
        name="fox_sample",
    )(page_table, *([kv2] * n_pages), *([lf_t] * n_pages),
      _pad_tokens(kvf_new, n_seq, n_tok), _pad_tokens(qf, n_seq, n_tok), _pad_tokens(sm, n_seq, n_tok), lft)


def _postmix_kernel(on_ref, of_ref, gm0_ref, gm1_ref, x_ref, wn_ref, wf_ref, wo_ref, g_ref, y_ref):
    a = _dot(on_ref[...], wn_ref[...])
    b = _dot(of_ref[...], wf_ref[...])
    merged = gm0_ref[...] * a + gm1_ref[...] * b
    z = _dot(merged.astype(BF16), wo_ref[...])
    y_ref[...] = x_ref[...] + _rms(z, g_ref[...])


def _postmix(o_n, o_f, gm, x2, wn, wf, wo, g):
    n, d = x2.shape
    tm = 256
    row = lambda i: (i, 0)
    const = lambda i: (0, 0)
    return pl.pallas_call(
        _postmix_kernel,
        grid=(n // tm,),
        in_specs=[
            pl.BlockSpec((tm, NSA_Q_W), row),
            pl.BlockSpec((tm, FOX_W), row),
            pl.BlockSpec((tm, d), lambda i: (i, 0)),
            pl.BlockSpec((tm, d), lambda i: (i, 1)),
            pl.BlockSpec((tm, d), row),
            pl.BlockSpec(wn.shape, const),
            pl.BlockSpec(wf.shape, const),
            pl.BlockSpec(wo.shape, const),
            pl.BlockSpec((1, d), const),
        ],
        out_specs=pl.BlockSpec((tm, d), row),
        out_shape=jax.ShapeDtypeStruct((n, d), F32),
        compiler_params=_cparams(("arbitrary",)),
        name="postmix",
    )(o_n, o_f, gm, gm, x2, wn, wf, wo, g)


FFN_TM = 512
FFN_TF = 512
HALO = 16


def _ffn_kernel(*refs, seq_tiles, n_tok):
    if n_tok is None:
        (x_ref, xh_ref, g_ref, wg_ref, wu_ref, wd_ref, wc_ref, bc_ref, gp_ref,
         y_ref, gt_ref, h_scr, hh_scr, acc_scr) = refs
    else:
        (x_ref, s0_ref, s1_ref, g_ref, wg_ref, wu_ref, wd_ref, wc_ref, bc_ref, gp_ref,
         y_ref, gt_ref, h_scr, acc_scr) = refs
    i = pl.program_id(0)
    f = pl.program_id(1)
    tm = x_ref.shape[0]

    @pl.when(f == 0)
    def _():
        h_scr[...] = _rms(x_ref[...], g_ref[...]).astype(BF16)
        acc_scr[...] = jnp.zeros_like(acc_scr)
        if n_tok is None:
            hh_scr[...] = _rms(xh_ref[...], g_ref[...]).astype(BF16)

    h2 = h_scr[...]
    gate = _dot(h2, wg_ref[...])
    up = _dot(h2, wu_ref[...])
    row = lax.broadcasted_iota(jnp.int32, gate.shape, 0)
    r1 = pltpu.roll(gate, 1, axis=0)
    r2 = pltpu.roll(gate, 2, axis=0)
    if n_tok is None:
        first = (i % seq_tiles) == 0
        gh = jnp.where(first, 0.0, _dot(hh_scr[...], wg_ref[...]))
        p1 = gh[HALO - 1:HALO, :]
        p2 = gh[HALO - 2:HALO - 1, :]
        g1 = jnp.where(row == 0, p1, r1)
        g2 = jnp.where(row == 0, p2, jnp.where(row == 1, p1, r2))
        gt_ref[...] = gate[tm - 8:tm, :]
    else:
        assert n_tok & (n_tok - 1) == 0
        rt = row & (n_tok - 1)
        g1 = jnp.where(rt == 0, s1_ref[...], r1)
        g2 = jnp.where(rt == 0, s0_ref[...], jnp.where(rt == 1, s1_ref[...], r2))
        gt_ref[...] = gate
    wc = wc_ref[...]
    gc = bc_ref[...] + wc[0:1, :] * g2 + wc[1:2, :] * g1 + wc[2:3, :] * gate
    act = jax.nn.gelu(gc, approximate=True) * up
    acc_scr[...] += _dot(act.astype(BF16), wd_ref[...])

    @pl.when(f == pl.num_programs(1) - 1)
    def _():
        y_ref[...] = x_ref[...] + _rms(acc_scr[...], gp_ref[...])


def _ffn(x2, g_pre, w_up_b, w_down_b, w_conv, b_conv, g_post, *, seq_len=None, state=None):
    n, d = x2.shape
    d_ff = w_down_b.shape[0]
    tf = FFN_TF
    nf = d_ff // tf
    tm = min(FFN_TM, n)
    common_w = [
        pl.BlockSpec((1, d), lambda i, f: (0, 0)),
        pl.BlockSpec((d, tf), lambda i, f: (0, f)),
        pl.BlockSpec((d, tf), lambda i, f: (0, nf + f)),
        pl.BlockSpec((tf, d), lambda i, f: (f, 0)),
        pl.BlockSpec((CONV_WIDTH, tf), lambda i, f: (0, f)),
        pl.BlockSpec((1, tf), lambda i, f: (0, f)),
        pl.BlockSpec((1, d), lambda i, f: (0, 0)),
    ]
    w_args = (g_pre, w_up_b, w_up_b, w_down_b, w_conv, b_conv, g_post)
    row = lambda i, f: (i, 0)
    if state is None:
        seq_tiles = seq_len // tm
        halo_blocks = tm // HALO
        in_specs = [pl.BlockSpec((tm, d), row),
                    pl.BlockSpec((HALO, d), lambda i, f: (jnp.maximum(i * halo_blocks - 1, 0), 0))] + common_w
        args = (x2, x2) + w_args
        gt_shape = jax.ShapeDtypeStruct((n // tm * 8, d_ff), F32)
        gt_spec = pl.BlockSpec((8, tf), lambda i, f: (i, f))
        scratch = [pltpu.VMEM((tm, d), BF16), pltpu.VMEM((HALO, d), BF16), pltpu.VMEM((tm, d), F32)]
        kern = functools.partial(_ffn_kernel, seq_tiles=seq_tiles, n_tok=None)
    else:
        n_tok = n // state.shape[0]
        s0 = jnp.repeat(state[:, 0], n_tok, axis=0)
        s1 = jnp.repeat(state[:, 1], n_tok, axis=0)
        in_specs = [pl.BlockSpec((tm, d), row),
                    pl.BlockSpec((tm, tf), lambda i, f: (i, f)),
                    pl.BlockSpec((tm, tf), lambda i, f: (i, f))] + common_w
        args = (x2, s0, s1) + w_args
        gt_shape = jax.ShapeDtypeStruct((n, d_ff), F32)
        gt_spec = pl.BlockSpec((tm, tf), lambda i, f: (i, f))
        scratch = [pltpu.VMEM((tm, d), BF16), pltpu.VMEM((tm, d), F32)]
        kern = functools.partial(_ffn_kernel, seq_tiles=None, n_tok=n_tok)
    return pl.pallas_call(
        kern,
        grid=(n // tm, nf),
        in_specs=in_specs,
        out_specs=(pl.BlockSpec((tm, d), row), gt_spec),
        out_shape=(jax.ShapeDtypeStruct((n, d), F32), gt_shape),
        scratch_shapes=scratch,
        compiler_params=_cparams(("arbitrary", "arbitrary")),
        name="ffn",
    )(*args)


def _rope_tables(pos):
    half = HEAD_DIM // 2
    inv_freq = ROPE_THETA ** (-jnp.arange(half, dtype=F32) / half)
    ang = pos.astype(F32)[:, None] * inv_freq[None, :]
    cos, sin = jnp.cos(ang), jnp.sin(ang)
    return jnp.concatenate([cos, cos], axis=-1), jnp.concatenate([-sin, sin], axis=-1)


def _cmp_weights(w1, pe, w2):
    w1r = w1.reshape(CMP_RATIO, CMP_STRIDE * HEAD_DIM, HEAD_DIM)
    w1cat = jnp.concatenate([w1r[r] for r in range(CMP_RATIO)], axis=1).astype(BF16)
    pe8 = jnp.pad(pe.reshape(CMP_RATIO, CMP_STRIDE * HEAD_DIM), ((0, 8 - CMP_RATIO), (0, 0)))
    return w1cat, pe8, w2.astype(BF16)


def kernel(x_prompt, x_sample, cache_nsa_cmp_kv, cache_nsa_sel_kv, cache_nsa_win_kv, cache_fox_kv, cache_fox_logf, state_ffn_conv, page_table, g_pre_mix, w_in, b_fgt, w_cmp_k1, pe_cmp_k, w_cmp_k2, w_cmp_v1, pe_cmp_v, w_cmp_v2, w_nsa_o, w_fox_o, w_out, g_post_mix, g_pre_ffn, w_up, w_conv, b_conv, w_down, g_post_ffn):
    b_p, t_p, d = x_prompt.shape
    b_s, t_s, _ = x_sample.shape
    depth = w_in.shape[0]
    page = cache_nsa_cmp_kv.shape[2]
    past = page_table.shape[1] * page
    g_n, n_h = N_NSA_GROUPS, N_FOX_HEADS

    cos_p, sin_p = _rope_tables(jnp.tile(jnp.arange(t_p), b_p))
    cos_s, sin_s = _rope_tables(jnp.tile(past + jnp.arange(t_s), b_s))

    y_p = x_prompt.reshape(b_p * t_p, d)
    y_s = x_sample.reshape(b_s * t_s, d)
    outs = {k: [] for k in ('cmp_p', 'cmp_s', 'sel_p', 'sel_s', 'win_p', 'win_s',
                            'fox_p', 'fox_s', 'lf_p', 'lf_s', 'conv_p', 'conv_s')}
    o_q = NSA_Q_W
    o_g = o_q + 3 * KV_W
    o_f = o_g + N_GATE_COLS
    o_ff = o_f + 3 * FOX_W
    o_m = o_ff + n_h
    for l in range(depth):
        w = w_in[l]
        w_main = jnp.concatenate([w[:, :o_g], w[:, o_f:o_ff], w[:, o_m:]], axis=1).astype(BF16)
        w_small = jnp.concatenate([w[:, o_g:o_f], w[:, o_ff:o_m],
                                   jnp.zeros((d, LANES - N_GATE_COLS - n_h), F32)], axis=1).astype(BF16)
        bf_row = jnp.zeros((1, LANES), F32).at[0, LOGF_COL0:LOGF_COL0 + n_h].set(b_fgt[l])
        g1 = g_pre_mix[l][None, :]
        cmp_k = _cmp_weights(w_cmp_k1[l], pe_cmp_k[l], w_cmp_k2[l])
        cmp_v = _cmp_weights(w_cmp_v1[l], pe_cmp_v[l], w_cmp_v2[l])
        wn, wf, wo = w_nsa_o[l].astype(BF16), w_fox_o[l].astype(BF16), w_out[l].astype(BF16)
        wu, wd = w_up[l].astype(BF16), w_down[l].astype(BF16)
        ffn_w = (g_pre_ffn[l][None, :], wu, wd, w_conv[l], b_conv[l][None, :], g_post_ffn[l][None, :])

        (qn, kvc, kvs, kvw, qf, kvf, gm, sm, kvs_b, kvw_b, kvf_b) = _project(y_p, g1, cos_p, sin_p, w_main, w_small, bf_row)
        w1cat = jnp.stack([cmp_k[0], cmp_v[0]])
        pe8 = jnp.stack([cmp_k[1], cmp_v[1]])
        w2 = jnp.stack([cmp_k[2], cmp_v[2]])
        ckv = _compress_prompt(kvc, b_p, t_p, w1cat, pe8, w2)
        fcol, frow = _fcum_prompt(sm, b_p, t_p)
        o_n = _nsa_prompt(qn, sm, ckv, kvs_b, kvw_b, b_p, t_p)
        o_fx = _fox_prompt(qf, kvf_b, fcol, frow, b_p, t_p)
        y1 = _postmix(o_n, o_fx, gm, y_p, wn, wf, wo, g_post_mix[l][None, :])
        y_p, gt = _ffn(y1, *ffn_w, seq_len=t_p)
        n_win = min(WINDOW, t_p)
        outs['cmp_p'].append(kvc.reshape(b_p, t_p, 2, g_n, HEAD_DIM))
        outs['sel_p'].append(kvs.reshape(b_p, t_p, 2, g_n, HEAD_DIM))
        outs['win_p'].append(kvw.reshape(b_p, t_p, 2, g_n, HEAD_DIM)[:, t_p - n_win:])
        outs['fox_p'].append(kvf.reshape(b_p, t_p, 2, n_h, HEAD_DIM))
        outs['lf_p'].append(sm[:, LOGF_COL0:LOGF_COL0 + n_h].reshape(b_p, t_p, n_h))
        tiles_per_seq = t_p // FFN_TM
        gt = gt.reshape(b_p, tiles_per_seq, 8, -1)
        outs['conv_p'].append(gt[:, -1, 8 - (CONV_WIDTH - 1):])

        (qn, kvc, kvs, kvw, qf, kvf, gm, sm, _, _, _) = _project(y_s, g1, cos_s, sin_s, w_main, w_small, bf_row)
        win_buf = cache_nsa_win_kv[l]
        o_n = _nsa_sample(page_table, cache_nsa_cmp_kv[l], cache_nsa_sel_kv[l], win_buf, kvs, kvw, qn, sm,
                          cmp_k + cmp_v, t_s)
        o_fx = _fox_sample(page_table, cache_fox_kv[l], cache_fox_logf[l], kvf, qf, sm, t_s)
        o_n = o_n[:, :t_s].reshape(b_s * t_s, -1)
        o_fx = o_fx[:, :t_s].reshape(b_s * t_s, -1)
        y1 = _postmix(o_n, o_fx, gm, y_s, wn, wf, wo, g_post_mix[l][None, :])
        y_s, gt = _ffn(y1, *ffn_w, state=state_ffn_conv[l])
        kw_new = kvw.reshape(b_s, t_s, 2, g_n, HEAD_DIM)
        n_win = min(WINDOW, win_buf.shape[1] + t_s)
        outs['cmp_s'].append(kvc.reshape(b_s, t_s, 2, g_n, HEAD_DIM))
        outs['sel_s'].append(kvs.reshape(b_s, t_s, 2, g_n, HEAD_DIM))
        outs['win_s'].append(jnp.concatenate([win_buf, kw_new], axis=1)[:, -n_win:])
        outs['fox_s'].append(kvf.reshape(b_s, t_s, 2, n_h, HEAD_DIM))
        outs['lf_s'].append(sm[:, LOGF_COL0:LOGF_COL0 + n_h].reshape(b_s, t_s, n_h))
        gfull = jnp.concatenate([state_ffn_conv[l], gt.reshape(b_s, t_s, -1)], axis=1)
        outs['conv_s'].append(gfull[:, t_s:])

    st = {k: jnp.stack(v) for k, v in outs.items()}
    return (y_p.reshape(b_p, t_p, d), y_s.reshape(b_s, t_s, d),
            st['cmp_p'], st['cmp_s'], st['sel_p'], st['sel_s'], st['win_p'], st['win_s'],
            st['fox_p'], st['fox_s'], st['lf_p'], st['lf_s'], st['conv_p'], st['conv_s'])
```

```python
import functools

import numpy as np
import jax
import jax.numpy as jnp
from jax import lax
from jax.experimental import pallas as pl
from jax.experimental.pallas import tpu as pltpu

F32 = jnp.float32
BF16 = jnp.bfloat16

HEAD_DIM = 128
N_NSA_HEADS = 8
N_NSA_GROUPS = 2
NSA_HPG = N_NSA_HEADS // N_NSA_GROUPS
N_FOX_HEADS = 8
CMP_BLOCK = 32
CMP_STRIDE = 16
CMP_RATIO = CMP_BLOCK // CMP_STRIDE
SEL_BLOCK = 64
SEL_TOPK = 16
N_LOCAL_BLOCKS = 2
WINDOW = 512
CONV_WIDTH = 3
ROPE_THETA = 10000.0
RMS_EPS = 1e-6
FORCE_BONUS = 1e4
NEG_INF = -1e30
QK_SCALE = HEAD_DIM ** -0.5

N_GATE_COLS = N_NSA_HEADS * 3
LOGF_COL0 = N_GATE_COLS
LANES = 128
VMEM_LIMIT = 56 * 1024 * 1024

PROJ_TN = 512
KV_W = 2 * N_NSA_GROUPS * HEAD_DIM
NSA_Q_W = N_NSA_HEADS * HEAD_DIM
FOX_W = N_FOX_HEADS * HEAD_DIM


def _cparams(sem):
    return pltpu.CompilerParams(dimension_semantics=sem, vmem_limit_bytes=VMEM_LIMIT)


def _dot(a, b):
    return jnp.dot(a, b, preferred_element_type=F32)


def _dot_nt(a, b):
    return lax.dot_general(a, b, (((1,), (1,)), ((), ())), preferred_element_type=F32)


def _rms(x, g):
    return x * lax.rsqrt(jnp.mean(x * x, axis=-1, keepdims=True) + RMS_EPS) * g


def _masked_softmax(s, mask):
    sm = jnp.where(mask, s, NEG_INF)
    m = jnp.max(sm, axis=-1, keepdims=True)
    e = jnp.where(mask, jnp.exp(sm - m), 0.0)
    l = jnp.sum(e, axis=-1, keepdims=True)
    return e / jnp.where(l > 0.0, l, 1.0)


def _lane_tile_max(mx, s):
    for c in range(s.shape[1] // LANES):
        mx = jnp.maximum(mx, s[:, c * LANES:(c + 1) * LANES])
    return mx


def _exp_accumulate(carry, s, m, v):
    ls, acc = carry
    p = jnp.exp(s - m)
    for c in range(s.shape[1] // LANES):
        ls = ls + p[:, c * LANES:(c + 1) * LANES]
    return ls, acc + _dot(p.astype(BF16), v)


def _split3(x):
    hi = x.astype(BF16)
    r = x - hi.astype(F32)
    mid = r.astype(BF16)
    lo = (r - mid.astype(F32)).astype(BF16)
    return hi, mid, lo


def _topk_mask(score, k, n_sel):
    st = score.T
    nv = -(-n_sel // 8)
    slabs = [st[8 * v:8 * v + 8, :] for v in range(nv)]
    sub = lax.broadcasted_iota(jnp.int32, (8, LANES), 0)
    ranks = [jnp.zeros((8, LANES), F32) for _ in range(nv)]
    for b2 in range(n_sel):
        row = jnp.broadcast_to(st[b2:b2 + 1, :], (8, LANES))
        for v in range(nv):
            if b2 < 8 * v:
                beats = row >= slabs[v]
            elif b2 >= 8 * v + 8:
                beats = row > slabs[v]
            else:
                beats = (row > slabs[v]) | ((row == slabs[v]) & (sub > b2 - 8 * v))
            ranks[v] = ranks[v] + jnp.where(beats, 1.0, 0.0)
    sel = [jnp.where((ranks[v] < k) & (sub + 8 * v < n_sel), 1.0, 0.0) for v in range(nv)]
    sel_t = jnp.concatenate(sel + [jnp.zeros((LANES - 8 * nv, LANES), F32)], axis=0)
    return sel_t.T


def _sel_scores(imp, tpos, n_sel):
    bidx = lax.broadcasted_iota(jnp.int32, imp.shape, 1)
    cur = jnp.right_shift(tpos, 6)
    valid = bidx <= cur
    forced = (bidx == 0) | (valid & (bidx > cur - N_LOCAL_BLOCKS))
    score = jnp.where(valid, jnp.where(forced, imp + FORCE_BONUS, imp), NEG_INF)
    return jnp.where(bidx < n_sel, score, -jnp.inf)


def _proj_kernel(x_ref, g_ref, cos_ref, sin_ref, w_ref, ws_ref, bf_ref,
                 qn_ref, kvc_ref, kvs_ref, kvw_ref, qf_ref, kvf_ref, gm_ref, sm_ref,
                 kvsb_ref, kvwb_ref, kvfb_ref, h_scr):
    j = pl.program_id(1)

    @pl.when(j == 0)
    def _():
        x = x_ref[...]
        y = x * lax.rsqrt(jnp.mean(x * x, axis=-1, keepdims=True) + RMS_EPS)
        h = (y * g_ref[...]).astype(BF16)
        h_scr[...] = h
        s = _dot(h, ws_ref[...])
        lane = lax.broadcasted_iota(jnp.int32, s.shape, 1)
        z = s + bf_ref[...]
        lf = jnp.minimum(z, 0.0) - jnp.log1p(jnp.exp(-jnp.abs(z)))
        sm_ref[...] = jnp.where(lane < N_GATE_COLS, jax.nn.sigmoid(s),
                                jnp.where(lane < LOGF_COL0 + N_FOX_HEADS, lf, 0.0))

    acc = _dot(h_scr[...], w_ref[...])
    cos = cos_ref[...]
    sin = sin_ref[...]

    def head(k):
        return acc[:, k * HEAD_DIM:(k + 1) * HEAD_DIM]

    def rope(a):
        return a * cos + pltpu.roll(a, HEAD_DIM // 2, axis=1) * sin

    def kv_rows(ref, bref):
        for k in range(N_NSA_GROUPS):
            r = rope(head(k))
            ref[:, k * HEAD_DIM:(k + 1) * HEAD_DIM] = r
            if bref is not None:
                bref[:, k * HEAD_DIM:(k + 1) * HEAD_DIM] = r.astype(BF16)
        v = acc[:, N_NSA_GROUPS * HEAD_DIM:]
        ref[:, N_NSA_GROUPS * HEAD_DIM:] = v
        if bref is not None:
            bref[:, N_NSA_GROUPS * HEAD_DIM:] = v.astype(BF16)

    @pl.when(j < 2)
    def _():
        for k in range(PROJ_TN // HEAD_DIM):
            qn_ref[:, k * HEAD_DIM:(k + 1) * HEAD_DIM] = (rope(head(k)) * QK_SCALE).astype(BF16)

    @pl.when(j == 2)
    def _():
        kv_rows(kvc_ref, None)

    @pl.when(j == 3)
    def _():
        kv_rows(kvs_ref, kvsb_ref)

    @pl.when(j == 4)
    def _():
        kv_rows(kvw_ref, kvwb_ref)

    @pl.when((j >= 5) & (j < 7))
    def _():
        qf_ref[...] = (acc * QK_SCALE).astype(BF16)

    @pl.when((j >= 7) & (j < 11))
    def _():
        kvf_ref[...] = acc
        kvfb_ref[...] = acc.astype(BF16)

    @pl.when(j >= 11)
    def _():
        gm_ref[...] = jax.nn.sigmoid(acc)


def _project(x2, g, cos2, sin2, w_main, w_small, bf_row):
    n, d = x2.shape
    tm = 512
    n_j = w_main.shape[1] // PROJ_TN
    tn = PROJ_TN

    def clip(lo, hi):
        return lambda i, j: (i, jnp.clip(j - lo, 0, hi - lo))

    row = lambda i, j: (i, 0)
    out_shape = (
        jax.ShapeDtypeStruct((n, NSA_Q_W), BF16),
        jax.ShapeDtypeStruct((n, KV_W), F32),
        jax.ShapeDtypeStruct((n, KV_W), F32),
        jax.ShapeDtypeStruct((n, KV_W), F32),
        jax.ShapeDtypeStruct((n, FOX_W), BF16),
        jax.ShapeDtypeStruct((n, 2 * FOX_W), F32),
        jax.ShapeDtypeStruct((n, 2 * d), F32),
        jax.ShapeDtypeStruct((n, LANES), F32),
        jax.ShapeDtypeStruct((n, KV_W), BF16),
        jax.ShapeDtypeStruct((n, KV_W), BF16),
        jax.ShapeDtypeStruct((n, 2 * FOX_W), BF16),
    )
    out_specs = (
        pl.BlockSpec((tm, tn), clip(0, 1)),
        pl.BlockSpec((tm, tn), row),
        pl.BlockSpec((tm, tn), row),
        pl.BlockSpec((tm, tn), row),
        pl.BlockSpec((tm, tn), clip(5, 6)),
        pl.BlockSpec((tm, tn), clip(7, 10)),
        pl.BlockSpec((tm, tn), clip(11, 18)),
        pl.BlockSpec((tm, LANES), row),
        pl.BlockSpec((tm, tn), row),
        pl.BlockSpec((tm, tn), row),
        pl.BlockSpec((tm, tn), clip(7, 10)),
    )
    in_specs = [
        pl.BlockSpec((tm, d), row),
        pl.BlockSpec((1, d), lambda i, j: (0, 0)),
        pl.BlockSpec((tm, LANES), row),
        pl.BlockSpec((tm, LANES), row),
        pl.BlockSpec((d, tn), lambda i, j: (0, j)),
        pl.BlockSpec((d, LANES), lambda i, j: (0, 0)),
        pl.BlockSpec((1, LANES), lambda i, j: (0, 0)),
    ]
    return pl.pallas_call(
        _proj_kernel,
        grid=(n // tm, n_j),
        in_specs=in_specs,
        out_specs=out_specs,
        out_shape=out_shape,
        scratch_shapes=[pltpu.VMEM((tm, d), BF16)],
        compiler_params=_cparams(("arbitrary", "arbitrary")),
        name="proj",
    )(x2, g, cos2, sin2, w_main, w_small, bf_row)


def _compress_tail(xc, w1, pe8, w2):
    n = xc.shape[0]
    part = _dot(xc, w1)
    pp = _dot(pe8.astype(BF16), w1)
    pe_term = pp[0:1, :HEAD_DIM] + pp[1:2, HEAD_DIM:]
    hid = pe_term + part[:, :HEAD_DIM] + pltpu.roll(part[:, HEAD_DIM:], n - 1, axis=0)
    return _dot(jax.nn.gelu(hid, approximate=True).astype(BF16), w2)


def _cmp_prompt_kernel(x_ref, w1_ref, pe_ref, w2_ref, o_ref):
    n = x_ref.shape[0] // CMP_STRIDE
    xc = jnp.concatenate([x_ref[pl.ds(i, n, stride=CMP_STRIDE), :] for i in range(CMP_STRIDE)],
                         axis=1).astype(BF16)
    o_ref[0, 0] = _compress_tail(xc, w1_ref[0], pe_ref[0], w2_ref[0]).astype(BF16)


def _compress_prompt(kvc, b_n, t_n, w1cat, pe8, w2):
    n = t_n // CMP_STRIDE
    return pl.pallas_call(
        _cmp_prompt_kernel,
        grid=(b_n, 2 * N_NSA_GROUPS),
        in_specs=[
            pl.BlockSpec((t_n, HEAD_DIM), lambda b, s: (b, s)),
            pl.BlockSpec((1, CMP_STRIDE * HEAD_DIM, 2 * HEAD_DIM), lambda b, s: (s // N_NSA_GROUPS, 0, 0)),
            pl.BlockSpec((1, 8, CMP_STRIDE * HEAD_DIM), lambda b, s: (s // N_NSA_GROUPS, 0, 0)),
            pl.BlockSpec((1, HEAD_DIM, HEAD_DIM), lambda b, s: (s // N_NSA_GROUPS, 0, 0)),
        ],
        out_specs=pl.BlockSpec((1, 1, n, HEAD_DIM), lambda b, s: (b, s, 0, 0)),
        out_shape=jax.ShapeDtypeStruct((b_n, 2 * N_NSA_GROUPS, n, HEAD_DIM), BF16),
        compiler_params=_cparams(("arbitrary", "arbitrary")),
        name="cmp_prompt",
    )(kvc, w1cat, pe8, w2)


def _fcum_kernel(x_ref, frow_ref, carry_scr):
    i = pl.program_id(1)

    @pl.when(i == 0)
    def _():
        carry_scr[...] = jnp.zeros_like(carry_scr)

    x = x_ref[...]
    tb = x.shape[0]
    r = lax.broadcasted_iota(jnp.int32, (tb, tb), 0)
    c = lax.broadcasted_iota(jnp.int32, (tb, tb), 1)
    tri = jnp.where(r >= c, 1.0, 0.0).astype(BF16)
    hi, mid, lo = _split3(x)
    cs = _dot(tri, hi) + _dot(tri, mid) + _dot(tri, lo) + carry_scr[0:1, :]
    carry_scr[...] = jnp.broadcast_to(cs[tb - 1:tb, :], carry_scr.shape)
    frow_ref[0] = cs.T[LOGF_COL0:LOGF_COL0 + N_FOX_HEADS, :]


def _fcum_prompt(sm, b_n, t_n):
    tb = 512
    nb = t_n // tb
    return pl.pallas_call(
        _fcum_kernel,
        grid=(b_n, nb),
        in_specs=[pl.BlockSpec((tb, LANES), lambda b, i: (b * nb + i, 0))],
        out_specs=pl.BlockSpec((1, N_FOX_HEADS, tb), lambda b, i: (b, 0, i)),
        out_shape=jax.ShapeDtypeStruct((b_n, N_FOX_HEADS, t_n), F32),
        scratch_shapes=[pltpu.VMEM((8, LANES), F32)],
        compiler_params=_cparams(("arbitrary", "arbitrary")),
        name="fcum_prompt",
    )(sm)


NSA_TQ = 128
NSA_TK = 256
NSA_NSUB = 2


def _tile_rows(a, reps):
    return jnp.concatenate([a] * reps, axis=0)


def _nsa_prompt_kernel(q_ref, sm_ref, ck_ref, ks_ref, kw_ref, cover_ref, expand_ref, o_ref, s_scr, *, n_cmp, n_sel):
    i = pl.program_id(1)
    tq, nsub = NSA_TQ, NSA_NSUB
    assert nsub * tq == NSA_TK
    t0 = i * (tq * nsub)
    n_ck = ck_ref.shape[2]
    band = WINDOW + tq
    tpos = [t0 + u * tq + lax.broadcasted_iota(jnp.int32, (tq, 1), 0) for u in range(nsub)]

    for g in range(N_NSA_GROUPS):
        ck = ck_ref[0, g]
        cv = ck_ref[0, N_NSA_GROUPS + g]
        o_cs, o_ws, selms = [], [], []
        for u in range(nsub):
            q = jnp.concatenate([q_ref[u * tq:(u + 1) * tq, (g * NSA_HPG + h) * HEAD_DIM:(g * NSA_HPG + h + 1) * HEAD_DIM]
                                 for h in range(NSA_HPG)], axis=0)
            s_c = _dot_nt(q, ck)
            cidx = lax.broadcasted_iota(jnp.int32, (tq, n_ck), 1)
            mc = jnp.where((cidx * CMP_STRIDE + CMP_BLOCK - 1 <= tpos[u]) & (cidx < n_cmp), 1.0, 0.0)
            p_c = _masked_softmax(s_c, _tile_rows(mc, NSA_HPG) > 0.5)
            p_cb = p_c.astype(BF16)
            imp4 = _dot(p_cb, cover_ref[...])
            imp = imp4[0:tq] + imp4[tq:2 * tq] + imp4[2 * tq:3 * tq] + imp4[3 * tq:4 * tq]
            selms.append(_topk_mask(_sel_scores(imp, tpos[u], n_sel), min(SEL_TOPK, n_sel), n_sel).astype(BF16))
            o_cs.append(_dot(p_cb, cv))
            w0 = pl.multiple_of(jnp.maximum(t0 + u * tq - WINDOW, 0), tq)
            kw = kw_ref[pl.ds(w0, band), g * HEAD_DIM:(g + 1) * HEAD_DIM]
            vw = kw_ref[pl.ds(w0, band), (N_NSA_GROUPS + g) * HEAD_DIM:(N_NSA_GROUPS + g + 1) * HEAD_DIM]
            s_w = _dot_nt(q, kw)
            wpos = w0 + lax.broadcasted_iota(jnp.int32, (tq, band), 1)
            mw = jnp.where((wpos <= tpos[u]) & (wpos > tpos[u] - WINDOW), 1.0, 0.0)
            p_w = _masked_softmax(s_w, _tile_rows(mw, NSA_HPG) > 0.5)
            o_ws.append(_dot(p_w.astype(BF16), vw))

        tb = tq * nsub
        q_all = jnp.concatenate([q_ref[:, (g * NSA_HPG + h) * HEAD_DIM:(g * NSA_HPG + h + 1) * HEAD_DIM]
                                 for h in range(NSA_HPG)], axis=0)
        selm_all = jnp.concatenate(selms, axis=0)
        tpos_all = t0 + lax.broadcasted_iota(jnp.int32, (tb, 1), 0)

        def sel_scores(k0, causal, q_all=q_all, selm_all=selm_all, g=g):
            k = ks_ref[pl.ds(k0, NSA_TK), g * HEAD_DIM:(g + 1) * HEAD_DIM]
            mk = _dot(selm_all, expand_ref[:, pl.ds(k0, NSA_TK)]) > 0.5
            if causal:
                mk = mk & (k0 + lax.broadcasted_iota(jnp.int32, (tb, NSA_TK), 1) <= tpos_all)
            s = _dot_nt(q_all, k).reshape(NSA_HPG, tb, NSA_TK)
            return jnp.where(mk[None], s, NEG_INF).reshape(NSA_HPG * tb, NSA_TK)

        def p1(kt, mx):
            k0 = pl.multiple_of(kt * NSA_TK, NSA_TK)
            s = sel_scores(k0, False)
            s_scr[:, pl.ds(k0, NSA_TK)] = s
            return _lane_tile_max(mx, s)

        n_full = t0 // NSA_TK
        mx = lax.fori_loop(0, n_full, p1, jnp.full((NSA_HPG * tb, LANES), -jnp.inf, F32))
        k_diag = pl.multiple_of(t0, NSA_TK)
        s = sel_scores(k_diag, True)
        s_scr[:, pl.ds(k_diag, NSA_TK)] = s
        m = jnp.max(_lane_tile_max(mx, s), axis=-1, keepdims=True)

        def p2(kt, carry, m=m, g=g):
            k0 = pl.multiple_of(kt * NSA_TK, NSA_TK)
            v = ks_ref[pl.ds(k0, NSA_TK), (N_NSA_GROUPS + g) * HEAD_DIM:(N_NSA_GROUPS + g + 1) * HEAD_DIM]
            return _exp_accumulate(carry, s_scr[:, pl.ds(k0, NSA_TK)], m, v)

        zeros = jnp.zeros((NSA_HPG * tb, LANES), F32)
        ls, acc = lax.fori_loop(0, n_full + 1, p2, (zeros, zeros))
        o_s = acc / jnp.sum(ls, axis=-1, keepdims=True)

        for u in range(nsub):
            gates = sm_ref[u * tq:(u + 1) * tq, :]
            for h in range(NSA_HPG):
                hh = g * NSA_HPG + h
                rows = slice(h * tq, (h + 1) * tq)
                rows_s = slice(h * tb + u * tq, h * tb + (u + 1) * tq)
                out = (gates[:, 3 * hh:3 * hh + 1] * o_cs[u][rows] + gates[:, 3 * hh + 1:3 * hh + 2] * o_s[rows_s]
                       + gates[:, 3 * hh + 2:3 * hh + 3] * o_ws[u][rows])
                o_ref[u * tq:(u + 1) * tq, hh * HEAD_DIM:(hh + 1) * HEAD_DIM] = out.astype(BF16)


def _cover_matrix(n_rows, n_cmp, n_sel):
    c = np.arange(n_rows)[:, None] * CMP_STRIDE
    b = np.arange(LANES)[None, :] * SEL_BLOCK
    m = (c < b + SEL_BLOCK) & (c + CMP_BLOCK > b) & (np.arange(n_rows)[:, None] < n_cmp) & (np.arange(LANES)[None, :] < n_sel)
    return jnp.asarray(m.astype(np.float32), dtype=BF16)


def _expand_matrix(n_keys):
    m = (np.arange(n_keys)[None, :] // SEL_BLOCK) == np.arange(LANES)[:, None]
    return jnp.asarray(m.astype(np.float32), dtype=BF16)


def _nsa_prompt(qn, sm, ckv, kvs_b, kvw_b, b_n, t_n):
    blk = NSA_TQ * NSA_NSUB
    nq = t_n // blk
    n_cmp = t_n // CMP_STRIDE - CMP_RATIO + 1
    n_sel = t_n // SEL_BLOCK
    n_ck = ckv.shape[2]
    cover = _cover_matrix(n_ck, n_cmp, n_sel)
    expand = _expand_matrix(t_n)
    kern = functools.partial(_nsa_prompt_kernel, n_cmp=n_cmp, n_sel=n_sel)
    return pl.pallas_call(
        kern,
        grid=(b_n, nq),
        in_specs=[
            pl.BlockSpec((blk, NSA_Q_W), lambda b, i: (b * nq + i, 0)),
            pl.BlockSpec((blk, LANES), lambda b, i: (b * nq + i, 0)),
            pl.BlockSpec((1, 2 * N_NSA_GROUPS, n_ck, HEAD_DIM), lambda b, i: (b, 0, 0, 0)),
            pl.BlockSpec((t_n, KV_W), lambda b, i: (b, 0)),
            pl.BlockSpec((t_n, KV_W), lambda b, i: (b, 0)),
            pl.BlockSpec((n_ck, LANES), lambda b, i: (0, 0)),
            pl.BlockSpec((LANES, t_n), lambda b, i: (0, 0)),
        ],
        out_specs=pl.BlockSpec((blk, NSA_Q_W), lambda b, i: (b * nq + i, 0)),
        out_shape=jax.ShapeDtypeStruct((b_n * t_n, NSA_Q_W), BF16),
        scratch_shapes=[pltpu.VMEM((NSA_HPG * blk, t_n), F32)],
        compiler_params=_cparams(("arbitrary", "arbitrary")),
        name="nsa_prompt",
    )(qn, sm, ckv, kvs_b, kvw_b, cover, expand)


FOX_BLK = 512


def _fox_prompt_kernel(q_ref, k_ref, v_ref, frow_ref, o_ref, s_scr):
    h = pl.program_id(1)
    blk = FOX_BLK
    assert q_ref.shape[0] % blk == 0

    def q_block(qb, _):
        t0 = pl.multiple_of(qb * blk, blk)
        q = q_ref[pl.ds(t0, blk), :]

        def scores(k0):
            return _dot_nt(q, k_ref[pl.ds(k0, blk), :]) - frow_ref[0, pl.ds(h, 1), pl.ds(k0, blk)]

        def p1(kt, mx):
            k0 = pl.multiple_of(kt * blk, blk)
            s = scores(k0)
            s_scr[:, pl.ds(k0, blk)] = s
            return _lane_tile_max(mx, s)

        n_full = t0 // blk
        mx = lax.fori_loop(0, n_full, p1, jnp.full((blk, LANES), -jnp.inf, F32))
        causal = (t0 + lax.broadcasted_iota(jnp.int32, (blk, blk), 1)
                  <= t0 + lax.broadcasted_iota(jnp.int32, (blk, 1), 0))
        s = jnp.where(causal, scores(t0), NEG_INF)
        s_scr[:, pl.ds(t0, blk)] = s
        m = jnp.max(_lane_tile_max(mx, s), axis=-1, keepdims=True)

        def p2(kt, carry):
            k0 = pl.multiple_of(kt * blk, blk)
            return _exp_accumulate(carry, s_scr[:, pl.ds(k0, blk)], m, v_ref[pl.ds(k0, blk), :])

        zeros = jnp.zeros((blk, LANES), F32)
        ls, acc = lax.fori_loop(0, n_full + 1, p2, (zeros, zeros))
        o_ref[pl.ds(t0, blk), :] = (acc / jnp.sum(ls, axis=-1, keepdims=True)).astype(BF16)
        return 0

    lax.fori_loop(0, q_ref.shape[0] // blk, q_block, 0)


def _fox_prompt(qf, kvf_b, frow, b_n, t_n):
    nh = N_FOX_HEADS
    return pl.pallas_call(
        _fox_prompt_kernel,
        grid=(b_n, nh),
        in_specs=[
            pl.BlockSpec((t_n, HEAD_DIM), lambda b, h: (b, h)),
            pl.BlockSpec((t_n, HEAD_DIM), lambda b, h: (b, h)),
            pl.BlockSpec((t_n, HEAD_DIM), lambda b, h: (b, nh + h)),
            pl.BlockSpec((1, nh, t_n), lambda b, h: (b, 0, 0)),
        ],
        out_specs=pl.BlockSpec((t_n, HEAD_DIM), lambda b, h: (b, h)),
        out_shape=jax.ShapeDtypeStruct((b_n * t_n, FOX_W), BF16),
        scratch_shapes=[pltpu.VMEM((FOX_BLK, t_n), F32)],
        compiler_params=_cparams(("arbitrary", "arbitrary")),
        name="fox_prompt",
    )(qf, kvf_b, kvf_b, frow)


FOX_SAMPLE_CH = 512
TOK_PAD = 8
NEW_PAD = 128


def _with_new(past, new8):
    pad = jnp.zeros((NEW_PAD - new8.shape[0], new8.shape[1]), new8.dtype)
    return jnp.concatenate([past, new8, pad], axis=0)


def _nsa_sample_kernel(pt_ref, *refs, n_pages, page, n_buf, n_tok):
    cmp_pages = refs[:n_pages]
    sel_pages = refs[n_pages:2 * n_pages]
    (win_ref, kvs_new_ref, kvw_new_ref, q_ref, sm_ref, w1k_ref, w1v_ref, pek_ref, pev_ref,
     w2k_ref, w2v_ref, cover_ref, expand_ref, o_ref) = refs[2 * n_pages:]
    del pt_ref
    past = n_pages * page
    n_slab = 2 * N_NSA_GROUPS
    chunks_per_page = page // CMP_STRIDE
    n_chunk = n_pages * chunks_per_page
    n_cmp = (past + n_tok + CMP_STRIDE - 1) // CMP_STRIDE - CMP_RATIO + 1
    n_sel = (past + n_tok + SEL_BLOCK - 1) // SEL_BLOCK
    tp = TOK_PAD
    tpos = past + lax.broadcasted_iota(jnp.int32, (tp, 1), 0)
    gates = sm_ref[0]
    kvs_new = kvs_new_ref[0]
    kvw_new = kvw_new_ref[0]

    def compress(slab, w1_ref, pe_ref, w2_ref):
        cols = []
        for i in range(CMP_STRIDE):
            cols.append(jnp.concatenate(
                [cmp_pages[p][pl.ds(i * n_slab + slab, chunks_per_page, stride=CMP_STRIDE * n_slab), :]
                 for p in range(n_pages)], axis=0))
        xc = jnp.concatenate(cols, axis=1).astype(BF16)
        return _compress_tail(xc, w1_ref[...], pe_ref[...], w2_ref[...]).astype(BF16)

    def sel_rows(slab):
        return jnp.concatenate([sel_pages[p][pl.ds(slab, page, stride=n_slab), :] for p in range(n_pages)], axis=0)

    qs, o_cs, scores = [], [], []
    for g in range(N_NSA_GROUPS):
        q = jnp.concatenate([q_ref[0, :, (g * NSA_HPG + h) * HEAD_DIM:(g * NSA_HPG + h + 1) * HEAD_DIM]
                             for h in range(NSA_HPG)], axis=0)
        ck = compress(g, w1k_ref, pek_ref, w2k_ref)
        cv = compress(N_NSA_GROUPS + g, w1v_ref, pev_ref, w2v_ref)
        s_c = _dot_nt(q, ck)
        cidx = lax.broadcasted_iota(jnp.int32, (tp, n_chunk), 1)
        mc = jnp.where((cidx * CMP_STRIDE + CMP_BLOCK - 1 <= tpos) & (cidx < n_cmp), 1.0, 0.0)
        p_c = _masked_softmax(s_c, _tile_rows(mc, NSA_HPG) > 0.5)
        p_cb = p_c.astype(BF16)
        imp4 = _dot(p_cb, cover_ref[...])
        imp = imp4[0:tp] + imp4[tp:2 * tp] + imp4[2 * tp:3 * tp] + imp4[3 * tp:4 * tp]
        qs.append(q)
        o_cs.append(_dot(p_cb, cv))
        scores.append(_sel_scores(imp, tpos, n_sel))

    score_all = jnp.concatenate(scores + [jnp.zeros((LANES - N_NSA_GROUPS * tp, LANES), F32)], axis=0)
    selm_all = _topk_mask(score_all, min(SEL_TOPK, n_sel), n_sel)

    for g in range(N_NSA_GROUPS):
        q, o_c = qs[g], o_cs[g]
        selm = selm_all[g * tp:(g + 1) * tp].astype(BF16)

        n_keys = past + NEW_PAD
        k_all = _with_new(sel_rows(g), kvs_new[:, g * HEAD_DIM:(g + 1) * HEAD_DIM]).astype(BF16)
        v_all = _with_new(sel_rows(N_NSA_GROUPS + g),
                          kvs_new[:, (N_NSA_GROUPS + g) * HEAD_DIM:(N_NSA_GROUPS + g + 1) * HEAD_DIM]).astype(BF16)
        s_s = _dot_nt(q, k_all)
        selx = _dot(selm, expand_ref[...])
        kpos = lax.broadcasted_iota(jnp.int32, (tp, n_keys), 1)
        ms = jnp.where((selx > 0.5) & (kpos <= tpos) & (kpos < past + n_tok), 1.0, 0.0)
        p_s = _masked_softmax(s_s, _tile_rows(ms, NSA_HPG) > 0.5)
        o_s = _dot(p_s.astype(BF16), v_all)

        kw_all = _with_new(win_ref[pl.ds(g, n_buf, stride=n_slab), :],
                           kvw_new[:, g * HEAD_DIM:(g + 1) * HEAD_DIM]).astype(BF16)
        vw_all = _with_new(win_ref[pl.ds(N_NSA_GROUPS + g, n_buf, stride=n_slab), :],
                           kvw_new[:, (N_NSA_GROUPS + g) * HEAD_DIM:(N_NSA_GROUPS + g + 1) * HEAD_DIM]).astype(BF16)
        s_w = _dot_nt(q, kw_all)
        wpos = past - n_buf + lax.broadcasted_iota(jnp.int32, (tp, n_buf + NEW_PAD), 1)
        mw = jnp.where((wpos <= tpos) & (wpos > tpos - WINDOW) & (wpos < past + n_tok), 1.0, 0.0)
        p_w = _masked_softmax(s_w, _tile_rows(mw, NSA_HPG) > 0.5)
        o_w = _dot(p_w.astype(BF16), vw_all)

        for h in range(NSA_HPG):
            hh = g * NSA_HPG + h
            rows = slice(h * tp, (h + 1) * tp)
            out = (gates[:, 3 * hh:3 * hh + 1] * o_c[rows] + gates[:, 3 * hh + 1:3 * hh + 2] * o_s[rows]
                   + gates[:, 3 * hh + 2:3 * hh + 3] * o_w[rows])
            o_ref[0, :, hh * HEAD_DIM:(hh + 1) * HEAD_DIM] = out.astype(BF16)


def _pad_tokens(a, n_seq, n_tok):
    a = a.reshape(n_seq, n_tok, a.shape[-1])
    return jnp.pad(a, ((0, 0), (0, TOK_PAD - n_tok), (0, 0)))


def _nsa_sample(page_table, cache_cmp, cache_sel, win_buf, kvs_new, kvw_new, qn, sm, cmp_w, n_tok):
    n_seq, n_pages = page_table.shape
    page = cache_cmp.shape[1]
    n_slab = 2 * N_NSA_GROUPS
    n_buf = win_buf.shape[1]
    past = n_pages * page
    cmp2 = cache_cmp.reshape(-1, HEAD_DIM)
    sel2 = cache_sel.reshape(-1, HEAD_DIM)
    win2 = win_buf.reshape(-1, HEAD_DIM)
    n_chunk = past // CMP_STRIDE
    n_cmp = (past + n_tok + CMP_STRIDE - 1) // CMP_STRIDE - CMP_RATIO + 1
    n_sel = (past + n_tok + SEL_BLOCK - 1) // SEL_BLOCK
    cover = _cover_matrix(n_chunk, min(n_cmp, n_chunk), n_sel)
    expand = _expand_matrix(past + NEW_PAD)
    w1k, pek, w2k, w1v, pev, w2v = cmp_w

    def page_spec(p):
        return pl.BlockSpec((page * n_slab, HEAD_DIM), lambda b, pt, p=p: (pt[b, p], 0))

    const2 = lambda b, pt: (0, 0)
    seq3 = lambda b, pt: (b, 0, 0)
    in_specs = ([page_spec(p) for p in range(n_pages)] + [page_spec(p) for p in range(n_pages)] + [
        pl.BlockSpec((n_buf * n_slab, HEAD_DIM), lambda b, pt: (b, 0)),
        pl.BlockSpec((1, TOK_PAD, KV_W), seq3),
        pl.BlockSpec((1, TOK_PAD, KV_W), seq3),
        pl.BlockSpec((1, TOK_PAD, NSA_Q_W), seq3),
        pl.BlockSpec((1, TOK_PAD, LANES), seq3),
        pl.BlockSpec(w1k.shape, const2),
        pl.BlockSpec(w1v.shape, const2),
        pl.BlockSpec(pek.shape, const2),
        pl.BlockSpec(pev.shape, const2),
        pl.BlockSpec(w2k.shape, const2),
        pl.BlockSpec(w2v.shape, const2),
        pl.BlockSpec(cover.shape, const2),
        pl.BlockSpec(expand.shape, const2),
    ])
    kern = functools.partial(_nsa_sample_kernel, n_pages=n_pages, page=page, n_buf=n_buf, n_tok=n_tok)
    grid_spec = pltpu.PrefetchScalarGridSpec(
        num_scalar_prefetch=1, grid=(n_seq,), in_specs=in_specs,
        out_specs=pl.BlockSpec((1, TOK_PAD, NSA_Q_W), seq3))
    return pl.pallas_call(
        kern,
        grid_spec=grid_spec,
        out_shape=jax.ShapeDtypeStruct((n_seq, TOK_PAD, NSA_Q_W), BF16),
        compiler_params=_cparams(("arbitrary",)),
        name="nsa_sample",
    )(page_table, *([cmp2] * n_pages), *([sel2] * n_pages), win2,
      _pad_tokens(kvs_new, n_seq, n_tok), _pad_tokens(kvw_new, n_seq, n_tok),
      _pad_tokens(qn, n_seq, n_tok), _pad_tokens(sm, n_seq, n_tok),
      w1k, w1v, pek, pev, w2k, w2v, cover, expand)


def _fox_sample_kernel(pt_ref, *refs, n_pages, page, n_tok):
    kv_pages = refs[:n_pages]
    lf_pages = refs[n_pages:2 * n_pages]
    kvf_new_ref, q_ref, lfn_ref, o_ref, mask_scr, s_scr = refs[2 * n_pages:]
    del pt_ref
    nh = N_FOX_HEADS
    tp = TOK_PAD
    rows_pp = 2 * nh
    page_rows = page * rows_pp
    n_chunk = page_rows // LANES
    n_q = nh * tp
    assert rows_pp == 16 and LANES % rows_pp == 0

    @pl.when(pl.program_id(0) == 0)
    def _():
        qrow = lax.broadcasted_iota(jnp.int32, mask_scr.shape, 0)
        lane = lax.broadcasted_iota(jnp.int32, mask_scr.shape, 1)
        mask_scr[...] = jnp.where((lane & (rows_pp - 1)) == jnp.right_shift(qrow, 3), 0.0, NEG_INF)

    x = jnp.concatenate([lf_pages[p][0] for p in range(n_pages)], axis=0)
    n_r = x.shape[0]
    la = lax.broadcasted_iota(jnp.int32, (LANES, LANES), 0)
    lb = lax.broadcasted_iota(jnp.int32, (LANES, LANES), 1)
    same = (la & (rows_pp - 1)) == (lb & (rows_pp - 1))
    u_in = jnp.where(same & (jnp.right_shift(la, 4) <= jnp.right_shift(lb, 4)), 1.0, 0.0).astype(BF16)
    u_all = jnp.where(same, 1.0, 0.0).astype(BF16)
    xh, xm, xl = _split3(x)
    within = _dot(xh, u_in) + _dot(xm, u_in) + _dot(xl, u_in)
    tot = _dot(xh, u_all) + _dot(xm, u_all) + _dot(xl, u_all)
    ra = lax.broadcasted_iota(jnp.int32, (n_r, n_r), 0)
    rb = lax.broadcasted_iota(jnp.int32, (n_r, n_r), 1)
    before = jnp.where(rb < ra, 1.0, 0.0).astype(BF16)
    th, tm_, tl = _split3(tot)
    offs = _dot(before, th) + _dot(before, tm_) + _dot(before, tl)
    f_end = offs[n_r - 1:n_r, :] + tot[n_r - 1:n_r, :]
    bias = f_end - (within + offs)

    q_all = jnp.concatenate([q_ref[0, :, h * HEAD_DIM:(h + 1) * HEAD_DIM] for h in range(nh)], axis=0)

    zero_tile = jnp.zeros((nh, HEAD_DIM), F32)
    ch = mask_scr.shape[1]
    lt = ch // LANES
    steps = [(p, c) for p in range(n_pages) for c in range(page_rows // ch)]
    mx = jnp.full((n_q, LANES), -jnp.inf, F32)
    for p, c in steps:
        k_b = kv_pages[p][pl.ds(c * ch, ch), :].astype(BF16)
        r0 = p * n_chunk + c * lt
        brow = jnp.concatenate([bias[r0 + i:r0 + i + 1, :] for i in range(lt)], axis=1)
        s = _dot_nt(q_all, k_b) + brow + mask_scr[...]
        s_scr[:, p * page_rows + c * ch:p * page_rows + (c + 1) * ch] = s
        mx = _lane_tile_max(mx, s)

    kvf_new = kvf_new_ref[0]
    pad = jnp.zeros((LANES - n_q, HEAD_DIM), F32)
    k_new = jnp.concatenate([kvf_new[:, h * HEAD_DIM:(h + 1) * HEAD_DIM] for h in range(nh)] + [pad], axis=0)
    v_new = jnp.concatenate([kvf_new[:, (nh + h) * HEAD_DIM:(nh + h + 1) * HEAD_DIM] for h in range(nh)] + [pad], axis=0)
    g_in = jnp.where((jnp.right_shift(la, 3) == jnp.right_shift(lb, 3)) & (la <= lb), 1.0, 0.0).astype(BF16)
    nh_, nm_, nl_ = _split3(lfn_ref[0])
    c_new = (_dot(nh_, g_in) + _dot(nm_, g_in) + _dot(nl_, g_in))[0:1, :]
    qrow = lax.broadcasted_iota(jnp.int32, (n_q, LANES), 0)
    lane = lax.broadcasted_iota(jnp.int32, (n_q, LANES), 1)
    ok = ((jnp.right_shift(lane, 3) == jnp.right_shift(qrow, 3)) & ((lane & (tp - 1)) <= (qrow & (tp - 1)))
          & ((lane & (tp - 1)) < n_tok) & (lane < n_q))
    s_new = jnp.where(ok, _dot_nt(q_all, k_new.astype(BF16)) - c_new, NEG_INF)
    m = jnp.max(jnp.maximum(mx, s_new), axis=-1, keepdims=True)

    carry = _exp_accumulate((jnp.zeros((n_q, LANES), F32), jnp.zeros((n_q, HEAD_DIM), F32)),
                            s_new, m, v_new.astype(BF16))
    for p, c in steps:
        if (c + 1) * ch + nh <= page_rows:
            v_b = kv_pages[p][pl.ds(c * ch + nh, ch), :].astype(BF16)
        else:
            v_b = jnp.concatenate([kv_pages[p][pl.ds(c * ch + nh, ch - nh), :], zero_tile], axis=0).astype(BF16)
        carry = _exp_accumulate(carry, s_scr[:, p * page_rows + c * ch:p * page_rows + (c + 1) * ch], m, v_b)
    ls, acc = carry
    o = acc / jnp.sum(ls, axis=-1, keepdims=True)
    for h in range(nh):
        o_ref[0, :, h * HEAD_DIM:(h + 1) * HEAD_DIM] = o[h * tp:(h + 1) * tp].astype(BF16)


def _fox_sample(page_table, cache_fox, cache_logf, kvf_new, qf, sm, n_tok):
    n_seq, n_pages = page_table.shape
    page = cache_fox.shape[1]
    nh = N_FOX_HEADS
    kv2 = cache_fox.reshape(-1, HEAD_DIM)
    rows_pp = 2 * nh
    n_chunk = page * rows_pp // LANES
    lf_c = jnp.pad(cache_logf, ((0, 0), (0, 0), (0, rows_pp - nh))).reshape(-1, n_chunk, LANES)
    lf_new = sm[:, LOGF_COL0:LOGF_COL0 + nh].reshape(n_seq, n_tok, nh)
    lfn = jnp.pad(jnp.swapaxes(lf_new, 1, 2), ((0, 0), (0, 0), (0, TOK_PAD - n_tok))).reshape(n_seq, 1, nh * TOK_PAD)
    lfn = jnp.pad(lfn, ((0, 0), (0, 7), (0, LANES - nh * TOK_PAD)))

    seq3 = lambda b, pt: (b, 0, 0)
    in_specs = ([pl.BlockSpec((page * rows_pp, HEAD_DIM), lambda b, pt, p=p: (pt[b, p], 0)) for p in range(n_pages)]
                + [pl.BlockSpec((1, n_chunk, LANES), lambda b, pt, p=p: (pt[b, p], 0, 0)) for p in range(n_pages)]
                + [pl.BlockSpec((1, TOK_PAD, 2 * FOX_W), seq3),
                   pl.BlockSpec((1, TOK_PAD, FOX_W), seq3),
                   pl.BlockSpec((1, 8, LANES), seq3)])
    kern = functools.partial(_fox_sample_kernel, n_pages=n_pages, page=page, n_tok=n_tok)
    grid_spec = pltpu.PrefetchScalarGridSpec(
        num_scalar_prefetch=1, grid=(n_seq,), in_specs=in_specs,
        out_specs=pl.BlockSpec((1, TOK_PAD, FOX_W), seq3),
        scratch_shapes=[pltpu.VMEM((nh * TOK_PAD, FOX_SAMPLE_CH), F32),
                        pltpu.VMEM((nh * TOK_PAD, n_pages * page * rows_pp), F32)])
    return pl.pallas_call(
        kern,
        grid_spec=grid_spec,
        out_shape=jax.ShapeDtypeStruct((n_seq, TOK_PAD, FOX_W), BF16),
        compiler_params=_cparams(("arbitrary",)),
        name="fox_sample",
    )(page_table, *([kv2] * n_pages), *([lf_c] * n_pages),
      _pad_tokens(kvf_new, n_seq, n_tok), _pad_tokens(qf, n_seq, n_tok), lfn)


def _postmix_kernel(on_ref, of_ref, gm0_ref, gm1_ref, x_ref, wn_ref, wf_ref, wo_ref, g_ref, y_ref):
    a = _dot(on_ref[...], wn_ref[...])
    b = _dot(of_ref[...], wf_ref[...])
    merged = gm0_ref[...] * a + gm1_ref[...] * b
    z = _dot(merged.astype(BF16), wo_ref[...])
    y_ref[...] = x_ref[...] + _rms(z, g_ref[...])


def _postmix(o_n, o_f, gm, x2, wn, wf, wo, g):
    n, d = x2.shape
    tm = 256
    row = lambda i: (i, 0)
    const = lambda i: (0, 0)
    return pl.pallas_call(
        _postmix_kernel,
        grid=(n // tm,),
        in_specs=[
            pl.BlockSpec((tm, NSA_Q_W), row),
            pl.BlockSpec((tm, FOX_W), row),
            pl.BlockSpec((tm, d), lambda i: (i, 0)),
            pl.BlockSpec((tm, d), lambda i: (i, 1)),
            pl.BlockSpec((tm, d), row),
            pl.BlockSpec(wn.shape, const),
            pl.BlockSpec(wf.shape, const),
            pl.BlockSpec(wo.shape, const),
            pl.BlockSpec((1, d), const),
        ],
        out_specs=pl.BlockSpec((tm, d), row),
        out_shape=jax.ShapeDtypeStruct((n, d), F32),
        compiler_params=_cparams(("arbitrary",)),
        name="postmix",
    )(o_n, o_f, gm, gm, x2, wn, wf, wo, g)


FFN_TM = 512
FFN_TF = 512
HALO = 16


def _ffn_kernel(*refs, seq_tiles, n_tok):
    if n_tok is None:
        (x_ref, xh_ref, g_ref, wg_ref, wu_ref, wd_ref, wc_ref, bc_ref, gp_ref,
         y_ref, gt_ref, h_scr, hh_scr, acc_scr) = refs
    else:
        (x_ref, s0_ref, s1_ref, g_ref, wg_ref, wu_ref, wd_ref, wc_ref, bc_ref, gp_ref,
         y_ref, gt_ref, h_scr, acc_scr) = refs
    i = pl.program_id(0)
    f = pl.program_id(1)
    tm = x_ref.shape[0]

    @pl.when(f == 0)
    def _():
        h_scr[...] = _rms(x_ref[...], g_ref[...]).astype(BF16)
        acc_scr[...] = jnp.zeros_like(acc_scr)
        if n_tok is None:
            hh_scr[...] = _rms(xh_ref[...], g_ref[...]).astype(BF16)

    h2 = h_scr[...]
    gate = _dot(h2, wg_ref[...])
    up = _dot(h2, wu_ref[...])
    row = lax.broadcasted_iota(jnp.int32, gate.shape, 0)
    r1 = pltpu.roll(gate, 1, axis=0)
    r2 = pltpu.roll(gate, 2, axis=0)
    if n_tok is None:
        first = (i % seq_tiles) == 0
        gh = jnp.where(first, 0.0, _dot(hh_scr[...], wg_ref[...]))
        p1 = gh[HALO - 1:HALO, :]
        p2 = gh[HALO - 2:HALO - 1, :]
        g1 = jnp.where(row == 0, p1, r1)
        g2 = jnp.where(row == 0, p2, jnp.where(row == 1, p1, r2))
        gt_ref[...] = gate[tm - 8:tm, :]
    else:
        assert n_tok & (n_tok - 1) == 0
        rt = row & (n_tok - 1)
        g1 = jnp.where(rt == 0, s1_ref[...], r1)
        g2 = jnp.where(rt == 0, s0_ref[...], jnp.where(rt == 1, s1_ref[...], r2))
        gt_ref[...] = gate
    wc = wc_ref[...]
    gc = bc_ref[...] + wc[0:1, :] * g2 + wc[1:2, :] * g1 + wc[2:3, :] * gate
    act = jax.nn.gelu(gc, approximate=True) * up
    acc_scr[...] += _dot(act.astype(BF16), wd_ref[...])

    @pl.when(f == pl.num_programs(1) - 1)
    def _():
        y_ref[...] = x_ref[...] + _rms(acc_scr[...], gp_ref[...])


def _ffn(x2, g_pre, w_up_b, w_down_b, w_conv, b_conv, g_post, *, seq_len=None, state=None):
    n, d = x2.shape
    d_ff = w_down_b.shape[0]
    tf = FFN_TF
    nf = d_ff // tf
    tm = min(FFN_TM, n)
    common_w = [
        pl.BlockSpec((1, d), lambda i, f: (0, 0)),
        pl.BlockSpec((d, tf), lambda i, f: (0, f)),
        pl.BlockSpec((d, tf), lambda i, f: (0, nf + f)),
        pl.BlockSpec((tf, d), lambda i, f: (f, 0)),
        pl.BlockSpec((CONV_WIDTH, tf), lambda i, f: (0, f)),
        pl.BlockSpec((1, tf), lambda i, f: (0, f)),
        pl.BlockSpec((1, d), lambda i, f: (0, 0)),
    ]
    w_args = (g_pre, w_up_b, w_up_b, w_down_b, w_conv, b_conv, g_post)
    row = lambda i, f: (i, 0)
    if state is None:
        seq_tiles = seq_len // tm
        halo_blocks = tm // HALO
        in_specs = [pl.BlockSpec((tm, d), row),
                    pl.BlockSpec((HALO, d), lambda i, f: (jnp.maximum(i * halo_blocks - 1, 0), 0))] + common_w
        args = (x2, x2) + w_args
        gt_shape = jax.ShapeDtypeStruct((n // tm * 8, d_ff), F32)
        gt_spec = pl.BlockSpec((8, tf), lambda i, f: (i, f))
        scratch = [pltpu.VMEM((tm, d), BF16), pltpu.VMEM((HALO, d), BF16), pltpu.VMEM((tm, d), F32)]
        kern = functools.partial(_ffn_kernel, seq_tiles=seq_tiles, n_tok=None)
    else:
        n_tok = n // state.shape[0]
        s0 = jnp.repeat(state[:, 0], n_tok, axis=0)
        s1 = jnp.repeat(state[:, 1], n_tok, axis=0)
        in_specs = [pl.BlockSpec((tm, d), row),
                    pl.BlockSpec((tm, tf), lambda i, f: (i, f)),
                    pl.BlockSpec((tm, tf), lambda i, f: (i, f))] + common_w
        args = (x2, s0, s1) + w_args
        gt_shape = jax.ShapeDtypeStruct((n, d_ff), F32)
        gt_spec = pl.BlockSpec((tm, tf), lambda i, f: (i, f))
        scratch = [pltpu.VMEM((tm, d), BF16), pltpu.VMEM((tm, d), F32)]
        kern = functools.partial(_ffn_kernel, seq_tiles=None, n_tok=n_tok)
    return pl.pallas_call(
        kern,
        grid=(n // tm, nf),
        in_specs=in_specs,
        out_specs=(pl.BlockSpec((tm, d), row), gt_spec),
        out_shape=(jax.ShapeDtypeStruct((n, d), F32), gt_shape),
        scratch_shapes=scratch,
        compiler_params=_cparams(("arbitrary", "arbitrary")),
        name="ffn",
    )(*args)


def _rope_tables(pos):
    half = HEAD_DIM // 2
    inv_freq = ROPE_THETA ** (-jnp.arange(half, dtype=F32) / half)
    ang = pos.astype(F32)[:, None] * inv_freq[None, :]
    cos, sin = jnp.cos(ang), jnp.sin(ang)
    return jnp.concatenate([cos, cos], axis=-1), jnp.concatenate([-sin, sin], axis=-1)


def _cmp_weights(w1, pe, w2):
    w1r = w1.reshape(CMP_RATIO, CMP_STRIDE * HEAD_DIM, HEAD_DIM)
    w1cat = jnp.concatenate([w1r[r] for r in range(CMP_RATIO)], axis=1).astype(BF16)
    pe8 = jnp.pad(pe.reshape(CMP_RATIO, CMP_STRIDE * HEAD_DIM), ((0, 8 - CMP_RATIO), (0, 0)))
    return w1cat, pe8, w2.astype(BF16)


def kernel(x_prompt, x_sample, cache_nsa_cmp_kv, cache_nsa_sel_kv, cache_nsa_win_kv, cache_fox_kv, cache_fox_logf, state_ffn_conv, page_table, g_pre_mix, w_in, b_fgt, w_cmp_k1, pe_cmp_k, w_cmp_k2, w_cmp_v1, pe_cmp_v, w_cmp_v2, w_nsa_o, w_fox_o, w_out, g_post_mix, g_pre_ffn, w_up, w_conv, b_conv, w_down, g_post_ffn):
    b_p, t_p, d = x_prompt.shape
    b_s, t_s, _ = x_sample.shape
    depth = w_in.shape[0]
    page = cache_nsa_cmp_kv.shape[2]
    past = page_table.shape[1] * page
    g_n, n_h = N_NSA_GROUPS, N_FOX_HEADS

    cos_p, sin_p = _rope_tables(jnp.tile(jnp.arange(t_p), b_p))
    cos_s, sin_s = _rope_tables(jnp.tile(past + jnp.arange(t_s), b_s))

    y_p = x_prompt.reshape(b_p * t_p, d)
    y_s = x_sample.reshape(b_s * t_s, d)
    outs = {k: [] for k in ('cmp_p', 'cmp_s', 'sel_p', 'sel_s', 'win_p', 'win_s',
                            'fox_p', 'fox_s', 'lf_p', 'lf_s', 'conv_p', 'conv_s')}
    o_q = NSA_Q_W
    o_g = o_q + 3 * KV_W
    o_f = o_g + N_GATE_COLS
    o_ff = o_f + 3 * FOX_W
    o_m = o_ff + n_h
    for l in range(depth):
        w = w_in[l]
        w_main = jnp.concatenate([w[:, :o_g], w[:, o_f:o_ff], w[:, o_m:]], axis=1).astype(BF16)
        w_small = jnp.concatenate([w[:, o_g:o_f], w[:, o_ff:o_m],
                                   jnp.zeros((d, LANES - N_GATE_COLS - n_h), F32)], axis=1).astype(BF16)
        bf_row = jnp.zeros((1, LANES), F32).at[0, LOGF_COL0:LOGF_COL0 + n_h].set(b_fgt[l])
        g1 = g_pre_mix[l][None, :]
        cmp_k = _cmp_weights(w_cmp_k1[l], pe_cmp_k[l], w_cmp_k2[l])
        cmp_v = _cmp_weights(w_cmp_v1[l], pe_cmp_v[l], w_cmp_v2[l])
        wn, wf, wo = w_nsa_o[l].astype(BF16), w_fox_o[l].astype(BF16), w_out[l].astype(BF16)
        wu, wd = w_up[l].astype(BF16), w_down[l].astype(BF16)
        ffn_w = (g_pre_ffn[l][None, :], wu, wd, w_conv[l], b_conv[l][None, :], g_post_ffn[l][None, :])

        (qn, kvc, kvs, kvw, qf, kvf, gm, sm, kvs_b, kvw_b, kvf_b) = _project(y_p, g1, cos_p, sin_p, w_main, w_small, bf_row)
        w1cat = jnp.stack([cmp_k[0], cmp_v[0]])
        pe8 = jnp.stack([cmp_k[1], cmp_v[1]])
        w2 = jnp.stack([cmp_k[2], cmp_v[2]])
        ckv = _compress_prompt(kvc, b_p, t_p, w1cat, pe8, w2)
        frow = _fcum_prompt(sm, b_p, t_p)
        o_n = _nsa_prompt(qn, sm, ckv, kvs_b, kvw_b, b_p, t_p)
        o_fx = _fox_prompt(qf, kvf_b, frow, b_p, t_p)
        y1 = _postmix(o_n, o_fx, gm, y_p, wn, wf, wo, g_post_mix[l][None, :])
        y_p, gt = _ffn(y1, *ffn_w, seq_len=t_p)
        n_win = min(WINDOW, t_p)
        outs['cmp_p'].append(kvc.reshape(b_p, t_p, 2, g_n, HEAD_DIM))
        outs['sel_p'].append(kvs.reshape(b_p, t_p, 2, g_n, HEAD_DIM))
        outs['win_p'].append(kvw.reshape(b_p, t_p, 2, g_n, HEAD_DIM)[:, t_p - n_win:])
        outs['fox_p'].append(kvf.reshape(b_p, t_p, 2, n_h, HEAD_DIM))
        outs['lf_p'].append(sm[:, LOGF_COL0:LOGF_COL0 + n_h].reshape(b_p, t_p, n_h))
        tiles_per_seq = t_p // FFN_TM
        gt = gt.reshape(b_p, tiles_per_seq, 8, -1)
        outs['conv_p'].append(gt[:, -1, 8 - (CONV_WIDTH - 1):])

        (qn, kvc, kvs, kvw, qf, kvf, gm, sm, _, _, _) = _project(y_s, g1, cos_s, sin_s, w_main, w_small, bf_row)
        win_buf = cache_nsa_win_kv[l]
        o_n = _nsa_sample(page_table, cache_nsa_cmp_kv[l], cache_nsa_sel_kv[l], win_buf, kvs, kvw, qn, sm,
                          cmp_k + cmp_v, t_s)
        o_fx = _fox_sample(page_table, cache_fox_kv[l], cache_fox_logf[l], kvf, qf, sm, t_s)
        o_n = o_n[:, :t_s].reshape(b_s * t_s, -1)
        o_fx = o_fx[:, :t_s].reshape(b_s * t_s, -1)
        y1 = _postmix(o_n, o_fx, gm, y_s, wn, wf, wo, g_post_mix[l][None, :])
        y_s, gt = _ffn(y1, *ffn_w, state=state_ffn_conv[l])
        kw_new = kvw.reshape(b_s, t_s, 2, g_n, HEAD_DIM)
        n_win = min(WINDOW, win_buf.shape[1] + t_s)
        outs['cmp_s'].append(kvc.reshape(b_s, t_s, 2, g_n, HEAD_DIM))
        outs['sel_s'].append(kvs.reshape(b_s, t_s, 2, g_n, HEAD_DIM))
        outs['win_s'].append(jnp.concatenate([win_buf, kw_new], axis=1)[:, -n_win:])
        outs['fox_s'].append(kvf.reshape(b_s, t_s, 2, n_h, HEAD_DIM))
        outs['lf_s'].append(sm[:, LOGF_COL0:LOGF_COL0 + n_h].reshape(b_s, t_s, n_h))
        gfull = jnp.concatenate([state_ffn_conv[l], gt.reshape(b_s, t_s, -1)], axis=1)
        outs['conv_s'].append(gfull[:, t_s:])

    st = {k: jnp.stack(v) for k, v in outs.items()}
    return (y_p.reshape(b_p, t_p, d), y_s.reshape(b_s, t_s, d),
            st['cmp_p'], st['cmp_s'], st['sel_p'], st['sel_s'], st['win_p'], st['win_s'],
            st['fox_p'], st['fox_s'], st['lf_p'], st['lf_s'], st['conv_p'], st['conv_s'])
```

```python
import functools

import numpy as np
import jax
import jax.numpy as jnp
from jax import lax
from jax.experimental import pallas as pl
from jax.experimental.pallas import tpu as pltpu

F32 = jnp.float32
BF16 = jnp.bfloat16

HEAD_DIM = 128
N_NSA_HEADS = 8
N_NSA_GROUPS = 2
NSA_HPG = N_NSA_HEADS // N_NSA_GROUPS
N_FOX_HEADS = 8
CMP_BLOCK = 32
CMP_STRIDE = 16
CMP_RATIO = CMP_BLOCK // CMP_STRIDE
SEL_BLOCK = 64
SEL_TOPK = 16
N_LOCAL_BLOCKS = 2
WINDOW = 512
CONV_WIDTH = 3
ROPE_THETA = 10000.0
RMS_EPS = 1e-6
FORCE_BONUS = 1e4
NEG_INF = -1e30
QK_SCALE = HEAD_DIM ** -0.5

N_GATE_COLS = N_NSA_HEADS * 3
LOGF_COL0 = N_GATE_COLS
LANES = 128
VMEM_LIMIT = 56 * 1024 * 1024

PROJ_TN = 512
KV_W = 2 * N_NSA_GROUPS * HEAD_DIM
NSA_Q_W = N_NSA_HEADS * HEAD_DIM
FOX_W = N_FOX_HEADS * HEAD_DIM


def _cparams(sem):
    return pltpu.CompilerParams(dimension_semantics=sem, vmem_limit_bytes=VMEM_LIMIT)


def _dot(a, b):
    return jnp.dot(a, b, preferred_element_type=F32)


def _dot_nt(a, b):
    return lax.dot_general(a, b, (((1,), (1,)), ((), ())), preferred_element_type=F32)


def _rms(x, g):
    return x * lax.rsqrt(jnp.mean(x * x, axis=-1, keepdims=True) + RMS_EPS) * g


def _masked_softmax(s, mask):
    sm = jnp.where(mask, s, NEG_INF)
    m = jnp.max(sm, axis=-1, keepdims=True)
    e = jnp.where(mask, jnp.exp(sm - m), 0.0)
    l = jnp.sum(e, axis=-1, keepdims=True)
    return e / jnp.where(l > 0.0, l, 1.0)


def _masked_exp(s, mask):
    sm = jnp.where(mask, s, NEG_INF)
    e = jnp.where(mask, jnp.exp(sm - jnp.max(sm, axis=-1, keepdims=True)), 0.0)
    l = jnp.sum(e, axis=-1, keepdims=True)
    return e, jnp.where(l > 0.0, l, 1.0)


def _lane_tile_max(mx, s):
    for c in range(s.shape[1] // LANES):
        mx = jnp.maximum(mx, s[:, c * LANES:(c + 1) * LANES])
    return mx


def _exp_accumulate(carry, s, m, v, lane_shift=0):
    ls, acc = carry
    p = jnp.exp(s - m)
    tiles = [p[:, c * LANES:(c + 1) * LANES] for c in range(s.shape[1] // LANES)]
    for t in tiles:
        ls = ls + t
    if lane_shift:
        p = jnp.concatenate([pltpu.roll(t, lane_shift, axis=1) for t in tiles], axis=1)
    return ls, acc + _dot(p.astype(BF16), v)


def _split3(x):
    hi = x.astype(BF16)
    r = x - hi.astype(F32)
    mid = r.astype(BF16)
    lo = (r - mid.astype(F32)).astype(BF16)
    return hi, mid, lo


def _topk_mask(score, k, n_sel):
    st = score.T
    nv = -(-n_sel // 8)
    slabs = [st[8 * v:8 * v + 8, :] for v in range(nv)]
    sub = lax.broadcasted_iota(jnp.int32, (8, LANES), 0)
    ranks = [jnp.zeros((8, LANES), F32) for _ in range(nv)]
    for b2 in range(n_sel):
        row = jnp.broadcast_to(st[b2:b2 + 1, :], (8, LANES))
        for v in range(nv):
            if b2 < 8 * v:
                beats = row >= slabs[v]
            elif b2 >= 8 * v + 8:
                beats = row > slabs[v]
            else:
                beats = (row > slabs[v]) | ((row == slabs[v]) & (sub > b2 - 8 * v))
            ranks[v] = ranks[v] + jnp.where(beats, 1.0, 0.0)
    sel = [jnp.where((ranks[v] < k) & (sub + 8 * v < n_sel), 1.0, 0.0) for v in range(nv)]
    sel_t = jnp.concatenate(sel + [jnp.zeros((LANES - 8 * nv, LANES), F32)], axis=0)
    return sel_t.T


def _sel_scores(imp, tpos, n_sel):
    bidx = lax.broadcasted_iota(jnp.int32, imp.shape, 1)
    cur = jnp.right_shift(tpos, 6)
    valid = bidx <= cur
    forced = (bidx == 0) | (valid & (bidx > cur - N_LOCAL_BLOCKS))
    score = jnp.where(valid, jnp.where(forced, imp + FORCE_BONUS, imp), NEG_INF)
    return jnp.where(bidx < n_sel, score, -jnp.inf)


def _proj_kernel(x_ref, g_ref, cos_ref, sin_ref, w_ref, ws_ref, bf_ref,
                 qn_ref, kvc_ref, kvs_ref, kvw_ref, qf_ref, kvf_ref, gm_ref, sm_ref,
                 kvsb_ref, kvwb_ref, kvfb_ref, h_scr):
    j = pl.program_id(1)

    @pl.when(j == 0)
    def _():
        x = x_ref[...]
        y = x * lax.rsqrt(jnp.mean(x * x, axis=-1, keepdims=True) + RMS_EPS)
        h = (y * g_ref[...]).astype(BF16)
        h_scr[...] = h
        s = _dot(h, ws_ref[...])
        lane = lax.broadcasted_iota(jnp.int32, s.shape, 1)
        z = s + bf_ref[...]
        lf = jnp.minimum(z, 0.0) - jnp.log1p(jnp.exp(-jnp.abs(z)))
        sm_ref[...] = jnp.where(lane < N_GATE_COLS, jax.nn.sigmoid(s),
                                jnp.where(lane < LOGF_COL0 + N_FOX_HEADS, lf, 0.0))

    cos = cos_ref[...]
    sin = sin_ref[...]
    half_w = PROJ_TN // 2
    halves = [slice(0, half_w), slice(half_w, PROJ_TN)]

    def mm(cols):
        return _dot(h_scr[...], w_ref[:, cols])

    def rope2(a):
        return jnp.concatenate(
            [a[:, k * HEAD_DIM:(k + 1) * HEAD_DIM] * cos
             + pltpu.roll(a[:, k * HEAD_DIM:(k + 1) * HEAD_DIM], HEAD_DIM // 2, axis=1) * sin
             for k in range(half_w // HEAD_DIM)], axis=1)

    def kv_rows(ref, bref):
        assert half_w == N_NSA_GROUPS * HEAD_DIM
        for cols, is_key in zip(halves, (True, False)):
            a = mm(cols)
            a = rope2(a) if is_key else a
            ref[:, cols] = a
            if bref is not None:
                bref[:, cols] = a.astype(BF16)

    @pl.when(j < 2)
    def _():
        for cols in halves:
            qn_ref[:, cols] = (rope2(mm(cols)) * QK_SCALE).astype(BF16)

    @pl.when(j == 2)
    def _():
        kv_rows(kvc_ref, None)

    @pl.when(j == 3)
    def _():
        kv_rows(kvs_ref, kvsb_ref)

    @pl.when(j == 4)
    def _():
        kv_rows(kvw_ref, kvwb_ref)

    @pl.when((j >= 5) & (j < 7))
    def _():
        for cols in halves:
            qf_ref[:, cols] = (mm(cols) * QK_SCALE).astype(BF16)

    @pl.when((j >= 7) & (j < 11))
    def _():
        for cols in halves:
            a = mm(cols)
            kvf_ref[:, cols] = a
            kvfb_ref[:, cols] = a.astype(BF16)

    @pl.when(j >= 11)
    def _():
        for cols in halves:
            gm_ref[:, cols] = jax.nn.sigmoid(mm(cols))


def _project(x2, g, cos2, sin2, w_main, w_small, bf_row):
    n, d = x2.shape
    tm = 512
    n_j = w_main.shape[1] // PROJ_TN
    tn = PROJ_TN

    def clip(lo, hi):
        return lambda i, j: (i, jnp.clip(j - lo, 0, hi - lo))

    row = lambda i, j: (i, 0)
    out_shape = (
        jax.ShapeDtypeStruct((n, NSA_Q_W), BF16),
        jax.ShapeDtypeStruct((n, KV_W), F32),
        jax.ShapeDtypeStruct((n, KV_W), F32),
        jax.ShapeDtypeStruct((n, KV_W), F32),
        jax.ShapeDtypeStruct((n, FOX_W), BF16),
        jax.ShapeDtypeStruct((n, 2 * FOX_W), F32),
        jax.ShapeDtypeStruct((n, 2 * d), F32),
        jax.ShapeDtypeStruct((n, LANES), F32),
        jax.ShapeDtypeStruct((n, KV_W), BF16),
        jax.ShapeDtypeStruct((n, KV_W), BF16),
        jax.ShapeDtypeStruct((n, 2 * FOX_W), BF16),
    )
    out_specs = (
        pl.BlockSpec((tm, tn), clip(0, 1)),
        pl.BlockSpec((tm, tn), row),
        pl.BlockSpec((tm, tn), row),
        pl.BlockSpec((tm, tn), row),
        pl.BlockSpec((tm, tn), clip(5, 6)),
        pl.BlockSpec((tm, tn), clip(7, 10)),
        pl.BlockSpec((tm, tn), clip(11, 18)),
        pl.BlockSpec((tm, LANES), row),
        pl.BlockSpec((tm, tn), row),
        pl.BlockSpec((tm, tn), row),
        pl.BlockSpec((tm, tn), clip(7, 10)),
    )
    in_specs = [
        pl.BlockSpec((tm, d), row),
        pl.BlockSpec((1, d), lambda i, j: (0, 0)),
        pl.BlockSpec((tm, LANES), row),
        pl.BlockSpec((tm, LANES), row),
        pl.BlockSpec((d, tn), lambda i, j: (0, j)),
        pl.BlockSpec((d, LANES), lambda i, j: (0, 0)),
        pl.BlockSpec((1, LANES), lambda i, j: (0, 0)),
    ]
    return pl.pallas_call(
        _proj_kernel,
        grid=(n // tm, n_j),
        in_specs=in_specs,
        out_specs=out_specs,
        out_shape=out_shape,
        scratch_shapes=[pltpu.VMEM((tm, d), BF16)],
        compiler_params=_cparams(("arbitrary", "arbitrary")),
        name="proj",
    )(x2, g, cos2, sin2, w_main, w_small, bf_row)


def _compress_tail(xc, w1, pe8, w2):
    n = xc.shape[0]
    part = _dot(xc, w1)
    pp = _dot(pe8.astype(BF16), w1)
    pe_term = pp[0:1, :HEAD_DIM] + pp[1:2, HEAD_DIM:]
    hid = pe_term + part[:, :HEAD_DIM] + pltpu.roll(part[:, HEAD_DIM:], n - 1, axis=0)
    return _dot(jax.nn.gelu(hid, approximate=True).astype(BF16), w2)


def _cmp_prompt_kernel(x_ref, w1_ref, pe_ref, w2_ref, o_ref):
    n = x_ref.shape[0] // CMP_STRIDE
    xc = jnp.concatenate([x_ref[pl.ds(i, n, stride=CMP_STRIDE), :] for i in range(CMP_STRIDE)],
                         axis=1).astype(BF16)
    o_ref[0, 0] = _compress_tail(xc, w1_ref[0], pe_ref[0], w2_ref[0]).astype(BF16)


def _compress_prompt(kvc, b_n, t_n, w1cat, pe8, w2):
    n = t_n // CMP_STRIDE
    return pl.pallas_call(
        _cmp_prompt_kernel,
        grid=(b_n, 2 * N_NSA_GROUPS),
        in_specs=[
            pl.BlockSpec((t_n, HEAD_DIM), lambda b, s: (b, s)),
            pl.BlockSpec((1, CMP_STRIDE * HEAD_DIM, 2 * HEAD_DIM), lambda b, s: (s // N_NSA_GROUPS, 0, 0)),
            pl.BlockSpec((1, 8, CMP_STRIDE * HEAD_DIM), lambda b, s: (s // N_NSA_GROUPS, 0, 0)),
            pl.BlockSpec((1, HEAD_DIM, HEAD_DIM), lambda b, s: (s // N_NSA_GROUPS, 0, 0)),
        ],
        out_specs=pl.BlockSpec((1, 1, n, HEAD_DIM), lambda b, s: (b, s, 0, 0)),
        out_shape=jax.ShapeDtypeStruct((b_n, 2 * N_NSA_GROUPS, n, HEAD_DIM), BF16),
        compiler_params=_cparams(("arbitrary", "arbitrary")),
        name="cmp_prompt",
    )(kvc, w1cat, pe8, w2)


def _fcum_kernel(x_ref, frow_ref, carry_scr):
    i = pl.program_id(1)

    @pl.when(i == 0)
    def _():
        carry_scr[...] = jnp.zeros_like(carry_scr)

    x = x_ref[...]
    tb = x.shape[0]
    r = lax.broadcasted_iota(jnp.int32, (tb, tb), 0)
    c = lax.broadcasted_iota(jnp.int32, (tb, tb), 1)
    tri = jnp.where(r >= c, 1.0, 0.0).astype(BF16)
    hi, mid, lo = _split3(x)
    cs = _dot(tri, hi) + _dot(tri, mid) + _dot(tri, lo) + carry_scr[0:1, :]
    carry_scr[...] = jnp.broadcast_to(cs[tb - 1:tb, :], carry_scr.shape)
    frow_ref[0] = cs.T[LOGF_COL0:LOGF_COL0 + N_FOX_HEADS, :]


def _fcum_prompt(sm, b_n, t_n):
    tb = 512
    nb = t_n // tb
    return pl.pallas_call(
        _fcum_kernel,
        grid=(b_n, nb),
        in_specs=[pl.BlockSpec((tb, LANES), lambda b, i: (b * nb + i, 0))],
        out_specs=pl.BlockSpec((1, N_FOX_HEADS, tb), lambda b, i: (b, 0, i)),
        out_shape=jax.ShapeDtypeStruct((b_n, N_FOX_HEADS, t_n), F32),
        scratch_shapes=[pltpu.VMEM((8, LANES), F32)],
        compiler_params=_cparams(("arbitrary", "arbitrary")),
        name="fcum_prompt",
    )(sm)


NSA_TQ = 128
NSA_TK = 256
NSA_NSUB = 2


def _tile_rows(a, reps):
    return jnp.concatenate([a] * reps, axis=0)


def _nsa_prompt_kernel(q_ref, sm_ref, ck_ref, ks_ref, kw_ref, cover_ref, expand_ref, o_ref, s_scr, *, n_cmp, n_sel):
    i = pl.program_id(1)
    tq, nsub = NSA_TQ, NSA_NSUB
    assert nsub * tq == NSA_TK
    t0 = i * (tq * nsub)
    n_ck = ck_ref.shape[2]
    band = WINDOW + tq
    tpos = [t0 + u * tq + lax.broadcasted_iota(jnp.int32, (tq, 1), 0) for u in range(nsub)]

    for g in range(N_NSA_GROUPS):
        ck = ck_ref[0, g]
        cv = ck_ref[0, N_NSA_GROUPS + g]
        o_cs, o_ws, selms = [], [], []
        for u in range(nsub):
            q = jnp.concatenate([q_ref[u * tq:(u + 1) * tq, (g * NSA_HPG + h) * HEAD_DIM:(g * NSA_HPG + h + 1) * HEAD_DIM]
                                 for h in range(NSA_HPG)], axis=0)
            s_c = _dot_nt(q, ck)
            cidx = lax.broadcasted_iota(jnp.int32, (tq, n_ck), 1)
            mc = jnp.where((cidx * CMP_STRIDE + CMP_BLOCK - 1 <= tpos[u]) & (cidx < n_cmp), 1.0, 0.0)
            e_c, l_c = _masked_exp(s_c, _tile_rows(mc, NSA_HPG) > 0.5)
            e_cb = e_c.astype(BF16)
            imp4 = _dot(e_cb, cover_ref[...]) / l_c
            imp = imp4[0:tq] + imp4[tq:2 * tq] + imp4[2 * tq:3 * tq] + imp4[3 * tq:4 * tq]
            selms.append(_topk_mask(_sel_scores(imp, tpos[u], n_sel), min(SEL_TOPK, n_sel), n_sel).astype(BF16))
            o_cs.append(_dot(e_cb, cv) / l_c)
            w0 = pl.multiple_of(jnp.maximum(t0 + u * tq - WINDOW, 0), tq)
            kw = kw_ref[pl.ds(w0, band), g * HEAD_DIM:(g + 1) * HEAD_DIM]
            vw = kw_ref[pl.ds(w0, band), (N_NSA_GROUPS + g) * HEAD_DIM:(N_NSA_GROUPS + g + 1) * HEAD_DIM]
            s_w = _dot_nt(q, kw)
            wpos = w0 + lax.broadcasted_iota(jnp.int32, (tq, band), 1)
            bw = jnp.where((wpos <= tpos[u]) & (wpos > tpos[u] - WINDOW), 0.0, NEG_INF)
            s_w = s_w + _tile_rows(bw, NSA_HPG)
            e_w = jnp.exp(s_w - jnp.max(s_w, axis=-1, keepdims=True))
            o_ws.append(_dot(e_w.astype(BF16), vw) / jnp.sum(e_w, axis=-1, keepdims=True))

        tb = tq * nsub
        q_all = jnp.concatenate([q_ref[:, (g * NSA_HPG + h) * HEAD_DIM:(g * NSA_HPG + h + 1) * HEAD_DIM]
                                 for h in range(NSA_HPG)], axis=0)
        selm_all = jnp.concatenate(selms, axis=0)
        tpos_all = t0 + lax.broadcasted_iota(jnp.int32, (tb, 1), 0)

        def sel_scores(k0, causal, q_all=q_all, selm_all=selm_all, g=g):
            k = ks_ref[pl.ds(k0, NSA_TK), g * HEAD_DIM:(g + 1) * HEAD_DIM]
            mk = _dot(selm_all, expand_ref[:, pl.ds(k0, NSA_TK)]) > 0.5
            if causal:
                mk = mk & (k0 + lax.broadcasted_iota(jnp.int32, (tb, NSA_TK), 1) <= tpos_all)
            s = _dot_nt(q_all, k).reshape(NSA_HPG, tb, NSA_TK)
            return jnp.where(mk[None], s, NEG_INF).reshape(NSA_HPG * tb, NSA_TK)

        def p1(kt, mx):
            k0 = pl.multiple_of(kt * NSA_TK, NSA_TK)
            s = sel_scores(k0, False)
            s_scr[:, pl.ds(k0, NSA_TK)] = s
            return _lane_tile_max(mx, s)

        n_full = t0 // NSA_TK
        mx = lax.fori_loop(0, n_full, p1, jnp.full((NSA_HPG * tb, LANES), -jnp.inf, F32))
        k_diag = pl.multiple_of(t0, NSA_TK)
        s = sel_scores(k_diag, True)
        s_scr[:, pl.ds(k_diag, NSA_TK)] = s
        m = jnp.max(_lane_tile_max(mx, s), axis=-1, keepdims=True)

        def p2(kt, carry, m=m, g=g):
            k0 = pl.multiple_of(kt * NSA_TK, NSA_TK)
            v = ks_ref[pl.ds(k0, NSA_TK), (N_NSA_GROUPS + g) * HEAD_DIM:(N_NSA_GROUPS + g + 1) * HEAD_DIM]
            return _exp_accumulate(carry, s_scr[:, pl.ds(k0, NSA_TK)], m, v)

        zeros = jnp.zeros((NSA_HPG * tb, LANES), F32)
        ls, acc = lax.fori_loop(0, n_full + 1, p2, (zeros, zeros))
        o_s = acc / jnp.sum(ls, axis=-1, keepdims=True)

        for u in range(nsub):
            gates = sm_ref[u * tq:(u + 1) * tq, :]
            for h in range(NSA_HPG):
                hh = g * NSA_HPG + h
                rows = slice(h * tq, (h + 1) * tq)
                rows_s = slice(h * tb + u * tq, h * tb + (u + 1) * tq)
                out = (gates[:, 3 * hh:3 * hh + 1] * o_cs[u][rows] + gates[:, 3 * hh + 1:3 * hh + 2] * o_s[rows_s]
                       + gates[:, 3 * hh + 2:3 * hh + 3] * o_ws[u][rows])
                o_ref[u * tq:(u + 1) * tq, hh * HEAD_DIM:(hh + 1) * HEAD_DIM] = out.astype(BF16)


def _cover_matrix(n_rows, n_cmp, n_sel):
    c = np.arange(n_rows)[:, None] * CMP_STRIDE
    b = np.arange(LANES)[None, :] * SEL_BLOCK
    m = (c < b + SEL_BLOCK) & (c + CMP_BLOCK > b) & (np.arange(n_rows)[:, None] < n_cmp) & (np.arange(LANES)[None, :] < n_sel)
    return jnp.asarray(m.astype(np.float32), dtype=BF16)


def _expand_matrix(n_keys, rows_per_key=1):
    m = (np.arange(n_keys * rows_per_key)[None, :] // (rows_per_key * SEL_BLOCK)) == np.arange(LANES)[:, None]
    return jnp.asarray(m.astype(np.float32), dtype=BF16)


def _nsa_prompt(qn, sm, ckv, kvs_b, kvw_b, b_n, t_n):
    blk = NSA_TQ * NSA_NSUB
    nq = t_n // blk
    n_cmp = t_n // CMP_STRIDE - CMP_RATIO + 1
    n_sel = t_n // SEL_BLOCK
    n_ck = ckv.shape[2]
    cover = _cover_matrix(n_ck, n_cmp, n_sel)
    expand = _expand_matrix(t_n)
    kern = functools.partial(_nsa_prompt_kernel, n_cmp=n_cmp, n_sel=n_sel)
    return pl.pallas_call(
        kern,
        grid=(b_n, nq),
        in_specs=[
            pl.BlockSpec((blk, NSA_Q_W), lambda b, i: (b * nq + i, 0)),
            pl.BlockSpec((blk, LANES), lambda b, i: (b * nq + i, 0)),
            pl.BlockSpec((1, 2 * N_NSA_GROUPS, n_ck, HEAD_DIM), lambda b, i: (b, 0, 0, 0)),
            pl.BlockSpec((t_n, KV_W), lambda b, i: (b, 0)),
            pl.BlockSpec((t_n, KV_W), lambda b, i: (b, 0)),
            pl.BlockSpec((n_ck, LANES), lambda b, i: (0, 0)),
            pl.BlockSpec((LANES, t_n), lambda b, i: (0, 0)),
        ],
        out_specs=pl.BlockSpec((blk, NSA_Q_W), lambda b, i: (b * nq + i, 0)),
        out_shape=jax.ShapeDtypeStruct((b_n * t_n, NSA_Q_W), BF16),
        scratch_shapes=[pltpu.VMEM((NSA_HPG * blk, t_n), F32)],
        compiler_params=_cparams(("arbitrary", "arbitrary")),
        name="nsa_prompt",
    )(qn, sm, ckv, kvs_b, kvw_b, cover, expand)


FOX_BLK = 512


def _fox_prompt_kernel(q_ref, k_ref, v_ref, frow_ref, o_ref, s_scr):
    h = pl.program_id(1)
    blk = FOX_BLK
    assert q_ref.shape[0] % blk == 0

    def q_block(qb, _):
        t0 = pl.multiple_of(qb * blk, blk)
        q = q_ref[pl.ds(t0, blk), :]

        def scores(k0):
            return _dot_nt(q, k_ref[pl.ds(k0, blk), :]) - frow_ref[0, pl.ds(h, 1), pl.ds(k0, blk)]

        def p1(kt, mx):
            k0 = pl.multiple_of(kt * blk, blk)
            s = scores(k0)
            s_scr[:, pl.ds(k0, blk)] = s
            return _lane_tile_max(mx, s)

        n_full = t0 // blk
        mx = lax.fori_loop(0, n_full, p1, jnp.full((blk, LANES), -jnp.inf, F32))
        causal = (t0 + lax.broadcasted_iota(jnp.int32, (blk, blk), 1)
                  <= t0 + lax.broadcasted_iota(jnp.int32, (blk, 1), 0))
        s = jnp.where(causal, scores(t0), NEG_INF)
        s_scr[:, pl.ds(t0, blk)] = s
        m = jnp.max(_lane_tile_max(mx, s), axis=-1, keepdims=True)

        def p2(kt, carry):
            k0 = pl.multiple_of(kt * blk, blk)
            return _exp_accumulate(carry, s_scr[:, pl.ds(k0, blk)], m, v_ref[pl.ds(k0, blk), :])

        zeros = jnp.zeros((blk, LANES), F32)
        ls, acc = lax.fori_loop(0, n_full + 1, p2, (zeros, zeros))
        o_ref[pl.ds(t0, blk), :] = (acc / jnp.sum(ls, axis=-1, keepdims=True)).astype(BF16)
        return 0

    lax.fori_loop(0, q_ref.shape[0] // blk, q_block, 0)


def _fox_prompt(qf, kvf_b, frow, b_n, t_n):
    nh = N_FOX_HEADS
    return pl.pallas_call(
        _fox_prompt_kernel,
        grid=(b_n, nh),
        in_specs=[
            pl.BlockSpec((t_n, HEAD_DIM), lambda b, h: (b, h)),
            pl.BlockSpec((t_n, HEAD_DIM), lambda b, h: (b, h)),
            pl.BlockSpec((t_n, HEAD_DIM), lambda b, h: (b, nh + h)),
            pl.BlockSpec((1, nh, t_n), lambda b, h: (b, 0, 0)),
        ],
        out_specs=pl.BlockSpec((t_n, HEAD_DIM), lambda b, h: (b, h)),
        out_shape=jax.ShapeDtypeStruct((b_n * t_n, FOX_W), BF16),
        scratch_shapes=[pltpu.VMEM((FOX_BLK, t_n), F32)],
        compiler_params=_cparams(("arbitrary", "arbitrary")),
        name="fox_prompt",
    )(qf, kvf_b, kvf_b, frow)


FOX_SAMPLE_CH = 512
TOK_PAD = 8


def _nsa_sample_kernel(pt_ref, *refs, n_pages, page, n_buf, n_tok):
    cmp_pages = refs[:n_pages]
    sel_pages = refs[n_pages:2 * n_pages]
    (win_ref, kvs_new_ref, kvw_new_ref, q_ref, sm_ref, w1k_ref, w1v_ref, pek_ref, pev_ref,
     w2k_ref, w2v_ref, cover_ref, expand_ref, perm_ref, o_ref) = refs[2 * n_pages:]
    del pt_ref
    past = n_pages * page
    n_slab = 2 * N_NSA_GROUPS
    chunks_per_page = page // CMP_STRIDE
    n_chunk = n_pages * chunks_per_page
    n_cmp = (past + n_tok + CMP_STRIDE - 1) // CMP_STRIDE - CMP_RATIO + 1
    n_sel = (past + n_tok + SEL_BLOCK - 1) // SEL_BLOCK
    tp = TOK_PAD
    tpos = past + lax.broadcasted_iota(jnp.int32, (tp, 1), 0)
    gates = sm_ref[0]
    kvs_new = kvs_new_ref[0]
    kvw_new = kvw_new_ref[0]

    assert n_pages % 2 == 0
    perm = []
    for p in range(0, n_pages, 2):
        both = _dot(perm_ref[...], jnp.concatenate([cmp_pages[p][...].astype(BF16),
                                                    cmp_pages[p + 1][...].astype(BF16)], axis=1))
        perm += [both[:, :HEAD_DIM], both[:, HEAD_DIM:]]

    def compress(slab, w1_ref, pe_ref, w2_ref):
        cols = []
        for i in range(CMP_STRIDE):
            r0 = (slab * CMP_STRIDE + i) * chunks_per_page
            cols.append(jnp.concatenate([perm[p][r0:r0 + chunks_per_page, :] for p in range(n_pages)], axis=0))
        xc = jnp.concatenate(cols, axis=1).astype(BF16)
        return _compress_tail(xc, w1_ref[...], pe_ref[...], w2_ref[...]).astype(BF16)

    o_cs, scores = [], []
    for g in range(N_NSA_GROUPS):
        q = jnp.concatenate([q_ref[0, :, (g * NSA_HPG + h) * HEAD_DIM:(g * NSA_HPG + h + 1) * HEAD_DIM]
                             for h in range(NSA_HPG)], axis=0)
        ck = compress(g, w1k_ref, pek_ref, w2k_ref)
        cv = compress(N_NSA_GROUPS + g, w1v_ref, pev_ref, w2v_ref)
        s_c = _dot_nt(q, ck)
        cidx = lax.broadcasted_iota(jnp.int32, (tp, n_chunk), 1)
        mc = jnp.where((cidx * CMP_STRIDE + CMP_BLOCK - 1 <= tpos) & (cidx < n_cmp), 1.0, 0.0)
        p_c = _masked_softmax(s_c, _tile_rows(mc, NSA_HPG) > 0.5)
        p_cb = p_c.astype(BF16)
        imp4 = _dot(p_cb, cover_ref[...])
        imp = imp4[0:tp] + imp4[tp:2 * tp] + imp4[2 * tp:3 * tp] + imp4[3 * tp:4 * tp]
        o_cs.append(_dot(p_cb, cv))
        scores.append(_sel_scores(imp, tpos, n_sel))

    score_all = jnp.concatenate(scores + [jnp.zeros((LANES - N_NSA_GROUPS * tp, LANES), F32)], axis=0)
    selm_all = _topk_mask(score_all, min(SEL_TOPK, n_sel), n_sel)

    n_hq = N_NSA_HEADS * tp
    hq_rows = NSA_HPG * tp
    ch = page * n_slab
    assert n_buf * n_slab % ch == 0 and (past % SEL_BLOCK) + n_tok <= SEL_BLOCK
    q_all = jnp.concatenate([q_ref[0, :, hh * HEAD_DIM:(hh + 1) * HEAD_DIM] for hh in range(N_NSA_HEADS)], axis=0)
    sel_q = jnp.concatenate([selm_all[(hh // NSA_HPG) * tp:(hh // NSA_HPG + 1) * tp]
                             for hh in range(N_NSA_HEADS)], axis=0)
    sel_qb = sel_q.astype(BF16)
    qrow = lax.broadcasted_iota(jnp.int32, (n_hq, 1), 0)
    assert hq_rows & (hq_rows - 1) == 0 and tp & (tp - 1) == 0 and n_slab & (n_slab - 1) == 0
    g_row = jnp.right_shift(qrow, hq_rows.bit_length() - 1)
    t_row = qrow & (tp - 1)
    tpos_q = past + t_row
    lane = lax.broadcasted_iota(jnp.int32, (n_hq, ch), 1)
    slab_ok = (lane & (n_slab - 1)) == g_row
    lane_n = lax.broadcasted_iota(jnp.int32, (n_hq, LANES), 1)
    new_ok = ((jnp.right_shift(lane_n, tp.bit_length() - 1) == g_row) & ((lane_n & (tp - 1)) <= t_row)
              & ((lane_n & (tp - 1)) < n_tok))
    pad_new = jnp.zeros((LANES - n_slab * tp, HEAD_DIM), F32)

    def new_rows(new):
        return jnp.concatenate([new[:, s * HEAD_DIM:(s + 1) * HEAD_DIM] for s in range(n_slab)] + [pad_new],
                               axis=0).astype(BF16)

    def attend(parts):
        mx = jnp.full((n_hq, LANES), -jnp.inf, F32)
        for s, _, _ in parts:
            mx = _lane_tile_max(mx, s)
        m = jnp.max(mx, axis=-1, keepdims=True)
        carry = (jnp.zeros((n_hq, LANES), F32), jnp.zeros((n_hq, HEAD_DIM), F32))
        for s, v, shift in parts:
            carry = _exp_accumulate(carry, s, m, v, lane_shift=shift)
        return carry[1] / jnp.sum(carry[0], axis=-1, keepdims=True)

    parts = []
    for p in range(n_pages):
        kb = sel_pages[p][...].astype(BF16)
        picked = _dot(sel_qb, expand_ref[:, p * ch:(p + 1) * ch]) > 0.5
        parts.append((jnp.where(picked & slab_ok, _dot_nt(q_all, kb), NEG_INF), kb, N_NSA_GROUPS))
    kn = new_rows(kvs_new)
    blk_new = past // SEL_BLOCK
    picked_new = sel_q[:, blk_new:blk_new + 1] > 0.5
    parts.append((jnp.where(new_ok & picked_new, _dot_nt(q_all, kn), NEG_INF), kn, N_NSA_GROUPS * tp))
    o_s = attend(parts)

    parts = []
    for c in range(n_buf * n_slab // ch):
        kb = win_ref[c * ch:(c + 1) * ch, :].astype(BF16)
        wpos = past - n_buf + c * (ch // n_slab) + jnp.right_shift(lane, n_slab.bit_length() - 1)
        ok = slab_ok & (wpos <= tpos_q) & (wpos > tpos_q - WINDOW)
        parts.append((jnp.where(ok, _dot_nt(q_all, kb), NEG_INF), kb, N_NSA_GROUPS))
    kn = new_rows(kvw_new)
    parts.append((jnp.where(new_ok, _dot_nt(q_all, kn), NEG_INF), kn, N_NSA_GROUPS * tp))
    o_w = attend(parts)

    for hh in range(N_NSA_HEADS):
        g, h = divmod(hh, NSA_HPG)
        rows = slice(hh * tp, (hh + 1) * tp)
        out = (gates[:, 3 * hh:3 * hh + 1] * o_cs[g][h * tp:(h + 1) * tp] + gates[:, 3 * hh + 1:3 * hh + 2] * o_s[rows]
               + gates[:, 3 * hh + 2:3 * hh + 3] * o_w[rows])
        o_ref[0, :, hh * HEAD_DIM:(hh + 1) * HEAD_DIM] = out.astype(BF16)


def _pad_tokens(a, n_seq, n_tok):
    a = a.reshape(n_seq, n_tok, a.shape[-1])
    return jnp.pad(a, ((0, 0), (0, TOK_PAD - n_tok), (0, 0)))


def _nsa_sample(page_table, cache_cmp, cache_sel, win_buf, kvs_new, kvw_new, qn, sm, cmp_w, n_tok):
    n_seq, n_pages = page_table.shape
    page = cache_cmp.shape[1]
    n_slab = 2 * N_NSA_GROUPS
    n_buf = win_buf.shape[1]
    past = n_pages * page
    cmp2 = cache_cmp.reshape(-1, HEAD_DIM)
    sel2 = cache_sel.reshape(-1, HEAD_DIM)
    win2 = win_buf.reshape(-1, HEAD_DIM)
    n_chunk = past // CMP_STRIDE
    n_cmp = (past + n_tok + CMP_STRIDE - 1) // CMP_STRIDE - CMP_RATIO + 1
    n_sel = (past + n_tok + SEL_BLOCK - 1) // SEL_BLOCK
    cover = _cover_matrix(n_chunk, min(n_cmp, n_chunk), n_sel)
    expand = _expand_matrix(past, n_slab)
    chunks = page // CMP_STRIDE
    src = np.arange(page * n_slab).reshape(chunks, CMP_STRIDE, n_slab).transpose(2, 1, 0).reshape(-1)
    perm = jnp.asarray((src[:, None] == np.arange(page * n_slab)[None, :]).astype(np.float32), dtype=BF16)
    w1k, pek, w2k, w1v, pev, w2v = cmp_w

    def page_spec(p):
        return pl.BlockSpec((page * n_slab, HEAD_DIM), lambda b, pt, p=p: (pt[b, p], 0))

    const2 = lambda b, pt: (0, 0)
    seq3 = lambda b, pt: (b, 0, 0)
    in_specs = ([page_spec(p) for p in range(n_pages)] + [page_spec(p) for p in range(n_pages)] + [
        pl.BlockSpec((n_buf * n_slab, HEAD_DIM), lambda b, pt: (b, 0)),
        pl.BlockSpec((1, TOK_PAD, KV_W), seq3),
        pl.BlockSpec((1, TOK_PAD, KV_W), seq3),
        pl.BlockSpec((1, TOK_PAD, NSA_Q_W), seq3),
        pl.BlockSpec((1, TOK_PAD, LANES), seq3),
        pl.BlockSpec(w1k.shape, const2),
        pl.BlockSpec(w1v.shape, const2),
        pl.BlockSpec(pek.shape, const2),
        pl.BlockSpec(pev.shape, const2),
        pl.BlockSpec(w2k.shape, const2),
        pl.BlockSpec(w2v.shape, const2),
        pl.BlockSpec(cover.shape, const2),
        pl.BlockSpec(expand.shape, const2),
        pl.BlockSpec(perm.shape, const2),
    ])
    kern = functools.partial(_nsa_sample_kernel, n_pages=n_pages, page=page, n_buf=n_buf, n_tok=n_tok)
    grid_spec = pltpu.PrefetchScalarGridSpec(
        num_scalar_prefetch=1, grid=(n_seq,), in_specs=in_specs,
        out_specs=pl.BlockSpec((1, TOK_PAD, NSA_Q_W), seq3))
    return pl.pallas_call(
        kern,
        grid_spec=grid_spec,
        out_shape=jax.ShapeDtypeStruct((n_seq, TOK_PAD, NSA_Q_W), BF16),
        compiler_params=_cparams(("arbitrary",)),
        name="nsa_sample",
    )(page_table, *([cmp2] * n_pages), *([sel2] * n_pages), win2,
      _pad_tokens(kvs_new, n_seq, n_tok), _pad_tokens(kvw_new, n_seq, n_tok),
      _pad_tokens(qn, n_seq, n_tok), _pad_tokens(sm, n_seq, n_tok),
      w1k, w1v, pek, pev, w2k, w2v, cover, expand, perm)


def _fox_sample_kernel(pt_ref, *refs, n_pages, page, n_tok):
    kv_pages = refs[:n_pages]
    lf_pages = refs[n_pages:2 * n_pages]
    k_new_ref, v_new_ref, q_ref, lfn_ref, o_ref, mask_scr, s_scr, kb_scr = refs[2 * n_pages:]
    del pt_ref
    nh = N_FOX_HEADS
    rows_pp = 2 * nh
    page_rows = page * rows_pp
    n_chunk = page_rows // LANES
    n_q = nh * n_tok
    assert rows_pp == 16 and LANES % rows_pp == 0 and n_q <= LANES

    @pl.when(pl.program_id(0) == 0)
    def _():
        qrow = lax.broadcasted_iota(jnp.int32, mask_scr.shape, 0)
        lane = lax.broadcasted_iota(jnp.int32, mask_scr.shape, 1)
        mask_scr[...] = jnp.where((lane & (rows_pp - 1)) == (qrow & (nh - 1)), 0.0, NEG_INF)

    x = jnp.concatenate([lf_pages[p][0] for p in range(n_pages)], axis=0)
    n_r = x.shape[0]
    la = lax.broadcasted_iota(jnp.int32, (LANES, LANES), 0)
    lb = lax.broadcasted_iota(jnp.int32, (LANES, LANES), 1)
    same = (la & (rows_pp - 1)) == (lb & (rows_pp - 1))
    u_in = jnp.where(same & (jnp.right_shift(la, 4) <= jnp.right_shift(lb, 4)), 1.0, 0.0).astype(BF16)
    u_all = jnp.where(same, 1.0, 0.0).astype(BF16)
    xh, xm, xl = _split3(x)
    within = _dot(xh, u_in) + _dot(xm, u_in) + _dot(xl, u_in)
    tot = _dot(xh, u_all) + _dot(xm, u_all) + _dot(xl, u_all)
    ra = lax.broadcasted_iota(jnp.int32, (n_r, n_r), 0)
    rb = lax.broadcasted_iota(jnp.int32, (n_r, n_r), 1)
    before = jnp.where(rb < ra, 1.0, 0.0).astype(BF16)
    th, tm_, tl = _split3(tot)
    offs = _dot(before, th) + _dot(before, tm_) + _dot(before, tl)
    f_end = offs[n_r - 1:n_r, :] + tot[n_r - 1:n_r, :]
    bias = f_end - (within + offs)

    q_all = q_ref[0]

    ch = mask_scr.shape[1]
    lt = ch // LANES
    steps = [(p, c) for p in range(n_pages) for c in range(page_rows // ch)]
    mx = jnp.full((n_q, LANES), -jnp.inf, F32)
    for p, c in steps:
        k_b = kv_pages[p][pl.ds(c * ch, ch), :].astype(BF16)
        kb_scr[p * page_rows + c * ch:p * page_rows + (c + 1) * ch, :] = k_b
        r0 = p * n_chunk + c * lt
        brow = jnp.concatenate([bias[r0 + i:r0 + i + 1, :] for i in range(lt)], axis=1)
        s = _dot_nt(q_all, k_b) + brow + mask_scr[...]
        s_scr[:, p * page_rows + c * ch:p * page_rows + (c + 1) * ch] = s
        mx = _lane_tile_max(mx, s)

    pad = jnp.zeros((LANES - n_q, HEAD_DIM), F32)
    k_new = jnp.concatenate([k_new_ref[0], pad], axis=0).astype(BF16)
    v_new = jnp.concatenate([v_new_ref[0], pad], axis=0).astype(BF16)
    g_in = jnp.where(((la & (nh - 1)) == (lb & (nh - 1))) & (la <= lb), 1.0, 0.0).astype(BF16)
    nh_, nm_, nl_ = _split3(lfn_ref[0])
    c_new = (_dot(nh_, g_in) + _dot(nm_, g_in) + _dot(nl_, g_in))[0:1, :]
    qrow = lax.broadcasted_iota(jnp.int32, (n_q, LANES), 0)
    lane = lax.broadcasted_iota(jnp.int32, (n_q, LANES), 1)
    ok = ((lane & (nh - 1)) == (qrow & (nh - 1))) & (lane <= qrow)
    s_new = jnp.where(ok, _dot_nt(q_all, k_new) - c_new, NEG_INF)
    m = jnp.max(jnp.maximum(mx, s_new), axis=-1, keepdims=True)

    carry = _exp_accumulate((jnp.zeros((n_q, LANES), F32), jnp.zeros((n_q, HEAD_DIM), F32)), s_new, m, v_new)
    for p, c in steps:
        rows = slice(p * page_rows + c * ch, p * page_rows + (c + 1) * ch)
        carry = _exp_accumulate(carry, s_scr[:, rows], m, kb_scr[rows, :], lane_shift=nh)
    ls, acc = carry
    o_ref[0] = (acc / jnp.sum(ls, axis=-1, keepdims=True)).astype(BF16)


def _fox_sample(page_table, cache_fox, cache_logf, kvf_new, qf, sm, n_tok):
    n_seq, n_pages = page_table.shape
    page = cache_fox.shape[1]
    nh = N_FOX_HEADS
    kv2 = cache_fox.reshape(-1, HEAD_DIM)
    rows_pp = 2 * nh
    n_chunk = page * rows_pp // LANES
    lf_c = jnp.pad(cache_logf, ((0, 0), (0, 0), (0, rows_pp - nh))).reshape(-1, n_chunk, LANES)
    n_q = n_tok * nh
    lfn = sm[:, LOGF_COL0:LOGF_COL0 + nh].reshape(n_seq, 1, n_q)
    lfn = jnp.pad(lfn, ((0, 0), (0, 7), (0, LANES - n_q)))
    kv_new = kvf_new.reshape(n_seq, n_tok, 2, nh, HEAD_DIM)
    k_new = kv_new[:, :, 0].reshape(n_seq, n_q, HEAD_DIM)
    v_new = kv_new[:, :, 1].reshape(n_seq, n_q, HEAD_DIM)
    q3 = qf.reshape(n_seq, n_q, HEAD_DIM)

    seq3 = lambda b, pt: (b, 0, 0)
    in_specs = ([pl.BlockSpec((page * rows_pp, HEAD_DIM), lambda b, pt, p=p: (pt[b, p], 0)) for p in range(n_pages)]
                + [pl.BlockSpec((1, n_chunk, LANES), lambda b, pt, p=p: (pt[b, p], 0, 0)) for p in range(n_pages)]
                + [pl.BlockSpec((1, n_q, HEAD_DIM), seq3),
                   pl.BlockSpec((1, n_q, HEAD_DIM), seq3),
                   pl.BlockSpec((1, n_q, HEAD_DIM), seq3),
                   pl.BlockSpec((1, 8, LANES), seq3)])
    kern = functools.partial(_fox_sample_kernel, n_pages=n_pages, page=page, n_tok=n_tok)
    n_rows = n_pages * page * rows_pp
    grid_spec = pltpu.PrefetchScalarGridSpec(
        num_scalar_prefetch=1, grid=(n_seq,), in_specs=in_specs,
        out_specs=pl.BlockSpec((1, n_q, HEAD_DIM), seq3),
        scratch_shapes=[pltpu.VMEM((n_q, FOX_SAMPLE_CH), F32),
                        pltpu.VMEM((n_q, n_rows), F32),
                        pltpu.VMEM((n_rows, HEAD_DIM), BF16)])
    out = pl.pallas_call(
        kern,
        grid_spec=grid_spec,
        out_shape=jax.ShapeDtypeStruct((n_seq, n_q, HEAD_DIM), BF16),
        compiler_params=_cparams(("arbitrary",)),
        name="fox_sample",
    )(page_table, *([kv2] * n_pages), *([lf_c] * n_pages), k_new, v_new, q3, lfn)
    return out.reshape(n_seq * n_tok, nh * HEAD_DIM)


def _postmix_kernel(on_ref, of_ref, gm0_ref, gm1_ref, x_ref, wn_ref, wf_ref, wo_ref, g_ref, y_ref):
    a = _dot(on_ref[...], wn_ref[...])
    b = _dot(of_ref[...], wf_ref[...])
    merged = gm0_ref[...] * a + gm1_ref[...] * b
    z = _dot(merged.astype(BF16), wo_ref[...])
    y_ref[...] = x_ref[...] + _rms(z, g_ref[...])


def _postmix(o_n, o_f, gm, x2, wn, wf, wo, g):
    n, d = x2.shape
    tm = 256
    row = lambda i: (i, 0)
    const = lambda i: (0, 0)
    return pl.pallas_call(
        _postmix_kernel,
        grid=(n // tm,),
        in_specs=[
            pl.BlockSpec((tm, NSA_Q_W), row),
            pl.BlockSpec((tm, FOX_W), row),
            pl.BlockSpec((tm, d), lambda i: (i, 0)),
            pl.BlockSpec((tm, d), lambda i: (i, 1)),
            pl.BlockSpec((tm, d), row),
            pl.BlockSpec(wn.shape, const),
            pl.BlockSpec(wf.shape, const),
            pl.BlockSpec(wo.shape, const),
            pl.BlockSpec((1, d), const),
        ],
        out_specs=pl.BlockSpec((tm, d), row),
        out_shape=jax.ShapeDtypeStruct((n, d), F32),
        compiler_params=_cparams(("arbitrary",)),
        name="postmix",
    )(o_n, o_f, gm, gm, x2, wn, wf, wo, g)


FFN_TM = 512
FFN_TF = 512
HALO = 16


def _ffn_kernel(*refs, seq_tiles, n_tok):
    if n_tok is None:
        (x_ref, xh_ref, g_ref, wg_ref, wu_ref, wd_ref, wc_ref, bc_ref, gp_ref,
         y_ref, gt_ref, h_scr, hh_scr, acc_scr) = refs
    else:
        (x_ref, s0_ref, s1_ref, g_ref, wg_ref, wu_ref, wd_ref, wc_ref, bc_ref, gp_ref,
         y_ref, gt_ref, h_scr, acc_scr) = refs
    i = pl.program_id(0)
    f = pl.program_id(1)
    tm = x_ref.shape[0]

    @pl.when(f == 0)
    def _():
        h_scr[...] = _rms(x_ref[...], g_ref[...]).astype(BF16)
        acc_scr[...] = jnp.zeros_like(acc_scr)
        if n_tok is None:
            hh_scr[...] = _rms(xh_ref[...], g_ref[...]).astype(BF16)

    h2 = h_scr[...]
    tf = wg_ref.shape[1]
    n_split = 2
    tfs = tf // n_split
    for c in range(n_split):
        cols = slice(c * tfs, (c + 1) * tfs)
        gate = _dot(h2, wg_ref[:, cols])
        up = _dot(h2, wu_ref[:, cols])
        row = lax.broadcasted_iota(jnp.int32, gate.shape, 0)
        r1 = pltpu.roll(gate, 1, axis=0)
        r2 = pltpu.roll(gate, 2, axis=0)
        if n_tok is None:
            first = (i % seq_tiles) == 0
            gh = jnp.where(first, 0.0, _dot(hh_scr[...], wg_ref[:, cols]))
            p1 = gh[HALO - 1:HALO, :]
            p2 = gh[HALO - 2:HALO - 1, :]
            g1 = jnp.where(row == 0, p1, r1)
            g2 = jnp.where(row == 0, p2, jnp.where(row == 1, p1, r2))
            gt_ref[:, cols] = gate[tm - 8:tm, :]
        else:
            assert n_tok & (n_tok - 1) == 0
            rt = row & (n_tok - 1)
            g1 = jnp.where(rt == 0, s1_ref[:, cols], r1)
            g2 = jnp.where(rt == 0, s0_ref[:, cols], jnp.where(rt == 1, s1_ref[:, cols], r2))
            gt_ref[:, cols] = gate
        wc = wc_ref[:, cols]
        gc = bc_ref[:, cols] + wc[0:1, :] * g2 + wc[1:2, :] * g1 + wc[2:3, :] * gate
        act = jax.nn.gelu(gc, approximate=True) * up
        acc_scr[...] += _dot(act.astype(BF16), wd_ref[cols, :])

    @pl.when(f == pl.num_programs(1) - 1)
    def _():
        y_ref[...] = x_ref[...] + _rms(acc_scr[...], gp_ref[...])


def _ffn(x2, g_pre, w_up_b, w_down_b, w_conv, b_conv, g_post, *, seq_len=None, state=None):
    n, d = x2.shape
    d_ff = w_down_b.shape[0]
    tf = FFN_TF
    nf = d_ff // tf
    tm = min(FFN_TM, n)
    common_w = [
        pl.BlockSpec((1, d), lambda i, f: (0, 0)),
        pl.BlockSpec((d, tf), lambda i, f: (0, f)),
        pl.BlockSpec((d, tf), lambda i, f: (0, nf + f)),
        pl.BlockSpec((tf, d), lambda i, f: (f, 0)),
        pl.BlockSpec((CONV_WIDTH, tf), lambda i, f: (0, f)),
        pl.BlockSpec((1, tf), lambda i, f: (0, f)),
        pl.BlockSpec((1, d), lambda i, f: (0, 0)),
    ]
    w_args = (g_pre, w_up_b, w_up_b, w_down_b, w_conv, b_conv, g_post)
    row = lambda i, f: (i, 0)
    if state is None:
        seq_tiles = seq_len // tm
        halo_blocks = tm // HALO
        in_specs = [pl.BlockSpec((tm, d), row),
                    pl.BlockSpec((HALO, d), lambda i, f: (jnp.maximum(i * halo_blocks - 1, 0), 0))] + common_w
        args = (x2, x2) + w_args
        gt_shape = jax.ShapeDtypeStruct((n // tm * 8, d_ff), F32)
        gt_spec = pl.BlockSpec((8, tf), lambda i, f: (i, f))
        scratch = [pltpu.VMEM((tm, d), BF16), pltpu.VMEM((HALO, d), BF16), pltpu.VMEM((tm, d), F32)]
        kern = functools.partial(_ffn_kernel, seq_tiles=seq_tiles, n_tok=None)
    else:
        n_tok = n // state.shape[0]
        s0 = jnp.repeat(state[:, 0], n_tok, axis=0)
        s1 = jnp.repeat(state[:, 1], n_tok, axis=0)
        in_specs = [pl.BlockSpec((tm, d), row),
                    pl.BlockSpec((tm, tf), lambda i, f: (i, f)),
                    pl.BlockSpec((tm, tf), lambda i, f: (i, f))] + common_w
        args = (x2, s0, s1) + w_args
        gt_shape = jax.ShapeDtypeStruct((n, d_ff), F32)
        gt_spec = pl.BlockSpec((tm, tf), lambda i, f: (i, f))
        scratch = [pltpu.VMEM((tm, d), BF16), pltpu.VMEM((tm, d), F32)]
        kern = functools.partial(_ffn_kernel, seq_tiles=None, n_tok=n_tok)
    return pl.pallas_call(
        kern,
        grid=(n // tm, nf),
        in_specs=in_specs,
        out_specs=(pl.BlockSpec((tm, d), row), gt_spec),
        out_shape=(jax.ShapeDtypeStruct((n, d), F32), gt_shape),
        scratch_shapes=scratch,
        compiler_params=_cparams(("arbitrary", "arbitrary")),
        name="ffn",
    )(*args)


def _rope_tables(pos):
    half = HEAD_DIM // 2
    inv_freq = ROPE_THETA ** (-jnp.arange(half, dtype=F32) / half)
    ang = pos.astype(F32)[:, None] * inv_freq[None, :]
    cos, sin = jnp.cos(ang), jnp.sin(ang)
    return jnp.concatenate([cos, cos], axis=-1), jnp.concatenate([-sin, sin], axis=-1)


def _cmp_weights(w1, pe, w2):
    w1r = w1.reshape(CMP_RATIO, CMP_STRIDE * HEAD_DIM, HEAD_DIM)
    w1cat = jnp.concatenate([w1r[r] for r in range(CMP_RATIO)], axis=1).astype(BF16)
    pe8 = jnp.pad(pe.reshape(CMP_RATIO, CMP_STRIDE * HEAD_DIM), ((0, 8 - CMP_RATIO), (0, 0)))
    return w1cat, pe8, w2.astype(BF16)


def kernel(x_prompt, x_sample, cache_nsa_cmp_kv, cache_nsa_sel_kv, cache_nsa_win_kv, cache_fox_kv, cache_fox_logf, state_ffn_conv, page_table, g_pre_mix, w_in, b_fgt, w_cmp_k1, pe_cmp_k, w_cmp_k2, w_cmp_v1, pe_cmp_v, w_cmp_v2, w_nsa_o, w_fox_o, w_out, g_post_mix, g_pre_ffn, w_up, w_conv, b_conv, w_down, g_post_ffn):
    b_p, t_p, d = x_prompt.shape
    b_s, t_s, _ = x_sample.shape
    depth = w_in.shape[0]
    page = cache_nsa_cmp_kv.shape[2]
    past = page_table.shape[1] * page
    g_n, n_h = N_NSA_GROUPS, N_FOX_HEADS

    cos_p, sin_p = _rope_tables(jnp.tile(jnp.arange(t_p), b_p))
    cos_s, sin_s = _rope_tables(jnp.tile(past + jnp.arange(t_s), b_s))

    y_p = x_prompt.reshape(b_p * t_p, d)
    y_s = x_sample.reshape(b_s * t_s, d)
    outs = {k: [] for k in ('cmp_p', 'cmp_s', 'sel_p', 'sel_s', 'win_p', 'win_s',
                            'fox_p', 'fox_s', 'lf_p', 'lf_s', 'conv_p', 'conv_s')}
    o_q = NSA_Q_W
    o_g = o_q + 3 * KV_W
    o_f = o_g + N_GATE_COLS
    o_ff = o_f + 3 * FOX_W
    o_m = o_ff + n_h
    for l in range(depth):
        w = w_in[l]
        w_main = jnp.concatenate([w[:, :o_g], w[:, o_f:o_ff], w[:, o_m:]], axis=1).astype(BF16)
        w_small = jnp.concatenate([w[:, o_g:o_f], w[:, o_ff:o_m],
                                   jnp.zeros((d, LANES - N_GATE_COLS - n_h), F32)], axis=1).astype(BF16)
        bf_row = jnp.zeros((1, LANES), F32).at[0, LOGF_COL0:LOGF_COL0 + n_h].set(b_fgt[l])
        g1 = g_pre_mix[l][None, :]
        cmp_k = _cmp_weights(w_cmp_k1[l], pe_cmp_k[l], w_cmp_k2[l])
        cmp_v = _cmp_weights(w_cmp_v1[l], pe_cmp_v[l], w_cmp_v2[l])
        wn, wf, wo = w_nsa_o[l].astype(BF16), w_fox_o[l].astype(BF16), w_out[l].astype(BF16)
        wu, wd = w_up[l].astype(BF16), w_down[l].astype(BF16)
        ffn_w = (g_pre_ffn[l][None, :], wu, wd, w_conv[l], b_conv[l][None, :], g_post_ffn[l][None, :])

        (qn, kvc, kvs, kvw, qf, kvf, gm, sm, kvs_b, kvw_b, kvf_b) = _project(y_p, g1, cos_p, sin_p, w_main, w_small, bf_row)
        w1cat = jnp.stack([cmp_k[0], cmp_v[0]])
        pe8 = jnp.stack([cmp_k[1], cmp_v[1]])
        w2 = jnp.stack([cmp_k[2], cmp_v[2]])
        ckv = _compress_prompt(kvc, b_p, t_p, w1cat, pe8, w2)
        frow = _fcum_prompt(sm, b_p, t_p)
        o_n = _nsa_prompt(qn, sm, ckv, kvs_b, kvw_b, b_p, t_p)
        o_fx = _fox_prompt(qf, kvf_b, frow, b_p, t_p)
        y1 = _postmix(o_n, o_fx, gm, y_p, wn, wf, wo, g_post_mix[l][None, :])
        y_p, gt = _ffn(y1, *ffn_w, seq_len=t_p)
        n_win = min(WINDOW, t_p)
        outs['cmp_p'].append(kvc.reshape(b_p, t_p, 2, g_n, HEAD_DIM))
        outs['sel_p'].append(kvs.reshape(b_p, t_p, 2, g_n, HEAD_DIM))
        outs['win_p'].append(kvw.reshape(b_p, t_p, 2, g_n, HEAD_DIM)[:, t_p - n_win:])
        outs['fox_p'].append(kvf.reshape(b_p, t_p, 2, n_h, HEAD_DIM))
        outs['lf_p'].append(sm[:, LOGF_COL0:LOGF_COL0 + n_h].reshape(b_p, t_p, n_h))
        tiles_per_seq = t_p // FFN_TM
        gt = gt.reshape(b_p, tiles_per_seq, 8, -1)
        outs['conv_p'].append(gt[:, -1, 8 - (CONV_WIDTH - 1):])

        (qn, kvc, kvs, kvw, qf, kvf, gm, sm, _, _, _) = _project(y_s, g1, cos_s, sin_s, w_main, w_small, bf_row)
        win_buf = cache_nsa_win_kv[l]
        o_n = _nsa_sample(page_table, cache_nsa_cmp_kv[l], cache_nsa_sel_kv[l], win_buf, kvs, kvw, qn, sm,
                          cmp_k + cmp_v, t_s)
        o_fx = _fox_sample(page_table, cache_fox_kv[l], cache_fox_logf[l], kvf, qf, sm, t_s)
        o_n = o_n[:, :t_s].reshape(b_s * t_s, -1)
        y1 = _postmix(o_n, o_fx, gm, y_s, wn, wf, wo, g_post_mix[l][None, :])
        y_s, gt = _ffn(y1, *ffn_w, state=state_ffn_conv[l])
        kw_new = kvw.reshape(b_s, t_s, 2, g_n, HEAD_DIM)
        n_win = min(WINDOW, win_buf.shape[1] + t_s)
        outs['cmp_s'].append(kvc.reshape(b_s, t_s, 2, g_n, HEAD_DIM))
        outs['sel_s'].append(kvs.reshape(b_s, t_s, 2, g_n, HEAD_DIM))
        outs['win_s'].append(jnp.concatenate([win_buf, kw_new], axis=1)[:, -n_win:])
        outs['fox_s'].append(kvf.reshape(b_s, t_s, 2, n_h, HEAD_DIM))
        outs['lf_s'].append(sm[:, LOGF_COL0:LOGF_COL0 + n_h].reshape(b_s, t_s, n_h))
        gfull = jnp.concatenate([state_ffn_conv[l], gt.reshape(b_s, t_s, -1)], axis=1)
        outs['conv_s'].append(gfull[:, t_s:])

    st = {k: jnp.stack(v) for k, v in outs.items()}
    return (y_p.reshape(b_p, t_p, d), y_s.reshape(b_s, t_s, d),
            st['cmp_p'], st['cmp_s'], st['sel_p'], st['sel_s'], st['win_p'], st['win_s'],
            st['fox_p'], st['fox_s'], st['lf_p'], st['lf_s'], st['conv_p'], st['conv_s'])
```

```python
import functools

import numpy as np
import jax
import jax.numpy as jnp
from jax import lax
from jax.experimental import pallas as pl
from jax.experimental.pallas import tpu as pltpu

F32 = jnp.float32
BF16 = jnp.bfloat16

HEAD_DIM = 128
N_NSA_HEADS = 8
N_NSA_GROUPS = 2
NSA_HPG = N_NSA_HEADS // N_NSA_GROUPS
N_FOX_HEADS = 8
CMP_BLOCK = 32
CMP_STRIDE = 16
CMP_RATIO = CMP_BLOCK // CMP_STRIDE
SEL_BLOCK = 64
SEL_TOPK = 16
N_LOCAL_BLOCKS = 2
WINDOW = 512
CONV_WIDTH = 3
ROPE_THETA = 10000.0
RMS_EPS = 1e-6
FORCE_BONUS = 1e4
NEG_INF = -1e30
LOG2E = 1.4426950408889634
QK_SCALE = HEAD_DIM ** -0.5 * LOG2E

N_GATE_COLS = N_NSA_HEADS * 3
LOGF_COL0 = N_GATE_COLS
LANES = 128
VMEM_LIMIT = 56 * 1024 * 1024

PROJ_TN = 512
KV_W = 2 * N_NSA_GROUPS * HEAD_DIM
NSA_Q_W = N_NSA_HEADS * HEAD_DIM
FOX_W = N_FOX_HEADS * HEAD_DIM


def _cparams(sem):
    return pltpu.CompilerParams(dimension_semantics=sem, vmem_limit_bytes=VMEM_LIMIT)


def _dot(a, b):
    return jnp.dot(a, b, preferred_element_type=F32)


def _dot_nt(a, b):
    return lax.dot_general(a, b, (((1,), (1,)), ((), ())), preferred_element_type=F32)


def _rms(x, g):
    return x * lax.rsqrt(jnp.mean(x * x, axis=-1, keepdims=True) + RMS_EPS) * g


def _masked_softmax(s, mask):
    sm = jnp.where(mask, s, NEG_INF)
    m = jnp.max(sm, axis=-1, keepdims=True)
    e = jnp.where(mask, jnp.exp2(sm - m), 0.0)
    l = jnp.sum(e, axis=-1, keepdims=True)
    return e / jnp.where(l > 0.0, l, 1.0)


def _transpose_rows(src_ref, dst_ref, cols=slice(None)):
    for c in range(src_ref.shape[0] // LANES):
        rows = slice(c * LANES, (c + 1) * LANES)
        dst_ref[:, rows] = src_ref[rows, cols].astype(F32).T.astype(BF16)


def _masked_exp(s, mask):
    sm = jnp.where(mask, s, NEG_INF)
    e = jnp.where(mask, jnp.exp2(sm - jnp.max(sm, axis=-1, keepdims=True)), 0.0)
    l = jnp.sum(e, axis=-1, keepdims=True)
    return e, jnp.where(l > 0.0, l, 1.0)


def _lane_tile_max(mx, s):
    for c in range(s.shape[1] // LANES):
        mx = jnp.maximum(mx, s[:, c * LANES:(c + 1) * LANES])
    return mx


def _exp_accumulate(carry, s, m, v, lane_shift=0):
    ls, acc = carry
    p = jnp.exp2(s - m)
    tiles = [p[:, c * LANES:(c + 1) * LANES] for c in range(s.shape[1] // LANES)]
    for t in tiles:
        ls = ls + t
    if lane_shift:
        p = jnp.concatenate([pltpu.roll(t, lane_shift, axis=1) for t in tiles], axis=1)
    return ls, acc + _dot(p.astype(BF16), v)


def _split3(x):
    hi = x.astype(BF16)
    r = x - hi.astype(F32)
    mid = r.astype(BF16)
    lo = (r - mid.astype(F32)).astype(BF16)
    return hi, mid, lo


def _topk_mask(score, k, n_sel):
    st = score.T
    nv = -(-n_sel // 8)
    slabs = [st[8 * v:8 * v + 8, :] for v in range(nv)]
    sub = lax.broadcasted_iota(jnp.int32, (8, LANES), 0)
    ranks = [jnp.zeros((8, LANES), F32) for _ in range(nv)]
    for b2 in range(n_sel):
        row = jnp.broadcast_to(st[b2:b2 + 1, :], (8, LANES))
        for v in range(nv):
            if b2 < 8 * v:
                beats = row >= slabs[v]
            elif b2 >= 8 * v + 8:
                beats = row > slabs[v]
            else:
                beats = (row > slabs[v]) | ((row == slabs[v]) & (sub > b2 - 8 * v))
            ranks[v] = ranks[v] + jnp.where(beats, 1.0, 0.0)
    sel = [jnp.where((ranks[v] < k) & (sub + 8 * v < n_sel), 1.0, 0.0) for v in range(nv)]
    sel_t = jnp.concatenate(sel + [jnp.zeros((LANES - 8 * nv, LANES), F32)], axis=0)
    return sel_t.T


def _sel_scores(imp, tpos, n_sel):
    bidx = lax.broadcasted_iota(jnp.int32, imp.shape, 1)
    cur = jnp.right_shift(tpos, 6)
    valid = bidx <= cur
    forced = (bidx == 0) | (valid & (bidx > cur - N_LOCAL_BLOCKS))
    score = jnp.where(valid, jnp.where(forced, imp + FORCE_BONUS, imp), NEG_INF)
    return jnp.where(bidx < n_sel, score, -jnp.inf)


def _proj_kernel(x_ref, g_ref, cos_ref, sin_ref, w_ref, ws_ref, bf_ref,
                 qn_ref, kvc_ref, kvs_ref, kvw_ref, qf_ref, kvf_ref, gm_ref, sm_ref,
                 kvsb_ref, kvwb_ref, kvfb_ref, h_scr):
    j = pl.program_id(1)

    @pl.when(j == 0)
    def _():
        x = x_ref[...]
        y = x * lax.rsqrt(jnp.mean(x * x, axis=-1, keepdims=True) + RMS_EPS)
        h = (y * g_ref[...]).astype(BF16)
        h_scr[...] = h
        s = _dot(h, ws_ref[...])
        lane = lax.broadcasted_iota(jnp.int32, s.shape, 1)
        z = s + bf_ref[...]
        lf = jnp.minimum(z, 0.0) - jnp.log1p(jnp.exp(-jnp.abs(z)))
        sm_ref[...] = jnp.where(lane < N_GATE_COLS, jax.nn.sigmoid(s),
                                jnp.where(lane < LOGF_COL0 + N_FOX_HEADS, lf, 0.0))

    cos = cos_ref[...]
    sin = sin_ref[...]
    half_w = PROJ_TN // 2
    halves = [slice(0, half_w), slice(half_w, PROJ_TN)]

    def mm(cols):
        return _dot(h_scr[...], w_ref[:, cols])

    def rope2(a):
        return jnp.concatenate(
            [a[:, k * HEAD_DIM:(k + 1) * HEAD_DIM] * cos
             + pltpu.roll(a[:, k * HEAD_DIM:(k + 1) * HEAD_DIM], HEAD_DIM // 2, axis=1) * sin
             for k in range(half_w // HEAD_DIM)], axis=1)

    def kv_rows(ref, bref):
        assert half_w == N_NSA_GROUPS * HEAD_DIM
        for cols, is_key in zip(halves, (True, False)):
            a = mm(cols)
            a = rope2(a) if is_key else a
            ref[:, cols] = a
            if bref is not None:
                bref[:, cols] = a.astype(BF16)

    @pl.when(j < 2)
    def _():
        for cols in halves:
            qn_ref[:, cols] = (rope2(mm(cols)) * QK_SCALE).astype(BF16)

    @pl.when(j == 2)
    def _():
        kv_rows(kvc_ref, None)

    @pl.when(j == 3)
    def _():
        kv_rows(kvs_ref, kvsb_ref)

    @pl.when(j == 4)
    def _():
        kv_rows(kvw_ref, kvwb_ref)

    @pl.when((j >= 5) & (j < 7))
    def _():
        for cols in halves:
            qf_ref[:, cols] = (mm(cols) * QK_SCALE).astype(BF16)

    @pl.when((j >= 7) & (j < 11))
    def _():
        for cols in halves:
            a = mm(cols)
            kvf_ref[:, cols] = a
            kvfb_ref[:, cols] = a.astype(BF16)

    @pl.when(j >= 11)
    def _():
        for cols in halves:
            gm_ref[:, cols] = jax.nn.sigmoid(mm(cols))


def _project(x2, g, cos2, sin2, w_main, w_small, bf_row):
    n, d = x2.shape
    tm = 512
    n_j = w_main.shape[1] // PROJ_TN
    tn = PROJ_TN

    def clip(lo, hi):
        return lambda i, j: (i, jnp.clip(j - lo, 0, hi - lo))

    row = lambda i, j: (i, 0)
    out_shape = (
        jax.ShapeDtypeStruct((n, NSA_Q_W), BF16),
        jax.ShapeDtypeStruct((n, KV_W), F32),
        jax.ShapeDtypeStruct((n, KV_W), F32),
        jax.ShapeDtypeStruct((n, KV_W), F32),
        jax.ShapeDtypeStruct((n, FOX_W), BF16),
        jax.ShapeDtypeStruct((n, 2 * FOX_W), F32),
        jax.ShapeDtypeStruct((n, 2 * d), F32),
        jax.ShapeDtypeStruct((n, LANES), F32),
        jax.ShapeDtypeStruct((n, KV_W), BF16),
        jax.ShapeDtypeStruct((n, KV_W), BF16),
        jax.ShapeDtypeStruct((n, 2 * FOX_W), BF16),
    )
    out_specs = (
        pl.BlockSpec((tm, tn), clip(0, 1)),
        pl.BlockSpec((tm, tn), row),
        pl.BlockSpec((tm, tn), row),
        pl.BlockSpec((tm, tn), row),
        pl.BlockSpec((tm, tn), clip(5, 6)),
        pl.BlockSpec((tm, tn), clip(7, 10)),
        pl.BlockSpec((tm, tn), clip(11, 18)),
        pl.BlockSpec((tm, LANES), row),
        pl.BlockSpec((tm, tn), row),
        pl.BlockSpec((tm, tn), row),
        pl.BlockSpec((tm, tn), clip(7, 10)),
    )
    in_specs = [
        pl.BlockSpec((tm, d), row),
        pl.BlockSpec((1, d), lambda i, j: (0, 0)),
        pl.BlockSpec((tm, LANES), row),
        pl.BlockSpec((tm, LANES), row),
        pl.BlockSpec((d, tn), lambda i, j: (0, j)),
        pl.BlockSpec((d, LANES), lambda i, j: (0, 0)),
        pl.BlockSpec((1, LANES), lambda i, j: (0, 0)),
    ]
    return pl.pallas_call(
        _proj_kernel,
        grid=(n // tm, n_j),
        in_specs=in_specs,
        out_specs=out_specs,
        out_shape=out_shape,
        scratch_shapes=[pltpu.VMEM((tm, d), BF16)],
        compiler_params=_cparams(("arbitrary", "arbitrary")),
        name="proj",
    )(x2, g, cos2, sin2, w_main, w_small, bf_row)


def _compress_tail(xc, w1, pe8, w2):
    n = xc.shape[0]
    part = _dot(xc, w1)
    pp = _dot(pe8.astype(BF16), w1)
    pe_term = pp[0:1, :HEAD_DIM] + pp[1:2, HEAD_DIM:]
    hid = pe_term + part[:, :HEAD_DIM] + pltpu.roll(part[:, HEAD_DIM:], n - 1, axis=0)
    return _dot(jax.nn.gelu(hid, approximate=True).astype(BF16), w2)


def _cmp_prompt_kernel(x_ref, w1_ref, pe_ref, w2_ref, o_ref):
    n = x_ref.shape[0] // CMP_STRIDE
    xc = jnp.concatenate([x_ref[pl.ds(i, n, stride=CMP_STRIDE), :] for i in range(CMP_STRIDE)],
                         axis=1).astype(BF16)
    o_ref[0, 0] = _compress_tail(xc, w1_ref[0], pe_ref[0], w2_ref[0]).astype(BF16)


def _compress_prompt(kvc, b_n, t_n, w1cat, pe8, w2):
    n = t_n // CMP_STRIDE
    return pl.pallas_call(
        _cmp_prompt_kernel,
        grid=(b_n, 2 * N_NSA_GROUPS),
        in_specs=[
            pl.BlockSpec((t_n, HEAD_DIM), lambda b, s: (b, s)),
            pl.BlockSpec((1, CMP_STRIDE * HEAD_DIM, 2 * HEAD_DIM), lambda b, s: (s // N_NSA_GROUPS, 0, 0)),
            pl.BlockSpec((1, 8, CMP_STRIDE * HEAD_DIM), lambda b, s: (s // N_NSA_GROUPS, 0, 0)),
            pl.BlockSpec((1, HEAD_DIM, HEAD_DIM), lambda b, s: (s // N_NSA_GROUPS, 0, 0)),
        ],
        out_specs=pl.BlockSpec((1, 1, n, HEAD_DIM), lambda b, s: (b, s, 0, 0)),
        out_shape=jax.ShapeDtypeStruct((b_n, 2 * N_NSA_GROUPS, n, HEAD_DIM), BF16),
        compiler_params=_cparams(("arbitrary", "arbitrary")),
        name="cmp_prompt",
    )(kvc, w1cat, pe8, w2)


def _fcum_kernel(x_ref, frow_ref, carry_scr):
    i = pl.program_id(1)

    @pl.when(i == 0)
    def _():
        carry_scr[...] = jnp.zeros_like(carry_scr)

    x = x_ref[...]
    tb = x.shape[0]
    r = lax.broadcasted_iota(jnp.int32, (tb, tb), 0)
    c = lax.broadcasted_iota(jnp.int32, (tb, tb), 1)
    tri = jnp.where(r >= c, 1.0, 0.0).astype(BF16)
    hi, mid, lo = _split3(x)
    cs = _dot(tri, hi) + _dot(tri, mid) + _dot(tri, lo) + carry_scr[0:1, :]
    carry_scr[...] = jnp.broadcast_to(cs[tb - 1:tb, :], carry_scr.shape)
    frow_ref[0] = cs.T[LOGF_COL0:LOGF_COL0 + N_FOX_HEADS, :] * LOG2E


def _fcum_prompt(sm, b_n, t_n):
    tb = 512
    nb = t_n // tb
    return pl.pallas_call(
        _fcum_kernel,
        grid=(b_n, nb),
        in_specs=[pl.BlockSpec((tb, LANES), lambda b, i: (b * nb + i, 0))],
        out_specs=pl.BlockSpec((1, N_FOX_HEADS, tb), lambda b, i: (b, 0, i)),
        out_shape=jax.ShapeDtypeStruct((b_n, N_FOX_HEADS, t_n), F32),
        scratch_shapes=[pltpu.VMEM((8, LANES), F32)],
        compiler_params=_cparams(("arbitrary", "arbitrary")),
        name="fcum_prompt",
    )(sm)


NSA_TQ = 128
NSA_TK = 256
NSA_NSUB = 2


def _tile_rows(a, reps):
    return jnp.concatenate([a] * reps, axis=0)


def _nsa_prompt_kernel(q_ref, sm_ref, ck_ref, ks_ref, kw_ref, cover_ref, expand_ref, o_ref, s_scr, kst_scr, kwt_scr,
                       *, n_cmp, n_sel):
    i = pl.program_id(1)
    tq, nsub = NSA_TQ, NSA_NSUB
    assert nsub * tq == NSA_TK
    t0 = i * (tq * nsub)
    n_ck = ck_ref.shape[2]
    band = WINDOW + tq
    tpos = [t0 + u * tq + lax.broadcasted_iota(jnp.int32, (tq, 1), 0) for u in range(nsub)]

    @pl.when(i == 0)
    def _():
        for g in range(N_NSA_GROUPS):
            cols = slice(g * HEAD_DIM, (g + 1) * HEAD_DIM)
            _transpose_rows(ks_ref, kst_scr.at[g], cols)
            _transpose_rows(kw_ref, kwt_scr.at[g], cols)

    for g in range(N_NSA_GROUPS):
        ck = ck_ref[0, g]
        cv = ck_ref[0, N_NSA_GROUPS + g]
        o_cs, o_ws, selms = [], [], []
        for u in range(nsub):
            q = jnp.concatenate([q_ref[u * tq:(u + 1) * tq, (g * NSA_HPG + h) * HEAD_DIM:(g * NSA_HPG + h + 1) * HEAD_DIM]
                                 for h in range(NSA_HPG)], axis=0)
            s_c = _dot_nt(q, ck)
            cidx = lax.broadcasted_iota(jnp.int32, (tq, n_ck), 1)
            mc = jnp.where((cidx * CMP_STRIDE + CMP_BLOCK - 1 <= tpos[u]) & (cidx < n_cmp), 1.0, 0.0)
            e_c, l_c = _masked_exp(s_c, _tile_rows(mc, NSA_HPG) > 0.5)
            e_cb = e_c.astype(BF16)
            imp4 = _dot(e_cb, cover_ref[...]) / l_c
            imp = imp4[0:tq] + imp4[tq:2 * tq] + imp4[2 * tq:3 * tq] + imp4[3 * tq:4 * tq]
            selms.append(_topk_mask(_sel_scores(imp, tpos[u], n_sel), min(SEL_TOPK, n_sel), n_sel).astype(BF16))
            o_cs.append(_dot(e_cb, cv) / l_c)
            w0 = pl.multiple_of(jnp.maximum(t0 + u * tq - WINDOW, 0), tq)
            vw = kw_ref[pl.ds(w0, band), (N_NSA_GROUPS + g) * HEAD_DIM:(N_NSA_GROUPS + g + 1) * HEAD_DIM]
            s_w = _dot(q, kwt_scr[g, :, pl.ds(w0, band)])
            wpos = w0 + lax.broadcasted_iota(jnp.int32, (tq, band), 1)
            bw = jnp.where((wpos <= tpos[u]) & (wpos > tpos[u] - WINDOW), 0.0, NEG_INF)
            s_w = s_w + _tile_rows(bw, NSA_HPG)
            e_w = jnp.exp2(s_w - jnp.max(s_w, axis=-1, keepdims=True))
            o_ws.append(_dot(e_w.astype(BF16), vw) / jnp.sum(e_w, axis=-1, keepdims=True))

        tb = tq * nsub
        q_all = jnp.concatenate([q_ref[:, (g * NSA_HPG + h) * HEAD_DIM:(g * NSA_HPG + h + 1) * HEAD_DIM]
                                 for h in range(NSA_HPG)], axis=0)
        selm_all = jnp.concatenate(selms, axis=0)
        tpos_all = t0 + lax.broadcasted_iota(jnp.int32, (tb, 1), 0)

        def sel_scores(k0, q_all=q_all, selm_all=selm_all, g=g):
            mk = ((_dot(selm_all, expand_ref[:, pl.ds(k0, NSA_TK)]) > 0.5)
                  & (k0 + lax.broadcasted_iota(jnp.int32, (tb, NSA_TK), 1) <= tpos_all))
            s = _dot(q_all, kst_scr[g, :, pl.ds(k0, NSA_TK)]).reshape(NSA_HPG, tb, NSA_TK)
            return jnp.where(mk[None], s, NEG_INF).reshape(NSA_HPG * tb, NSA_TK)

        assert (ks_ref.shape[0] // NSA_TK) % 2 == 0
        n_pairs = (i + 2) // 2

        def p1(j, mx):
            for d in range(2):
                k0 = pl.multiple_of((2 * j + d) * NSA_TK, NSA_TK)
                s = sel_scores(k0)
                s_scr[:, pl.ds(k0, NSA_TK)] = s
                mx = _lane_tile_max(mx, s)
            return mx

        mx = lax.fori_loop(0, n_pairs, p1, jnp.full((NSA_HPG * tb, LANES), -jnp.inf, F32))
        m = jnp.max(mx, axis=-1, keepdims=True)

        def p2(j, carry, m=m, g=g):
            for d in range(2):
                k0 = pl.multiple_of((2 * j + d) * NSA_TK, NSA_TK)
                v = ks_ref[pl.ds(k0, NSA_TK), (N_NSA_GROUPS + g) * HEAD_DIM:(N_NSA_GROUPS + g + 1) * HEAD_DIM]
                carry = _exp_accumulate(carry, s_scr[:, pl.ds(k0, NSA_TK)], m, v)
            return carry

        zeros = jnp.zeros((NSA_HPG * tb, LANES), F32)
        ls, acc = lax.fori_loop(0, n_pairs, p2, (zeros, zeros))
        o_s = acc / jnp.sum(ls, axis=-1, keepdims=True)

        for u in range(nsub):
            gates = sm_ref[u * tq:(u + 1) * tq, :]
            for h in range(NSA_HPG):
                hh = g * NSA_HPG + h
                rows = slice(h * tq, (h + 1) * tq)
                rows_s = slice(h * tb + u * tq, h * tb + (u + 1) * tq)
                out = (gates[:, 3 * hh:3 * hh + 1] * o_cs[u][rows] + gates[:, 3 * hh + 1:3 * hh + 2] * o_s[rows_s]
                       + gates[:, 3 * hh + 2:3 * hh + 3] * o_ws[u][rows])
                o_ref[u * tq:(u + 1) * tq, hh * HEAD_DIM:(hh + 1) * HEAD_DIM] = out.astype(BF16)


def _cover_matrix(n_rows, n_cmp, n_sel):
    c = np.arange(n_rows)[:, None] * CMP_STRIDE
    b = np.arange(LANES)[None, :] * SEL_BLOCK
    m = (c < b + SEL_BLOCK) & (c + CMP_BLOCK > b) & (np.arange(n_rows)[:, None] < n_cmp) & (np.arange(LANES)[None, :] < n_sel)
    return jnp.asarray(m.astype(np.float32), dtype=BF16)


def _expand_matrix(n_keys):
    m = (np.arange(n_keys)[None, :] // SEL_BLOCK) == np.arange(LANES)[:, None]
    return jnp.asarray(m.astype(np.float32), dtype=BF16)


def _nsa_prompt(qn, sm, ckv, kvs_b, kvw_b, b_n, t_n):
    blk = NSA_TQ * NSA_NSUB
    nq = t_n // blk
    n_cmp = t_n // CMP_STRIDE - CMP_RATIO + 1
    n_sel = t_n // SEL_BLOCK
    n_ck = ckv.shape[2]
    cover = _cover_matrix(n_ck, n_cmp, n_sel)
    expand = _expand_matrix(t_n)
    kern = functools.partial(_nsa_prompt_kernel, n_cmp=n_cmp, n_sel=n_sel)
    return pl.pallas_call(
        kern,
        grid=(b_n, nq),
        in_specs=[
            pl.BlockSpec((blk, NSA_Q_W), lambda b, i: (b * nq + i, 0)),
            pl.BlockSpec((blk, LANES), lambda b, i: (b * nq + i, 0)),
            pl.BlockSpec((1, 2 * N_NSA_GROUPS, n_ck, HEAD_DIM), lambda b, i: (b, 0, 0, 0)),
            pl.BlockSpec((t_n, KV_W), lambda b, i: (b, 0)),
            pl.BlockSpec((t_n, KV_W), lambda b, i: (b, 0)),
            pl.BlockSpec((n_ck, LANES), lambda b, i: (0, 0)),
            pl.BlockSpec((LANES, t_n), lambda b, i: (0, 0)),
        ],
        out_specs=pl.BlockSpec((blk, NSA_Q_W), lambda b, i: (b * nq + i, 0)),
        out_shape=jax.ShapeDtypeStruct((b_n * t_n, NSA_Q_W), BF16),
        scratch_shapes=[pltpu.VMEM((NSA_HPG * blk, t_n), F32),
                        pltpu.VMEM((N_NSA_GROUPS, HEAD_DIM, t_n), BF16),
                        pltpu.VMEM((N_NSA_GROUPS, HEAD_DIM, t_n), BF16)],
        compiler_params=_cparams(("arbitrary", "arbitrary")),
        name="nsa_prompt",
    )(qn, sm, ckv, kvs_b, kvw_b, cover, expand)


FOX_BLK = 512


def _fox_prompt_kernel(q_ref, k_ref, v_ref, frow_ref, o_ref, s_scr, kt_scr):
    h = pl.program_id(1)
    blk = FOX_BLK
    assert q_ref.shape[0] % blk == 0
    _transpose_rows(k_ref, kt_scr)

    causal = lax.broadcasted_iota(jnp.int32, (blk, blk), 1) <= lax.broadcasted_iota(jnp.int32, (blk, 1), 0)
    zeros = jnp.zeros((blk, LANES), F32)
    for qb in range(q_ref.shape[0] // blk):
        q = q_ref[qb * blk:(qb + 1) * blk, :]
        mx = jnp.full((blk, LANES), -jnp.inf, F32)
        for kt in range(qb + 1):
            keys = slice(kt * blk, (kt + 1) * blk)
            s = _dot(q, kt_scr[:, keys]) - frow_ref[0, pl.ds(h, 1), keys]
            if kt == qb:
                s = jnp.where(causal, s, NEG_INF)
            s_scr[:, keys] = s
            mx = _lane_tile_max(mx, s)
        m = jnp.max(mx, axis=-1, keepdims=True)
        carry = (zeros, zeros)
        for kt in range(qb + 1):
            keys = slice(kt * blk, (kt + 1) * blk)
            carry = _exp_accumulate(carry, s_scr[:, keys], m, v_ref[keys, :])
        ls, acc = carry
        o_ref[qb * blk:(qb + 1) * blk, :] = (acc / jnp.sum(ls, axis=-1, keepdims=True)).astype(BF16)


def _fox_prompt(qf, kvf_b, frow, b_n, t_n):
    nh = N_FOX_HEADS
    return pl.pallas_call(
        _fox_prompt_kernel,
        grid=(b_n, nh),
        in_specs=[
            pl.BlockSpec((t_n, HEAD_DIM), lambda b, h: (b, h)),
            pl.BlockSpec((t_n, HEAD_DIM), lambda b, h: (b, h)),
            pl.BlockSpec((t_n, HEAD_DIM), lambda b, h: (b, nh + h)),
            pl.BlockSpec((1, nh, t_n), lambda b, h: (b, 0, 0)),
        ],
        out_specs=pl.BlockSpec((t_n, HEAD_DIM), lambda b, h: (b, h)),
        out_shape=jax.ShapeDtypeStruct((b_n * t_n, FOX_W), BF16),
        scratch_shapes=[pltpu.VMEM((FOX_BLK, t_n), F32), pltpu.VMEM((HEAD_DIM, t_n), BF16)],
        compiler_params=_cparams(("arbitrary", "arbitrary")),
        name="fox_prompt",
    )(qf, kvf_b, kvf_b, frow)


FOX_SAMPLE_CH = 512
TOK_PAD = 8
NEW_PAD = 128


def _nsa_sample_kernel(pt_ref, *refs, n_pages, page, n_buf, n_tok):
    cmp_pages = refs[:n_pages]
    sel_pages = refs[n_pages:2 * n_pages]
    (win_ref, kvs_new_ref, kvw_new_ref, q_ref, sm_ref, w1k_ref, w1v_ref, pek_ref, pev_ref,
     w2k_ref, w2v_ref, cover_ref, expand_ref, o_ref) = refs[2 * n_pages:]
    del pt_ref
    past = n_pages * page
    n_slab = 2 * N_NSA_GROUPS
    chunks_per_page = page // CMP_STRIDE
    n_chunk = n_pages * chunks_per_page
    n_cmp = (past + n_tok + CMP_STRIDE - 1) // CMP_STRIDE - CMP_RATIO + 1
    n_sel = (past + n_tok + SEL_BLOCK - 1) // SEL_BLOCK
    tp = TOK_PAD
    tpos = past + lax.broadcasted_iota(jnp.int32, (tp, 1), 0)
    gates = sm_ref[0]
    kvs_new = kvs_new_ref[0]
    kvw_new = kvw_new_ref[0]

    def compress(slab, w1_ref, pe_ref, w2_ref):
        cols = []
        for i in range(CMP_STRIDE):
            cols.append(jnp.concatenate(
                [cmp_pages[p][pl.ds(i * n_slab + slab, chunks_per_page, stride=CMP_STRIDE * n_slab), :]
                 for p in range(n_pages)], axis=0))
        xc = jnp.concatenate(cols, axis=1).astype(BF16)
        return _compress_tail(xc, w1_ref[...], pe_ref[...], w2_ref[...]).astype(BF16)

    qs, o_cs, scores = [], [], []
    for g in range(N_NSA_GROUPS):
        q = jnp.concatenate([q_ref[0, :, (g * NSA_HPG + h) * HEAD_DIM:(g * NSA_HPG + h + 1) * HEAD_DIM]
                             for h in range(NSA_HPG)], axis=0)
        ck = compress(g, w1k_ref, pek_ref, w2k_ref)
        cv = compress(N_NSA_GROUPS + g, w1v_ref, pev_ref, w2v_ref)
        s_c = _dot_nt(q, ck)
        cidx = lax.broadcasted_iota(jnp.int32, (tp, n_chunk), 1)
        mc = jnp.where((cidx * CMP_STRIDE + CMP_BLOCK - 1 <= tpos) & (cidx < n_cmp), 1.0, 0.0)
        p_c = _masked_softmax(s_c, _tile_rows(mc, NSA_HPG) > 0.5)
        p_cb = p_c.astype(BF16)
        imp4 = _dot(p_cb, cover_ref[...])
        imp = imp4[0:tp] + imp4[tp:2 * tp] + imp4[2 * tp:3 * tp] + imp4[3 * tp:4 * tp]
        qs.append(q)
        o_cs.append(_dot(p_cb, cv))
        scores.append(_sel_scores(imp, tpos, n_sel))

    score_all = jnp.concatenate(scores + [jnp.zeros((LANES - N_NSA_GROUPS * tp, LANES), F32)], axis=0)
    selm_all = _topk_mask(score_all, min(SEL_TOPK, n_sel), n_sel)

    def sel_rows(slab):
        return jnp.concatenate([sel_pages[p][pl.ds(slab, page, stride=n_slab), :] for p in range(n_pages)], axis=0)

    def with_new(cached, new):
        pad = jnp.zeros((NEW_PAD - new.shape[0], new.shape[1]), new.dtype)
        return jnp.concatenate([cached, new, pad], axis=0).astype(BF16)

    for g in range(N_NSA_GROUPS):
        q, o_c = qs[g], o_cs[g]
        selm = selm_all[g * tp:(g + 1) * tp].astype(BF16)
        kc = slice(g * HEAD_DIM, (g + 1) * HEAD_DIM)
        vc = slice((N_NSA_GROUPS + g) * HEAD_DIM, (N_NSA_GROUPS + g + 1) * HEAD_DIM)

        n_keys = past + NEW_PAD
        s_s = _dot_nt(q, with_new(sel_rows(g), kvs_new[:, kc]))
        selx = _dot(selm, expand_ref[...])
        kpos = lax.broadcasted_iota(jnp.int32, (tp, n_keys), 1)
        ms = jnp.where((selx > 0.5) & (kpos <= tpos) & (kpos < past + n_tok), 1.0, 0.0)
        p_s = _masked_softmax(s_s, _tile_rows(ms, NSA_HPG) > 0.5)
        o_s = _dot(p_s.astype(BF16), with_new(sel_rows(N_NSA_GROUPS + g), kvs_new[:, vc]))

        s_w = _dot_nt(q, with_new(win_ref[pl.ds(g, n_buf, stride=n_slab), :], kvw_new[:, kc]))
        wpos = past - n_buf + lax.broadcasted_iota(jnp.int32, (tp, n_buf + NEW_PAD), 1)
        mw = jnp.where((wpos <= tpos) & (wpos > tpos - WINDOW) & (wpos < past + n_tok), 1.0, 0.0)
        p_w = _masked_softmax(s_w, _tile_rows(mw, NSA_HPG) > 0.5)
        o_w = _dot(p_w.astype(BF16), with_new(win_ref[pl.ds(N_NSA_GROUPS + g, n_buf, stride=n_slab), :],
                                              kvw_new[:, vc]))

        for h in range(NSA_HPG):
            hh = g * NSA_HPG + h
            rows = slice(h * tp, (h + 1) * tp)
            out = (gates[:, 3 * hh:3 * hh + 1] * o_c[rows] + gates[:, 3 * hh + 1:3 * hh + 2] * o_s[rows]
                   + gates[:, 3 * hh + 2:3 * hh + 3] * o_w[rows])
            o_ref[0, :, hh * HEAD_DIM:(hh + 1) * HEAD_DIM] = out.astype(BF16)


def _pad_tokens(a, n_seq, n_tok):
    a = a.reshape(n_seq, n_tok, a.shape[-1])
    return jnp.pad(a, ((0, 0), (0, TOK_PAD - n_tok), (0, 0)))


def _nsa_sample(page_table, cache_cmp, cache_sel, win_buf, kvs_new, kvw_new, qn, sm, cmp_w, n_tok):
    n_seq, n_pages = page_table.shape
    page = cache_cmp.shape[1]
    n_slab = 2 * N_NSA_GROUPS
    n_buf = win_buf.shape[1]
    past = n_pages * page
    cmp2 = cache_cmp.reshape(-1, HEAD_DIM)
    sel2 = cache_sel.reshape(-1, HEAD_DIM)
    win2 = win_buf.reshape(-1, HEAD_DIM)
    n_chunk = past // CMP_STRIDE
    n_cmp = (past + n_tok + CMP_STRIDE - 1) // CMP_STRIDE - CMP_RATIO + 1
    n_sel = (past + n_tok + SEL_BLOCK - 1) // SEL_BLOCK
    cover = _cover_matrix(n_chunk, min(n_cmp, n_chunk), n_sel)
    expand = _expand_matrix(past + NEW_PAD)
    w1k, pek, w2k, w1v, pev, w2v = cmp_w

    def page_spec(p):
        return pl.BlockSpec((page * n_slab, HEAD_DIM), lambda b, pt, p=p: (pt[b, p], 0))

    const2 = lambda b, pt: (0, 0)
    seq3 = lambda b, pt: (b, 0, 0)
    in_specs = ([page_spec(p) for p in range(n_pages)] + [page_spec(p) for p in range(n_pages)] + [
        pl.BlockSpec((n_buf * n_slab, HEAD_DIM), lambda b, pt: (b, 0)),
        pl.BlockSpec((1, TOK_PAD, KV_W), seq3),
        pl.BlockSpec((1, TOK_PAD, KV_W), seq3),
        pl.BlockSpec((1, TOK_PAD, NSA_Q_W), seq3),
        pl.BlockSpec((1, TOK_PAD, LANES), seq3),
        pl.BlockSpec(w1k.shape, const2),
        pl.BlockSpec(w1v.shape, const2),
        pl.BlockSpec(pek.shape, const2),
        pl.BlockSpec(pev.shape, const2),
        pl.BlockSpec(w2k.shape, const2),
        pl.BlockSpec(w2v.shape, const2),
        pl.BlockSpec(cover.shape, const2),
        pl.BlockSpec(expand.shape, const2),
    ])
    kern = functools.partial(_nsa_sample_kernel, n_pages=n_pages, page=page, n_buf=n_buf, n_tok=n_tok)
    grid_spec = pltpu.PrefetchScalarGridSpec(
        num_scalar_prefetch=1, grid=(n_seq,), in_specs=in_specs,
        out_specs=pl.BlockSpec((1, TOK_PAD, NSA_Q_W), seq3))
    return pl.pallas_call(
        kern,
        grid_spec=grid_spec,
        out_shape=jax.ShapeDtypeStruct((n_seq, TOK_PAD, NSA_Q_W), BF16),
        compiler_params=_cparams(("arbitrary",)),
        name="nsa_sample",
    )(page_table, *([cmp2] * n_pages), *([sel2] * n_pages), win2,
      _pad_tokens(kvs_new, n_seq, n_tok), _pad_tokens(kvw_new, n_seq, n_tok),
      _pad_tokens(qn, n_seq, n_tok), _pad_tokens(sm, n_seq, n_tok),
      w1k, w1v, pek, pev, w2k, w2v, cover, expand)


def _fox_sample_kernel(pt_ref, *refs, n_pages, page, n_tok):
    kv_pages = refs[:n_pages]
    lf_pages = refs[n_pages:2 * n_pages]
    k_new_ref, v_new_ref, q_ref, lfn_ref, o_ref, mask_scr, s_scr, kb_scr = refs[2 * n_pages:]
    del pt_ref
    nh = N_FOX_HEADS
    rows_pp = 2 * nh
    page_rows = page * rows_pp
    n_chunk = page_rows // LANES
    n_q = nh * n_tok
    assert rows_pp == 16 and LANES % rows_pp == 0 and n_q <= LANES

    @pl.when(pl.program_id(0) == 0)
    def _():
        qrow = lax.broadcasted_iota(jnp.int32, mask_scr.shape, 0)
        lane = lax.broadcasted_iota(jnp.int32, mask_scr.shape, 1)
        mask_scr[...] = jnp.where((lane & (rows_pp - 1)) == (qrow & (nh - 1)), 0.0, NEG_INF)

    x = jnp.concatenate([lf_pages[p][0] for p in range(n_pages)], axis=0)
    n_r = x.shape[0]
    la = lax.broadcasted_iota(jnp.int32, (LANES, LANES), 0)
    lb = lax.broadcasted_iota(jnp.int32, (LANES, LANES), 1)
    same = (la & (rows_pp - 1)) == (lb & (rows_pp - 1))
    u_in = jnp.where(same & (jnp.right_shift(la, 4) <= jnp.right_shift(lb, 4)), 1.0, 0.0).astype(BF16)
    u_all = jnp.where(same, 1.0, 0.0).astype(BF16)
    xh, xm, xl = _split3(x)
    within = _dot(xh, u_in) + _dot(xm, u_in) + _dot(xl, u_in)
    tot = _dot(xh, u_all) + _dot(xm, u_all) + _dot(xl, u_all)
    ra = lax.broadcasted_iota(jnp.int32, (n_r, n_r), 0)
    rb = lax.broadcasted_iota(jnp.int32, (n_r, n_r), 1)
    before = jnp.where(rb < ra, 1.0, 0.0).astype(BF16)
    th, tm_, tl = _split3(tot)
    offs = _dot(before, th) + _dot(before, tm_) + _dot(before, tl)
    f_end = offs[n_r - 1:n_r, :] + tot[n_r - 1:n_r, :]
    bias = (f_end - (within + offs)) * LOG2E

    q_all = q_ref[0]

    ch = mask_scr.shape[1]
    lt = ch // LANES
    steps = [(p, c) for p in range(n_pages) for c in range(page_rows // ch)]
    mx = jnp.full((n_q, LANES), -jnp.inf, F32)
    for p, c in steps:
        k_b = kv_pages[p][pl.ds(c * ch, ch), :].astype(BF16)
        kb_scr[p * page_rows + c * ch:p * page_rows + (c + 1) * ch, :] = k_b
        r0 = p * n_chunk + c * lt
        brow = jnp.concatenate([bias[r0 + i:r0 + i + 1, :] for i in range(lt)], axis=1)
        s = _dot_nt(q_all, k_b) + brow + mask_scr[...]
        s_scr[:, p * page_rows + c * ch:p * page_rows + (c + 1) * ch] = s
        mx = _lane_tile_max(mx, s)

    pad = jnp.zeros((LANES - n_q, HEAD_DIM), F32)
    k_new = jnp.concatenate([k_new_ref[0], pad], axis=0).astype(BF16)
    v_new = jnp.concatenate([v_new_ref[0], pad], axis=0).astype(BF16)
    g_in = jnp.where(((la & (nh - 1)) == (lb & (nh - 1))) & (la <= lb), 1.0, 0.0).astype(BF16)
    nh_, nm_, nl_ = _split3(lfn_ref[0])
    c_new = (_dot(nh_, g_in) + _dot(nm_, g_in) + _dot(nl_, g_in))[0:1, :] * LOG2E
    qrow = lax.broadcasted_iota(jnp.int32, (n_q, LANES), 0)
    lane = lax.broadcasted_iota(jnp.int32, (n_q, LANES), 1)
    ok = ((lane & (nh - 1)) == (qrow & (nh - 1))) & (lane <= qrow)
    s_new = jnp.where(ok, _dot_nt(q_all, k_new) - c_new, NEG_INF)
    m = jnp.max(jnp.maximum(mx, s_new), axis=-1, keepdims=True)

    carry = _exp_accumulate((jnp.zeros((n_q, LANES), F32), jnp.zeros((n_q, HEAD_DIM), F32)), s_new, m, v_new)
    for p, c in steps:
        rows = slice(p * page_rows + c * ch, p * page_rows + (c + 1) * ch)
        carry = _exp_accumulate(carry, s_scr[:, rows], m, kb_scr[rows, :], lane_shift=nh)
    ls, acc = carry
    o_ref[0] = (acc / jnp.sum(ls, axis=-1, keepdims=True)).astype(BF16)


def _fox_sample(page_table, cache_fox, cache_logf, kvf_new, qf, sm, n_tok):
    n_seq, n_pages = page_table.shape
    page = cache_fox.shape[1]
    nh = N_FOX_HEADS
    kv2 = cache_fox.reshape(-1, HEAD_DIM)
    rows_pp = 2 * nh
    n_chunk = page * rows_pp // LANES
    lf_c = jnp.pad(cache_logf, ((0, 0), (0, 0), (0, rows_pp - nh))).reshape(-1, n_chunk, LANES)
    n_q = n_tok * nh
    lfn = sm[:, LOGF_COL0:LOGF_COL0 + nh].reshape(n_seq, 1, n_q)
    lfn = jnp.pad(lfn, ((0, 0), (0, 7), (0, LANES - n_q)))
    kv_new = kvf_new.reshape(n_seq, n_tok, 2, nh, HEAD_DIM)
    k_new = kv_new[:, :, 0].reshape(n_seq, n_q, HEAD_DIM)
    v_new = kv_new[:, :, 1].reshape(n_seq, n_q, HEAD_DIM)
    q3 = qf.reshape(n_seq, n_q, HEAD_DIM)

    seq3 = lambda b, pt: (b, 0, 0)
    in_specs = ([pl.BlockSpec((page * rows_pp, HEAD_DIM), lambda b, pt, p=p: (pt[b, p], 0)) for p in range(n_pages)]
                + [pl.BlockSpec((1, n_chunk, LANES), lambda b, pt, p=p: (pt[b, p], 0, 0)) for p in range(n_pages)]
                + [pl.BlockSpec((1, n_q, HEAD_DIM), seq3),
                   pl.BlockSpec((1, n_q, HEAD_DIM), seq3),
                   pl.BlockSpec((1, n_q, HEAD_DIM), seq3),
                   pl.BlockSpec((1, 8, LANES), seq3)])
    kern = functools.partial(_fox_sample_kernel, n_pages=n_pages, page=page, n_tok=n_tok)
    n_rows = n_pages * page * rows_pp
    grid_spec = pltpu.PrefetchScalarGridSpec(
        num_scalar_prefetch=1, grid=(n_seq,), in_specs=in_specs,
        out_specs=pl.BlockSpec((1, n_q, HEAD_DIM), seq3),
        scratch_shapes=[pltpu.VMEM((n_q, FOX_SAMPLE_CH), F32),
                        pltpu.VMEM((n_q, n_rows), F32),
                        pltpu.VMEM((n_rows, HEAD_DIM), BF16)])
    out = pl.pallas_call(
        kern,
        grid_spec=grid_spec,
        out_shape=jax.ShapeDtypeStruct((n_seq, n_q, HEAD_DIM), BF16),
        compiler_params=_cparams(("arbitrary",)),
        name="fox_sample",
    )(page_table, *([kv2] * n_pages), *([lf_c] * n_pages), k_new, v_new, q3, lfn)
    return out.reshape(n_seq * n_tok, nh * HEAD_DIM)


def _postmix_kernel(on_ref, of_ref, gm0_ref, gm1_ref, x_ref, wn_ref, wf_ref, wo_ref, g_ref, y_ref):
    a = _dot(on_ref[...], wn_ref[...])
    b = _dot(of_ref[...], wf_ref[...])
    merged = gm0_ref[...] * a + gm1_ref[...] * b
    z = _dot(merged.astype(BF16), wo_ref[...])
    y_ref[...] = x_ref[...] + _rms(z, g_ref[...])


def _postmix(o_n, o_f, gm, x2, wn, wf, wo, g):
    n, d = x2.shape
    tm = 256
    row = lambda i: (i, 0)
    const = lambda i: (0, 0)
    return pl.pallas_call(
        _postmix_kernel,
        grid=(n // tm,),
        in_specs=[
            pl.BlockSpec((tm, NSA_Q_W), row),
            pl.BlockSpec((tm, FOX_W), row),
            pl.BlockSpec((tm, d), lambda i: (i, 0)),
            pl.BlockSpec((tm, d), lambda i: (i, 1)),
            pl.BlockSpec((tm, d), row),
            pl.BlockSpec(wn.shape, const),
            pl.BlockSpec(wf.shape, const),
            pl.BlockSpec(wo.shape, const),
            pl.BlockSpec((1, d), const),
        ],
        out_specs=pl.BlockSpec((tm, d), row),
        out_shape=jax.ShapeDtypeStruct((n, d), F32),
        compiler_params=_cparams(("arbitrary",)),
        name="postmix",
    )(o_n, o_f, gm, gm, x2, wn, wf, wo, g)


FFN_TM = 512
FFN_TF = 512
HALO = 16


def _ffn_kernel(*refs, seq_tiles, n_tok):
    if n_tok is None:
        (x_ref, xh_ref, g_ref, wg_ref, wu_ref, wd_ref, wc_ref, bc_ref, gp_ref,
         y_ref, gt_ref, h_scr, hh_scr, acc_scr) = refs
    else:
        (x_ref, s0_ref, s1_ref, g_ref, wg_ref, wu_ref, wd_ref, wc_ref, bc_ref, gp_ref,
         y_ref, gt_ref, h_scr, acc_scr) = refs
    i = pl.program_id(0)
    f = pl.program_id(1)
    tm = x_ref.shape[0]

    @pl.when(f == 0)
    def _():
        h_scr[...] = _rms(x_ref[...], g_ref[...]).astype(BF16)
        acc_scr[...] = jnp.zeros_like(acc_scr)
        if n_tok is None:
            hh_scr[...] = _rms(xh_ref[...], g_ref[...]).astype(BF16)

    h2 = h_scr[...]
    gate = _dot(h2, wg_ref[...])
    up = _dot(h2, wu_ref[...])
    row = lax.broadcasted_iota(jnp.int32, gate.shape, 0)
    r1 = pltpu.roll(gate, 1, axis=0)
    r2 = pltpu.roll(gate, 2, axis=0)
    if n_tok is None:
        first = (i % seq_tiles) == 0
        gh = jnp.where(first, 0.0, _dot(hh_scr[...], wg_ref[...]))
        p1 = gh[HALO - 1:HALO, :]
        p2 = gh[HALO - 2:HALO - 1, :]
        g1 = jnp.where(row == 0, p1, r1)
        g2 = jnp.where(row == 0, p2, jnp.where(row == 1, p1, r2))
        gt_ref[...] = gate[tm - 8:tm, :]
    else:
        assert n_tok & (n_tok - 1) == 0
        rt = row & (n_tok - 1)
        g1 = jnp.where(rt == 0, s1_ref[...], r1)
        g2 = jnp.where(rt == 0, s0_ref[...], jnp.where(rt == 1, s1_ref[...], r2))
        gt_ref[...] = gate
    wc = wc_ref[...]
    gc = bc_ref[...] + wc[0:1, :] * g2 + wc[1:2, :] * g1 + wc[2:3, :] * gate
    act = jax.nn.gelu(gc, approximate=True) * up
    acc_scr[...] += _dot(act.astype(BF16), wd_ref[...])

    @pl.when(f == pl.num_programs(1) - 1)
    def _():
        y_ref[...] = x_ref[...] + _rms(acc_scr[...], gp_ref[...])


def _ffn(x2, g_pre, w_up_b, w_down_b, w_conv, b_conv, g_post, *, seq_len=None, state=None):
    n, d = x2.shape
    d_ff = w_down_b.shape[0]
    tf = FFN_TF
    nf = d_ff // tf
    tm = min(FFN_TM, n)
    common_w = [
        pl.BlockSpec((1, d), lambda i, f: (0, 0)),
        pl.BlockSpec((d, tf), lambda i, f: (0, f)),
        pl.BlockSpec((d, tf), lambda i, f: (0, nf + f)),
        pl.BlockSpec((tf, d), lambda i, f: (f, 0)),
        pl.BlockSpec((CONV_WIDTH, tf), lambda i, f: (0, f)),
        pl.BlockSpec((1, tf), lambda i, f: (0, f)),
        pl.BlockSpec((1, d), lambda i, f: (0, 0)),
    ]
    w_args = (g_pre, w_up_b, w_up_b, w_down_b, w_conv, b_conv, g_post)
    row = lambda i, f: (i, 0)
    if state is None:
        seq_tiles = seq_len // tm
        halo_blocks = tm // HALO
        in_specs = [pl.BlockSpec((tm, d), row),
                    pl.BlockSpec((HALO, d), lambda i, f: (jnp.maximum(i * halo_blocks - 1, 0), 0))] + common_w
        args = (x2, x2) + w_args
        gt_shape = jax.ShapeDtypeStruct((n // tm * 8, d_ff), F32)
        gt_spec = pl.BlockSpec((8, tf), lambda i, f: (i, f))
        scratch = [pltpu.VMEM((tm, d), BF16), pltpu.VMEM((HALO, d), BF16), pltpu.VMEM((tm, d), F32)]
        kern = functools.partial(_ffn_kernel, seq_tiles=seq_tiles, n_tok=None)
    else:
        n_tok = n // state.shape[0]
        s0 = jnp.repeat(state[:, 0], n_tok, axis=0)
        s1 = jnp.repeat(state[:, 1], n_tok, axis=0)
        in_specs = [pl.BlockSpec((tm, d), row),
                    pl.BlockSpec((tm, tf), lambda i, f: (i, f)),
                    pl.BlockSpec((tm, tf), lambda i, f: (i, f))] + common_w
        args = (x2, s0, s1) + w_args
        gt_shape = jax.ShapeDtypeStruct((n, d_ff), F32)
        gt_spec = pl.BlockSpec((tm, tf), lambda i, f: (i, f))
        scratch = [pltpu.VMEM((tm, d), BF16), pltpu.VMEM((tm, d), F32)]
        kern = functools.partial(_ffn_kernel, seq_tiles=None, n_tok=n_tok)
    return pl.pallas_call(
        kern,
        grid=(n // tm, nf),
        in_specs=in_specs,
        out_specs=(pl.BlockSpec((tm, d), row), gt_spec),
        out_shape=(jax.ShapeDtypeStruct((n, d), F32), gt_shape),
        scratch_shapes=scratch,
        compiler_params=_cparams(("arbitrary", "arbitrary")),
        name="ffn",
    )(*args)


def _rope_tables(pos):
    half = HEAD_DIM // 2
    inv_freq = ROPE_THETA ** (-jnp.arange(half, dtype=F32) / half)
    ang = pos.astype(F32)[:, None] * inv_freq[None, :]
    cos, sin = jnp.cos(ang), jnp.sin(ang)
    return jnp.concatenate([cos, cos], axis=-1), jnp.concatenate([-sin, sin], axis=-1)


def _cmp_weights(w1, pe, w2):
    w1r = w1.reshape(CMP_RATIO, CMP_STRIDE * HEAD_DIM, HEAD_DIM)
    w1cat = jnp.concatenate([w1r[r] for r in range(CMP_RATIO)], axis=1).astype(BF16)
    pe8 = jnp.pad(pe.reshape(CMP_RATIO, CMP_STRIDE * HEAD_DIM), ((0, 8 - CMP_RATIO), (0, 0)))
    return w1cat, pe8, w2.astype(BF16)


def kernel(x_prompt, x_sample, cache_nsa_cmp_kv, cache_nsa_sel_kv, cache_nsa_win_kv, cache_fox_kv, cache_fox_logf, state_ffn_conv, page_table, g_pre_mix, w_in, b_fgt, w_cmp_k1, pe_cmp_k, w_cmp_k2, w_cmp_v1, pe_cmp_v, w_cmp_v2, w_nsa_o, w_fox_o, w_out, g_post_mix, g_pre_ffn, w_up, w_conv, b_conv, w_down, g_post_ffn):
    b_p, t_p, d = x_prompt.shape
    b_s, t_s, _ = x_sample.shape
    depth = w_in.shape[0]
    page = cache_nsa_cmp_kv.shape[2]
    past = page_table.shape[1] * page
    g_n, n_h = N_NSA_GROUPS, N_FOX_HEADS

    cos_p, sin_p = _rope_tables(jnp.tile(jnp.arange(t_p), b_p))
    cos_s, sin_s = _rope_tables(jnp.tile(past + jnp.arange(t_s), b_s))

    y_p = x_prompt.reshape(b_p * t_p, d)
    y_s = x_sample.reshape(b_s * t_s, d)
    outs = {k: [] for k in ('cmp_p', 'cmp_s', 'sel_p', 'sel_s', 'win_p', 'win_s',
                            'fox_p', 'fox_s', 'lf_p', 'lf_s', 'conv_p', 'conv_s')}
    o_q = NSA_Q_W
    o_g = o_q + 3 * KV_W
    o_f = o_g + N_GATE_COLS
    o_ff = o_f + 3 * FOX_W
    o_m = o_ff + n_h
    for l in range(depth):
        w = w_in[l]
        w_main = jnp.concatenate([w[:, :o_g], w[:, o_f:o_ff], w[:, o_m:]], axis=1).astype(BF16)
        w_small = jnp.concatenate([w[:, o_g:o_f], w[:, o_ff:o_m],
                                   jnp.zeros((d, LANES - N_GATE_COLS - n_h), F32)], axis=1).astype(BF16)
        bf_row = jnp.zeros((1, LANES), F32).at[0, LOGF_COL0:LOGF_COL0 + n_h].set(b_fgt[l])
        g1 = g_pre_mix[l][None, :]
        cmp_k = _cmp_weights(w_cmp_k1[l], pe_cmp_k[l], w_cmp_k2[l])
        cmp_v = _cmp_weights(w_cmp_v1[l], pe_cmp_v[l], w_cmp_v2[l])
        wn, wf, wo = w_nsa_o[l].astype(BF16), w_fox_o[l].astype(BF16), w_out[l].astype(BF16)
        wu, wd = w_up[l].astype(BF16), w_down[l].astype(BF16)
        ffn_w = (g_pre_ffn[l][None, :], wu, wd, w_conv[l], b_conv[l][None, :], g_post_ffn[l][None, :])

        (qn, kvc, kvs, kvw, qf, kvf, gm, sm, kvs_b, kvw_b, kvf_b) = _project(y_p, g1, cos_p, sin_p, w_main, w_small, bf_row)
        w1cat = jnp.stack([cmp_k[0], cmp_v[0]])
        pe8 = jnp.stack([cmp_k[1], cmp_v[1]])
        w2 = jnp.stack([cmp_k[2], cmp_v[2]])
        ckv = _compress_prompt(kvc, b_p, t_p, w1cat, pe8, w2)
        frow = _fcum_prompt(sm, b_p, t_p)
        o_n = _nsa_prompt(qn, sm, ckv, kvs_b, kvw_b, b_p, t_p)
        o_fx = _fox_prompt(qf, kvf_b, frow, b_p, t_p)
        y1 = _postmix(o_n, o_fx, gm, y_p, wn, wf, wo, g_post_mix[l][None, :])
        y_p, gt = _ffn(y1, *ffn_w, seq_len=t_p)
        n_win = min(WINDOW, t_p)
        outs['cmp_p'].append(kvc.reshape(b_p, t_p, 2, g_n, HEAD_DIM))
        outs['sel_p'].append(kvs.reshape(b_p, t_p, 2, g_n, HEAD_DIM))
        outs['win_p'].append(kvw.reshape(b_p, t_p, 2, g_n, HEAD_DIM)[:, t_p - n_win:])
        outs['fox_p'].append(kvf.reshape(b_p, t_p, 2, n_h, HEAD_DIM))
        outs['lf_p'].append(sm[:, LOGF_COL0:LOGF_COL0 + n_h].reshape(b_p, t_p, n_h))
        tiles_per_seq = t_p // FFN_TM
        gt = gt.reshape(b_p, tiles_per_seq, 8, -1)
        outs['conv_p'].append(gt[:, -1, 8 - (CONV_WIDTH - 1):])

        (qn, kvc, kvs, kvw, qf, kvf, gm, sm, _, _, _) = _project(y_s, g1, cos_s, sin_s, w_main, w_small, bf_row)
        win_buf = cache_nsa_win_kv[l]
        o_n = _nsa_sample(page_table, cache_nsa_cmp_kv[l], cache_nsa_sel_kv[l], win_buf, kvs, kvw, qn, sm,
                          cmp_k + cmp_v, t_s)
        o_fx = _fox_sample(page_table, cache_fox_kv[l], cache_fox_logf[l], kvf, qf, sm, t_s)
        o_n = o_n[:, :t_s].reshape(b_s * t_s, -1)
        y1 = _postmix(o_n, o_fx, gm, y_s, wn, wf, wo, g_post_mix[l][None, :])
        y_s, gt = _ffn(y1, *ffn_w, state=state_ffn_conv[l])
        kw_new = kvw.reshape(b_s, t_s, 2, g_n, HEAD_DIM)
        n_win = min(WINDOW, win_buf.shape[1] + t_s)
        outs['cmp_s'].append(kvc.reshape(b_s, t_s, 2, g_n, HEAD_DIM))
        outs['sel_s'].append(kvs.reshape(b_s, t_s, 2, g_n, HEAD_DIM))
        outs['win_s'].append(jnp.concatenate([win_buf, kw_new], axis=1)[:, -n_win:])
        outs['fox_s'].append(kvf.reshape(b_s, t_s, 2, n_h, HEAD_DIM))
        outs['lf_s'].append(sm[:, LOGF_COL0:LOGF_COL0 + n_h].reshape(b_s, t_s, n_h))
        gfull = jnp.concatenate([state_ffn_conv[l], gt.reshape(b_s, t_s, -1)], axis=1)
        outs['conv_s'].append(gfull[:, t_s:])

    st = {k: jnp.stack(v) for k, v in outs.items()}
    return (y_p.reshape(b_p, t_p, d), y_s.reshape(b_s, t_s, d),
            st['cmp_p'], st['cmp_s'], st['sel_p'], st['sel_s'], st['win_p'], st['win_s'],
            st['fox_p'], st['fox_s'], st['lf_p'], st['lf_s'], st['conv_p'], st['conv_s'])
```

```python
import functools

import numpy as np
import jax
import jax.numpy as jnp
from jax import lax
from jax.experimental import pallas as pl
from jax.experimental.pallas import tpu as pltpu

F32 = jnp.float32
BF16 = jnp.bfloat16

HEAD_DIM = 128
N_NSA_HEADS = 8
N_NSA_GROUPS = 2
NSA_HPG = N_NSA_HEADS // N_NSA_GROUPS
N_FOX_HEADS = 8
CMP_BLOCK = 32
CMP_STRIDE = 16
CMP_RATIO = CMP_BLOCK // CMP_STRIDE
SEL_BLOCK = 64
SEL_TOPK = 16
N_LOCAL_BLOCKS = 2
WINDOW = 512
CONV_WIDTH = 3
ROPE_THETA = 10000.0
RMS_EPS = 1e-6
FORCE_BONUS = 1e4
NEG_INF = -1e30
LOG2E = 1.4426950408889634
QK_SCALE = HEAD_DIM ** -0.5 * LOG2E

N_GATE_COLS = N_NSA_HEADS * 3
LOGF_COL0 = N_GATE_COLS
LANES = 128
VMEM_LIMIT = 56 * 1024 * 1024

PROJ_TN = 512
KV_W = 2 * N_NSA_GROUPS * HEAD_DIM
NSA_Q_W = N_NSA_HEADS * HEAD_DIM
FOX_W = N_FOX_HEADS * HEAD_DIM


def _cparams(sem):
    return pltpu.CompilerParams(dimension_semantics=sem, vmem_limit_bytes=VMEM_LIMIT)


def _dot(a, b):
    return jnp.dot(a, b, preferred_element_type=F32)


def _dot_nt(a, b):
    return lax.dot_general(a, b, (((1,), (1,)), ((), ())), preferred_element_type=F32)


def _rms(x, g):
    return x * lax.rsqrt(jnp.mean(x * x, axis=-1, keepdims=True) + RMS_EPS) * g


def _masked_softmax(s, mask):
    sm = jnp.where(mask, s, NEG_INF)
    m = jnp.max(sm, axis=-1, keepdims=True)
    e = jnp.where(mask, jnp.exp2(sm - m), 0.0)
    l = jnp.sum(e, axis=-1, keepdims=True)
    return e / jnp.where(l > 0.0, l, 1.0)


def _transpose_rows(src_ref, dst_ref, cols=slice(None)):
    for c in range(src_ref.shape[0] // LANES):
        rows = slice(c * LANES, (c + 1) * LANES)
        dst_ref[:, rows] = src_ref[rows, cols].astype(F32).T.astype(BF16)


def _masked_exp(s, mask):
    sm = jnp.where(mask, s, NEG_INF)
    e = jnp.where(mask, jnp.exp2(sm - jnp.max(sm, axis=-1, keepdims=True)), 0.0)
    l = jnp.sum(e, axis=-1, keepdims=True)
    return e, jnp.where(l > 0.0, l, 1.0)


def _lane_tile_max(mx, s):
    for c in range(s.shape[1] // LANES):
        mx = jnp.maximum(mx, s[:, c * LANES:(c + 1) * LANES])
    return mx


def _exp_accumulate(carry, s, m, v, lane_shift=0):
    ls, acc = carry
    p = jnp.exp2(s - m)
    tiles = [p[:, c * LANES:(c + 1) * LANES] for c in range(s.shape[1] // LANES)]
    for t in tiles:
        ls = ls + t
    if lane_shift:
        p = jnp.concatenate([pltpu.roll(t, lane_shift, axis=1) for t in tiles], axis=1)
    return ls, acc + _dot(p.astype(BF16), v)


def _split3(x):
    hi = x.astype(BF16)
    r = x - hi.astype(F32)
    mid = r.astype(BF16)
    lo = (r - mid.astype(F32)).astype(BF16)
    return hi, mid, lo


def _topk_mask(score, k, n_sel):
    st = score.T
    nv = -(-n_sel // 8)
    slabs = [st[8 * v:8 * v + 8, :] for v in range(nv)]
    sub = lax.broadcasted_iota(jnp.int32, (8, LANES), 0)
    ranks = [jnp.zeros((8, LANES), F32) for _ in range(nv)]
    for b2 in range(n_sel):
        row = jnp.broadcast_to(st[b2:b2 + 1, :], (8, LANES))
        for v in range(nv):
            if b2 < 8 * v:
                beats = row >= slabs[v]
            elif b2 >= 8 * v + 8:
                beats = row > slabs[v]
            else:
                beats = (row > slabs[v]) | ((row == slabs[v]) & (sub > b2 - 8 * v))
            ranks[v] = ranks[v] + jnp.where(beats, 1.0, 0.0)
    sel = [jnp.where((ranks[v] < k) & (sub + 8 * v < n_sel), 1.0, 0.0) for v in range(nv)]
    sel_t = jnp.concatenate(sel + [jnp.zeros((LANES - 8 * nv, LANES), F32)], axis=0)
    return sel_t.T


def _sel_scores(imp, tpos, n_sel):
    bidx = lax.broadcasted_iota(jnp.int32, imp.shape, 1)
    cur = jnp.right_shift(tpos, 6)
    valid = bidx <= cur
    forced = (bidx == 0) | (valid & (bidx > cur - N_LOCAL_BLOCKS))
    score = jnp.where(valid, jnp.where(forced, imp + FORCE_BONUS, imp), NEG_INF)
    return jnp.where(bidx < n_sel, score, -jnp.inf)


def _proj_kernel(x_ref, g_ref, cos_ref, sin_ref, w_ref, ws_ref, bf_ref,
                 qn_ref, kvc_ref, kvs_ref, kvw_ref, qf_ref, kvf_ref, gm_ref, sm_ref,
                 kvsb_ref, kvwb_ref, kvfb_ref, h_scr):
    j = pl.program_id(1)

    @pl.when(j == 0)
    def _():
        x = x_ref[...]
        y = x * lax.rsqrt(jnp.mean(x * x, axis=-1, keepdims=True) + RMS_EPS)
        h = (y * g_ref[...]).astype(BF16)
        h_scr[...] = h
        s = _dot(h, ws_ref[...])
        lane = lax.broadcasted_iota(jnp.int32, s.shape, 1)
        z = s + bf_ref[...]
        lf = jnp.minimum(z, 0.0) - jnp.log1p(jnp.exp(-jnp.abs(z)))
        sm_ref[...] = jnp.where(lane < N_GATE_COLS, jax.nn.sigmoid(s),
                                jnp.where(lane < LOGF_COL0 + N_FOX_HEADS, lf, 0.0))

    cos = cos_ref[...]
    sin = sin_ref[...]
    half_w = PROJ_TN // 2
    halves = [slice(0, half_w), slice(half_w, PROJ_TN)]

    def mm(cols):
        return _dot(h_scr[...], w_ref[:, cols])

    def rope2(a):
        return jnp.concatenate(
            [a[:, k * HEAD_DIM:(k + 1) * HEAD_DIM] * cos
             + pltpu.roll(a[:, k * HEAD_DIM:(k + 1) * HEAD_DIM], HEAD_DIM // 2, axis=1) * sin
             for k in range(half_w // HEAD_DIM)], axis=1)

    def kv_rows(ref, bref):
        assert half_w == N_NSA_GROUPS * HEAD_DIM
        for cols, is_key in zip(halves, (True, False)):
            a = mm(cols)
            a = rope2(a) if is_key else a
            ref[:, cols] = a
            if bref is not None:
                bref[:, cols] = a.astype(BF16)

    @pl.when(j < 2)
    def _():
        for cols in halves:
            qn_ref[:, cols] = (rope2(mm(cols)) * QK_SCALE).astype(BF16)

    @pl.when(j == 2)
    def _():
        kv_rows(kvc_ref, None)

    @pl.when(j == 3)
    def _():
        kv_rows(kvs_ref, kvsb_ref)

    @pl.when(j == 4)
    def _():
        kv_rows(kvw_ref, kvwb_ref)

    @pl.when((j >= 5) & (j < 7))
    def _():
        for cols in halves:
            qf_ref[:, cols] = (mm(cols) * QK_SCALE).astype(BF16)

    @pl.when((j >= 7) & (j < 11))
    def _():
        for cols in halves:
            a = mm(cols)
            kvf_ref[:, cols] = a
            kvfb_ref[:, cols] = a.astype(BF16)

    @pl.when(j >= 11)
    def _():
        for cols in halves:
            gm_ref[:, cols] = jax.nn.sigmoid(mm(cols))


def _project(x2, g, cos2, sin2, w_main, w_small, bf_row):
    n, d = x2.shape
    tm = 512
    n_j = w_main.shape[1] // PROJ_TN
    tn = PROJ_TN

    def clip(lo, hi):
        return lambda i, j: (i, jnp.clip(j - lo, 0, hi - lo))

    row = lambda i, j: (i, 0)
    out_shape = (
        jax.ShapeDtypeStruct((n, NSA_Q_W), BF16),
        jax.ShapeDtypeStruct((n, KV_W), F32),
        jax.ShapeDtypeStruct((n, KV_W), F32),
        jax.ShapeDtypeStruct((n, KV_W), F32),
        jax.ShapeDtypeStruct((n, FOX_W), BF16),
        jax.ShapeDtypeStruct((n, 2 * FOX_W), F32),
        jax.ShapeDtypeStruct((n, 2 * d), F32),
        jax.ShapeDtypeStruct((n, LANES), F32),
        jax.ShapeDtypeStruct((n, KV_W), BF16),
        jax.ShapeDtypeStruct((n, KV_W), BF16),
        jax.ShapeDtypeStruct((n, 2 * FOX_W), BF16),
    )
    out_specs = (
        pl.BlockSpec((tm, tn), clip(0, 1)),
        pl.BlockSpec((tm, tn), row),
        pl.BlockSpec((tm, tn), row),
        pl.BlockSpec((tm, tn), row),
        pl.BlockSpec((tm, tn), clip(5, 6)),
        pl.BlockSpec((tm, tn), clip(7, 10)),
        pl.BlockSpec((tm, tn), clip(11, 18)),
        pl.BlockSpec((tm, LANES), row),
        pl.BlockSpec((tm, tn), row),
        pl.BlockSpec((tm, tn), row),
        pl.BlockSpec((tm, tn), clip(7, 10)),
    )
    in_specs = [
        pl.BlockSpec((tm, d), row),
        pl.BlockSpec((1, d), lambda i, j: (0, 0)),
        pl.BlockSpec((tm, LANES), row),
        pl.BlockSpec((tm, LANES), row),
        pl.BlockSpec((d, tn), lambda i, j: (0, j)),
        pl.BlockSpec((d, LANES), lambda i, j: (0, 0)),
        pl.BlockSpec((1, LANES), lambda i, j: (0, 0)),
    ]
    return pl.pallas_call(
        _proj_kernel,
        grid=(n // tm, n_j),
        in_specs=in_specs,
        out_specs=out_specs,
        out_shape=out_shape,
        scratch_shapes=[pltpu.VMEM((tm, d), BF16)],
        compiler_params=_cparams(("arbitrary", "arbitrary")),
        name="proj",
    )(x2, g, cos2, sin2, w_main, w_small, bf_row)


def _compress_tail(xc, w1, pe8, w2):
    n = xc.shape[0]
    part = _dot(xc, w1)
    pp = _dot(pe8.astype(BF16), w1)
    pe_term = pp[0:1, :HEAD_DIM] + pp[1:2, HEAD_DIM:]
    hid = pe_term + part[:, :HEAD_DIM] + pltpu.roll(part[:, HEAD_DIM:], n - 1, axis=0)
    return _dot(jax.nn.gelu(hid, approximate=True).astype(BF16), w2)


def _cmp_prompt_kernel(x_ref, w1_ref, pe_ref, w2_ref, o_ref):
    n = x_ref.shape[0] // CMP_STRIDE
    xc = jnp.concatenate([x_ref[pl.ds(i, n, stride=CMP_STRIDE), :] for i in range(CMP_STRIDE)],
                         axis=1).astype(BF16)
    o_ref[0, 0] = _compress_tail(xc, w1_ref[0], pe_ref[0], w2_ref[0]).astype(BF16)


def _compress_prompt(kvc, b_n, t_n, w1cat, pe8, w2):
    n = t_n // CMP_STRIDE
    return pl.pallas_call(
        _cmp_prompt_kernel,
        grid=(b_n, 2 * N_NSA_GROUPS),
        in_specs=[
            pl.BlockSpec((t_n, HEAD_DIM), lambda b, s: (b, s)),
            pl.BlockSpec((1, CMP_STRIDE * HEAD_DIM, 2 * HEAD_DIM), lambda b, s: (s // N_NSA_GROUPS, 0, 0)),
            pl.BlockSpec((1, 8, CMP_STRIDE * HEAD_DIM), lambda b, s: (s // N_NSA_GROUPS, 0, 0)),
            pl.BlockSpec((1, HEAD_DIM, HEAD_DIM), lambda b, s: (s // N_NSA_GROUPS, 0, 0)),
        ],
        out_specs=pl.BlockSpec((1, 1, n, HEAD_DIM), lambda b, s: (b, s, 0, 0)),
        out_shape=jax.ShapeDtypeStruct((b_n, 2 * N_NSA_GROUPS, n, HEAD_DIM), BF16),
        compiler_params=_cparams(("arbitrary", "arbitrary")),
        name="cmp_prompt",
    )(kvc, w1cat, pe8, w2)


def _fcum_kernel(x_ref, frow_ref, carry_scr):
    i = pl.program_id(1)

    @pl.when(i == 0)
    def _():
        carry_scr[...] = jnp.zeros_like(carry_scr)

    x = x_ref[...]
    tb = x.shape[0]
    r = lax.broadcasted_iota(jnp.int32, (tb, tb), 0)
    c = lax.broadcasted_iota(jnp.int32, (tb, tb), 1)
    tri = jnp.where(r >= c, 1.0, 0.0).astype(BF16)
    hi, mid, lo = _split3(x)
    cs = _dot(tri, hi) + _dot(tri, mid) + _dot(tri, lo) + carry_scr[0:1, :]
    carry_scr[...] = jnp.broadcast_to(cs[tb - 1:tb, :], carry_scr.shape)
    frow_ref[0] = cs.T[LOGF_COL0:LOGF_COL0 + N_FOX_HEADS, :] * LOG2E


def _fcum_prompt(sm, b_n, t_n):
    tb = 512
    nb = t_n // tb
    return pl.pallas_call(
        _fcum_kernel,
        grid=(b_n, nb),
        in_specs=[pl.BlockSpec((tb, LANES), lambda b, i: (b * nb + i, 0))],
        out_specs=pl.BlockSpec((1, N_FOX_HEADS, tb), lambda b, i: (b, 0, i)),
        out_shape=jax.ShapeDtypeStruct((b_n, N_FOX_HEADS, t_n), F32),
        scratch_shapes=[pltpu.VMEM((8, LANES), F32)],
        compiler_params=_cparams(("arbitrary", "arbitrary")),
        name="fcum_prompt",
    )(sm)


NSA_TQ = 128
NSA_TK = 256
NSA_NSUB = 2


def _tile_rows(a, reps):
    return jnp.concatenate([a] * reps, axis=0)


def _nsa_front_kernel(q_ref, sm_ref, ck_ref, kw_ref, cover_ref, selm_ref, ocw_ref, kwt_scr, *, n_cmp, n_sel):
    i = pl.program_id(1)
    tq, nsub = NSA_TQ, NSA_NSUB
    t0 = i * (tq * nsub)
    n_ck = ck_ref.shape[2]
    band = WINDOW + tq
    tpos = [t0 + u * tq + lax.broadcasted_iota(jnp.int32, (tq, 1), 0) for u in range(nsub)]

    @pl.when(i == 0)
    def _():
        for g in range(N_NSA_GROUPS):
            _transpose_rows(kw_ref, kwt_scr.at[g], slice(g * HEAD_DIM, (g + 1) * HEAD_DIM))

    for g in range(N_NSA_GROUPS):
        ck = ck_ref[0, g]
        cv = ck_ref[0, N_NSA_GROUPS + g]
        o_cs, o_ws, selms = [], [], []
        for u in range(nsub):
            q = jnp.concatenate([q_ref[u * tq:(u + 1) * tq, (g * NSA_HPG + h) * HEAD_DIM:(g * NSA_HPG + h + 1) * HEAD_DIM]
                                 for h in range(NSA_HPG)], axis=0)
            s_c = _dot_nt(q, ck)
            cidx = lax.broadcasted_iota(jnp.int32, (tq, n_ck), 1)
            mc = jnp.where((cidx * CMP_STRIDE + CMP_BLOCK - 1 <= tpos[u]) & (cidx < n_cmp), 1.0, 0.0)
            e_c, l_c = _masked_exp(s_c, _tile_rows(mc, NSA_HPG) > 0.5)
            e_cb = e_c.astype(BF16)
            imp4 = _dot(e_cb, cover_ref[...]) / l_c
            imp = imp4[0:tq] + imp4[tq:2 * tq] + imp4[2 * tq:3 * tq] + imp4[3 * tq:4 * tq]
            selms.append(_topk_mask(_sel_scores(imp, tpos[u], n_sel), min(SEL_TOPK, n_sel), n_sel).astype(BF16))
            o_cs.append(_dot(e_cb, cv) / l_c)
            w0 = pl.multiple_of(jnp.maximum(t0 + u * tq - WINDOW, 0), tq)
            vw = kw_ref[pl.ds(w0, band), (N_NSA_GROUPS + g) * HEAD_DIM:(N_NSA_GROUPS + g + 1) * HEAD_DIM]
            s_w = _dot(q, kwt_scr[g, :, pl.ds(w0, band)])
            wpos = w0 + lax.broadcasted_iota(jnp.int32, (tq, band), 1)
            bw = jnp.where((wpos <= tpos[u]) & (wpos > tpos[u] - WINDOW), 0.0, NEG_INF)
            s_w = s_w + _tile_rows(bw, NSA_HPG)
            e_w = jnp.exp2(s_w - jnp.max(s_w, axis=-1, keepdims=True))
            o_ws.append(_dot(e_w.astype(BF16), vw) / jnp.sum(e_w, axis=-1, keepdims=True))

        for u in range(nsub):
            rows_u = slice(u * tq, (u + 1) * tq)
            selm_ref[rows_u, g * LANES:(g + 1) * LANES] = selms[u]
            gates = sm_ref[rows_u, :]
            for h in range(NSA_HPG):
                hh = g * NSA_HPG + h
                rows = slice(h * tq, (h + 1) * tq)
                ocw_ref[rows_u, hh * HEAD_DIM:(hh + 1) * HEAD_DIM] = (
                    gates[:, 3 * hh:3 * hh + 1] * o_cs[u][rows] + gates[:, 3 * hh + 2:3 * hh + 3] * o_ws[u][rows])


def _cover_matrix(n_rows, n_cmp, n_sel):
    c = np.arange(n_rows)[:, None] * CMP_STRIDE
    b = np.arange(LANES)[None, :] * SEL_BLOCK
    m = (c < b + SEL_BLOCK) & (c + CMP_BLOCK > b) & (np.arange(n_rows)[:, None] < n_cmp) & (np.arange(LANES)[None, :] < n_sel)
    return jnp.asarray(m.astype(np.float32), dtype=BF16)


def _expand_matrix(n_keys):
    m = (np.arange(n_keys)[None, :] // SEL_BLOCK) == np.arange(LANES)[:, None]
    return jnp.asarray(m.astype(np.float32), dtype=BF16)


def _nsa_sel_kernel(q_ref, k_ref, v_ref, selm_ref, sm_ref, ocw_ref, expand_ref, o_ref, s_scr, mask_scr, kt_scr):
    g = pl.program_id(1)
    blk = FOX_BLK
    _transpose_rows(k_ref, kt_scr)
    causal = lax.broadcasted_iota(jnp.int32, (blk, blk), 1) <= lax.broadcasted_iota(jnp.int32, (blk, 1), 0)
    zeros = jnp.zeros((blk, LANES), F32)
    lane = lax.broadcasted_iota(jnp.int32, (blk, LANES), 1)
    for qb in range(q_ref.shape[0] // blk):
        rows = slice(qb * blk, (qb + 1) * blk)
        selm = selm_ref[rows, :]
        for kt in range(qb + 1):
            keys = slice(kt * blk, (kt + 1) * blk)
            picked = _dot(selm, expand_ref[:, keys]) > 0.5
            if kt == qb:
                picked = picked & causal
            mask_scr[:, keys] = jnp.where(picked, 0.0, NEG_INF)
        gates = sm_ref[rows, :]

        def head(h, _, qb=qb, rows=rows, gates=gates):
            hcols = pl.ds(pl.multiple_of(h * HEAD_DIM, HEAD_DIM), HEAD_DIM)
            q = q_ref[rows, hcols]
            mx = jnp.full((blk, LANES), -jnp.inf, F32)
            for kt in range(qb + 1):
                keys = slice(kt * blk, (kt + 1) * blk)
                s = _dot(q, kt_scr[:, keys]) + mask_scr[:, keys]
                s_scr[:, keys] = s
                mx = _lane_tile_max(mx, s)
            m = jnp.max(mx, axis=-1, keepdims=True)
            carry = (zeros, zeros)
            for kt in range(qb + 1):
                keys = slice(kt * blk, (kt + 1) * blk)
                carry = _exp_accumulate(carry, s_scr[:, keys], m, v_ref[keys, :])
            ls, acc = carry
            gate = jnp.sum(jnp.where(lane == 3 * (g * NSA_HPG + h) + 1, gates, 0.0), axis=-1, keepdims=True)
            out = ocw_ref[rows, hcols] + gate * (acc / jnp.sum(ls, axis=-1, keepdims=True))
            o_ref[rows, hcols] = out.astype(BF16)
            return 0

        lax.fori_loop(0, NSA_HPG, head, 0)


def _nsa_prompt(qn, sm, ckv, kvs_b, kvw_b, b_n, t_n):
    blk = NSA_TQ * NSA_NSUB
    nq = t_n // blk
    n_cmp = t_n // CMP_STRIDE - CMP_RATIO + 1
    n_sel = t_n // SEL_BLOCK
    n_ck = ckv.shape[2]
    n_g = N_NSA_GROUPS
    cover = _cover_matrix(n_ck, n_cmp, n_sel)
    expand = _expand_matrix(t_n)
    selm, ocw = pl.pallas_call(
        functools.partial(_nsa_front_kernel, n_cmp=n_cmp, n_sel=n_sel),
        grid=(b_n, nq),
        in_specs=[
            pl.BlockSpec((blk, NSA_Q_W), lambda b, i: (b * nq + i, 0)),
            pl.BlockSpec((blk, LANES), lambda b, i: (b * nq + i, 0)),
            pl.BlockSpec((1, 2 * n_g, n_ck, HEAD_DIM), lambda b, i: (b, 0, 0, 0)),
            pl.BlockSpec((t_n, KV_W), lambda b, i: (b, 0)),
            pl.BlockSpec((n_ck, LANES), lambda b, i: (0, 0)),
        ],
        out_specs=(pl.BlockSpec((blk, n_g * LANES), lambda b, i: (b * nq + i, 0)),
                   pl.BlockSpec((blk, NSA_Q_W), lambda b, i: (b * nq + i, 0))),
        out_shape=(jax.ShapeDtypeStruct((b_n * t_n, n_g * LANES), BF16),
                   jax.ShapeDtypeStruct((b_n * t_n, NSA_Q_W), F32)),
        scratch_shapes=[pltpu.VMEM((n_g, HEAD_DIM, t_n), BF16)],
        compiler_params=_cparams(("arbitrary", "arbitrary")),
        name="nsa_front",
    )(qn, sm, ckv, kvw_b, cover)

    once = pl.Buffered(1)
    gw = NSA_HPG * HEAD_DIM
    return pl.pallas_call(
        _nsa_sel_kernel,
        grid=(b_n, n_g),
        in_specs=[
            pl.BlockSpec((t_n, gw), lambda b, g: (b, g), pipeline_mode=once),
            pl.BlockSpec((t_n, HEAD_DIM), lambda b, g: (b, g)),
            pl.BlockSpec((t_n, HEAD_DIM), lambda b, g: (b, n_g + g)),
            pl.BlockSpec((t_n, LANES), lambda b, g: (b, g), pipeline_mode=once),
            pl.BlockSpec((t_n, LANES), lambda b, g: (b, 0), pipeline_mode=once),
            pl.BlockSpec((t_n, gw), lambda b, g: (b, g), pipeline_mode=once),
            pl.BlockSpec((LANES, t_n), lambda b, g: (0, 0), pipeline_mode=once),
        ],
        out_specs=pl.BlockSpec((t_n, gw), lambda b, g: (b, g), pipeline_mode=once),
        out_shape=jax.ShapeDtypeStruct((b_n * t_n, NSA_Q_W), BF16),
        scratch_shapes=[pltpu.VMEM((FOX_BLK, t_n), F32), pltpu.VMEM((FOX_BLK, t_n), F32),
                        pltpu.VMEM((HEAD_DIM, t_n), BF16)],
        compiler_params=_cparams(("arbitrary", "arbitrary")),
        name="nsa_sel",
    )(qn, kvs_b, kvs_b, selm, sm, ocw, expand)


FOX_BLK = 512


def _fox_prompt_kernel(q_ref, k_ref, v_ref, frow_ref, o_ref, s_scr, kt_scr):
    h = pl.program_id(1)
    blk = FOX_BLK
    assert q_ref.shape[0] % blk == 0
    _transpose_rows(k_ref, kt_scr)

    causal = lax.broadcasted_iota(jnp.int32, (blk, blk), 1) <= lax.broadcasted_iota(jnp.int32, (blk, 1), 0)
    zeros = jnp.zeros((blk, LANES), F32)
    for qb in range(q_ref.shape[0] // blk):
        q = q_ref[qb * blk:(qb + 1) * blk, :]
        mx = jnp.full((blk, LANES), -jnp.inf, F32)
        for kt in range(qb + 1):
            keys = slice(kt * blk, (kt + 1) * blk)
            s = _dot(q, kt_scr[:, keys]) - frow_ref[0, pl.ds(h, 1), keys]
            if kt == qb:
                s = jnp.where(causal, s, NEG_INF)
            s_scr[:, keys] = s
            mx = _lane_tile_max(mx, s)
        m = jnp.max(mx, axis=-1, keepdims=True)
        carry = (zeros, zeros)
        for kt in range(qb + 1):
            keys = slice(kt * blk, (kt + 1) * blk)
            carry = _exp_accumulate(carry, s_scr[:, keys], m, v_ref[keys, :])
        ls, acc = carry
        o_ref[qb * blk:(qb + 1) * blk, :] = (acc / jnp.sum(ls, axis=-1, keepdims=True)).astype(BF16)


def _fox_prompt(qf, kvf_b, frow, b_n, t_n):
    nh = N_FOX_HEADS
    return pl.pallas_call(
        _fox_prompt_kernel,
        grid=(b_n, nh),
        in_specs=[
            pl.BlockSpec((t_n, HEAD_DIM), lambda b, h: (b, h)),
            pl.BlockSpec((t_n, HEAD_DIM), lambda b, h: (b, h)),
            pl.BlockSpec((t_n, HEAD_DIM), lambda b, h: (b, nh + h)),
            pl.BlockSpec((1, nh, t_n), lambda b, h: (b, 0, 0)),
        ],
        out_specs=pl.BlockSpec((t_n, HEAD_DIM), lambda b, h: (b, h)),
        out_shape=jax.ShapeDtypeStruct((b_n * t_n, FOX_W), BF16),
        scratch_shapes=[pltpu.VMEM((FOX_BLK, t_n), F32), pltpu.VMEM((HEAD_DIM, t_n), BF16)],
        compiler_params=_cparams(("arbitrary", "arbitrary")),
        name="fox_prompt",
    )(qf, kvf_b, kvf_b, frow)


FOX_SAMPLE_CH = 512
TOK_PAD = 8
NEW_PAD = 128
NSA_SAMPLE_SUB = 1


def _nsa_sample_kernel(pt_ref, *refs, n_pages, page, n_buf, n_tok, n_sub):
    del pt_ref
    per_seq = 2 * n_pages + 1
    kvs_new_ref, kvw_new_ref, q_ref, sm_ref = refs[n_sub * per_seq:n_sub * per_seq + 4]
    for sq in range(n_sub):
        seq_refs = refs[sq * per_seq:(sq + 1) * per_seq]
        _nsa_sample_one(seq_refs[:n_pages], seq_refs[n_pages:2 * n_pages], seq_refs[2 * n_pages],
                        kvs_new_ref[sq], kvw_new_ref[sq], q_ref.at[sq], sm_ref[sq],
                        *refs[n_sub * per_seq + 4:-1], refs[-1].at[sq],
                        n_pages=n_pages, page=page, n_buf=n_buf, n_tok=n_tok)


def _nsa_sample_one(cmp_pages, sel_pages, win_ref, kvs_new, kvw_new, q_ref, gates, w1k_ref, w1v_ref, pek_ref, pev_ref,
                    w2k_ref, w2v_ref, cover_ref, expand_ref, o_ref, *, n_pages, page, n_buf, n_tok):
    past = n_pages * page
    n_slab = 2 * N_NSA_GROUPS
    chunks_per_page = page // CMP_STRIDE
    n_chunk = n_pages * chunks_per_page
    n_cmp = (past + n_tok + CMP_STRIDE - 1) // CMP_STRIDE - CMP_RATIO + 1
    n_sel = (past + n_tok + SEL_BLOCK - 1) // SEL_BLOCK
    tp = TOK_PAD
    tpos = past + lax.broadcasted_iota(jnp.int32, (tp, 1), 0)

    rows_pc = CMP_STRIDE * n_slab
    swapped = [jnp.swapaxes(cmp_pages[p][...].reshape(chunks_per_page, rows_pc, HEAD_DIM), 0, 1)
               for p in range(n_pages)]

    def compress(slab, w1_ref, pe_ref, w2_ref):
        cols = []
        for i in range(CMP_STRIDE):
            cols.append(jnp.concatenate([swapped[p][i * n_slab + slab] for p in range(n_pages)], axis=0))
        xc = jnp.concatenate(cols, axis=1).astype(BF16)
        return _compress_tail(xc, w1_ref[...], pe_ref[...], w2_ref[...]).astype(BF16)

    qs, o_cs, scores = [], [], []
    for g in range(N_NSA_GROUPS):
        q = jnp.concatenate([q_ref[:, (g * NSA_HPG + h) * HEAD_DIM:(g * NSA_HPG + h + 1) * HEAD_DIM]
                             for h in range(NSA_HPG)], axis=0)
        ck = compress(g, w1k_ref, pek_ref, w2k_ref)
        cv = compress(N_NSA_GROUPS + g, w1v_ref, pev_ref, w2v_ref)
        s_c = _dot_nt(q, ck)
        cidx = lax.broadcasted_iota(jnp.int32, (tp, n_chunk), 1)
        mc = jnp.where((cidx * CMP_STRIDE + CMP_BLOCK - 1 <= tpos) & (cidx < n_cmp), 1.0, 0.0)
        p_c = _masked_softmax(s_c, _tile_rows(mc, NSA_HPG) > 0.5)
        p_cb = p_c.astype(BF16)
        imp4 = _dot(p_cb, cover_ref[...])
        imp = imp4[0:tp] + imp4[tp:2 * tp] + imp4[2 * tp:3 * tp] + imp4[3 * tp:4 * tp]
        qs.append(q)
        o_cs.append(_dot(p_cb, cv))
        scores.append(_sel_scores(imp, tpos, n_sel))

    score_all = jnp.concatenate(scores + [jnp.zeros((LANES - N_NSA_GROUPS * tp, LANES), F32)], axis=0)
    selm_all = _topk_mask(score_all, min(SEL_TOPK, n_sel), n_sel)

    def sel_rows(slab):
        return jnp.concatenate([sel_pages[p][pl.ds(slab, page, stride=n_slab), :] for p in range(n_pages)], axis=0)

    def with_new(cached, new):
        pad = jnp.zeros((NEW_PAD - new.shape[0], new.shape[1]), new.dtype)
        return jnp.concatenate([cached, new, pad], axis=0).astype(BF16)

    for g in range(N_NSA_GROUPS):
        q, o_c = qs[g], o_cs[g]
        selm = selm_all[g * tp:(g + 1) * tp].astype(BF16)
        kc = slice(g * HEAD_DIM, (g + 1) * HEAD_DIM)
        vc = slice((N_NSA_GROUPS + g) * HEAD_DIM, (N_NSA_GROUPS + g + 1) * HEAD_DIM)

        n_keys = past + NEW_PAD
        s_s = _dot_nt(q, with_new(sel_rows(g), kvs_new[:, kc]))
        selx = _dot(selm, expand_ref[...])
        kpos = lax.broadcasted_iota(jnp.int32, (tp, n_keys), 1)
        ms = jnp.where((selx > 0.5) & (kpos <= tpos) & (kpos < past + n_tok), 1.0, 0.0)
        p_s = _masked_softmax(s_s, _tile_rows(ms, NSA_HPG) > 0.5)
        o_s = _dot(p_s.astype(BF16), with_new(sel_rows(N_NSA_GROUPS + g), kvs_new[:, vc]))

        s_w = _dot_nt(q, with_new(win_ref[pl.ds(g, n_buf, stride=n_slab), :], kvw_new[:, kc]))
        wpos = past - n_buf + lax.broadcasted_iota(jnp.int32, (tp, n_buf + NEW_PAD), 1)
        mw = jnp.where((wpos <= tpos) & (wpos > tpos - WINDOW) & (wpos < past + n_tok), 1.0, 0.0)
        p_w = _masked_softmax(s_w, _tile_rows(mw, NSA_HPG) > 0.5)
        o_w = _dot(p_w.astype(BF16), with_new(win_ref[pl.ds(N_NSA_GROUPS + g, n_buf, stride=n_slab), :],
                                              kvw_new[:, vc]))

        for h in range(NSA_HPG):
            hh = g * NSA_HPG + h
            rows = slice(h * tp, (h + 1) * tp)
            out = (gates[:, 3 * hh:3 * hh + 1] * o_c[rows] + gates[:, 3 * hh + 1:3 * hh + 2] * o_s[rows]
                   + gates[:, 3 * hh + 2:3 * hh + 3] * o_w[rows])
            o_ref[:, hh * HEAD_DIM:(hh + 1) * HEAD_DIM] = out.astype(BF16)


def _pad_tokens(a, n_seq, n_tok):
    a = a.reshape(n_seq, n_tok, a.shape[-1])
    return jnp.pad(a, ((0, 0), (0, TOK_PAD - n_tok), (0, 0)))


def _nsa_sample(page_table, cache_cmp, cache_sel, win_buf, kvs_new, kvw_new, qn, sm, cmp_w, n_tok):
    n_seq, n_pages = page_table.shape
    page = cache_cmp.shape[1]
    n_slab = 2 * N_NSA_GROUPS
    n_buf = win_buf.shape[1]
    past = n_pages * page
    cmp2 = cache_cmp.reshape(-1, HEAD_DIM)
    sel2 = cache_sel.reshape(-1, HEAD_DIM)
    win2 = win_buf.reshape(-1, HEAD_DIM)
    n_chunk = past // CMP_STRIDE
    n_cmp = (past + n_tok + CMP_STRIDE - 1) // CMP_STRIDE - CMP_RATIO + 1
    n_sel = (past + n_tok + SEL_BLOCK - 1) // SEL_BLOCK
    cover = _cover_matrix(n_chunk, min(n_cmp, n_chunk), n_sel)
    expand = _expand_matrix(past + NEW_PAD)
    w1k, pek, w2k, w1v, pev, w2v = cmp_w

    n_sub = NSA_SAMPLE_SUB
    assert n_seq % n_sub == 0

    def page_spec(sq, p):
        return pl.BlockSpec((page * n_slab, HEAD_DIM), lambda b, pt, sq=sq, p=p: (pt[b * n_sub + sq, p], 0))

    const2 = lambda b, pt: (0, 0)
    seq3 = lambda b, pt: (b, 0, 0)
    per_seq = []
    for sq in range(n_sub):
        per_seq += ([page_spec(sq, p) for p in range(n_pages)] + [page_spec(sq, p) for p in range(n_pages)]
                    + [pl.BlockSpec((n_buf * n_slab, HEAD_DIM), lambda b, pt, sq=sq: (b * n_sub + sq, 0))])
    in_specs = (per_seq + [
        pl.BlockSpec((n_sub, TOK_PAD, KV_W), seq3),
        pl.BlockSpec((n_sub, TOK_PAD, KV_W), seq3),
        pl.BlockSpec((n_sub, TOK_PAD, NSA_Q_W), seq3),
        pl.BlockSpec((n_sub, TOK_PAD, LANES), seq3),
        pl.BlockSpec(w1k.shape, const2),
        pl.BlockSpec(w1v.shape, const2),
        pl.BlockSpec(pek.shape, const2),
        pl.BlockSpec(pev.shape, const2),
        pl.BlockSpec(w2k.shape, const2),
        pl.BlockSpec(w2v.shape, const2),
        pl.BlockSpec(cover.shape, const2),
        pl.BlockSpec(expand.shape, const2),
    ])
    kern = functools.partial(_nsa_sample_kernel, n_pages=n_pages, page=page, n_buf=n_buf, n_tok=n_tok, n_sub=n_sub)
    grid_spec = pltpu.PrefetchScalarGridSpec(
        num_scalar_prefetch=1, grid=(n_seq // n_sub,), in_specs=in_specs,
        out_specs=pl.BlockSpec((n_sub, TOK_PAD, NSA_Q_W), seq3))
    return pl.pallas_call(
        kern,
        grid_spec=grid_spec,
        out_shape=jax.ShapeDtypeStruct((n_seq, TOK_PAD, NSA_Q_W), BF16),
        compiler_params=_cparams(("arbitrary",)),
        name="nsa_sample",
    )(page_table, *(([cmp2] * n_pages + [sel2] * n_pages + [win2]) * n_sub),
      _pad_tokens(kvs_new, n_seq, n_tok), _pad_tokens(kvw_new, n_seq, n_tok),
      _pad_tokens(qn, n_seq, n_tok), _pad_tokens(sm, n_seq, n_tok),
      w1k, w1v, pek, pev, w2k, w2v, cover, expand)


def _fox_sample_kernel(pt_ref, *refs, n_pages, page, n_tok):
    kv_pages = refs[:n_pages]
    lf_pages = refs[n_pages:2 * n_pages]
    k_new_ref, v_new_ref, q_ref, lfn_ref, o_ref, mask_scr, s_scr, kb_scr = refs[2 * n_pages:]
    del pt_ref
    nh = N_FOX_HEADS
    rows_pp = 2 * nh
    page_rows = page * rows_pp
    n_chunk = page_rows // LANES
    n_q = nh * n_tok
    assert rows_pp == 16 and LANES % rows_pp == 0 and n_q <= LANES

    @pl.when(pl.program_id(0) == 0)
    def _():
        qrow = lax.broadcasted_iota(jnp.int32, mask_scr.shape, 0)
        lane = lax.broadcasted_iota(jnp.int32, mask_scr.shape, 1)
        mask_scr[...] = jnp.where((lane & (rows_pp - 1)) == (qrow & (nh - 1)), 0.0, NEG_INF)

    x = jnp.concatenate([lf_pages[p][0] for p in range(n_pages)], axis=0)
    n_r = x.shape[0]
    la = lax.broadcasted_iota(jnp.int32, (LANES, LANES), 0)
    lb = lax.broadcasted_iota(jnp.int32, (LANES, LANES), 1)
    same = (la & (rows_pp - 1)) == (lb & (rows_pp - 1))
    u_in = jnp.where(same & (jnp.right_shift(la, 4) <= jnp.right_shift(lb, 4)), 1.0, 0.0).astype(BF16)
    u_all = jnp.where(same, 1.0, 0.0).astype(BF16)
    xh, xm, xl = _split3(x)
    within = _dot(xh, u_in) + _dot(xm, u_in) + _dot(xl, u_in)
    tot = _dot(xh, u_all) + _dot(xm, u_all) + _dot(xl, u_all)
    ra = lax.broadcasted_iota(jnp.int32, (n_r, n_r), 0)
    rb = lax.broadcasted_iota(jnp.int32, (n_r, n_r), 1)
    before = jnp.where(rb < ra, 1.0, 0.0).astype(BF16)
    th, tm_, tl = _split3(tot)
    offs = _dot(before, th) + _dot(before, tm_) + _dot(before, tl)
    f_end = offs[n_r - 1:n_r, :] + tot[n_r - 1:n_r, :]
    bias = (f_end - (within + offs)) * LOG2E

    q_all = q_ref[0]

    ch = mask_scr.shape[1]
    lt = ch // LANES
    steps = [(p, c) for p in range(n_pages) for c in range(page_rows // ch)]
    mx = jnp.full((n_q, LANES), -jnp.inf, F32)
    for p, c in steps:
        k_b = kv_pages[p][pl.ds(c * ch, ch), :].astype(BF16)
        kb_scr[p * page_rows + c * ch:p * page_rows + (c + 1) * ch, :] = k_b
        r0 = p * n_chunk + c * lt
        brow = jnp.concatenate([bias[r0 + i:r0 + i + 1, :] for i in range(lt)], axis=1)
        s = _dot_nt(q_all, k_b) + brow + mask_scr[...]
        s_scr[:, p * page_rows + c * ch:p * page_rows + (c + 1) * ch] = s
        mx = _lane_tile_max(mx, s)

    pad = jnp.zeros((LANES - n_q, HEAD_DIM), F32)
    k_new = jnp.concatenate([k_new_ref[0], pad], axis=0).astype(BF16)
    v_new = jnp.concatenate([v_new_ref[0], pad], axis=0).astype(BF16)
    g_in = jnp.where(((la & (nh - 1)) == (lb & (nh - 1))) & (la <= lb), 1.0, 0.0).astype(BF16)
    nh_, nm_, nl_ = _split3(lfn_ref[0])
    c_new = (_dot(nh_, g_in) + _dot(nm_, g_in) + _dot(nl_, g_in))[0:1, :] * LOG2E
    qrow = lax.broadcasted_iota(jnp.int32, (n_q, LANES), 0)
    lane = lax.broadcasted_iota(jnp.int32, (n_q, LANES), 1)
    ok = ((lane & (nh - 1)) == (qrow & (nh - 1))) & (lane <= qrow)
    s_new = jnp.where(ok, _dot_nt(q_all, k_new) - c_new, NEG_INF)
    m = jnp.max(jnp.maximum(mx, s_new), axis=-1, keepdims=True)

    carry = _exp_accumulate((jnp.zeros((n_q, LANES), F32), jnp.zeros((n_q, HEAD_DIM), F32)), s_new, m, v_new)
    for p, c in steps:
        rows = slice(p * page_rows + c * ch, p * page_rows + (c + 1) * ch)
        carry = _exp_accumulate(carry, s_scr[:, rows], m, kb_scr[rows, :], lane_shift=nh)
    ls, acc = carry
    o_ref[0] = (acc / jnp.sum(ls, axis=-1, keepdims=True)).astype(BF16)


def _fox_sample(page_table, cache_fox, cache_logf, kvf_new, qf, sm, n_tok):
    n_seq, n_pages = page_table.shape
    page = cache_fox.shape[1]
    nh = N_FOX_HEADS
    kv2 = cache_fox.reshape(-1, HEAD_DIM)
    rows_pp = 2 * nh
    n_chunk = page * rows_pp // LANES
    lf_c = jnp.pad(cache_logf, ((0, 0), (0, 0), (0, rows_pp - nh))).reshape(-1, n_chunk, LANES)
    n_q = n_tok * nh
    lfn = sm[:, LOGF_COL0:LOGF_COL0 + nh].reshape(n_seq, 1, n_q)
    lfn = jnp.pad(lfn, ((0, 0), (0, 7), (0, LANES - n_q)))
    kv_new = kvf_new.reshape(n_seq, n_tok, 2, nh, HEAD_DIM)
    k_new = kv_new[:, :, 0].reshape(n_seq, n_q, HEAD_DIM)
    v_new = kv_new[:, :, 1].reshape(n_seq, n_q, HEAD_DIM)
    q3 = qf.reshape(n_seq, n_q, HEAD_DIM)

    seq3 = lambda b, pt: (b, 0, 0)
    in_specs = ([pl.BlockSpec((page * rows_pp, HEAD_DIM), lambda b, pt, p=p: (pt[b, p], 0)) for p in range(n_pages)]
                + [pl.BlockSpec((1, n_chunk, LANES), lambda b, pt, p=p: (pt[b, p], 0, 0)) for p in range(n_pages)]
                + [pl.BlockSpec((1, n_q, HEAD_DIM), seq3),
                   pl.BlockSpec((1, n_q, HEAD_DIM), seq3),
                   pl.BlockSpec((1, n_q, HEAD_DIM), seq3),
                   pl.BlockSpec((1, 8, LANES), seq3)])
    kern = functools.partial(_fox_sample_kernel, n_pages=n_pages, page=page, n_tok=n_tok)
    n_rows = n_pages * page * rows_pp
    grid_spec = pltpu.PrefetchScalarGridSpec(
        num_scalar_prefetch=1, grid=(n_seq,), in_specs=in_specs,
        out_specs=pl.BlockSpec((1, n_q, HEAD_DIM), seq3),
        scratch_shapes=[pltpu.VMEM((n_q, FOX_SAMPLE_CH), F32),
                        pltpu.VMEM((n_q, n_rows), F32),
                        pltpu.VMEM((n_rows, HEAD_DIM), BF16)])
    out = pl.pallas_call(
        kern,
        grid_spec=grid_spec,
        out_shape=jax.ShapeDtypeStruct((n_seq, n_q, HEAD_DIM), BF16),
        compiler_params=_cparams(("arbitrary",)),
        name="fox_sample",
    )(page_table, *([kv2] * n_pages), *([lf_c] * n_pages), k_new, v_new, q3, lfn)
    return out.reshape(n_seq * n_tok, nh * HEAD_DIM)


def _postmix_kernel(on_ref, of_ref, gm0_ref, gm1_ref, x_ref, wn_ref, wf_ref, wo_ref, g_ref, y_ref):
    a = _dot(on_ref[...], wn_ref[...])
    b = _dot(of_ref[...], wf_ref[...])
    merged = gm0_ref[...] * a + gm1_ref[...] * b
    z = _dot(merged.astype(BF16), wo_ref[...])
    y_ref[...] = x_ref[...] + _rms(z, g_ref[...])


def _postmix(o_n, o_f, gm, x2, wn, wf, wo, g):
    n, d = x2.shape
    tm = 256
    row = lambda i: (i, 0)
    const = lambda i: (0, 0)
    return pl.pallas_call(
        _postmix_kernel,
        grid=(n // tm,),
        in_specs=[
            pl.BlockSpec((tm, NSA_Q_W), row),
            pl.BlockSpec((tm, FOX_W), row),
            pl.BlockSpec((tm, d), lambda i: (i, 0)),
            pl.BlockSpec((tm, d), lambda i: (i, 1)),
            pl.BlockSpec((tm, d), row),
            pl.BlockSpec(wn.shape, const),
            pl.BlockSpec(wf.shape, const),
            pl.BlockSpec(wo.shape, const),
            pl.BlockSpec((1, d), const),
        ],
        out_specs=pl.BlockSpec((tm, d), row),
        out_shape=jax.ShapeDtypeStruct((n, d), F32),
        compiler_params=_cparams(("arbitrary",)),
        name="postmix",
    )(o_n, o_f, gm, gm, x2, wn, wf, wo, g)


FFN_TM = 512
FFN_TF = 512
HALO = 16


def _ffn_kernel(*refs, seq_tiles, n_tok):
    if n_tok is None:
        (x_ref, xh_ref, g_ref, wg_ref, wu_ref, wd_ref, wc_ref, bc_ref, gp_ref,
         y_ref, gt_ref, h_scr, hh_scr, acc_scr) = refs
    else:
        (x_ref, s0_ref, s1_ref, g_ref, wg_ref, wu_ref, wd_ref, wc_ref, bc_ref, gp_ref,
         y_ref, gt_ref, h_scr, acc_scr) = refs
    i = pl.program_id(0)
    f = pl.program_id(1)
    tm = x_ref.shape[0]

    @pl.when(f == 0)
    def _():
        h_scr[...] = _rms(x_ref[...], g_ref[...]).astype(BF16)
        acc_scr[...] = jnp.zeros_like(acc_scr)
        if n_tok is None:
            hh_scr[...] = _rms(xh_ref[...], g_ref[...]).astype(BF16)

    h2 = h_scr[...]
    gate = _dot(h2, wg_ref[...])
    up = _dot(h2, wu_ref[...])
    row = lax.broadcasted_iota(jnp.int32, gate.shape, 0)
    r1 = pltpu.roll(gate, 1, axis=0)
    r2 = pltpu.roll(gate, 2, axis=0)
    if n_tok is None:
        first = (i % seq_tiles) == 0
        gh = jnp.where(first, 0.0, _dot(hh_scr[...], wg_ref[...]))
        p1 = gh[HALO - 1:HALO, :]
        p2 = gh[HALO - 2:HALO - 1, :]
        g1 = jnp.where(row == 0, p1, r1)
        g2 = jnp.where(row == 0, p2, jnp.where(row == 1, p1, r2))
        gt_ref[...] = gate[tm - 8:tm, :]
    else:
        assert n_tok & (n_tok - 1) == 0
        rt = row & (n_tok - 1)
        g1 = jnp.where(rt == 0, s1_ref[...], r1)
        g2 = jnp.where(rt == 0, s0_ref[...], jnp.where(rt == 1, s1_ref[...], r2))
        gt_ref[...] = gate
    wc = wc_ref[...]
    gc = bc_ref[...] + wc[0:1, :] * g2 + wc[1:2, :] * g1 + wc[2:3, :] * gate
    act = jax.nn.gelu(gc, approximate=True) * up
    acc_scr[...] += _dot(act.astype(BF16), wd_ref[...])

    @pl.when(f == pl.num_programs(1) - 1)
    def _():
        y_ref[...] = x_ref[...] + _rms(acc_scr[...], gp_ref[...])


def _ffn(x2, g_pre, w_up_b, w_down_b, w_conv, b_conv, g_post, *, seq_len=None, state=None):
    n, d = x2.shape
    d_ff = w_down_b.shape[0]
    tf = FFN_TF
    nf = d_ff // tf
    tm = min(FFN_TM, n)
    common_w = [
        pl.BlockSpec((1, d), lambda i, f: (0, 0)),
        pl.BlockSpec((d, tf), lambda i, f: (0, f)),
        pl.BlockSpec((d, tf), lambda i, f: (0, nf + f)),
        pl.BlockSpec((tf, d), lambda i, f: (f, 0)),
        pl.BlockSpec((CONV_WIDTH, tf), lambda i, f: (0, f)),
        pl.BlockSpec((1, tf), lambda i, f: (0, f)),
        pl.BlockSpec((1, d), lambda i, f: (0, 0)),
    ]
    w_args = (g_pre, w_up_b, w_up_b, w_down_b, w_conv, b_conv, g_post)
    row = lambda i, f: (i, 0)
    if state is None:
        seq_tiles = seq_len // tm
        halo_blocks = tm // HALO
        in_specs = [pl.BlockSpec((tm, d), row),
                    pl.BlockSpec((HALO, d), lambda i, f: (jnp.maximum(i * halo_blocks - 1, 0), 0))] + common_w
        args = (x2, x2) + w_args
        gt_shape = jax.ShapeDtypeStruct((n // tm * 8, d_ff), F32)
        gt_spec = pl.BlockSpec((8, tf), lambda i, f: (i, f))
        scratch = [pltpu.VMEM((tm, d), BF16), pltpu.VMEM((HALO, d), BF16), pltpu.VMEM((tm, d), F32)]
        kern = functools.partial(_ffn_kernel, seq_tiles=seq_tiles, n_tok=None)
    else:
        n_tok = n // state.shape[0]
        s0 = jnp.repeat(state[:, 0], n_tok, axis=0)
        s1 = jnp.repeat(state[:, 1], n_tok, axis=0)
        in_specs = [pl.BlockSpec((tm, d), row),
                    pl.BlockSpec((tm, tf), lambda i, f: (i, f)),
                    pl.BlockSpec((tm, tf), lambda i, f: (i, f))] + common_w
        args = (x2, s0, s1) + w_args
        gt_shape = jax.ShapeDtypeStruct((n, d_ff), F32)
        gt_spec = pl.BlockSpec((tm, tf), lambda i, f: (i, f))
        scratch = [pltpu.VMEM((tm, d), BF16), pltpu.VMEM((tm, d), F32)]
        kern = functools.partial(_ffn_kernel, seq_tiles=None, n_tok=n_tok)
    return pl.pallas_call(
        kern,
        grid=(n // tm, nf),
        in_specs=in_specs,
        out_specs=(pl.BlockSpec((tm, d), row), gt_spec),
        out_shape=(jax.ShapeDtypeStruct((n, d), F32), gt_shape),
        scratch_shapes=scratch,
        compiler_params=_cparams(("arbitrary", "arbitrary")),
        name="ffn",
    )(*args)


def _rope_tables(pos):
    half = HEAD_DIM // 2
    inv_freq = ROPE_THETA ** (-jnp.arange(half, dtype=F32) / half)
    ang = pos.astype(F32)[:, None] * inv_freq[None, :]
    cos, sin = jnp.cos(ang), jnp.sin(ang)
    return jnp.concatenate([cos, cos], axis=-1), jnp.concatenate([-sin, sin], axis=-1)


def _cmp_weights(w1, pe, w2):
    w1r = w1.reshape(CMP_RATIO, CMP_STRIDE * HEAD_DIM, HEAD_DIM)
    w1cat = jnp.concatenate([w1r[r] for r in range(CMP_RATIO)], axis=1).astype(BF16)
    pe8 = jnp.pad(pe.reshape(CMP_RATIO, CMP_STRIDE * HEAD_DIM), ((0, 8 - CMP_RATIO), (0, 0)))
    return w1cat, pe8, w2.astype(BF16)


def kernel(x_prompt, x_sample, cache_nsa_cmp_kv, cache_nsa_sel_kv, cache_nsa_win_kv, cache_fox_kv, cache_fox_logf, state_ffn_conv, page_table, g_pre_mix, w_in, b_fgt, w_cmp_k1, pe_cmp_k, w_cmp_k2, w_cmp_v1, pe_cmp_v, w_cmp_v2, w_nsa_o, w_fox_o, w_out, g_post_mix, g_pre_ffn, w_up, w_conv, b_conv, w_down, g_post_ffn):
    b_p, t_p, d = x_prompt.shape
    b_s, t_s, _ = x_sample.shape
    depth = w_in.shape[0]
    page = cache_nsa_cmp_kv.shape[2]
    past = page_table.shape[1] * page
    g_n, n_h = N_NSA_GROUPS, N_FOX_HEADS

    cos_p, sin_p = _rope_tables(jnp.tile(jnp.arange(t_p), b_p))
    cos_s, sin_s = _rope_tables(jnp.tile(past + jnp.arange(t_s), b_s))

    y_p = x_prompt.reshape(b_p * t_p, d)
    y_s = x_sample.reshape(b_s * t_s, d)
    outs = {k: [] for k in ('cmp_p', 'cmp_s', 'sel_p', 'sel_s', 'win_p', 'win_s',
                            'fox_p', 'fox_s', 'lf_p', 'lf_s', 'conv_p', 'conv_s')}
    o_q = NSA_Q_W
    o_g = o_q + 3 * KV_W
    o_f = o_g + N_GATE_COLS
    o_ff = o_f + 3 * FOX_W
    o_m = o_ff + n_h
    for l in range(depth):
        w = w_in[l]
        w_main = jnp.concatenate([w[:, :o_g], w[:, o_f:o_ff], w[:, o_m:]], axis=1).astype(BF16)
        w_small = jnp.concatenate([w[:, o_g:o_f], w[:, o_ff:o_m],
                                   jnp.zeros((d, LANES - N_GATE_COLS - n_h), F32)], axis=1).astype(BF16)
        bf_row = jnp.zeros((1, LANES), F32).at[0, LOGF_COL0:LOGF_COL0 + n_h].set(b_fgt[l])
        g1 = g_pre_mix[l][None, :]
        cmp_k = _cmp_weights(w_cmp_k1[l], pe_cmp_k[l], w_cmp_k2[l])
        cmp_v = _cmp_weights(w_cmp_v1[l], pe_cmp_v[l], w_cmp_v2[l])
        wn, wf, wo = w_nsa_o[l].astype(BF16), w_fox_o[l].astype(BF16), w_out[l].astype(BF16)
        wu, wd = w_up[l].astype(BF16), w_down[l].astype(BF16)
        ffn_w = (g_pre_ffn[l][None, :], wu, wd, w_conv[l], b_conv[l][None, :], g_post_ffn[l][None, :])

        (qn, kvc, kvs, kvw, qf, kvf, gm, sm, kvs_b, kvw_b, kvf_b) = _project(y_p, g1, cos_p, sin_p, w_main, w_small, bf_row)
        w1cat = jnp.stack([cmp_k[0], cmp_v[0]])
        pe8 = jnp.stack([cmp_k[1], cmp_v[1]])
        w2 = jnp.stack([cmp_k[2], cmp_v[2]])
        ckv = _compress_prompt(kvc, b_p, t_p, w1cat, pe8, w2)
        frow = _fcum_prompt(sm, b_p, t_p)
        o_n = _nsa_prompt(qn, sm, ckv, kvs_b, kvw_b, b_p, t_p)
        o_fx = _fox_prompt(qf, kvf_b, frow, b_p, t_p)
        y1 = _postmix(o_n, o_fx, gm, y_p, wn, wf, wo, g_post_mix[l][None, :])
        y_p, gt = _ffn(y1, *ffn_w, seq_len=t_p)
        n_win = min(WINDOW, t_p)
        outs['cmp_p'].append(kvc.reshape(b_p, t_p, 2, g_n, HEAD_DIM))
        outs['sel_p'].append(kvs.reshape(b_p, t_p, 2, g_n, HEAD_DIM))
        outs['win_p'].append(kvw.reshape(b_p, t_p, 2, g_n, HEAD_DIM)[:, t_p - n_win:])
        outs['fox_p'].append(kvf.reshape(b_p, t_p, 2, n_h, HEAD_DIM))
        outs['lf_p'].append(sm[:, LOGF_COL0:LOGF_COL0 + n_h].reshape(b_p, t_p, n_h))
        tiles_per_seq = t_p // FFN_TM
        gt = gt.reshape(b_p, tiles_per_seq, 8, -1)
        outs['conv_p'].append(gt[:, -1, 8 - (CONV_WIDTH - 1):])

        (qn, kvc, kvs, kvw, qf, kvf, gm, sm, _, _, _) = _project(y_s, g1, cos_s, sin_s, w_main, w_small, bf_row)
        win_buf = cache_nsa_win_kv[l]
        o_n = _nsa_sample(page_table, cache_nsa_cmp_kv[l], cache_nsa_sel_kv[l], win_buf, kvs, kvw, qn, sm,
                          cmp_k + cmp_v, t_s)
        o_fx = _fox_sample(page_table, cache_fox_kv[l], cache_fox_logf[l], kvf, qf, sm, t_s)
        o_n = o_n[:, :t_s].reshape(b_s * t_s, -1)
        y1 = _postmix(o_n, o_fx, gm, y_s, wn, wf, wo, g_post_mix[l][None, :])
        y_s, gt = _ffn(y1, *ffn_w, state=state_ffn_conv[l])
        kw_new = kvw.reshape(b_s, t_s, 2, g_n, HEAD_DIM)
        n_win = min(WINDOW, win_buf.shape[1] + t_s)
        outs['cmp_s'].append(kvc.reshape(b_s, t_s, 2, g_n, HEAD_DIM))
        outs['sel_s'].append(kvs.reshape(b_s, t_s, 2, g_n, HEAD_DIM))
        outs['win_s'].append(jnp.concatenate([win_buf, kw_new], axis=1)[:, -n_win:])
        outs['fox_s'].append(kvf.reshape(b_s, t_s, 2, n_h, HEAD_DIM))
        outs['lf_s'].append(sm[:, LOGF_COL0:LOGF_COL0 + n_h].reshape(b_s, t_s, n_h))
        gfull = jnp.concatenate([state_ffn_conv[l], gt.reshape(b_s, t_s, -1)], axis=1)
        outs['conv_s'].append(gfull[:, t_s:])

    st = {k: jnp.stack(v) for k, v in outs.items()}
    return (y_p.reshape(b_p, t_p, d), y_s.reshape(b_s, t_s, d),
            st['cmp_p'], st['cmp_s'], st['sel_p'], st['sel_s'], st['win_p'], st['win_s'],
            st['fox_p'], st['fox_s'], st['lf_p'], st['lf_s'], st['conv_p'], st['conv_s'])
```

```python
import functools

import numpy as np
import jax
import jax.numpy as jnp
from jax import lax
from jax.experimental import pallas as pl
from jax.experimental.pallas import tpu as pltpu

F32 = jnp.float32
BF16 = jnp.bfloat16

HEAD_DIM = 128
N_NSA_HEADS = 8
N_NSA_GROUPS = 2
NSA_HPG = N_NSA_HEADS // N_NSA_GROUPS
N_FOX_HEADS = 8
CMP_BLOCK = 32
CMP_STRIDE = 16
CMP_RATIO = CMP_BLOCK // CMP_STRIDE
SEL_BLOCK = 64
SEL_TOPK = 16
N_LOCAL_BLOCKS = 2
WINDOW = 512
CONV_WIDTH = 3
ROPE_THETA = 10000.0
RMS_EPS = 1e-6
FORCE_BONUS = 1e4
NEG_INF = -1e30
LOG2E = 1.4426950408889634
QK_SCALE = HEAD_DIM ** -0.5 * LOG2E

N_GATE_COLS = N_NSA_HEADS * 3
LOGF_COL0 = N_GATE_COLS
LANES = 128
VMEM_LIMIT = 56 * 1024 * 1024

PROJ_TN = 512
KV_W = 2 * N_NSA_GROUPS * HEAD_DIM
NSA_Q_W = N_NSA_HEADS * HEAD_DIM
FOX_W = N_FOX_HEADS * HEAD_DIM


def _cparams(sem):
    return pltpu.CompilerParams(dimension_semantics=sem, vmem_limit_bytes=VMEM_LIMIT)


def _dot(a, b):
    return jnp.dot(a, b, preferred_element_type=F32)


def _dot_nt(a, b):
    return lax.dot_general(a, b, (((1,), (1,)), ((), ())), preferred_element_type=F32)


def _rms(x, g):
    return x * lax.rsqrt(jnp.mean(x * x, axis=-1, keepdims=True) + RMS_EPS) * g


def _masked_softmax(s, mask):
    sm = jnp.where(mask, s, NEG_INF)
    m = jnp.max(sm, axis=-1, keepdims=True)
    e = jnp.where(mask, jnp.exp2(sm - m), 0.0)
    l = jnp.sum(e, axis=-1, keepdims=True)
    return e / jnp.where(l > 0.0, l, 1.0)


def _transpose_rows(src_ref, dst_ref, cols=slice(None)):
    for c in range(src_ref.shape[0] // LANES):
        rows = slice(c * LANES, (c + 1) * LANES)
        dst_ref[:, rows] = src_ref[rows, cols].astype(F32).T.astype(BF16)


def _masked_exp(s, mask):
    sm = jnp.where(mask, s, NEG_INF)
    e = jnp.where(mask, jnp.exp2(sm - jnp.max(sm, axis=-1, keepdims=True)), 0.0)
    l = jnp.sum(e, axis=-1, keepdims=True)
    return e, jnp.where(l > 0.0, l, 1.0)


def _lane_tile_max(mx, s):
    for c in range(s.shape[1] // LANES):
        mx = jnp.maximum(mx, s[:, c * LANES:(c + 1) * LANES])
    return mx


def _exp_accumulate(carry, s, m, v, lane_shift=0):
    ls, acc = carry
    p = jnp.exp2(s - m)
    tiles = [p[:, c * LANES:(c + 1) * LANES] for c in range(s.shape[1] // LANES)]
    for t in tiles:
        ls = ls + t
    if lane_shift:
        p = jnp.concatenate([pltpu.roll(t, lane_shift, axis=1) for t in tiles], axis=1)
    return ls, acc + _dot(p.astype(BF16), v)


def _split3(x):
    hi = x.astype(BF16)
    r = x - hi.astype(F32)
    mid = r.astype(BF16)
    lo = (r - mid.astype(F32)).astype(BF16)
    return hi, mid, lo


def _topk_mask(score, k, n_sel):
    st = score.T
    nv = -(-n_sel // 8)
    slabs = [st[8 * v:8 * v + 8, :] for v in range(nv)]
    sub = lax.broadcasted_iota(jnp.int32, (8, LANES), 0)
    ranks = [jnp.zeros((8, LANES), F32) for _ in range(nv)]
    for b2 in range(n_sel):
        row = jnp.broadcast_to(st[b2:b2 + 1, :], (8, LANES))
        for v in range(nv):
            if b2 < 8 * v:
                beats = row >= slabs[v]
            elif b2 >= 8 * v + 8:
                beats = row > slabs[v]
            else:
                beats = (row > slabs[v]) | ((row == slabs[v]) & (sub > b2 - 8 * v))
            ranks[v] = ranks[v] + jnp.where(beats, 1.0, 0.0)
    sel = [jnp.where((ranks[v] < k) & (sub + 8 * v < n_sel), 1.0, 0.0) for v in range(nv)]
    sel_t = jnp.concatenate(sel + [jnp.zeros((LANES - 8 * nv, LANES), F32)], axis=0)
    return sel_t.T


def _sel_scores(imp, tpos, n_sel):
    bidx = lax.broadcasted_iota(jnp.int32, imp.shape, 1)
    cur = jnp.right_shift(tpos, 6)
    valid = bidx <= cur
    forced = (bidx == 0) | (valid & (bidx > cur - N_LOCAL_BLOCKS))
    score = jnp.where(valid, jnp.where(forced, imp + FORCE_BONUS, imp), NEG_INF)
    return jnp.where(bidx < n_sel, score, -jnp.inf)


def _proj_kernel(x_ref, g_ref, cos_ref, sin_ref, w_ref, ws_ref, bf_ref,
                 qn_ref, kvc_ref, kvs_ref, kvw_ref, qf_ref, kvf_ref, gm_ref, sm_ref,
                 kvsb_ref, kvwb_ref, kvfb_ref, h_scr):
    j = pl.program_id(1)

    @pl.when(j == 0)
    def _():
        x = x_ref[...]
        y = x * lax.rsqrt(jnp.mean(x * x, axis=-1, keepdims=True) + RMS_EPS)
        h = (y * g_ref[...]).astype(BF16)
        h_scr[...] = h
        s = _dot(h, ws_ref[...])
        lane = lax.broadcasted_iota(jnp.int32, s.shape, 1)
        z = s + bf_ref[...]
        lf = jnp.minimum(z, 0.0) - jnp.log1p(jnp.exp(-jnp.abs(z)))
        sm_ref[...] = jnp.where(lane < N_GATE_COLS, jax.nn.sigmoid(s),
                                jnp.where(lane < LOGF_COL0 + N_FOX_HEADS, lf, 0.0))

    cos = cos_ref[...]
    sin = sin_ref[...]
    half_w = PROJ_TN // 2
    halves = [slice(0, half_w), slice(half_w, PROJ_TN)]

    def mm(cols):
        return _dot(h_scr[...], w_ref[:, cols])

    def rope2(a):
        return jnp.concatenate(
            [a[:, k * HEAD_DIM:(k + 1) * HEAD_DIM] * cos
             + pltpu.roll(a[:, k * HEAD_DIM:(k + 1) * HEAD_DIM], HEAD_DIM // 2, axis=1) * sin
             for k in range(half_w // HEAD_DIM)], axis=1)

    def kv_rows(ref, bref):
        assert half_w == N_NSA_GROUPS * HEAD_DIM
        for cols, is_key in zip(halves, (True, False)):
            a = mm(cols)
            a = rope2(a) if is_key else a
            ref[:, cols] = a
            if bref is not None:
                bref[:, cols] = a.astype(BF16)

    @pl.when(j < 2)
    def _():
        for cols in halves:
            qn_ref[:, cols] = (rope2(mm(cols)) * QK_SCALE).astype(BF16)

    @pl.when(j == 2)
    def _():
        kv_rows(kvc_ref, None)

    @pl.when(j == 3)
    def _():
        kv_rows(kvs_ref, kvsb_ref)

    @pl.when(j == 4)
    def _():
        kv_rows(kvw_ref, kvwb_ref)

    @pl.when((j >= 5) & (j < 7))
    def _():
        for cols in halves:
            qf_ref[:, cols] = (mm(cols) * QK_SCALE).astype(BF16)

    @pl.when((j >= 7) & (j < 11))
    def _():
        for cols in halves:
            a = mm(cols)
            kvf_ref[:, cols] = a
            kvfb_ref[:, cols] = a.astype(BF16)

    @pl.when(j >= 11)
    def _():
        for cols in halves:
            gm_ref[:, cols] = jax.nn.sigmoid(mm(cols))


def _project(x2, g, cos2, sin2, w_main, w_small, bf_row):
    n, d = x2.shape
    tm = 512
    n_j = w_main.shape[1] // PROJ_TN
    tn = PROJ_TN

    def clip(lo, hi):
        return lambda i, j: (i, jnp.clip(j - lo, 0, hi - lo))

    row = lambda i, j: (i, 0)
    out_shape = (
        jax.ShapeDtypeStruct((n, NSA_Q_W), BF16),
        jax.ShapeDtypeStruct((n, KV_W), F32),
        jax.ShapeDtypeStruct((n, KV_W), F32),
        jax.ShapeDtypeStruct((n, KV_W), F32),
        jax.ShapeDtypeStruct((n, FOX_W), BF16),
        jax.ShapeDtypeStruct((n, 2 * FOX_W), F32),
        jax.ShapeDtypeStruct((n, 2 * d), F32),
        jax.ShapeDtypeStruct((n, LANES), F32),
        jax.ShapeDtypeStruct((n, KV_W), BF16),
        jax.ShapeDtypeStruct((n, KV_W), BF16),
        jax.ShapeDtypeStruct((n, 2 * FOX_W), BF16),
    )
    out_specs = (
        pl.BlockSpec((tm, tn), clip(0, 1)),
        pl.BlockSpec((tm, tn), row),
        pl.BlockSpec((tm, tn), row),
        pl.BlockSpec((tm, tn), row),
        pl.BlockSpec((tm, tn), clip(5, 6)),
        pl.BlockSpec((tm, tn), clip(7, 10)),
        pl.BlockSpec((tm, tn), clip(11, 18)),
        pl.BlockSpec((tm, LANES), row),
        pl.BlockSpec((tm, tn), row),
        pl.BlockSpec((tm, tn), row),
        pl.BlockSpec((tm, tn), clip(7, 10)),
    )
    in_specs = [
        pl.BlockSpec((tm, d), row),
        pl.BlockSpec((1, d), lambda i, j: (0, 0)),
        pl.BlockSpec((tm, LANES), row),
        pl.BlockSpec((tm, LANES), row),
        pl.BlockSpec((d, tn), lambda i, j: (0, j)),
        pl.BlockSpec((d, LANES), lambda i, j: (0, 0)),
        pl.BlockSpec((1, LANES), lambda i, j: (0, 0)),
    ]
    return pl.pallas_call(
        _proj_kernel,
        grid=(n // tm, n_j),
        in_specs=in_specs,
        out_specs=out_specs,
        out_shape=out_shape,
        scratch_shapes=[pltpu.VMEM((tm, d), BF16)],
        compiler_params=_cparams(("arbitrary", "arbitrary")),
        name="proj",
    )(x2, g, cos2, sin2, w_main, w_small, bf_row)


def _compress_tail(xc, w1, pe8, w2):
    n = xc.shape[0]
    part = _dot(xc, w1)
    pp = _dot(pe8.astype(BF16), w1)
    pe_term = pp[0:1, :HEAD_DIM] + pp[1:2, HEAD_DIM:]
    hid = pe_term + part[:, :HEAD_DIM] + pltpu.roll(part[:, HEAD_DIM:], n - 1, axis=0)
    return _dot(jax.nn.gelu(hid, approximate=True).astype(BF16), w2)


def _cmp_prompt_kernel(x_ref, w1_ref, pe_ref, w2_ref, o_ref):
    n = x_ref.shape[0] // CMP_STRIDE
    xc = jnp.concatenate([x_ref[pl.ds(i, n, stride=CMP_STRIDE), :] for i in range(CMP_STRIDE)],
                         axis=1).astype(BF16)
    o_ref[0, 0] = _compress_tail(xc, w1_ref[0], pe_ref[0], w2_ref[0]).astype(BF16)


def _compress_prompt(kvc, b_n, t_n, w1cat, pe8, w2):
    n = t_n // CMP_STRIDE
    return pl.pallas_call(
        _cmp_prompt_kernel,
        grid=(b_n, 2 * N_NSA_GROUPS),
        in_specs=[
            pl.BlockSpec((t_n, HEAD_DIM), lambda b, s: (b, s)),
            pl.BlockSpec((1, CMP_STRIDE * HEAD_DIM, 2 * HEAD_DIM), lambda b, s: (s // N_NSA_GROUPS, 0, 0)),
            pl.BlockSpec((1, 8, CMP_STRIDE * HEAD_DIM), lambda b, s: (s // N_NSA_GROUPS, 0, 0)),
            pl.BlockSpec((1, HEAD_DIM, HEAD_DIM), lambda b, s: (s // N_NSA_GROUPS, 0, 0)),
        ],
        out_specs=pl.BlockSpec((1, 1, n, HEAD_DIM), lambda b, s: (b, s, 0, 0)),
        out_shape=jax.ShapeDtypeStruct((b_n, 2 * N_NSA_GROUPS, n, HEAD_DIM), BF16),
        compiler_params=_cparams(("arbitrary", "arbitrary")),
        name="cmp_prompt",
    )(kvc, w1cat, pe8, w2)


def _fcum_kernel(x_ref, frow_ref, carry_scr):
    i = pl.program_id(1)

    @pl.when(i == 0)
    def _():
        carry_scr[...] = jnp.zeros_like(carry_scr)

    x = x_ref[...]
    tb = x.shape[0]
    r = lax.broadcasted_iota(jnp.int32, (tb, tb), 0)
    c = lax.broadcasted_iota(jnp.int32, (tb, tb), 1)
    tri = jnp.where(r >= c, 1.0, 0.0).astype(BF16)
    hi, mid, lo = _split3(x)
    cs = _dot(tri, hi) + _dot(tri, mid) + _dot(tri, lo) + carry_scr[0:1, :]
    carry_scr[...] = jnp.broadcast_to(cs[tb - 1:tb, :], carry_scr.shape)
    frow_ref[0] = cs.T[LOGF_COL0:LOGF_COL0 + N_FOX_HEADS, :] * LOG2E


def _fcum_prompt(sm, b_n, t_n):
    tb = 512
    nb = t_n // tb
    return pl.pallas_call(
        _fcum_kernel,
        grid=(b_n, nb),
        in_specs=[pl.BlockSpec((tb, LANES), lambda b, i: (b * nb + i, 0))],
        out_specs=pl.BlockSpec((1, N_FOX_HEADS, tb), lambda b, i: (b, 0, i)),
        out_shape=jax.ShapeDtypeStruct((b_n, N_FOX_HEADS, t_n), F32),
        scratch_shapes=[pltpu.VMEM((8, LANES), F32)],
        compiler_params=_cparams(("arbitrary", "arbitrary")),
        name="fcum_prompt",
    )(sm)


NSA_TQ = 128
NSA_NSUB = 2


def _tile_rows(a, reps):
    return jnp.concatenate([a] * reps, axis=0)


def _nsa_front_kernel(q_ref, sm_ref, ck_ref, kw_ref, cover_ref, selm_ref, ocw_ref, kwt_scr, *, n_cmp, n_sel):
    i = pl.program_id(1)
    tq, nsub = NSA_TQ, NSA_NSUB
    t0 = i * (tq * nsub)
    n_ck = ck_ref.shape[2]
    band = WINDOW + tq
    tpos = [t0 + u * tq + lax.broadcasted_iota(jnp.int32, (tq, 1), 0) for u in range(nsub)]

    @pl.when(i == 0)
    def _():
        for g in range(N_NSA_GROUPS):
            _transpose_rows(kw_ref, kwt_scr.at[g], slice(g * HEAD_DIM, (g + 1) * HEAD_DIM))

    for g in range(N_NSA_GROUPS):
        ck = ck_ref[0, g]
        cv = ck_ref[0, N_NSA_GROUPS + g]
        o_cs, o_ws, selms = [], [], []
        for u in range(nsub):
            q = jnp.concatenate([q_ref[u * tq:(u + 1) * tq, (g * NSA_HPG + h) * HEAD_DIM:(g * NSA_HPG + h + 1) * HEAD_DIM]
                                 for h in range(NSA_HPG)], axis=0)
            s_c = _dot_nt(q, ck)
            cidx = lax.broadcasted_iota(jnp.int32, (tq, n_ck), 1)
            mc = jnp.where((cidx * CMP_STRIDE + CMP_BLOCK - 1 <= tpos[u]) & (cidx < n_cmp), 1.0, 0.0)
            e_c, l_c = _masked_exp(s_c, _tile_rows(mc, NSA_HPG) > 0.5)
            e_cb = e_c.astype(BF16)
            imp4 = _dot(e_cb, cover_ref[...]) / l_c
            imp = imp4[0:tq] + imp4[tq:2 * tq] + imp4[2 * tq:3 * tq] + imp4[3 * tq:4 * tq]
            selms.append(_topk_mask(_sel_scores(imp, tpos[u], n_sel), min(SEL_TOPK, n_sel), n_sel).astype(BF16))
            o_cs.append(_dot(e_cb, cv) / l_c)
            w0 = pl.multiple_of(jnp.maximum(t0 + u * tq - WINDOW, 0), tq)
            vw = kw_ref[pl.ds(w0, band), (N_NSA_GROUPS + g) * HEAD_DIM:(N_NSA_GROUPS + g + 1) * HEAD_DIM]
            s_w = _dot(q, kwt_scr[g, :, pl.ds(w0, band)])
            wpos = w0 + lax.broadcasted_iota(jnp.int32, (tq, band), 1)
            bw = jnp.where((wpos <= tpos[u]) & (wpos > tpos[u] - WINDOW), 0.0, NEG_INF)
            s_w = s_w + _tile_rows(bw, NSA_HPG)
            e_w = jnp.exp2(s_w - jnp.max(s_w, axis=-1, keepdims=True))
            o_ws.append(_dot(e_w.astype(BF16), vw) / jnp.sum(e_w, axis=-1, keepdims=True))

        for u in range(nsub):
            rows_u = slice(u * tq, (u + 1) * tq)
            selm_ref[rows_u, g * LANES:(g + 1) * LANES] = selms[u]
            gates = sm_ref[rows_u, :]
            for h in range(NSA_HPG):
                hh = g * NSA_HPG + h
                rows = slice(h * tq, (h + 1) * tq)
                ocw_ref[rows_u, hh * HEAD_DIM:(hh + 1) * HEAD_DIM] = (
                    gates[:, 3 * hh:3 * hh + 1] * o_cs[u][rows] + gates[:, 3 * hh + 2:3 * hh + 3] * o_ws[u][rows])


def _cover_matrix(n_rows, n_cmp, n_sel):
    c = np.arange(n_rows)[:, None] * CMP_STRIDE
    b = np.arange(LANES)[None, :] * SEL_BLOCK
    m = (c < b + SEL_BLOCK) & (c + CMP_BLOCK > b) & (np.arange(n_rows)[:, None] < n_cmp) & (np.arange(LANES)[None, :] < n_sel)
    return jnp.asarray(m.astype(np.float32), dtype=BF16)


def _expand_matrix(n_keys):
    m = (np.arange(n_keys)[None, :] // SEL_BLOCK) == np.arange(LANES)[:, None]
    return jnp.asarray(m.astype(np.float32), dtype=BF16)


SEL_MASK_BIG = 2.0 ** 100


def _nsa_sel_kernel(q_ref, k_ref, v_ref, selm_ref, sm_ref, ocw_ref, expand_ref, o_ref, s_scr, kt_scr):
    hh = pl.program_id(1) * NSA_HPG + pl.program_id(2)
    blk = FOX_BLK

    @pl.when(pl.program_id(2) == 0)
    def _():
        _transpose_rows(k_ref, kt_scr.at[0:HEAD_DIM])
        kt_scr[HEAD_DIM:, :] = expand_ref[...]

    causal = lax.broadcasted_iota(jnp.int32, (blk, blk), 1) <= lax.broadcasted_iota(jnp.int32, (blk, 1), 0)
    zeros = jnp.zeros((blk, LANES), F32)
    lane = lax.broadcasted_iota(jnp.int32, (blk, LANES), 1)
    for qb in range(q_ref.shape[0] // blk):
        rows = slice(qb * blk, (qb + 1) * blk)
        unsel = ((selm_ref[rows, :].astype(F32) - 1.0) * SEL_MASK_BIG).astype(BF16)
        q = jnp.concatenate([q_ref[rows, :], unsel], axis=1)
        mx = jnp.full((blk, LANES), -jnp.inf, F32)
        for kt in range(qb + 1):
            keys = slice(kt * blk, (kt + 1) * blk)
            s = _dot(q, kt_scr[:, keys])
            if kt == qb:
                s = jnp.where(causal, s, NEG_INF)
            s_scr[:, keys] = s
            mx = _lane_tile_max(mx, s)
        m = jnp.max(mx, axis=-1, keepdims=True)
        carry = (zeros, zeros)
        for kt in range(qb + 1):
            keys = slice(kt * blk, (kt + 1) * blk)
            carry = _exp_accumulate(carry, s_scr[:, keys], m, v_ref[keys, :])
        ls, acc = carry
        gate = jnp.sum(jnp.where(lane == 3 * hh + 1, sm_ref[rows, :], 0.0), axis=-1, keepdims=True)
        out = ocw_ref[rows, :] + gate * (acc / jnp.sum(ls, axis=-1, keepdims=True))
        o_ref[rows, :] = out.astype(BF16)


def _nsa_prompt(qn, sm, ckv, kvs_b, kvw_b, b_n, t_n):
    blk = NSA_TQ * NSA_NSUB
    nq = t_n // blk
    n_cmp = t_n // CMP_STRIDE - CMP_RATIO + 1
    n_sel = t_n // SEL_BLOCK
    n_ck = ckv.shape[2]
    n_g = N_NSA_GROUPS
    cover = _cover_matrix(n_ck, n_cmp, n_sel)
    expand = _expand_matrix(t_n)
    selm, ocw = pl.pallas_call(
        functools.partial(_nsa_front_kernel, n_cmp=n_cmp, n_sel=n_sel),
        grid=(b_n, nq),
        in_specs=[
            pl.BlockSpec((blk, NSA_Q_W), lambda b, i: (b * nq + i, 0)),
            pl.BlockSpec((blk, LANES), lambda b, i: (b * nq + i, 0)),
            pl.BlockSpec((1, 2 * n_g, n_ck, HEAD_DIM), lambda b, i: (b, 0, 0, 0)),
            pl.BlockSpec((t_n, KV_W), lambda b, i: (b, 0)),
            pl.BlockSpec((n_ck, LANES), lambda b, i: (0, 0)),
        ],
        out_specs=(pl.BlockSpec((blk, n_g * LANES), lambda b, i: (b * nq + i, 0)),
                   pl.BlockSpec((blk, NSA_Q_W), lambda b, i: (b * nq + i, 0))),
        out_shape=(jax.ShapeDtypeStruct((b_n * t_n, n_g * LANES), BF16),
                   jax.ShapeDtypeStruct((b_n * t_n, NSA_Q_W), F32)),
        scratch_shapes=[pltpu.VMEM((n_g, HEAD_DIM, t_n), BF16)],
        compiler_params=_cparams(("arbitrary", "arbitrary")),
        name="nsa_front",
    )(qn, sm, ckv, kvw_b, cover)

    hpg = NSA_HPG
    return pl.pallas_call(
        _nsa_sel_kernel,
        grid=(b_n, n_g, hpg),
        in_specs=[
            pl.BlockSpec((t_n, HEAD_DIM), lambda b, g, h: (b, g * hpg + h)),
            pl.BlockSpec((t_n, HEAD_DIM), lambda b, g, h: (b, g)),
            pl.BlockSpec((t_n, HEAD_DIM), lambda b, g, h: (b, n_g + g)),
            pl.BlockSpec((t_n, LANES), lambda b, g, h: (b, g)),
            pl.BlockSpec((t_n, LANES), lambda b, g, h: (b, 0)),
            pl.BlockSpec((t_n, HEAD_DIM), lambda b, g, h: (b, g * hpg + h)),
            pl.BlockSpec((LANES, t_n), lambda b, g, h: (0, 0)),
        ],
        out_specs=pl.BlockSpec((t_n, HEAD_DIM), lambda b, g, h: (b, g * hpg + h)),
        out_shape=jax.ShapeDtypeStruct((b_n * t_n, NSA_Q_W), BF16),
        scratch_shapes=[pltpu.VMEM((FOX_BLK, t_n), F32), pltpu.VMEM((2 * HEAD_DIM, t_n), BF16)],
        compiler_params=_cparams(("arbitrary", "arbitrary", "arbitrary")),
        name="nsa_sel",
    )(qn, kvs_b, kvs_b, selm, sm, ocw, expand)


FOX_BLK = 512


def _fox_prompt_kernel(q_ref, k_ref, v_ref, frow_ref, o_ref, s_scr, kt_scr):
    h = pl.program_id(1)
    blk = FOX_BLK
    assert q_ref.shape[0] % blk == 0
    _transpose_rows(k_ref, kt_scr)

    causal = lax.broadcasted_iota(jnp.int32, (blk, blk), 1) <= lax.broadcasted_iota(jnp.int32, (blk, 1), 0)
    zeros = jnp.zeros((blk, LANES), F32)
    for qb in range(q_ref.shape[0] // blk):
        q = q_ref[qb * blk:(qb + 1) * blk, :]
        mx = jnp.full((blk, LANES), -jnp.inf, F32)
        for kt in range(qb + 1):
            keys = slice(kt * blk, (kt + 1) * blk)
            s = _dot(q, kt_scr[:, keys]) - frow_ref[0, pl.ds(h, 1), keys]
            if kt == qb:
                s = jnp.where(causal, s, NEG_INF)
            s_scr[:, keys] = s
            mx = _lane_tile_max(mx, s)
        m = jnp.max(mx, axis=-1, keepdims=True)
        carry = (zeros, zeros)
        for kt in range(qb + 1):
            keys = slice(kt * blk, (kt + 1) * blk)
            carry = _exp_accumulate(carry, s_scr[:, keys], m, v_ref[keys, :])
        ls, acc = carry
        o_ref[qb * blk:(qb + 1) * blk, :] = (acc / jnp.sum(ls, axis=-1, keepdims=True)).astype(BF16)


def _fox_prompt(qf, kvf_b, frow, b_n, t_n):
    nh = N_FOX_HEADS
    return pl.pallas_call(
        _fox_prompt_kernel,
        grid=(b_n, nh),
        in_specs=[
            pl.BlockSpec((t_n, HEAD_DIM), lambda b, h: (b, h)),
            pl.BlockSpec((t_n, HEAD_DIM), lambda b, h: (b, h)),
            pl.BlockSpec((t_n, HEAD_DIM), lambda b, h: (b, nh + h)),
            pl.BlockSpec((1, nh, t_n), lambda b, h: (b, 0, 0)),
        ],
        out_specs=pl.BlockSpec((t_n, HEAD_DIM), lambda b, h: (b, h)),
        out_shape=jax.ShapeDtypeStruct((b_n * t_n, FOX_W), BF16),
        scratch_shapes=[pltpu.VMEM((FOX_BLK, t_n), F32), pltpu.VMEM((HEAD_DIM, t_n), BF16)],
        compiler_params=_cparams(("arbitrary", "arbitrary")),
        name="fox_prompt",
    )(qf, kvf_b, kvf_b, frow)


FOX_SAMPLE_CH = 512
TOK_PAD = 8
NEW_PAD = 128
NSA_SAMPLE_SUB = 1


def _nsa_sample_kernel(pt_ref, *refs, n_pages, page, n_buf, n_tok, n_sub):
    del pt_ref
    per_seq = 2 * n_pages + 1
    kvs_new_ref, kvw_new_ref, q_ref, sm_ref = refs[n_sub * per_seq:n_sub * per_seq + 4]
    for sq in range(n_sub):
        seq_refs = refs[sq * per_seq:(sq + 1) * per_seq]
        _nsa_sample_one(seq_refs[:n_pages], seq_refs[n_pages:2 * n_pages], seq_refs[2 * n_pages],
                        kvs_new_ref[sq], kvw_new_ref[sq], q_ref.at[sq], sm_ref[sq],
                        *refs[n_sub * per_seq + 4:-1], refs[-1].at[sq],
                        n_pages=n_pages, page=page, n_buf=n_buf, n_tok=n_tok)


def _nsa_sample_one(cmp_pages, sel_pages, win_ref, kvs_new, kvw_new, q_ref, gates, w1k_ref, w1v_ref, pek_ref, pev_ref,
                    w2k_ref, w2v_ref, cover_ref, expand_ref, o_ref, *, n_pages, page, n_buf, n_tok):
    past = n_pages * page
    n_slab = 2 * N_NSA_GROUPS
    chunks_per_page = page // CMP_STRIDE
    n_chunk = n_pages * chunks_per_page
    n_cmp = (past + n_tok + CMP_STRIDE - 1) // CMP_STRIDE - CMP_RATIO + 1
    n_sel = (past + n_tok + SEL_BLOCK - 1) // SEL_BLOCK
    tp = TOK_PAD
    tpos = past + lax.broadcasted_iota(jnp.int32, (tp, 1), 0)

    rows_pc = CMP_STRIDE * n_slab
    swapped = [jnp.swapaxes(cmp_pages[p][...].reshape(chunks_per_page, rows_pc, HEAD_DIM), 0, 1)
               for p in range(n_pages)]

    def compress(slab, w1_ref, pe_ref, w2_ref):
        cols = []
        for i in range(CMP_STRIDE):
            cols.append(jnp.concatenate([swapped[p][i * n_slab + slab] for p in range(n_pages)], axis=0))
        xc = jnp.concatenate(cols, axis=1).astype(BF16)
        return _compress_tail(xc, w1_ref[...], pe_ref[...], w2_ref[...]).astype(BF16)

    qs, o_cs, scores = [], [], []
    for g in range(N_NSA_GROUPS):
        q = jnp.concatenate([q_ref[:, (g * NSA_HPG + h) * HEAD_DIM:(g * NSA_HPG + h + 1) * HEAD_DIM]
                             for h in range(NSA_HPG)], axis=0)
        ck = compress(g, w1k_ref, pek_ref, w2k_ref)
        cv = compress(N_NSA_GROUPS + g, w1v_ref, pev_ref, w2v_ref)
        s_c = _dot_nt(q, ck)
        cidx = lax.broadcasted_iota(jnp.int32, (tp, n_chunk), 1)
        mc = jnp.where((cidx * CMP_STRIDE + CMP_BLOCK - 1 <= tpos) & (cidx < n_cmp), 1.0, 0.0)
        p_c = _masked_softmax(s_c, _tile_rows(mc, NSA_HPG) > 0.5)
        p_cb = p_c.astype(BF16)
        imp4 = _dot(p_cb, cover_ref[...])
        imp = imp4[0:tp] + imp4[tp:2 * tp] + imp4[2 * tp:3 * tp] + imp4[3 * tp:4 * tp]
        qs.append(q)
        o_cs.append(_dot(p_cb, cv))
        scores.append(_sel_scores(imp, tpos, n_sel))

    score_all = jnp.concatenate(scores + [jnp.zeros((LANES - N_NSA_GROUPS * tp, LANES), F32)], axis=0)
    selm_all = _topk_mask(score_all, min(SEL_TOPK, n_sel), n_sel)

    def sel_rows(slab):
        return jnp.concatenate([sel_pages[p][pl.ds(slab, page, stride=n_slab), :] for p in range(n_pages)], axis=0)

    def with_new(cached, new):
        pad = jnp.zeros((NEW_PAD - new.shape[0], new.shape[1]), new.dtype)
        return jnp.concatenate([cached, new, pad], axis=0).astype(BF16)

    for g in range(N_NSA_GROUPS):
        q, o_c = qs[g], o_cs[g]
        selm = selm_all[g * tp:(g + 1) * tp].astype(BF16)
        kc = slice(g * HEAD_DIM, (g + 1) * HEAD_DIM)
        vc = slice((N_NSA_GROUPS + g) * HEAD_DIM, (N_NSA_GROUPS + g + 1) * HEAD_DIM)

        n_keys = past + NEW_PAD
        s_s = _dot_nt(q, with_new(sel_rows(g), kvs_new[:, kc]))
        selx = _dot(selm, expand_ref[...])
        kpos = lax.broadcasted_iota(jnp.int32, (tp, n_keys), 1)
        ms = jnp.where((selx > 0.5) & (kpos <= tpos) & (kpos < past + n_tok), 1.0, 0.0)
        p_s = _masked_softmax(s_s, _tile_rows(ms, NSA_HPG) > 0.5)
        o_s = _dot(p_s.astype(BF16), with_new(sel_rows(N_NSA_GROUPS + g), kvs_new[:, vc]))

        s_w = _dot_nt(q, with_new(win_ref[pl.ds(g, n_buf, stride=n_slab), :], kvw_new[:, kc]))
        wpos = past - n_buf + lax.broadcasted_iota(jnp.int32, (tp, n_buf + NEW_PAD), 1)
        mw = jnp.where((wpos <= tpos) & (wpos > tpos - WINDOW) & (wpos < past + n_tok), 1.0, 0.0)
        p_w = _masked_softmax(s_w, _tile_rows(mw, NSA_HPG) > 0.5)
        o_w = _dot(p_w.astype(BF16), with_new(win_ref[pl.ds(N_NSA_GROUPS + g, n_buf, stride=n_slab), :],
                                              kvw_new[:, vc]))

        for h in range(NSA_HPG):
            hh = g * NSA_HPG + h
            rows = slice(h * tp, (h + 1) * tp)
            out = (gates[:, 3 * hh:3 * hh + 1] * o_c[rows] + gates[:, 3 * hh + 1:3 * hh + 2] * o_s[rows]
                   + gates[:, 3 * hh + 2:3 * hh + 3] * o_w[rows])
            o_ref[:, hh * HEAD_DIM:(hh + 1) * HEAD_DIM] = out.astype(BF16)


def _pad_tokens(a, n_seq, n_tok):
    a = a.reshape(n_seq, n_tok, a.shape[-1])
    return jnp.pad(a, ((0, 0), (0, TOK_PAD - n_tok), (0, 0)))


def _nsa_sample(page_table, cache_cmp, cache_sel, win_buf, kvs_new, kvw_new, qn, sm, cmp_w, n_tok):
    n_seq, n_pages = page_table.shape
    page = cache_cmp.shape[1]
    n_slab = 2 * N_NSA_GROUPS
    n_buf = win_buf.shape[1]
    past = n_pages * page
    cmp2 = cache_cmp.reshape(-1, HEAD_DIM)
    sel2 = cache_sel.reshape(-1, HEAD_DIM)
    win2 = win_buf.reshape(-1, HEAD_DIM)
    n_chunk = past // CMP_STRIDE
    n_cmp = (past + n_tok + CMP_STRIDE - 1) // CMP_STRIDE - CMP_RATIO + 1
    n_sel = (past + n_tok + SEL_BLOCK - 1) // SEL_BLOCK
    cover = _cover_matrix(n_chunk, min(n_cmp, n_chunk), n_sel)
    expand = _expand_matrix(past + NEW_PAD)
    w1k, pek, w2k, w1v, pev, w2v = cmp_w

    n_sub = NSA_SAMPLE_SUB
    assert n_seq % n_sub == 0

    def page_spec(sq, p):
        return pl.BlockSpec((page * n_slab, HEAD_DIM), lambda b, pt, sq=sq, p=p: (pt[b * n_sub + sq, p], 0))

    const2 = lambda b, pt: (0, 0)
    seq3 = lambda b, pt: (b, 0, 0)
    per_seq = []
    for sq in range(n_sub):
        per_seq += ([page_spec(sq, p) for p in range(n_pages)] + [page_spec(sq, p) for p in range(n_pages)]
                    + [pl.BlockSpec((n_buf * n_slab, HEAD_DIM), lambda b, pt, sq=sq: (b * n_sub + sq, 0))])
    in_specs = (per_seq + [
        pl.BlockSpec((n_sub, TOK_PAD, KV_W), seq3),
        pl.BlockSpec((n_sub, TOK_PAD, KV_W), seq3),
        pl.BlockSpec((n_sub, TOK_PAD, NSA_Q_W), seq3),
        pl.BlockSpec((n_sub, TOK_PAD, LANES), seq3),
        pl.BlockSpec(w1k.shape, const2),
        pl.BlockSpec(w1v.shape, const2),
        pl.BlockSpec(pek.shape, const2),
        pl.BlockSpec(pev.shape, const2),
        pl.BlockSpec(w2k.shape, const2),
        pl.BlockSpec(w2v.shape, const2),
        pl.BlockSpec(cover.shape, const2),
        pl.BlockSpec(expand.shape, const2),
    ])
    kern = functools.partial(_nsa_sample_kernel, n_pages=n_pages, page=page, n_buf=n_buf, n_tok=n_tok, n_sub=n_sub)
    grid_spec = pltpu.PrefetchScalarGridSpec(
        num_scalar_prefetch=1, grid=(n_seq // n_sub,), in_specs=in_specs,
        out_specs=pl.BlockSpec((n_sub, TOK_PAD, NSA_Q_W), seq3))
    return pl.pallas_call(
        kern,
        grid_spec=grid_spec,
        out_shape=jax.ShapeDtypeStruct((n_seq, TOK_PAD, NSA_Q_W), BF16),
        compiler_params=_cparams(("arbitrary",)),
        name="nsa_sample",
    )(page_table, *(([cmp2] * n_pages + [sel2] * n_pages + [win2]) * n_sub),
      _pad_tokens(kvs_new, n_seq, n_tok), _pad_tokens(kvw_new, n_seq, n_tok),
      _pad_tokens(qn, n_seq, n_tok), _pad_tokens(sm, n_seq, n_tok),
      w1k, w1v, pek, pev, w2k, w2v, cover, expand)


def _fox_sample_kernel(pt_ref, *refs, n_pages, page, n_tok):
    kv_pages = refs[:n_pages]
    lf_pages = refs[n_pages:2 * n_pages]
    k_new_ref, v_new_ref, q_ref, lfn_ref, o_ref, mask_scr, s_scr, kb_scr = refs[2 * n_pages:]
    del pt_ref
    nh = N_FOX_HEADS
    rows_pp = 2 * nh
    page_rows = page * rows_pp
    n_chunk = page_rows // LANES
    n_q = nh * n_tok
    assert rows_pp == 16 and LANES % rows_pp == 0 and n_q <= LANES

    @pl.when(pl.program_id(0) == 0)
    def _():
        qrow = lax.broadcasted_iota(jnp.int32, mask_scr.shape, 0)
        lane = lax.broadcasted_iota(jnp.int32, mask_scr.shape, 1)
        mask_scr[...] = jnp.where((lane & (rows_pp - 1)) == (qrow & (nh - 1)), 0.0, NEG_INF)

    x = jnp.concatenate([lf_pages[p][0] for p in range(n_pages)], axis=0)
    n_r = x.shape[0]
    la = lax.broadcasted_iota(jnp.int32, (LANES, LANES), 0)
    lb = lax.broadcasted_iota(jnp.int32, (LANES, LANES), 1)
    same = (la & (rows_pp - 1)) == (lb & (rows_pp - 1))
    u_in = jnp.where(same & (jnp.right_shift(la, 4) <= jnp.right_shift(lb, 4)), 1.0, 0.0).astype(BF16)
    u_all = jnp.where(same, 1.0, 0.0).astype(BF16)
    xh, xm, xl = _split3(x)
    within = _dot(xh, u_in) + _dot(xm, u_in) + _dot(xl, u_in)
    tot = _dot(xh, u_all) + _dot(xm, u_all) + _dot(xl, u_all)
    ra = lax.broadcasted_iota(jnp.int32, (n_r, n_r), 0)
    rb = lax.broadcasted_iota(jnp.int32, (n_r, n_r), 1)
    before = jnp.where(rb < ra, 1.0, 0.0).astype(BF16)
    th, tm_, tl = _split3(tot)
    offs = _dot(before, th) + _dot(before, tm_) + _dot(before, tl)
    f_end = offs[n_r - 1:n_r, :] + tot[n_r - 1:n_r, :]
    bias = (f_end - (within + offs)) * LOG2E

    q_all = q_ref[0]

    ch = mask_scr.shape[1]
    lt = ch // LANES
    steps = [(p, c) for p in range(n_pages) for c in range(page_rows // ch)]
    mx = jnp.full((n_q, LANES), -jnp.inf, F32)
    for p, c in steps:
        k_b = kv_pages[p][pl.ds(c * ch, ch), :].astype(BF16)
        kb_scr[p * page_rows + c * ch:p * page_rows + (c + 1) * ch, :] = k_b
        r0 = p * n_chunk + c * lt
        brow = jnp.concatenate([bias[r0 + i:r0 + i + 1, :] for i in range(lt)], axis=1)
        s = _dot_nt(q_all, k_b) + brow + mask_scr[...]
        s_scr[:, p * page_rows + c * ch:p * page_rows + (c + 1) * ch] = s
        mx = _lane_tile_max(mx, s)

    pad = jnp.zeros((LANES - n_q, HEAD_DIM), F32)
    k_new = jnp.concatenate([k_new_ref[0], pad], axis=0).astype(BF16)
    v_new = jnp.concatenate([v_new_ref[0], pad], axis=0).astype(BF16)
    g_in = jnp.where(((la & (nh - 1)) == (lb & (nh - 1))) & (la <= lb), 1.0, 0.0).astype(BF16)
    nh_, nm_, nl_ = _split3(lfn_ref[0])
    c_new = (_dot(nh_, g_in) + _dot(nm_, g_in) + _dot(nl_, g_in))[0:1, :] * LOG2E
    qrow = lax.broadcasted_iota(jnp.int32, (n_q, LANES), 0)
    lane = lax.broadcasted_iota(jnp.int32, (n_q, LANES), 1)
    ok = ((lane & (nh - 1)) == (qrow & (nh - 1))) & (lane <= qrow)
    s_new = jnp.where(ok, _dot_nt(q_all, k_new) - c_new, NEG_INF)
    m = jnp.max(jnp.maximum(mx, s_new), axis=-1, keepdims=True)

    carry = _exp_accumulate((jnp.zeros((n_q, LANES), F32), jnp.zeros((n_q, HEAD_DIM), F32)), s_new, m, v_new)
    for p, c in steps:
        rows = slice(p * page_rows + c * ch, p * page_rows + (c + 1) * ch)
        carry = _exp_accumulate(carry, s_scr[:, rows], m, kb_scr[rows, :], lane_shift=nh)
    ls, acc = carry
    o_ref[0] = (acc / jnp.sum(ls, axis=-1, keepdims=True)).astype(BF16)


def _fox_sample(page_table, cache_fox, cache_logf, kvf_new, qf, sm, n_tok):
    n_seq, n_pages = page_table.shape
    page = cache_fox.shape[1]
    nh = N_FOX_HEADS
    kv2 = cache_fox.reshape(-1, HEAD_DIM)
    rows_pp = 2 * nh
    n_chunk = page * rows_pp // LANES
    lf_c = jnp.pad(cache_logf, ((0, 0), (0, 0), (0, rows_pp - nh))).reshape(-1, n_chunk, LANES)
    n_q = n_tok * nh
    lfn = sm[:, LOGF_COL0:LOGF_COL0 + nh].reshape(n_seq, 1, n_q)
    lfn = jnp.pad(lfn, ((0, 0), (0, 7), (0, LANES - n_q)))
    kv_new = kvf_new.reshape(n_seq, n_tok, 2, nh, HEAD_DIM)
    k_new = kv_new[:, :, 0].reshape(n_seq, n_q, HEAD_DIM)
    v_new = kv_new[:, :, 1].reshape(n_seq, n_q, HEAD_DIM)
    q3 = qf.reshape(n_seq, n_q, HEAD_DIM)

    seq3 = lambda b, pt: (b, 0, 0)
    in_specs = ([pl.BlockSpec((page * rows_pp, HEAD_DIM), lambda b, pt, p=p: (pt[b, p], 0)) for p in range(n_pages)]
                + [pl.BlockSpec((1, n_chunk, LANES), lambda b, pt, p=p: (pt[b, p], 0, 0)) for p in range(n_pages)]
                + [pl.BlockSpec((1, n_q, HEAD_DIM), seq3),
                   pl.BlockSpec((1, n_q, HEAD_DIM), seq3),
                   pl.BlockSpec((1, n_q, HEAD_DIM), seq3),
                   pl.BlockSpec((1, 8, LANES), seq3)])
    kern = functools.partial(_fox_sample_kernel, n_pages=n_pages, page=page, n_tok=n_tok)
    n_rows = n_pages * page * rows_pp
    grid_spec = pltpu.PrefetchScalarGridSpec(
        num_scalar_prefetch=1, grid=(n_seq,), in_specs=in_specs,
        out_specs=pl.BlockSpec((1, n_q, HEAD_DIM), seq3),
        scratch_shapes=[pltpu.VMEM((n_q, FOX_SAMPLE_CH), F32),
                        pltpu.VMEM((n_q, n_rows), F32),
                        pltpu.VMEM((n_rows, HEAD_DIM), BF16)])
    out = pl.pallas_call(
        kern,
        grid_spec=grid_spec,
        out_shape=jax.ShapeDtypeStruct((n_seq, n_q, HEAD_DIM), BF16),
        compiler_params=_cparams(("arbitrary",)),
        name="fox_sample",
    )(page_table, *([kv2] * n_pages), *([lf_c] * n_pages), k_new, v_new, q3, lfn)
    return out.reshape(n_seq * n_tok, nh * HEAD_DIM)


def _postmix_kernel(on_ref, of_ref, gm0_ref, gm1_ref, x_ref, wn_ref, wf_ref, wo_ref, g_ref, y_ref):
    a = _dot(on_ref[...], wn_ref[...])
    b = _dot(of_ref[...], wf_ref[...])
    merged = gm0_ref[...] * a + gm1_ref[...] * b
    z = _dot(merged.astype(BF16), wo_ref[...])
    y_ref[...] = x_ref[...] + _rms(z, g_ref[...])


def _postmix(o_n, o_f, gm, x2, wn, wf, wo, g):
    n, d = x2.shape
    tm = 256
    row = lambda i: (i, 0)
    const = lambda i: (0, 0)
    return pl.pallas_call(
        _postmix_kernel,
        grid=(n // tm,),
        in_specs=[
            pl.BlockSpec((tm, NSA_Q_W), row),
            pl.BlockSpec((tm, FOX_W), row),
            pl.BlockSpec((tm, d), lambda i: (i, 0)),
            pl.BlockSpec((tm, d), lambda i: (i, 1)),
            pl.BlockSpec((tm, d), row),
            pl.BlockSpec(wn.shape, const),
            pl.BlockSpec(wf.shape, const),
            pl.BlockSpec(wo.shape, const),
            pl.BlockSpec((1, d), const),
        ],
        out_specs=pl.BlockSpec((tm, d), row),
        out_shape=jax.ShapeDtypeStruct((n, d), F32),
        compiler_params=_cparams(("arbitrary",)),
        name="postmix",
    )(o_n, o_f, gm, gm, x2, wn, wf, wo, g)


FFN_TM = 512
FFN_TF = 512
HALO = 16


def _ffn_kernel(*refs, seq_tiles, n_tok):
    if n_tok is None:
        (x_ref, xh_ref, g_ref, wg_ref, wu_ref, wd_ref, wc_ref, bc_ref, gp_ref,
         y_ref, gt_ref, h_scr, hh_scr, acc_scr) = refs
    else:
        (x_ref, s0_ref, s1_ref, g_ref, wg_ref, wu_ref, wd_ref, wc_ref, bc_ref, gp_ref,
         y_ref, gt_ref, h_scr, acc_scr) = refs
    i = pl.program_id(0)
    f = pl.program_id(1)
    tm = x_ref.shape[0]

    @pl.when(f == 0)
    def _():
        h_scr[...] = _rms(x_ref[...], g_ref[...]).astype(BF16)
        acc_scr[...] = jnp.zeros_like(acc_scr)
        if n_tok is None:
            hh_scr[...] = _rms(xh_ref[...], g_ref[...]).astype(BF16)

    h2 = h_scr[...]
    gate = _dot(h2, wg_ref[...])
    up = _dot(h2, wu_ref[...])
    row = lax.broadcasted_iota(jnp.int32, gate.shape, 0)
    r1 = pltpu.roll(gate, 1, axis=0)
    r2 = pltpu.roll(gate, 2, axis=0)
    if n_tok is None:
        first = (i % seq_tiles) == 0
        gh = jnp.where(first, 0.0, _dot(hh_scr[...], wg_ref[...]))
        p1 = gh[HALO - 1:HALO, :]
        p2 = gh[HALO - 2:HALO - 1, :]
        g1 = jnp.where(row == 0, p1, r1)
        g2 = jnp.where(row == 0, p2, jnp.where(row == 1, p1, r2))
        gt_ref[...] = gate[tm - 8:tm, :]
    else:
        assert n_tok & (n_tok - 1) == 0
        rt = row & (n_tok - 1)
        g1 = jnp.where(rt == 0, s1_ref[...], r1)
        g2 = jnp.where(rt == 0, s0_ref[...], jnp.where(rt == 1, s1_ref[...], r2))
        gt_ref[...] = gate
    wc = wc_ref[...]
    gc = bc_ref[...] + wc[0:1, :] * g2 + wc[1:2, :] * g1 + wc[2:3, :] * gate
    act = jax.nn.gelu(gc, approximate=True) * up
    acc_scr[...] += _dot(act.astype(BF16), wd_ref[...])

    @pl.when(f == pl.num_programs(1) - 1)
    def _():
        y_ref[...] = x_ref[...] + _rms(acc_scr[...], gp_ref[...])


def _ffn(x2, g_pre, w_up_b, w_down_b, w_conv, b_conv, g_post, *, seq_len=None, state=None):
    n, d = x2.shape
    d_ff = w_down_b.shape[0]
    tf = FFN_TF
    nf = d_ff // tf
    tm = min(FFN_TM, n)
    common_w = [
        pl.BlockSpec((1, d), lambda i, f: (0, 0)),
        pl.BlockSpec((d, tf), lambda i, f: (0, f)),
        pl.BlockSpec((d, tf), lambda i, f: (0, nf + f)),
        pl.BlockSpec((tf, d), lambda i, f: (f, 0)),
        pl.BlockSpec((CONV_WIDTH, tf), lambda i, f: (0, f)),
        pl.BlockSpec((1, tf), lambda i, f: (0, f)),
        pl.BlockSpec((1, d), lambda i, f: (0, 0)),
    ]
    w_args = (g_pre, w_up_b, w_up_b, w_down_b, w_conv, b_conv, g_post)
    row = lambda i, f: (i, 0)
    if state is None:
        seq_tiles = seq_len // tm
        halo_blocks = tm // HALO
        in_specs = [pl.BlockSpec((tm, d), row),
                    pl.BlockSpec((HALO, d), lambda i, f: (jnp.maximum(i * halo_blocks - 1, 0), 0))] + common_w
        args = (x2, x2) + w_args
        gt_shape = jax.ShapeDtypeStruct((n // tm * 8, d_ff), F32)
        gt_spec = pl.BlockSpec((8, tf), lambda i, f: (i, f))
        scratch = [pltpu.VMEM((tm, d), BF16), pltpu.VMEM((HALO, d), BF16), pltpu.VMEM((tm, d), F32)]
        kern = functools.partial(_ffn_kernel, seq_tiles=seq_tiles, n_tok=None)
    else:
        n_tok = n // state.shape[0]
        s0 = jnp.repeat(state[:, 0], n_tok, axis=0)
        s1 = jnp.repeat(state[:, 1], n_tok, axis=0)
        in_specs = [pl.BlockSpec((tm, d), row),
                    pl.BlockSpec((tm, tf), lambda i, f: (i, f)),
                    pl.BlockSpec((tm, tf), lambda i, f: (i, f))] + common_w
        args = (x2, s0, s1) + w_args
        gt_shape = jax.ShapeDtypeStruct((n, d_ff), F32)
        gt_spec = pl.BlockSpec((tm, tf), lambda i, f: (i, f))
        scratch = [pltpu.VMEM((tm, d), BF16), pltpu.VMEM((tm, d), F32)]
        kern = functools.partial(_ffn_kernel, seq_tiles=None, n_tok=n_tok)
    return pl.pallas_call(
        kern,
        grid=(n // tm, nf),
        in_specs=in_specs,
        out_specs=(pl.BlockSpec((tm, d), row), gt_spec),
        out_shape=(jax.ShapeDtypeStruct((n, d), F32), gt_shape),
        scratch_shapes=scratch,
        compiler_params=_cparams(("arbitrary", "arbitrary")),
        name="ffn",
    )(*args)


def _rope_tables(pos):
    half = HEAD_DIM // 2
    inv_freq = ROPE_THETA ** (-jnp.arange(half, dtype=F32) / half)
    ang = pos.astype(F32)[:, None] * inv_freq[None, :]
    cos, sin = jnp.cos(ang), jnp.sin(ang)
    return jnp.concatenate([cos, cos], axis=-1), jnp.concatenate([-sin, sin], axis=-1)


def _cmp_weights(w1, pe, w2):
    w1r = w1.reshape(CMP_RATIO, CMP_STRIDE * HEAD_DIM, HEAD_DIM)
    w1cat = jnp.concatenate([w1r[r] for r in range(CMP_RATIO)], axis=1).astype(BF16)
    pe8 = jnp.pad(pe.reshape(CMP_RATIO, CMP_STRIDE * HEAD_DIM), ((0, 8 - CMP_RATIO), (0, 0)))
    return w1cat, pe8, w2.astype(BF16)


def kernel(x_prompt, x_sample, cache_nsa_cmp_kv, cache_nsa_sel_kv, cache_nsa_win_kv, cache_fox_kv, cache_fox_logf, state_ffn_conv, page_table, g_pre_mix, w_in, b_fgt, w_cmp_k1, pe_cmp_k, w_cmp_k2, w_cmp_v1, pe_cmp_v, w_cmp_v2, w_nsa_o, w_fox_o, w_out, g_post_mix, g_pre_ffn, w_up, w_conv, b_conv, w_down, g_post_ffn):
    b_p, t_p, d = x_prompt.shape
    b_s, t_s, _ = x_sample.shape
    depth = w_in.shape[0]
    page = cache_nsa_cmp_kv.shape[2]
    past = page_table.shape[1] * page
    g_n, n_h = N_NSA_GROUPS, N_FOX_HEADS

    cos_p, sin_p = _rope_tables(jnp.tile(jnp.arange(t_p), b_p))
    cos_s, sin_s = _rope_tables(jnp.tile(past + jnp.arange(t_s), b_s))

    y_p = x_prompt.reshape(b_p * t_p, d)
    y_s = x_sample.reshape(b_s * t_s, d)
    outs = {k: [] for k in ('cmp_p', 'cmp_s', 'sel_p', 'sel_s', 'win_p', 'win_s',
                            'fox_p', 'fox_s', 'lf_p', 'lf_s', 'conv_p', 'conv_s')}
    o_q = NSA_Q_W
    o_g = o_q + 3 * KV_W
    o_f = o_g + N_GATE_COLS
    o_ff = o_f + 3 * FOX_W
    o_m = o_ff + n_h
    for l in range(depth):
        w = w_in[l]
        w_main = jnp.concatenate([w[:, :o_g], w[:, o_f:o_ff], w[:, o_m:]], axis=1).astype(BF16)
        w_small = jnp.concatenate([w[:, o_g:o_f], w[:, o_ff:o_m],
                                   jnp.zeros((d, LANES - N_GATE_COLS - n_h), F32)], axis=1).astype(BF16)
        bf_row = jnp.zeros((1, LANES), F32).at[0, LOGF_COL0:LOGF_COL0 + n_h].set(b_fgt[l])
        g1 = g_pre_mix[l][None, :]
        cmp_k = _cmp_weights(w_cmp_k1[l], pe_cmp_k[l], w_cmp_k2[l])
        cmp_v = _cmp_weights(w_cmp_v1[l], pe_cmp_v[l], w_cmp_v2[l])
        wn, wf, wo = w_nsa_o[l].astype(BF16), w_fox_o[l].astype(BF16), w_out[l].astype(BF16)
        wu, wd = w_up[l].astype(BF16), w_down[l].astype(BF16)
        ffn_w = (g_pre_ffn[l][None, :], wu, wd, w_conv[l], b_conv[l][None, :], g_post_ffn[l][None, :])

        (qn, kvc, kvs, kvw, qf, kvf, gm, sm, kvs_b, kvw_b, kvf_b) = _project(y_p, g1, cos_p, sin_p, w_main, w_small, bf_row)
        w1cat = jnp.stack([cmp_k[0], cmp_v[0]])
        pe8 = jnp.stack([cmp_k[1], cmp_v[1]])
        w2 = jnp.stack([cmp_k[2], cmp_v[2]])
        ckv = _compress_prompt(kvc, b_p, t_p, w1cat, pe8, w2)
        frow = _fcum_prompt(sm, b_p, t_p)
        o_n = _nsa_prompt(qn, sm, ckv, kvs_b, kvw_b, b_p, t_p)
        o_fx = _fox_prompt(qf, kvf_b, frow, b_p, t_p)
        y1 = _postmix(o_n, o_fx, gm, y_p, wn, wf, wo, g_post_mix[l][None, :])
        y_p, gt = _ffn(y1, *ffn_w, seq_len=t_p)
        n_win = min(WINDOW, t_p)
        outs['cmp_p'].append(kvc.reshape(b_p, t_p, 2, g_n, HEAD_DIM))
        outs['sel_p'].append(kvs.reshape(b_p, t_p, 2, g_n, HEAD_DIM))
        outs['win_p'].append(kvw.reshape(b_p, t_p, 2, g_n, HEAD_DIM)[:, t_p - n_win:])
        outs['fox_p'].append(kvf.reshape(b_p, t_p, 2, n_h, HEAD_DIM))
        outs['lf_p'].append(sm[:, LOGF_COL0:LOGF_COL0 + n_h].reshape(b_p, t_p, n_h))
        tiles_per_seq = t_p // FFN_TM
        gt = gt.reshape(b_p, tiles_per_seq, 8, -1)
        outs['conv_p'].append(gt[:, -1, 8 - (CONV_WIDTH - 1):])

        (qn, kvc, kvs, kvw, qf, kvf, gm, sm, _, _, _) = _project(y_s, g1, cos_s, sin_s, w_main, w_small, bf_row)
        win_buf = cache_nsa_win_kv[l]
        o_n = _nsa_sample(page_table, cache_nsa_cmp_kv[l], cache_nsa_sel_kv[l], win_buf, kvs, kvw, qn, sm,
                          cmp_k + cmp_v, t_s)
        o_fx = _fox_sample(page_table, cache_fox_kv[l], cache_fox_logf[l], kvf, qf, sm, t_s)
        o_n = o_n[:, :t_s].reshape(b_s * t_s, -1)
        y1 = _postmix(o_n, o_fx, gm, y_s, wn, wf, wo, g_post_mix[l][None, :])
        y_s, gt = _ffn(y1, *ffn_w, state=state_ffn_conv[l])
        kw_new = kvw.reshape(b_s, t_s, 2, g_n, HEAD_DIM)
        n_win = min(WINDOW, win_buf.shape[1] + t_s)
        outs['cmp_s'].append(kvc.reshape(b_s, t_s, 2, g_n, HEAD_DIM))
        outs['sel_s'].append(kvs.reshape(b_s, t_s, 2, g_n, HEAD_DIM))
        outs['win_s'].append(jnp.concatenate([win_buf, kw_new], axis=1)[:, -n_win:])
        outs['fox_s'].append(kvf.reshape(b_s, t_s, 2, n_h, HEAD_DIM))
        outs['lf_s'].append(sm[:, LOGF_COL0:LOGF_COL0 + n_h].reshape(b_s, t_s, n_h))
        gfull = jnp.concatenate([state_ffn_conv[l], gt.reshape(b_s, t_s, -1)], axis=1)
        outs['conv_s'].append(gfull[:, t_s:])

    st = {k: jnp.stack(v) for k, v in outs.items()}
    return (y_p.reshape(b_p, t_p, d), y_s.reshape(b_s, t_s, d),
            st['cmp_p'], st['cmp_s'], st['sel_p'], st['sel_s'], st['win_p'], st['win_s'],
            st['fox_p'], st['fox_s'], st['lf_p'], st['lf_s'], st['conv_p'], st['conv_s'])
```

```python
import functools

import numpy as np
import jax
import jax.numpy as jnp
from jax import lax
from jax.experimental import pallas as pl
from jax.experimental.pallas import tpu as pltpu

F32 = jnp.float32
BF16 = jnp.bfloat16

HEAD_DIM = 128
N_NSA_HEADS = 8
N_NSA_GROUPS = 2
NSA_HPG = N_NSA_HEADS // N_NSA_GROUPS
N_FOX_HEADS = 8
CMP_BLOCK = 32
CMP_STRIDE = 16
CMP_RATIO = CMP_BLOCK // CMP_STRIDE
SEL_BLOCK = 64
SEL_TOPK = 16
N_LOCAL_BLOCKS = 2
WINDOW = 512
CONV_WIDTH = 3
ROPE_THETA = 10000.0
RMS_EPS = 1e-6
FORCE_BONUS = 1e4
NEG_INF = -1e30
LOG2E = 1.4426950408889634
QK_SCALE = HEAD_DIM ** -0.5 * LOG2E

N_GATE_COLS = N_NSA_HEADS * 3
LOGF_COL0 = N_GATE_COLS
LANES = 128
VMEM_LIMIT = 56 * 1024 * 1024

PROJ_TN = 512
PROJ_TM = 1024
KV_W = 2 * N_NSA_GROUPS * HEAD_DIM
NSA_Q_W = N_NSA_HEADS * HEAD_DIM
FOX_W = N_FOX_HEADS * HEAD_DIM


def _cparams(sem):
    return pltpu.CompilerParams(dimension_semantics=sem, vmem_limit_bytes=VMEM_LIMIT)


def _dot(a, b):
    return jnp.dot(a, b, preferred_element_type=F32)


def _dot_nt(a, b):
    return lax.dot_general(a, b, (((1,), (1,)), ((), ())), preferred_element_type=F32)


def _rms(x, g):
    return x * lax.rsqrt(jnp.mean(x * x, axis=-1, keepdims=True) + RMS_EPS) * g


def _masked_softmax(s, mask):
    sm = jnp.where(mask, s, NEG_INF)
    m = jnp.max(sm, axis=-1, keepdims=True)
    e = jnp.where(mask, jnp.exp2(sm - m), 0.0)
    l = jnp.sum(e, axis=-1, keepdims=True)
    return e / jnp.where(l > 0.0, l, 1.0)


def _transpose_rows(src_ref, dst_ref, cols=slice(None)):
    for c in range(src_ref.shape[0] // LANES):
        rows = slice(c * LANES, (c + 1) * LANES)
        dst_ref[:, rows] = src_ref[rows, cols].astype(F32).T.astype(BF16)


def _masked_exp(s, mask):
    sm = jnp.where(mask, s, NEG_INF)
    e = jnp.where(mask, jnp.exp2(sm - jnp.max(sm, axis=-1, keepdims=True)), 0.0)
    l = jnp.sum(e, axis=-1, keepdims=True)
    return e, jnp.where(l > 0.0, l, 1.0)


def _lane_tile_max(mx, s):
    for c in range(s.shape[1] // LANES):
        mx = jnp.maximum(mx, s[:, c * LANES:(c + 1) * LANES])
    return mx


def _exp_accumulate(carry, s, m, v, lane_shift=0):
    ls, acc = carry
    p = jnp.exp2(s - m)
    tiles = [p[:, c * LANES:(c + 1) * LANES] for c in range(s.shape[1] // LANES)]
    for t in tiles:
        ls = ls + t
    if lane_shift:
        p = jnp.concatenate([pltpu.roll(t, lane_shift, axis=1) for t in tiles], axis=1)
    return ls, acc + _dot(p.astype(BF16), v)


def _split3(x):
    hi = x.astype(BF16)
    r = x - hi.astype(F32)
    mid = r.astype(BF16)
    lo = (r - mid.astype(F32)).astype(BF16)
    return hi, mid, lo


def _topk_mask(score, k, n_sel):
    st = score.T
    nv = -(-n_sel // 8)
    slabs = [st[8 * v:8 * v + 8, :] for v in range(nv)]
    sub = lax.broadcasted_iota(jnp.int32, (8, LANES), 0)
    ranks = [jnp.zeros((8, LANES), F32) for _ in range(nv)]
    for b2 in range(n_sel):
        row = jnp.broadcast_to(st[b2:b2 + 1, :], (8, LANES))
        for v in range(nv):
            if b2 < 8 * v:
                beats = row >= slabs[v]
            elif b2 >= 8 * v + 8:
                beats = row > slabs[v]
            else:
                beats = (row > slabs[v]) | ((row == slabs[v]) & (sub > b2 - 8 * v))
            ranks[v] = ranks[v] + jnp.where(beats, 1.0, 0.0)
    sel = [jnp.where((ranks[v] < k) & (sub + 8 * v < n_sel), 1.0, 0.0) for v in range(nv)]
    sel_t = jnp.concatenate(sel + [jnp.zeros((LANES - 8 * nv, LANES), F32)], axis=0)
    return sel_t.T


def _sel_scores(imp, tpos, n_sel):
    bidx = lax.broadcasted_iota(jnp.int32, imp.shape, 1)
    cur = jnp.right_shift(tpos, 6)
    valid = bidx <= cur
    forced = (bidx == 0) | (valid & (bidx > cur - N_LOCAL_BLOCKS))
    score = jnp.where(valid, jnp.where(forced, imp + FORCE_BONUS, imp), NEG_INF)
    return jnp.where(bidx < n_sel, score, -jnp.inf)


def _proj_kernel(x_ref, g_ref, cos_ref, sin_ref, w_ref, ws_ref, bf_ref,
                 qn_ref, kvc_ref, kvs_ref, kvw_ref, qf_ref, kvf_ref, gm_ref, sm_ref,
                 kvsb_ref, kvwb_ref, kvfb_ref, h_scr):
    j = pl.program_id(1)

    @pl.when(j == 0)
    def _():
        x = x_ref[...]
        y = x * lax.rsqrt(jnp.mean(x * x, axis=-1, keepdims=True) + RMS_EPS)
        h = (y * g_ref[...]).astype(BF16)
        h_scr[...] = h
        s = _dot(h, ws_ref[...])
        lane = lax.broadcasted_iota(jnp.int32, s.shape, 1)
        z = s + bf_ref[...]
        lf = jnp.minimum(z, 0.0) - jnp.log1p(jnp.exp(-jnp.abs(z)))
        sm_ref[...] = jnp.where(lane < N_GATE_COLS, jax.nn.sigmoid(s),
                                jnp.where(lane < LOGF_COL0 + N_FOX_HEADS, lf, 0.0))

    cos = cos_ref[...]
    sin = sin_ref[...]
    half_w = PROJ_TN // 2
    halves = [slice(0, half_w), slice(half_w, PROJ_TN)]

    def mm(cols):
        return _dot(h_scr[...], w_ref[:, cols])

    def rope2(a):
        return jnp.concatenate(
            [a[:, k * HEAD_DIM:(k + 1) * HEAD_DIM] * cos
             + pltpu.roll(a[:, k * HEAD_DIM:(k + 1) * HEAD_DIM], HEAD_DIM // 2, axis=1) * sin
             for k in range(half_w // HEAD_DIM)], axis=1)

    def kv_rows(ref, bref):
        assert half_w == N_NSA_GROUPS * HEAD_DIM
        for cols, is_key in zip(halves, (True, False)):
            a = mm(cols)
            a = rope2(a) if is_key else a
            ref[:, cols] = a
            if bref is not None:
                bref[:, cols] = a.astype(BF16)

    @pl.when(j < 2)
    def _():
        for cols in halves:
            qn_ref[:, cols] = (rope2(mm(cols)) * QK_SCALE).astype(BF16)

    @pl.when(j == 2)
    def _():
        kv_rows(kvc_ref, None)

    @pl.when(j == 3)
    def _():
        kv_rows(kvs_ref, kvsb_ref)

    @pl.when(j == 4)
    def _():
        kv_rows(kvw_ref, kvwb_ref)

    @pl.when((j >= 5) & (j < 7))
    def _():
        for cols in halves:
            qf_ref[:, cols] = (mm(cols) * QK_SCALE).astype(BF16)

    @pl.when((j >= 7) & (j < 11))
    def _():
        for cols in halves:
            a = mm(cols)
            kvf_ref[:, cols] = a
            kvfb_ref[:, cols] = a.astype(BF16)

    @pl.when(j >= 11)
    def _():
        for cols in halves:
            gm_ref[:, cols] = jax.nn.sigmoid(mm(cols))


def _project(x2, g, cos2, sin2, w_main, w_small, bf_row):
    n, d = x2.shape
    tm = min(PROJ_TM, n)
    n_j = w_main.shape[1] // PROJ_TN
    tn = PROJ_TN

    def clip(lo, hi):
        return lambda i, j: (i, jnp.clip(j - lo, 0, hi - lo))

    row = lambda i, j: (i, 0)
    out_shape = (
        jax.ShapeDtypeStruct((n, NSA_Q_W), BF16),
        jax.ShapeDtypeStruct((n, KV_W), F32),
        jax.ShapeDtypeStruct((n, KV_W), F32),
        jax.ShapeDtypeStruct((n, KV_W), F32),
        jax.ShapeDtypeStruct((n, FOX_W), BF16),
        jax.ShapeDtypeStruct((n, 2 * FOX_W), F32),
        jax.ShapeDtypeStruct((n, 2 * d), F32),
        jax.ShapeDtypeStruct((n, LANES), F32),
        jax.ShapeDtypeStruct((n, KV_W), BF16),
        jax.ShapeDtypeStruct((n, KV_W), BF16),
        jax.ShapeDtypeStruct((n, 2 * FOX_W), BF16),
    )
    out_specs = (
        pl.BlockSpec((tm, tn), clip(0, 1)),
        pl.BlockSpec((tm, tn), row),
        pl.BlockSpec((tm, tn), row),
        pl.BlockSpec((tm, tn), row),
        pl.BlockSpec((tm, tn), clip(5, 6)),
        pl.BlockSpec((tm, tn), clip(7, 10)),
        pl.BlockSpec((tm, tn), clip(11, 18)),
        pl.BlockSpec((tm, LANES), row),
        pl.BlockSpec((tm, tn), row),
        pl.BlockSpec((tm, tn), row),
        pl.BlockSpec((tm, tn), clip(7, 10)),
    )
    in_specs = [
        pl.BlockSpec((tm, d), row, pipeline_mode=pl.Buffered(1)),
        pl.BlockSpec((1, d), lambda i, j: (0, 0)),
        pl.BlockSpec((tm, LANES), row),
        pl.BlockSpec((tm, LANES), row),
        pl.BlockSpec((d, tn), lambda i, j: (0, j)),
        pl.BlockSpec((d, LANES), lambda i, j: (0, 0)),
        pl.BlockSpec((1, LANES), lambda i, j: (0, 0)),
    ]
    return pl.pallas_call(
        _proj_kernel,
        grid=(n // tm, n_j),
        in_specs=in_specs,
        out_specs=out_specs,
        out_shape=out_shape,
        scratch_shapes=[pltpu.VMEM((tm, d), BF16)],
        compiler_params=_cparams(("arbitrary", "arbitrary")),
        name="proj",
    )(x2, g, cos2, sin2, w_main, w_small, bf_row)


def _compress_tail(xc, w1, pe8, w2):
    n = xc.shape[0]
    part = _dot(xc, w1)
    pp = _dot(pe8.astype(BF16), w1)
    pe_term = pp[0:1, :HEAD_DIM] + pp[1:2, HEAD_DIM:]
    hid = pe_term + part[:, :HEAD_DIM] + pltpu.roll(part[:, HEAD_DIM:], n - 1, axis=0)
    return _dot(jax.nn.gelu(hid, approximate=True).astype(BF16), w2)


def _cmp_prompt_kernel(x_ref, w1_ref, pe_ref, w2_ref, o_ref):
    n = x_ref.shape[0] // CMP_STRIDE
    xc = jnp.concatenate([x_ref[pl.ds(i, n, stride=CMP_STRIDE), :] for i in range(CMP_STRIDE)],
                         axis=1).astype(BF16)
    o_ref[0, 0] = _compress_tail(xc, w1_ref[0], pe_ref[0], w2_ref[0]).astype(BF16)


def _compress_prompt(kvc, b_n, t_n, w1cat, pe8, w2):
    n = t_n // CMP_STRIDE
    return pl.pallas_call(
        _cmp_prompt_kernel,
        grid=(b_n, 2 * N_NSA_GROUPS),
        in_specs=[
            pl.BlockSpec((t_n, HEAD_DIM), lambda b, s: (b, s)),
            pl.BlockSpec((1, CMP_STRIDE * HEAD_DIM, 2 * HEAD_DIM), lambda b, s: (s // N_NSA_GROUPS, 0, 0)),
            pl.BlockSpec((1, 8, CMP_STRIDE * HEAD_DIM), lambda b, s: (s // N_NSA_GROUPS, 0, 0)),
            pl.BlockSpec((1, HEAD_DIM, HEAD_DIM), lambda b, s: (s // N_NSA_GROUPS, 0, 0)),
        ],
        out_specs=pl.BlockSpec((1, 1, n, HEAD_DIM), lambda b, s: (b, s, 0, 0)),
        out_shape=jax.ShapeDtypeStruct((b_n, 2 * N_NSA_GROUPS, n, HEAD_DIM), BF16),
        compiler_params=_cparams(("arbitrary", "arbitrary")),
        name="cmp_prompt",
    )(kvc, w1cat, pe8, w2)


def _fcum_kernel(x_ref, frow_ref, carry_scr):
    i = pl.program_id(1)

    @pl.when(i == 0)
    def _():
        carry_scr[...] = jnp.zeros_like(carry_scr)

    x = x_ref[...]
    tb = x.shape[0]
    r = lax.broadcasted_iota(jnp.int32, (tb, tb), 0)
    c = lax.broadcasted_iota(jnp.int32, (tb, tb), 1)
    tri = jnp.where(r >= c, 1.0, 0.0).astype(BF16)
    hi, mid, lo = _split3(x)
    cs = _dot(tri, hi) + _dot(tri, mid) + _dot(tri, lo) + carry_scr[0:1, :]
    carry_scr[...] = jnp.broadcast_to(cs[tb - 1:tb, :], carry_scr.shape)
    frow_ref[0] = cs.T[LOGF_COL0:LOGF_COL0 + N_FOX_HEADS, :] * LOG2E


def _fcum_prompt(sm, b_n, t_n):
    tb = 512
    nb = t_n // tb
    return pl.pallas_call(
        _fcum_kernel,
        grid=(b_n, nb),
        in_specs=[pl.BlockSpec((tb, LANES), lambda b, i: (b * nb + i, 0))],
        out_specs=pl.BlockSpec((1, N_FOX_HEADS, tb), lambda b, i: (b, 0, i)),
        out_shape=jax.ShapeDtypeStruct((b_n, N_FOX_HEADS, t_n), F32),
        scratch_shapes=[pltpu.VMEM((8, LANES), F32)],
        compiler_params=_cparams(("arbitrary", "arbitrary")),
        name="fcum_prompt",
    )(sm)


NSA_TQ = 128
NSA_NSUB = 2


def _tile_rows(a, reps):
    return jnp.concatenate([a] * reps, axis=0)


def _nsa_front_kernel(q_ref, sm_ref, ck_ref, kw_ref, cover_ref, selm_ref, ocw_ref, kwt_scr, *, n_cmp, n_sel):
    i = pl.program_id(1)
    tq, nsub = NSA_TQ, NSA_NSUB
    t0 = i * (tq * nsub)
    n_ck = ck_ref.shape[2]
    band = WINDOW + tq
    tpos = [t0 + u * tq + lax.broadcasted_iota(jnp.int32, (tq, 1), 0) for u in range(nsub)]

    @pl.when(i == 0)
    def _():
        for g in range(N_NSA_GROUPS):
            _transpose_rows(kw_ref, kwt_scr.at[g], slice(g * HEAD_DIM, (g + 1) * HEAD_DIM))

    for g in range(N_NSA_GROUPS):
        ck = ck_ref[0, g]
        cv = ck_ref[0, N_NSA_GROUPS + g]
        o_cs, o_ws, selms = [], [], []
        for u in range(nsub):
            q = jnp.concatenate([q_ref[u * tq:(u + 1) * tq, (g * NSA_HPG + h) * HEAD_DIM:(g * NSA_HPG + h + 1) * HEAD_DIM]
                                 for h in range(NSA_HPG)], axis=0)
            s_c = _dot_nt(q, ck)
            cidx = lax.broadcasted_iota(jnp.int32, (tq, n_ck), 1)
            mc = jnp.where((cidx * CMP_STRIDE + CMP_BLOCK - 1 <= tpos[u]) & (cidx < n_cmp), 1.0, 0.0)
            e_c, l_c = _masked_exp(s_c, _tile_rows(mc, NSA_HPG) > 0.5)
            e_cb = e_c.astype(BF16)
            imp4 = _dot(e_cb, cover_ref[...]) / l_c
            imp = imp4[0:tq] + imp4[tq:2 * tq] + imp4[2 * tq:3 * tq] + imp4[3 * tq:4 * tq]
            selms.append(_topk_mask(_sel_scores(imp, tpos[u], n_sel), min(SEL_TOPK, n_sel), n_sel).astype(BF16))
            o_cs.append(_dot(e_cb, cv) / l_c)
            w0 = pl.multiple_of(jnp.maximum(t0 + u * tq - WINDOW, 0), tq)
            vw = kw_ref[pl.ds(w0, band), (N_NSA_GROUPS + g) * HEAD_DIM:(N_NSA_GROUPS + g + 1) * HEAD_DIM]
            s_w = _dot(q, kwt_scr[g, :, pl.ds(w0, band)])
            wpos = w0 + lax.broadcasted_iota(jnp.int32, (tq, band), 1)
            bw = jnp.where((wpos <= tpos[u]) & (wpos > tpos[u] - WINDOW), 0.0, NEG_INF)
            s_w = s_w + _tile_rows(bw, NSA_HPG)
            e_w = jnp.exp2(s_w - jnp.max(s_w, axis=-1, keepdims=True))
            o_ws.append(_dot(e_w.astype(BF16), vw) / jnp.sum(e_w, axis=-1, keepdims=True))

        for u in range(nsub):
            rows_u = slice(u * tq, (u + 1) * tq)
            selm_ref[rows_u, g * LANES:(g + 1) * LANES] = selms[u]
            gates = sm_ref[rows_u, :]
            for h in range(NSA_HPG):
                hh = g * NSA_HPG + h
                rows = slice(h * tq, (h + 1) * tq)
                ocw_ref[rows_u, hh * HEAD_DIM:(hh + 1) * HEAD_DIM] = (
                    gates[:, 3 * hh:3 * hh + 1] * o_cs[u][rows] + gates[:, 3 * hh + 2:3 * hh + 3] * o_ws[u][rows])


def _cover_matrix(n_rows, n_cmp, n_sel):
    c = np.arange(n_rows)[:, None] * CMP_STRIDE
    b = np.arange(LANES)[None, :] * SEL_BLOCK
    m = (c < b + SEL_BLOCK) & (c + CMP_BLOCK > b) & (np.arange(n_rows)[:, None] < n_cmp) & (np.arange(LANES)[None, :] < n_sel)
    return jnp.asarray(m.astype(np.float32), dtype=BF16)


def _expand_matrix(n_keys):
    m = (np.arange(n_keys)[None, :] // SEL_BLOCK) == np.arange(LANES)[:, None]
    return jnp.asarray(m.astype(np.float32), dtype=BF16)


SEL_MASK_BIG = 2.0 ** 100


def _nsa_sel_kernel(q_ref, k_ref, v_ref, selm_ref, sm_ref, ocw_ref, expand_ref, o_ref, s_scr, kt_scr):
    hh = pl.program_id(1) * NSA_HPG + pl.program_id(2)
    blk = FOX_BLK

    @pl.when(pl.program_id(2) == 0)
    def _():
        _transpose_rows(k_ref, kt_scr.at[0:HEAD_DIM])
        kt_scr[HEAD_DIM:, :] = expand_ref[...]

    causal = lax.broadcasted_iota(jnp.int32, (blk, blk), 1) <= lax.broadcasted_iota(jnp.int32, (blk, 1), 0)
    zeros = jnp.zeros((blk, LANES), F32)
    lane = lax.broadcasted_iota(jnp.int32, (blk, LANES), 1)
    for qb in range(q_ref.shape[0] // blk):
        rows = slice(qb * blk, (qb + 1) * blk)
        unsel = ((selm_ref[rows, :].astype(F32) - 1.0) * SEL_MASK_BIG).astype(BF16)
        q = jnp.concatenate([q_ref[rows, :], unsel], axis=1)
        mx = jnp.full((blk, LANES), -jnp.inf, F32)
        for kt in range(qb + 1):
            keys = slice(kt * blk, (kt + 1) * blk)
            s = _dot(q, kt_scr[:, keys])
            if kt == qb:
                s = jnp.where(causal, s, NEG_INF)
            s_scr[:, keys] = s
            mx = _lane_tile_max(mx, s)
        m = jnp.max(mx, axis=-1, keepdims=True)
        carry = (zeros, zeros)
        for kt in range(qb + 1):
            keys = slice(kt * blk, (kt + 1) * blk)
            carry = _exp_accumulate(carry, s_scr[:, keys], m, v_ref[keys, :])
        ls, acc = carry
        gate = jnp.sum(jnp.where(lane == 3 * hh + 1, sm_ref[rows, :], 0.0), axis=-1, keepdims=True)
        out = ocw_ref[rows, :] + gate * (acc / jnp.sum(ls, axis=-1, keepdims=True))
        o_ref[rows, :] = out.astype(BF16)


def _nsa_prompt(qn, sm, ckv, kvs_b, kvw_b, b_n, t_n):
    blk = NSA_TQ * NSA_NSUB
    nq = t_n // blk
    n_cmp = t_n // CMP_STRIDE - CMP_RATIO + 1
    n_sel = t_n // SEL_BLOCK
    n_ck = ckv.shape[2]
    n_g = N_NSA_GROUPS
    cover = _cover_matrix(n_ck, n_cmp, n_sel)
    expand = _expand_matrix(t_n)
    selm, ocw = pl.pallas_call(
        functools.partial(_nsa_front_kernel, n_cmp=n_cmp, n_sel=n_sel),
        grid=(b_n, nq),
        in_specs=[
            pl.BlockSpec((blk, NSA_Q_W), lambda b, i: (b * nq + i, 0)),
            pl.BlockSpec((blk, LANES), lambda b, i: (b * nq + i, 0)),
            pl.BlockSpec((1, 2 * n_g, n_ck, HEAD_DIM), lambda b, i: (b, 0, 0, 0)),
            pl.BlockSpec((t_n, KV_W), lambda b, i: (b, 0)),
            pl.BlockSpec((n_ck, LANES), lambda b, i: (0, 0)),
        ],
        out_specs=(pl.BlockSpec((blk, n_g * LANES), lambda b, i: (b * nq + i, 0)),
                   pl.BlockSpec((blk, NSA_Q_W), lambda b, i: (b * nq + i, 0))),
        out_shape=(jax.ShapeDtypeStruct((b_n * t_n, n_g * LANES), BF16),
                   jax.ShapeDtypeStruct((b_n * t_n, NSA_Q_W), F32)),
        scratch_shapes=[pltpu.VMEM((n_g, HEAD_DIM, t_n), BF16)],
        compiler_params=_cparams(("arbitrary", "arbitrary")),
        name="nsa_front",
    )(qn, sm, ckv, kvw_b, cover)

    hpg = NSA_HPG
    return pl.pallas_call(
        _nsa_sel_kernel,
        grid=(b_n, n_g, hpg),
        in_specs=[
            pl.BlockSpec((t_n, HEAD_DIM), lambda b, g, h: (b, g * hpg + h)),
            pl.BlockSpec((t_n, HEAD_DIM), lambda b, g, h: (b, g)),
            pl.BlockSpec((t_n, HEAD_DIM), lambda b, g, h: (b, n_g + g)),
            pl.BlockSpec((t_n, LANES), lambda b, g, h: (b, g)),
            pl.BlockSpec((t_n, LANES), lambda b, g, h: (b, 0)),
            pl.BlockSpec((t_n, HEAD_DIM), lambda b, g, h: (b, g * hpg + h)),
            pl.BlockSpec((LANES, t_n), lambda b, g, h: (0, 0)),
        ],
        out_specs=pl.BlockSpec((t_n, HEAD_DIM), lambda b, g, h: (b, g * hpg + h)),
        out_shape=jax.ShapeDtypeStruct((b_n * t_n, NSA_Q_W), BF16),
        scratch_shapes=[pltpu.VMEM((FOX_BLK, t_n), F32), pltpu.VMEM((2 * HEAD_DIM, t_n), BF16)],
        compiler_params=_cparams(("arbitrary", "arbitrary", "arbitrary")),
        name="nsa_sel",
    )(qn, kvs_b, kvs_b, selm, sm, ocw, expand)


FOX_BLK = 512


def _fox_prompt_kernel(q_ref, k_ref, v_ref, frow_ref, o_ref, s_scr, kt_scr):
    h = pl.program_id(1)
    blk = FOX_BLK
    assert q_ref.shape[0] % blk == 0
    _transpose_rows(k_ref, kt_scr)

    causal = lax.broadcasted_iota(jnp.int32, (blk, blk), 1) <= lax.broadcasted_iota(jnp.int32, (blk, 1), 0)
    zeros = jnp.zeros((blk, LANES), F32)
    for qb in range(q_ref.shape[0] // blk):
        q = q_ref[qb * blk:(qb + 1) * blk, :]
        mx = jnp.full((blk, LANES), -jnp.inf, F32)
        for kt in range(qb + 1):
            keys = slice(kt * blk, (kt + 1) * blk)
            s = _dot(q, kt_scr[:, keys]) - frow_ref[0, pl.ds(h, 1), keys]
            if kt == qb:
                s = jnp.where(causal, s, NEG_INF)
            s_scr[:, keys] = s
            mx = _lane_tile_max(mx, s)
        m = jnp.max(mx, axis=-1, keepdims=True)
        carry = (zeros, zeros)
        for kt in range(qb + 1):
            keys = slice(kt * blk, (kt + 1) * blk)
            carry = _exp_accumulate(carry, s_scr[:, keys], m, v_ref[keys, :])
        ls, acc = carry
        o_ref[qb * blk:(qb + 1) * blk, :] = (acc / jnp.sum(ls, axis=-1, keepdims=True)).astype(BF16)


def _fox_prompt(qf, kvf_b, frow, b_n, t_n):
    nh = N_FOX_HEADS
    return pl.pallas_call(
        _fox_prompt_kernel,
        grid=(b_n, nh),
        in_specs=[
            pl.BlockSpec((t_n, HEAD_DIM), lambda b, h: (b, h)),
            pl.BlockSpec((t_n, HEAD_DIM), lambda b, h: (b, h)),
            pl.BlockSpec((t_n, HEAD_DIM), lambda b, h: (b, nh + h)),
            pl.BlockSpec((1, nh, t_n), lambda b, h: (b, 0, 0)),
        ],
        out_specs=pl.BlockSpec((t_n, HEAD_DIM), lambda b, h: (b, h)),
        out_shape=jax.ShapeDtypeStruct((b_n * t_n, FOX_W), BF16),
        scratch_shapes=[pltpu.VMEM((FOX_BLK, t_n), F32), pltpu.VMEM((HEAD_DIM, t_n), BF16)],
        compiler_params=_cparams(("arbitrary", "arbitrary")),
        name="fox_prompt",
    )(qf, kvf_b, kvf_b, frow)


FOX_SAMPLE_CH = 512
TOK_PAD = 8
NEW_PAD = 128
NSA_SAMPLE_SUB = 1


def _nsa_sample_kernel(pt_ref, *refs, n_pages, page, n_buf, n_tok, n_sub):
    del pt_ref
    per_seq = 2 * n_pages + 1
    kvs_new_ref, kvw_new_ref, q_ref, sm_ref = refs[n_sub * per_seq:n_sub * per_seq + 4]
    for sq in range(n_sub):
        seq_refs = refs[sq * per_seq:(sq + 1) * per_seq]
        _nsa_sample_one(seq_refs[:n_pages], seq_refs[n_pages:2 * n_pages], seq_refs[2 * n_pages],
                        kvs_new_ref[sq], kvw_new_ref[sq], q_ref.at[sq], sm_ref[sq],
                        *refs[n_sub * per_seq + 4:-1], refs[-1].at[sq],
                        n_pages=n_pages, page=page, n_buf=n_buf, n_tok=n_tok)


def _nsa_sample_one(cmp_pages, sel_pages, win_ref, kvs_new, kvw_new, q_ref, gates, w1k_ref, w1v_ref, pek_ref, pev_ref,
                    w2k_ref, w2v_ref, cover_ref, expand_ref, o_ref, *, n_pages, page, n_buf, n_tok):
    past = n_pages * page
    n_slab = 2 * N_NSA_GROUPS
    chunks_per_page = page // CMP_STRIDE
    n_chunk = n_pages * chunks_per_page
    n_cmp = (past + n_tok + CMP_STRIDE - 1) // CMP_STRIDE - CMP_RATIO + 1
    n_sel = (past + n_tok + SEL_BLOCK - 1) // SEL_BLOCK
    tp = TOK_PAD
    tpos = past + lax.broadcasted_iota(jnp.int32, (tp, 1), 0)

    rows_pc = CMP_STRIDE * n_slab
    swapped = [jnp.swapaxes(cmp_pages[p][...].reshape(chunks_per_page, rows_pc, HEAD_DIM), 0, 1)
               for p in range(n_pages)]

    def compress(slab, w1_ref, pe_ref, w2_ref):
        cols = []
        for i in range(CMP_STRIDE):
            cols.append(jnp.concatenate([swapped[p][i * n_slab + slab] for p in range(n_pages)], axis=0))
        xc = jnp.concatenate(cols, axis=1).astype(BF16)
        return _compress_tail(xc, w1_ref[...], pe_ref[...], w2_ref[...]).astype(BF16)

    qs, o_cs, scores = [], [], []
    for g in range(N_NSA_GROUPS):
        q = jnp.concatenate([q_ref[:, (g * NSA_HPG + h) * HEAD_DIM:(g * NSA_HPG + h + 1) * HEAD_DIM]
                             for h in range(NSA_HPG)], axis=0)
        ck = compress(g, w1k_ref, pek_ref, w2k_ref)
        cv = compress(N_NSA_GROUPS + g, w1v_ref, pev_ref, w2v_ref)
        s_c = _dot_nt(q, ck)
        cidx = lax.broadcasted_iota(jnp.int32, (tp, n_chunk), 1)
        mc = jnp.where((cidx * CMP_STRIDE + CMP_BLOCK - 1 <= tpos) & (cidx < n_cmp), 1.0, 0.0)
        p_c = _masked_softmax(s_c, _tile_rows(mc, NSA_HPG) > 0.5)
        p_cb = p_c.astype(BF16)
        imp4 = _dot(p_cb, cover_ref[...])
        imp = imp4[0:tp] + imp4[tp:2 * tp] + imp4[2 * tp:3 * tp] + imp4[3 * tp:4 * tp]
        qs.append(q)
        o_cs.append(_dot(p_cb, cv))
        scores.append(_sel_scores(imp, tpos, n_sel))

    score_all = jnp.concatenate(scores + [jnp.zeros((LANES - N_NSA_GROUPS * tp, LANES), F32)], axis=0)
    selm_all = _topk_mask(score_all, min(SEL_TOPK, n_sel), n_sel)

    def sel_rows(slab):
        return jnp.concatenate([sel_pages[p][pl.ds(slab, page, stride=n_slab), :] for p in range(n_pages)], axis=0)

    def with_new(cached, new):
        pad = jnp.zeros((NEW_PAD - new.shape[0], new.shape[1]), new.dtype)
        return jnp.concatenate([cached, new, pad], axis=0).astype(BF16)

    for g in range(N_NSA_GROUPS):
        q, o_c = qs[g], o_cs[g]
        selm = selm_all[g * tp:(g + 1) * tp].astype(BF16)
        kc = slice(g * HEAD_DIM, (g + 1) * HEAD_DIM)
        vc = slice((N_NSA_GROUPS + g) * HEAD_DIM, (N_NSA_GROUPS + g + 1) * HEAD_DIM)

        n_keys = past + NEW_PAD
        s_s = _dot_nt(q, with_new(sel_rows(g), kvs_new[:, kc]))
        selx = _dot(selm, expand_ref[...])
        kpos = lax.broadcasted_iota(jnp.int32, (tp, n_keys), 1)
        ms = jnp.where((selx > 0.5) & (kpos <= tpos) & (kpos < past + n_tok), 1.0, 0.0)
        p_s = _masked_softmax(s_s, _tile_rows(ms, NSA_HPG) > 0.5)
        o_s = _dot(p_s.astype(BF16), with_new(sel_rows(N_NSA_GROUPS + g), kvs_new[:, vc]))

        s_w = _dot_nt(q, with_new(win_ref[pl.ds(g, n_buf, stride=n_slab), :], kvw_new[:, kc]))
        wpos = past - n_buf + lax.broadcasted_iota(jnp.int32, (tp, n_buf + NEW_PAD), 1)
        mw = jnp.where((wpos <= tpos) & (wpos > tpos - WINDOW) & (wpos < past + n_tok), 1.0, 0.0)
        p_w = _masked_softmax(s_w, _tile_rows(mw, NSA_HPG) > 0.5)
        o_w = _dot(p_w.astype(BF16), with_new(win_ref[pl.ds(N_NSA_GROUPS + g, n_buf, stride=n_slab), :],
                                              kvw_new[:, vc]))

        for h in range(NSA_HPG):
            hh = g * NSA_HPG + h
            rows = slice(h * tp, (h + 1) * tp)
            out = (gates[:, 3 * hh:3 * hh + 1] * o_c[rows] + gates[:, 3 * hh + 1:3 * hh + 2] * o_s[rows]
                   + gates[:, 3 * hh + 2:3 * hh + 3] * o_w[rows])
            o_ref[:, hh * HEAD_DIM:(hh + 1) * HEAD_DIM] = out.astype(BF16)


def _pad_tokens(a, n_seq, n_tok):
    a = a.reshape(n_seq, n_tok, a.shape[-1])
    return jnp.pad(a, ((0, 0), (0, TOK_PAD - n_tok), (0, 0)))


def _nsa_sample(page_table, cache_cmp, cache_sel, win_buf, kvs_new, kvw_new, qn, sm, cmp_w, n_tok):
    n_seq, n_pages = page_table.shape
    page = cache_cmp.shape[1]
    n_slab = 2 * N_NSA_GROUPS
    n_buf = win_buf.shape[1]
    past = n_pages * page
    cmp2 = cache_cmp.reshape(-1, HEAD_DIM)
    sel2 = cache_sel.reshape(-1, HEAD_DIM)
    win2 = win_buf.reshape(-1, HEAD_DIM)
    n_chunk = past // CMP_STRIDE
    n_cmp = (past + n_tok + CMP_STRIDE - 1) // CMP_STRIDE - CMP_RATIO + 1
    n_sel = (past + n_tok + SEL_BLOCK - 1) // SEL_BLOCK
    cover = _cover_matrix(n_chunk, min(n_cmp, n_chunk), n_sel)
    expand = _expand_matrix(past + NEW_PAD)
    w1k, pek, w2k, w1v, pev, w2v = cmp_w

    n_sub = NSA_SAMPLE_SUB
    assert n_seq % n_sub == 0

    def page_spec(sq, p):
        return pl.BlockSpec((page * n_slab, HEAD_DIM), lambda b, pt, sq=sq, p=p: (pt[b * n_sub + sq, p], 0))

    const2 = lambda b, pt: (0, 0)
    seq3 = lambda b, pt: (b, 0, 0)
    per_seq = []
    for sq in range(n_sub):
        per_seq += ([page_spec(sq, p) for p in range(n_pages)] + [page_spec(sq, p) for p in range(n_pages)]
                    + [pl.BlockSpec((n_buf * n_slab, HEAD_DIM), lambda b, pt, sq=sq: (b * n_sub + sq, 0))])
    in_specs = (per_seq + [
        pl.BlockSpec((n_sub, TOK_PAD, KV_W), seq3),
        pl.BlockSpec((n_sub, TOK_PAD, KV_W), seq3),
        pl.BlockSpec((n_sub, TOK_PAD, NSA_Q_W), seq3),
        pl.BlockSpec((n_sub, TOK_PAD, LANES), seq3),
        pl.BlockSpec(w1k.shape, const2),
        pl.BlockSpec(w1v.shape, const2),
        pl.BlockSpec(pek.shape, const2),
        pl.BlockSpec(pev.shape, const2),
        pl.BlockSpec(w2k.shape, const2),
        pl.BlockSpec(w2v.shape, const2),
        pl.BlockSpec(cover.shape, const2),
        pl.BlockSpec(expand.shape, const2),
    ])
    kern = functools.partial(_nsa_sample_kernel, n_pages=n_pages, page=page, n_buf=n_buf, n_tok=n_tok, n_sub=n_sub)
    grid_spec = pltpu.PrefetchScalarGridSpec(
        num_scalar_prefetch=1, grid=(n_seq // n_sub,), in_specs=in_specs,
        out_specs=pl.BlockSpec((n_sub, TOK_PAD, NSA_Q_W), seq3))
    return pl.pallas_call(
        kern,
        grid_spec=grid_spec,
        out_shape=jax.ShapeDtypeStruct((n_seq, TOK_PAD, NSA_Q_W), BF16),
        compiler_params=_cparams(("arbitrary",)),
        name="nsa_sample",
    )(page_table, *(([cmp2] * n_pages + [sel2] * n_pages + [win2]) * n_sub),
      _pad_tokens(kvs_new, n_seq, n_tok), _pad_tokens(kvw_new, n_seq, n_tok),
      _pad_tokens(qn, n_seq, n_tok), _pad_tokens(sm, n_seq, n_tok),
      w1k, w1v, pek, pev, w2k, w2v, cover, expand)


def _fox_sample_kernel(pt_ref, *refs, n_pages, page, n_tok):
    kv_pages = refs[:n_pages]
    lf_pages = refs[n_pages:2 * n_pages]
    k_new_ref, v_new_ref, q_ref, lfn_ref, o_ref, mask_scr, s_scr, kb_scr = refs[2 * n_pages:]
    del pt_ref
    nh = N_FOX_HEADS
    rows_pp = 2 * nh
    page_rows = page * rows_pp
    n_chunk = page_rows // LANES
    n_q = nh * n_tok
    assert rows_pp == 16 and LANES % rows_pp == 0 and n_q <= LANES

    @pl.when(pl.program_id(0) == 0)
    def _():
        qrow = lax.broadcasted_iota(jnp.int32, mask_scr.shape, 0)
        lane = lax.broadcasted_iota(jnp.int32, mask_scr.shape, 1)
        mask_scr[...] = jnp.where((lane & (rows_pp - 1)) == (qrow & (nh - 1)), 0.0, NEG_INF)

    x = jnp.concatenate([lf_pages[p][0] for p in range(n_pages)], axis=0)
    n_r = x.shape[0]
    la = lax.broadcasted_iota(jnp.int32, (LANES, LANES), 0)
    lb = lax.broadcasted_iota(jnp.int32, (LANES, LANES), 1)
    same = (la & (rows_pp - 1)) == (lb & (rows_pp - 1))
    u_in = jnp.where(same & (jnp.right_shift(la, 4) <= jnp.right_shift(lb, 4)), 1.0, 0.0).astype(BF16)
    u_all = jnp.where(same, 1.0, 0.0).astype(BF16)
    xh, xm, xl = _split3(x)
    within = _dot(xh, u_in) + _dot(xm, u_in) + _dot(xl, u_in)
    tot = _dot(xh, u_all) + _dot(xm, u_all) + _dot(xl, u_all)
    ra = lax.broadcasted_iota(jnp.int32, (n_r, n_r), 0)
    rb = lax.broadcasted_iota(jnp.int32, (n_r, n_r), 1)
    before = jnp.where(rb < ra, 1.0, 0.0).astype(BF16)
    th, tm_, tl = _split3(tot)
    offs = _dot(before, th) + _dot(before, tm_) + _dot(before, tl)
    f_end = offs[n_r - 1:n_r, :] + tot[n_r - 1:n_r, :]
    bias = (f_end - (within + offs)) * LOG2E

    q_all = q_ref[0]

    ch = mask_scr.shape[1]
    lt = ch // LANES
    steps = [(p, c) for p in range(n_pages) for c in range(page_rows // ch)]
    mx = jnp.full((n_q, LANES), -jnp.inf, F32)
    for p, c in steps:
        k_b = kv_pages[p][pl.ds(c * ch, ch), :].astype(BF16)
        kb_scr[p * page_rows + c * ch:p * page_rows + (c + 1) * ch, :] = k_b
        r0 = p * n_chunk + c * lt
        brow = jnp.concatenate([bias[r0 + i:r0 + i + 1, :] for i in range(lt)], axis=1)
        s = _dot_nt(q_all, k_b) + brow + mask_scr[...]
        s_scr[:, p * page_rows + c * ch:p * page_rows + (c + 1) * ch] = s
        mx = _lane_tile_max(mx, s)

    pad = jnp.zeros((LANES - n_q, HEAD_DIM), F32)
    k_new = jnp.concatenate([k_new_ref[0], pad], axis=0).astype(BF16)
    v_new = jnp.concatenate([v_new_ref[0], pad], axis=0).astype(BF16)
    g_in = jnp.where(((la & (nh - 1)) == (lb & (nh - 1))) & (la <= lb), 1.0, 0.0).astype(BF16)
    nh_, nm_, nl_ = _split3(lfn_ref[0])
    c_new = (_dot(nh_, g_in) + _dot(nm_, g_in) + _dot(nl_, g_in))[0:1, :] * LOG2E
    qrow = lax.broadcasted_iota(jnp.int32, (n_q, LANES), 0)
    lane = lax.broadcasted_iota(jnp.int32, (n_q, LANES), 1)
    ok = ((lane & (nh - 1)) == (qrow & (nh - 1))) & (lane <= qrow)
    s_new = jnp.where(ok, _dot_nt(q_all, k_new) - c_new, NEG_INF)
    m = jnp.max(jnp.maximum(mx, s_new), axis=-1, keepdims=True)

    carry = _exp_accumulate((jnp.zeros((n_q, LANES), F32), jnp.zeros((n_q, HEAD_DIM), F32)), s_new, m, v_new)
    for p, c in steps:
        rows = slice(p * page_rows + c * ch, p * page_rows + (c + 1) * ch)
        carry = _exp_accumulate(carry, s_scr[:, rows], m, kb_scr[rows, :], lane_shift=nh)
    ls, acc = carry
    o_ref[0] = (acc / jnp.sum(ls, axis=-1, keepdims=True)).astype(BF16)


def _fox_sample(page_table, cache_fox, cache_logf, kvf_new, qf, sm, n_tok):
    n_seq, n_pages = page_table.shape
    page = cache_fox.shape[1]
    nh = N_FOX_HEADS
    kv2 = cache_fox.reshape(-1, HEAD_DIM)
    rows_pp = 2 * nh
    n_chunk = page * rows_pp // LANES
    lf_c = jnp.pad(cache_logf, ((0, 0), (0, 0), (0, rows_pp - nh))).reshape(-1, n_chunk, LANES)
    n_q = n_tok * nh
    lfn = sm[:, LOGF_COL0:LOGF_COL0 + nh].reshape(n_seq, 1, n_q)
    lfn = jnp.pad(lfn, ((0, 0), (0, 7), (0, LANES - n_q)))
    kv_new = kvf_new.reshape(n_seq, n_tok, 2, nh, HEAD_DIM)
    k_new = kv_new[:, :, 0].reshape(n_seq, n_q, HEAD_DIM)
    v_new = kv_new[:, :, 1].reshape(n_seq, n_q, HEAD_DIM)
    q3 = qf.reshape(n_seq, n_q, HEAD_DIM)

    seq3 = lambda b, pt: (b, 0, 0)
    in_specs = ([pl.BlockSpec((page * rows_pp, HEAD_DIM), lambda b, pt, p=p: (pt[b, p], 0)) for p in range(n_pages)]
                + [pl.BlockSpec((1, n_chunk, LANES), lambda b, pt, p=p: (pt[b, p], 0, 0)) for p in range(n_pages)]
                + [pl.BlockSpec((1, n_q, HEAD_DIM), seq3),
                   pl.BlockSpec((1, n_q, HEAD_DIM), seq3),
                   pl.BlockSpec((1, n_q, HEAD_DIM), seq3),
                   pl.BlockSpec((1, 8, LANES), seq3)])
    kern = functools.partial(_fox_sample_kernel, n_pages=n_pages, page=page, n_tok=n_tok)
    n_rows = n_pages * page * rows_pp
    grid_spec = pltpu.PrefetchScalarGridSpec(
        num_scalar_prefetch=1, grid=(n_seq,), in_specs=in_specs,
        out_specs=pl.BlockSpec((1, n_q, HEAD_DIM), seq3),
        scratch_shapes=[pltpu.VMEM((n_q, FOX_SAMPLE_CH), F32),
                        pltpu.VMEM((n_q, n_rows), F32),
                        pltpu.VMEM((n_rows, HEAD_DIM), BF16)])
    out = pl.pallas_call(
        kern,
        grid_spec=grid_spec,
        out_shape=jax.ShapeDtypeStruct((n_seq, n_q, HEAD_DIM), BF16),
        compiler_params=_cparams(("arbitrary",)),
        name="fox_sample",
    )(page_table, *([kv2] * n_pages), *([lf_c] * n_pages), k_new, v_new, q3, lfn)
    return out.reshape(n_seq * n_tok, nh * HEAD_DIM)


def _postmix_kernel(on_ref, of_ref, gm0_ref, gm1_ref, x_ref, wn_ref, wf_ref, wo_ref, g_ref, y_ref):
    a = _dot(on_ref[...], wn_ref[...])
    b = _dot(of_ref[...], wf_ref[...])
    merged = gm0_ref[...] * a + gm1_ref[...] * b
    z = _dot(merged.astype(BF16), wo_ref[...])
    y_ref[...] = x_ref[...] + _rms(z, g_ref[...])


def _postmix(o_n, o_f, gm, x2, wn, wf, wo, g):
    n, d = x2.shape
    tm = 256
    row = lambda i: (i, 0)
    const = lambda i: (0, 0)
    return pl.pallas_call(
        _postmix_kernel,
        grid=(n // tm,),
        in_specs=[
            pl.BlockSpec((tm, NSA_Q_W), row),
            pl.BlockSpec((tm, FOX_W), row),
            pl.BlockSpec((tm, d), lambda i: (i, 0)),
            pl.BlockSpec((tm, d), lambda i: (i, 1)),
            pl.BlockSpec((tm, d), row),
            pl.BlockSpec(wn.shape, const),
            pl.BlockSpec(wf.shape, const),
            pl.BlockSpec(wo.shape, const),
            pl.BlockSpec((1, d), const),
        ],
        out_specs=pl.BlockSpec((tm, d), row),
        out_shape=jax.ShapeDtypeStruct((n, d), F32),
        compiler_params=_cparams(("arbitrary",)),
        name="postmix",
    )(o_n, o_f, gm, gm, x2, wn, wf, wo, g)


FFN_TM = 1024
FFN_TF = 512
HALO = 16


def _ffn_kernel(*refs, seq_tiles, n_tok):
    if n_tok is None:
        (x_ref, xh_ref, g_ref, wg_ref, wu_ref, wd_ref, wc_ref, bc_ref, gp_ref,
         y_ref, gt_ref, h_scr, hh_scr, acc_scr) = refs
    else:
        (x_ref, s0_ref, s1_ref, g_ref, wg_ref, wu_ref, wd_ref, wc_ref, bc_ref, gp_ref,
         y_ref, gt_ref, h_scr, acc_scr) = refs
    i = pl.program_id(0)
    f = pl.program_id(1)
    tm = x_ref.shape[0]

    @pl.when(f == 0)
    def _():
        h_scr[...] = _rms(x_ref[...], g_ref[...]).astype(BF16)
        acc_scr[...] = jnp.zeros_like(acc_scr)
        if n_tok is None:
            hh_scr[...] = _rms(xh_ref[...], g_ref[...]).astype(BF16)

    h2 = h_scr[...]
    gate = _dot(h2, wg_ref[...])
    up = _dot(h2, wu_ref[...])
    row = lax.broadcasted_iota(jnp.int32, gate.shape, 0)
    r1 = pltpu.roll(gate, 1, axis=0)
    r2 = pltpu.roll(gate, 2, axis=0)
    if n_tok is None:
        first = (i % seq_tiles) == 0
        gh = jnp.where(first, 0.0, _dot(hh_scr[...], wg_ref[...]))
        p1 = gh[HALO - 1:HALO, :]
        p2 = gh[HALO - 2:HALO - 1, :]
        g1 = jnp.where(row == 0, p1, r1)
        g2 = jnp.where(row == 0, p2, jnp.where(row == 1, p1, r2))
        gt_ref[...] = gate[tm - 8:tm, :]
    else:
        assert n_tok & (n_tok - 1) == 0
        rt = row & (n_tok - 1)
        g1 = jnp.where(rt == 0, s1_ref[...], r1)
        g2 = jnp.where(rt == 0, s0_ref[...], jnp.where(rt == 1, s1_ref[...], r2))
        gt_ref[...] = gate
    wc = wc_ref[...]
    gc = bc_ref[...] + wc[0:1, :] * g2 + wc[1:2, :] * g1 + wc[2:3, :] * gate
    act = jax.nn.gelu(gc, approximate=True) * up
    acc_scr[...] += _dot(act.astype(BF16), wd_ref[...])

    @pl.when(f == pl.num_programs(1) - 1)
    def _():
        y_ref[...] = x_ref[...] + _rms(acc_scr[...], gp_ref[...])


def _ffn(x2, g_pre, w_up_b, w_down_b, w_conv, b_conv, g_post, *, seq_len=None, state=None):
    n, d = x2.shape
    d_ff = w_down_b.shape[0]
    tf = FFN_TF
    nf = d_ff // tf
    tm = min(FFN_TM, n)
    common_w = [
        pl.BlockSpec((1, d), lambda i, f: (0, 0)),
        pl.BlockSpec((d, tf), lambda i, f: (0, f)),
        pl.BlockSpec((d, tf), lambda i, f: (0, nf + f)),
        pl.BlockSpec((tf, d), lambda i, f: (f, 0)),
        pl.BlockSpec((CONV_WIDTH, tf), lambda i, f: (0, f)),
        pl.BlockSpec((1, tf), lambda i, f: (0, f)),
        pl.BlockSpec((1, d), lambda i, f: (0, 0)),
    ]
    w_args = (g_pre, w_up_b, w_up_b, w_down_b, w_conv, b_conv, g_post)
    row = lambda i, f: (i, 0)
    once = pl.Buffered(1)
    if state is None:
        seq_tiles = seq_len // tm
        halo_blocks = tm // HALO
        in_specs = [pl.BlockSpec((tm, d), row, pipeline_mode=once),
                    pl.BlockSpec((HALO, d), lambda i, f: (jnp.maximum(i * halo_blocks - 1, 0), 0))] + common_w
        args = (x2, x2) + w_args
        gt_shape = jax.ShapeDtypeStruct((n // tm * 8, d_ff), F32)
        gt_spec = pl.BlockSpec((8, tf), lambda i, f: (i, f))
        scratch = [pltpu.VMEM((tm, d), BF16), pltpu.VMEM((HALO, d), BF16), pltpu.VMEM((tm, d), F32)]
        kern = functools.partial(_ffn_kernel, seq_tiles=seq_tiles, n_tok=None)
    else:
        n_tok = n // state.shape[0]
        s0 = jnp.repeat(state[:, 0], n_tok, axis=0)
        s1 = jnp.repeat(state[:, 1], n_tok, axis=0)
        in_specs = [pl.BlockSpec((tm, d), row, pipeline_mode=once),
                    pl.BlockSpec((tm, tf), lambda i, f: (i, f)),
                    pl.BlockSpec((tm, tf), lambda i, f: (i, f))] + common_w
        args = (x2, s0, s1) + w_args
        gt_shape = jax.ShapeDtypeStruct((n, d_ff), F32)
        gt_spec = pl.BlockSpec((tm, tf), lambda i, f: (i, f))
        scratch = [pltpu.VMEM((tm, d), BF16), pltpu.VMEM((tm, d), F32)]
        kern = functools.partial(_ffn_kernel, seq_tiles=None, n_tok=n_tok)
    return pl.pallas_call(
        kern,
        grid=(n // tm, nf),
        in_specs=in_specs,
        out_specs=(pl.BlockSpec((tm, d), row, pipeline_mode=once), gt_spec),
        out_shape=(jax.ShapeDtypeStruct((n, d), F32), gt_shape),
        scratch_shapes=scratch,
        compiler_params=_cparams(("arbitrary", "arbitrary")),
        name="ffn",
    )(*args)


def _rope_tables(pos):
    half = HEAD_DIM // 2
    inv_freq = ROPE_THETA ** (-jnp.arange(half, dtype=F32) / half)
    ang = pos.astype(F32)[:, None] * inv_freq[None, :]
    cos, sin = jnp.cos(ang), jnp.sin(ang)
    return jnp.concatenate([cos, cos], axis=-1), jnp.concatenate([-sin, sin], axis=-1)


def _cmp_weights(w1, pe, w2):
    w1r = w1.reshape(CMP_RATIO, CMP_STRIDE * HEAD_DIM, HEAD_DIM)
    w1cat = jnp.concatenate([w1r[r] for r in range(CMP_RATIO)], axis=1).astype(BF16)
    pe8 = jnp.pad(pe.reshape(CMP_RATIO, CMP_STRIDE * HEAD_DIM), ((0, 8 - CMP_RATIO), (0, 0)))
    return w1cat, pe8, w2.astype(BF16)


def kernel(x_prompt, x_sample, cache_nsa_cmp_kv, cache_nsa_sel_kv, cache_nsa_win_kv, cache_fox_kv, cache_fox_logf, state_ffn_conv, page_table, g_pre_mix, w_in, b_fgt, w_cmp_k1, pe_cmp_k, w_cmp_k2, w_cmp_v1, pe_cmp_v, w_cmp_v2, w_nsa_o, w_fox_o, w_out, g_post_mix, g_pre_ffn, w_up, w_conv, b_conv, w_down, g_post_ffn):
    b_p, t_p, d = x_prompt.shape
    b_s, t_s, _ = x_sample.shape
    depth = w_in.shape[0]
    page = cache_nsa_cmp_kv.shape[2]
    past = page_table.shape[1] * page
    g_n, n_h = N_NSA_GROUPS, N_FOX_HEADS

    cos_p, sin_p = _rope_tables(jnp.tile(jnp.arange(t_p), b_p))
    cos_s, sin_s = _rope_tables(jnp.tile(past + jnp.arange(t_s), b_s))

    y_p = x_prompt.reshape(b_p * t_p, d)
    y_s = x_sample.reshape(b_s * t_s, d)
    outs = {k: [] for k in ('cmp_p', 'cmp_s', 'sel_p', 'sel_s', 'win_p', 'win_s',
                            'fox_p', 'fox_s', 'lf_p', 'lf_s', 'conv_p', 'conv_s')}
    o_q = NSA_Q_W
    o_g = o_q + 3 * KV_W
    o_f = o_g + N_GATE_COLS
    o_ff = o_f + 3 * FOX_W
    o_m = o_ff + n_h
    for l in range(depth):
        w = w_in[l]
        w_main = jnp.concatenate([w[:, :o_g], w[:, o_f:o_ff], w[:, o_m:]], axis=1).astype(BF16)
        w_small = jnp.concatenate([w[:, o_g:o_f], w[:, o_ff:o_m],
                                   jnp.zeros((d, LANES - N_GATE_COLS - n_h), F32)], axis=1).astype(BF16)
        bf_row = jnp.zeros((1, LANES), F32).at[0, LOGF_COL0:LOGF_COL0 + n_h].set(b_fgt[l])
        g1 = g_pre_mix[l][None, :]
        cmp_k = _cmp_weights(w_cmp_k1[l], pe_cmp_k[l], w_cmp_k2[l])
        cmp_v = _cmp_weights(w_cmp_v1[l], pe_cmp_v[l], w_cmp_v2[l])
        wn, wf, wo = w_nsa_o[l].astype(BF16), w_fox_o[l].astype(BF16), w_out[l].astype(BF16)
        wu, wd = w_up[l].astype(BF16), w_down[l].astype(BF16)
        ffn_w = (g_pre_ffn[l][None, :], wu, wd, w_conv[l], b_conv[l][None, :], g_post_ffn[l][None, :])

        (qn, kvc, kvs, kvw, qf, kvf, gm, sm, kvs_b, kvw_b, kvf_b) = _project(y_p, g1, cos_p, sin_p, w_main, w_small, bf_row)
        w1cat = jnp.stack([cmp_k[0], cmp_v[0]])
        pe8 = jnp.stack([cmp_k[1], cmp_v[1]])
        w2 = jnp.stack([cmp_k[2], cmp_v[2]])
        ckv = _compress_prompt(kvc, b_p, t_p, w1cat, pe8, w2)
        frow = _fcum_prompt(sm, b_p, t_p)
        o_n = _nsa_prompt(qn, sm, ckv, kvs_b, kvw_b, b_p, t_p)
        o_fx = _fox_prompt(qf, kvf_b, frow, b_p, t_p)
        y1 = _postmix(o_n, o_fx, gm, y_p, wn, wf, wo, g_post_mix[l][None, :])
        y_p, gt = _ffn(y1, *ffn_w, seq_len=t_p)
        n_win = min(WINDOW, t_p)
        outs['cmp_p'].append(kvc.reshape(b_p, t_p, 2, g_n, HEAD_DIM))
        outs['sel_p'].append(kvs.reshape(b_p, t_p, 2, g_n, HEAD_DIM))
        outs['win_p'].append(kvw.reshape(b_p, t_p, 2, g_n, HEAD_DIM)[:, t_p - n_win:])
        outs['fox_p'].append(kvf.reshape(b_p, t_p, 2, n_h, HEAD_DIM))
        outs['lf_p'].append(sm[:, LOGF_COL0:LOGF_COL0 + n_h].reshape(b_p, t_p, n_h))
        tiles_per_seq = t_p // FFN_TM
        gt = gt.reshape(b_p, tiles_per_seq, 8, -1)
        outs['conv_p'].append(gt[:, -1, 8 - (CONV_WIDTH - 1):])

        (qn, kvc, kvs, kvw, qf, kvf, gm, sm, _, _, _) = _project(y_s, g1, cos_s, sin_s, w_main, w_small, bf_row)
        win_buf = cache_nsa_win_kv[l]
        o_n = _nsa_sample(page_table, cache_nsa_cmp_kv[l], cache_nsa_sel_kv[l], win_buf, kvs, kvw, qn, sm,
                          cmp_k + cmp_v, t_s)
        o_fx = _fox_sample(page_table, cache_fox_kv[l], cache_fox_logf[l], kvf, qf, sm, t_s)
        o_n = o_n[:, :t_s].reshape(b_s * t_s, -1)
        y1 = _postmix(o_n, o_fx, gm, y_s, wn, wf, wo, g_post_mix[l][None, :])
        y_s, gt = _ffn(y1, *ffn_w, state=state_ffn_conv[l])
        kw_new = kvw.reshape(b_s, t_s, 2, g_n, HEAD_DIM)
        n_win = min(WINDOW, win_buf.shape[1] + t_s)
        outs['cmp_s'].append(kvc.reshape(b_s, t_s, 2, g_n, HEAD_DIM))
        outs['sel_s'].append(kvs.reshape(b_s, t_s, 2, g_n, HEAD_DIM))
        outs['win_s'].append(jnp.concatenate([win_buf, kw_new], axis=1)[:, -n_win:])
        outs['fox_s'].append(kvf.reshape(b_s, t_s, 2, n_h, HEAD_DIM))
        outs['lf_s'].append(sm[:, LOGF_COL0:LOGF_COL0 + n_h].reshape(b_s, t_s, n_h))
        gfull = jnp.concatenate([state_ffn_conv[l], gt.reshape(b_s, t_s, -1)], axis=1)
        outs['conv_s'].append(gfull[:, t_s:])

    st = {k: jnp.stack(v) for k, v in outs.items()}
    return (y_p.reshape(b_p, t_p, d), y_s.reshape(b_s, t_s, d),
            st['cmp_p'], st['cmp_s'], st['sel_p'], st['sel_s'], st['win_p'], st['win_s'],
            st['fox_p'], st['fox_s'], st['lf_p'], st['lf_s'], st['conv_p'], st['conv_s'])
```

```python
import functools

import numpy as np
import jax
import jax.numpy as jnp
from jax import lax
from jax.experimental import pallas as pl
from jax.experimental.pallas import tpu as pltpu

F32 = jnp.float32
BF16 = jnp.bfloat16

HEAD_DIM = 128
N_NSA_HEADS = 8
N_NSA_GROUPS = 2
NSA_HPG = N_NSA_HEADS // N_NSA_GROUPS
N_FOX_HEADS = 8
CMP_BLOCK = 32
CMP_STRIDE = 16
CMP_RATIO = CMP_BLOCK // CMP_STRIDE
SEL_BLOCK = 64
SEL_TOPK = 16
N_LOCAL_BLOCKS = 2
WINDOW = 512
CONV_WIDTH = 3
ROPE_THETA = 10000.0
RMS_EPS = 1e-6
FORCE_BONUS = 1e4
NEG_INF = -1e30
LOG2E = 1.4426950408889634
QK_SCALE = HEAD_DIM ** -0.5 * LOG2E

N_GATE_COLS = N_NSA_HEADS * 3
LOGF_COL0 = N_GATE_COLS
LANES = 128
VMEM_LIMIT = 56 * 1024 * 1024

PROJ_TN = 512
PROJ_TM = 1024
KV_W = 2 * N_NSA_GROUPS * HEAD_DIM
NSA_Q_W = N_NSA_HEADS * HEAD_DIM
FOX_W = N_FOX_HEADS * HEAD_DIM


def _cparams(sem):
    return pltpu.CompilerParams(dimension_semantics=sem, vmem_limit_bytes=VMEM_LIMIT)


def _dot(a, b):
    return jnp.dot(a, b, preferred_element_type=F32)


def _dot_nt(a, b):
    return lax.dot_general(a, b, (((1,), (1,)), ((), ())), preferred_element_type=F32)


def _rms(x, g):
    return x * lax.rsqrt(jnp.mean(x * x, axis=-1, keepdims=True) + RMS_EPS) * g


def _masked_softmax(s, mask):
    sm = jnp.where(mask, s, NEG_INF)
    m = jnp.max(sm, axis=-1, keepdims=True)
    e = jnp.where(mask, jnp.exp2(sm - m), 0.0)
    l = jnp.sum(e, axis=-1, keepdims=True)
    return e / jnp.where(l > 0.0, l, 1.0)


def _transpose_rows(src_ref, dst_ref, cols=slice(None)):
    for c in range(src_ref.shape[0] // LANES):
        rows = slice(c * LANES, (c + 1) * LANES)
        dst_ref[:, rows] = src_ref[rows, cols].astype(F32).T.astype(BF16)


def _masked_exp(s, mask):
    sm = jnp.where(mask, s, NEG_INF)
    e = jnp.where(mask, jnp.exp2(sm - jnp.max(sm, axis=-1, keepdims=True)), 0.0)
    l = jnp.sum(e, axis=-1, keepdims=True)
    return e, jnp.where(l > 0.0, l, 1.0)


def _lane_tile_max(mx, s):
    for c in range(s.shape[1] // LANES):
        mx = jnp.maximum(mx, s[:, c * LANES:(c + 1) * LANES])
    return mx


def _exp_accumulate(carry, s, m, v, lane_shift=0):
    ls, acc = carry
    p = jnp.exp2(s - m)
    tiles = [p[:, c * LANES:(c + 1) * LANES] for c in range(s.shape[1] // LANES)]
    for t in tiles:
        ls = ls + t
    if lane_shift:
        p = jnp.concatenate([pltpu.roll(t, lane_shift, axis=1) for t in tiles], axis=1)
    return ls, acc + _dot(p.astype(BF16), v)


def _split3(x):
    hi = x.astype(BF16)
    r = x - hi.astype(F32)
    mid = r.astype(BF16)
    lo = (r - mid.astype(F32)).astype(BF16)
    return hi, mid, lo


def _topk_mask(score, k, n_sel):
    st = score.T
    nv = -(-n_sel // 8)
    slabs = [st[8 * v:8 * v + 8, :] for v in range(nv)]
    sub = lax.broadcasted_iota(jnp.int32, (8, LANES), 0)
    ranks = [jnp.zeros((8, LANES), F32) for _ in range(nv)]
    for b2 in range(n_sel):
        row = jnp.broadcast_to(st[b2:b2 + 1, :], (8, LANES))
        for v in range(nv):
            if b2 < 8 * v:
                beats = row >= slabs[v]
            elif b2 >= 8 * v + 8:
                beats = row > slabs[v]
            else:
                beats = (row > slabs[v]) | ((row == slabs[v]) & (sub > b2 - 8 * v))
            ranks[v] = ranks[v] + jnp.where(beats, 1.0, 0.0)
    sel = [jnp.where((ranks[v] < k) & (sub + 8 * v < n_sel), 1.0, 0.0) for v in range(nv)]
    sel_t = jnp.concatenate(sel + [jnp.zeros((LANES - 8 * nv, LANES), F32)], axis=0)
    return sel_t.T


def _sel_scores(imp, tpos, n_sel):
    bidx = lax.broadcasted_iota(jnp.int32, imp.shape, 1)
    cur = jnp.right_shift(tpos, 6)
    valid = bidx <= cur
    forced = (bidx == 0) | (valid & (bidx > cur - N_LOCAL_BLOCKS))
    score = jnp.where(valid, jnp.where(forced, imp + FORCE_BONUS, imp), NEG_INF)
    return jnp.where(bidx < n_sel, score, -jnp.inf)


def _proj_kernel(x_ref, g_ref, cos_ref, sin_ref, w_ref, ws_ref, bf_ref,
                 qn_ref, kvc_ref, kvs_ref, kvw_ref, qf_ref, kvf_ref, gm_ref, sm_ref,
                 kvsb_ref, kvwb_ref, kvfb_ref, h_scr):
    j = pl.program_id(1)

    @pl.when(j == 0)
    def _():
        x = x_ref[...]
        y = x * lax.rsqrt(jnp.mean(x * x, axis=-1, keepdims=True) + RMS_EPS)
        h = (y * g_ref[...]).astype(BF16)
        h_scr[...] = h
        s = _dot(h, ws_ref[...])
        lane = lax.broadcasted_iota(jnp.int32, s.shape, 1)
        z = s + bf_ref[...]
        lf = jnp.minimum(z, 0.0) - jnp.log1p(jnp.exp(-jnp.abs(z)))
        sm_ref[...] = jnp.where(lane < N_GATE_COLS, jax.nn.sigmoid(s),
                                jnp.where(lane < LOGF_COL0 + N_FOX_HEADS, lf, 0.0))

    cos = cos_ref[...]
    sin = sin_ref[...]
    half_w = PROJ_TN // 2
    halves = [slice(0, half_w), slice(half_w, PROJ_TN)]

    def mm(cols):
        return _dot(h_scr[...], w_ref[:, cols])

    def rope2(a):
        return jnp.concatenate(
            [a[:, k * HEAD_DIM:(k + 1) * HEAD_DIM] * cos
             + pltpu.roll(a[:, k * HEAD_DIM:(k + 1) * HEAD_DIM], HEAD_DIM // 2, axis=1) * sin
             for k in range(half_w // HEAD_DIM)], axis=1)

    def kv_rows(ref, bref):
        assert half_w == N_NSA_GROUPS * HEAD_DIM
        for cols, is_key in zip(halves, (True, False)):
            a = mm(cols)
            a = rope2(a) if is_key else a
            ref[:, cols] = a
            if bref is not None:
                bref[:, cols] = a.astype(BF16)

    @pl.when(j < 2)
    def _():
        for cols in halves:
            qn_ref[:, cols] = (rope2(mm(cols)) * QK_SCALE).astype(BF16)

    @pl.when(j == 2)
    def _():
        kv_rows(kvc_ref, None)

    @pl.when(j == 3)
    def _():
        kv_rows(kvs_ref, kvsb_ref)

    @pl.when(j == 4)
    def _():
        kv_rows(kvw_ref, kvwb_ref)

    @pl.when((j >= 5) & (j < 7))
    def _():
        for cols in halves:
            qf_ref[:, cols] = (mm(cols) * QK_SCALE).astype(BF16)

    @pl.when((j >= 7) & (j < 11))
    def _():
        for cols in halves:
            a = mm(cols)
            kvf_ref[:, cols] = a
            kvfb_ref[:, cols] = a.astype(BF16)

    @pl.when(j >= 11)
    def _():
        for cols in halves:
            gm_ref[:, cols] = jax.nn.sigmoid(mm(cols))


def _project(x2, g, cos2, sin2, w_main, w_small, bf_row):
    n, d = x2.shape
    tm = min(PROJ_TM, n)
    n_j = w_main.shape[1] // PROJ_TN
    tn = PROJ_TN

    def clip(lo, hi):
        return lambda i, j: (i, jnp.clip(j - lo, 0, hi - lo))

    row = lambda i, j: (i, 0)
    out_shape = (
        jax.ShapeDtypeStruct((n, NSA_Q_W), BF16),
        jax.ShapeDtypeStruct((n, KV_W), F32),
        jax.ShapeDtypeStruct((n, KV_W), F32),
        jax.ShapeDtypeStruct((n, KV_W), F32),
        jax.ShapeDtypeStruct((n, FOX_W), BF16),
        jax.ShapeDtypeStruct((n, 2 * FOX_W), F32),
        jax.ShapeDtypeStruct((n, 2 * d), F32),
        jax.ShapeDtypeStruct((n, LANES), F32),
        jax.ShapeDtypeStruct((n, KV_W), BF16),
        jax.ShapeDtypeStruct((n, KV_W), BF16),
        jax.ShapeDtypeStruct((n, 2 * FOX_W), BF16),
    )
    out_specs = (
        pl.BlockSpec((tm, tn), clip(0, 1)),
        pl.BlockSpec((tm, tn), row),
        pl.BlockSpec((tm, tn), row),
        pl.BlockSpec((tm, tn), row),
        pl.BlockSpec((tm, tn), clip(5, 6)),
        pl.BlockSpec((tm, tn), clip(7, 10)),
        pl.BlockSpec((tm, tn), clip(11, 18)),
        pl.BlockSpec((tm, LANES), row),
        pl.BlockSpec((tm, tn), row),
        pl.BlockSpec((tm, tn), row),
        pl.BlockSpec((tm, tn), clip(7, 10)),
    )
    in_specs = [
        pl.BlockSpec((tm, d), row, pipeline_mode=pl.Buffered(1)),
        pl.BlockSpec((1, d), lambda i, j: (0, 0)),
        pl.BlockSpec((tm, LANES), row),
        pl.BlockSpec((tm, LANES), row),
        pl.BlockSpec((d, tn), lambda i, j: (0, j)),
        pl.BlockSpec((d, LANES), lambda i, j: (0, 0)),
        pl.BlockSpec((1, LANES), lambda i, j: (0, 0)),
    ]
    return pl.pallas_call(
        _proj_kernel,
        grid=(n // tm, n_j),
        in_specs=in_specs,
        out_specs=out_specs,
        out_shape=out_shape,
        scratch_shapes=[pltpu.VMEM((tm, d), BF16)],
        compiler_params=_cparams(("arbitrary", "arbitrary")),
        name="proj",
    )(x2, g, cos2, sin2, w_main, w_small, bf_row)


def _compress_tail(xc, w1, pe8, w2):
    n = xc.shape[0]
    part = _dot(xc, w1)
    pp = _dot(pe8.astype(BF16), w1)
    pe_term = pp[0:1, :HEAD_DIM] + pp[1:2, HEAD_DIM:]
    hid = pe_term + part[:, :HEAD_DIM] + pltpu.roll(part[:, HEAD_DIM:], n - 1, axis=0)
    return _dot(jax.nn.gelu(hid, approximate=True).astype(BF16), w2)


def _cmp_prompt_kernel(x_ref, w1_ref, pe_ref, w2_ref, o_ref):
    n = x_ref.shape[0] // CMP_STRIDE
    xc = jnp.concatenate([x_ref[pl.ds(i, n, stride=CMP_STRIDE), :] for i in range(CMP_STRIDE)],
                         axis=1).astype(BF16)
    o_ref[0, 0] = _compress_tail(xc, w1_ref[0], pe_ref[0], w2_ref[0]).astype(BF16)


def _compress_prompt(kvc, b_n, t_n, w1cat, pe8, w2):
    n = t_n // CMP_STRIDE
    return pl.pallas_call(
        _cmp_prompt_kernel,
        grid=(b_n, 2 * N_NSA_GROUPS),
        in_specs=[
            pl.BlockSpec((t_n, HEAD_DIM), lambda b, s: (b, s)),
            pl.BlockSpec((1, CMP_STRIDE * HEAD_DIM, 2 * HEAD_DIM), lambda b, s: (s // N_NSA_GROUPS, 0, 0)),
            pl.BlockSpec((1, 8, CMP_STRIDE * HEAD_DIM), lambda b, s: (s // N_NSA_GROUPS, 0, 0)),
            pl.BlockSpec((1, HEAD_DIM, HEAD_DIM), lambda b, s: (s // N_NSA_GROUPS, 0, 0)),
        ],
        out_specs=pl.BlockSpec((1, 1, n, HEAD_DIM), lambda b, s: (b, s, 0, 0)),
        out_shape=jax.ShapeDtypeStruct((b_n, 2 * N_NSA_GROUPS, n, HEAD_DIM), BF16),
        compiler_params=_cparams(("arbitrary", "arbitrary")),
        name="cmp_prompt",
    )(kvc, w1cat, pe8, w2)


def _fcum_kernel(x_ref, frow_ref, carry_scr):
    i = pl.program_id(1)

    @pl.when(i == 0)
    def _():
        carry_scr[...] = jnp.zeros_like(carry_scr)

    x = x_ref[...]
    tb = x.shape[0]
    r = lax.broadcasted_iota(jnp.int32, (tb, tb), 0)
    c = lax.broadcasted_iota(jnp.int32, (tb, tb), 1)
    tri = jnp.where(r >= c, 1.0, 0.0).astype(BF16)
    hi, mid, lo = _split3(x)
    cs = _dot(tri, hi) + _dot(tri, mid) + _dot(tri, lo) + carry_scr[0:1, :]
    carry_scr[...] = jnp.broadcast_to(cs[tb - 1:tb, :], carry_scr.shape)
    frow_ref[0] = cs.T[LOGF_COL0:LOGF_COL0 + N_FOX_HEADS, :] * LOG2E


def _fcum_prompt(sm, b_n, t_n):
    tb = 512
    nb = t_n // tb
    return pl.pallas_call(
        _fcum_kernel,
        grid=(b_n, nb),
        in_specs=[pl.BlockSpec((tb, LANES), lambda b, i: (b * nb + i, 0))],
        out_specs=pl.BlockSpec((1, N_FOX_HEADS, tb), lambda b, i: (b, 0, i)),
        out_shape=jax.ShapeDtypeStruct((b_n, N_FOX_HEADS, t_n), F32),
        scratch_shapes=[pltpu.VMEM((8, LANES), F32)],
        compiler_params=_cparams(("arbitrary", "arbitrary")),
        name="fcum_prompt",
    )(sm)


NSA_TQ = 128
NSA_NSUB = 2


def _tile_rows(a, reps):
    return jnp.concatenate([a] * reps, axis=0)


def _nsa_front_kernel(q_ref, sm_ref, ck_ref, kw_ref, cover_ref, selm_ref, ocw_ref, kwt_scr, *, n_cmp, n_sel):
    i = pl.program_id(1)
    tq, nsub = NSA_TQ, NSA_NSUB
    t0 = i * (tq * nsub)
    n_ck = ck_ref.shape[2]
    band = WINDOW + tq
    tpos = [t0 + u * tq + lax.broadcasted_iota(jnp.int32, (tq, 1), 0) for u in range(nsub)]

    @pl.when(i == 0)
    def _():
        for g in range(N_NSA_GROUPS):
            _transpose_rows(kw_ref, kwt_scr.at[g], slice(g * HEAD_DIM, (g + 1) * HEAD_DIM))

    for g in range(N_NSA_GROUPS):
        ck = ck_ref[0, g]
        cv = ck_ref[0, N_NSA_GROUPS + g]
        o_cs, o_ws, selms = [], [], []
        for u in range(nsub):
            q = jnp.concatenate([q_ref[u * tq:(u + 1) * tq, (g * NSA_HPG + h) * HEAD_DIM:(g * NSA_HPG + h + 1) * HEAD_DIM]
                                 for h in range(NSA_HPG)], axis=0)
            s_c = _dot_nt(q, ck)
            cidx = lax.broadcasted_iota(jnp.int32, (tq, n_ck), 1)
            mc = jnp.where((cidx * CMP_STRIDE + CMP_BLOCK - 1 <= tpos[u]) & (cidx < n_cmp), 1.0, 0.0)
            e_c, l_c = _masked_exp(s_c, _tile_rows(mc, NSA_HPG) > 0.5)
            e_cb = e_c.astype(BF16)
            imp4 = _dot(e_cb, cover_ref[...]) / l_c
            imp = imp4[0:tq] + imp4[tq:2 * tq] + imp4[2 * tq:3 * tq] + imp4[3 * tq:4 * tq]
            selms.append(_topk_mask(_sel_scores(imp, tpos[u], n_sel), min(SEL_TOPK, n_sel), n_sel).astype(BF16))
            o_cs.append(_dot(e_cb, cv) / l_c)
            w0 = pl.multiple_of(jnp.maximum(t0 + u * tq - WINDOW, 0), tq)
            vw = kw_ref[pl.ds(w0, band), (N_NSA_GROUPS + g) * HEAD_DIM:(N_NSA_GROUPS + g + 1) * HEAD_DIM]
            s_w = _dot(q, kwt_scr[g, :, pl.ds(w0, band)])
            wpos = w0 + lax.broadcasted_iota(jnp.int32, (tq, band), 1)
            bw = jnp.where((wpos <= tpos[u]) & (wpos > tpos[u] - WINDOW), 0.0, NEG_INF)
            s_w = s_w + _tile_rows(bw, NSA_HPG)
            e_w = jnp.exp2(s_w - jnp.max(s_w, axis=-1, keepdims=True))
            o_ws.append(_dot(e_w.astype(BF16), vw) / jnp.sum(e_w, axis=-1, keepdims=True))

        for u in range(nsub):
            rows_u = slice(u * tq, (u + 1) * tq)
            selm_ref[rows_u, g * LANES:(g + 1) * LANES] = selms[u]
            gates = sm_ref[rows_u, :]
            for h in range(NSA_HPG):
                hh = g * NSA_HPG + h
                rows = slice(h * tq, (h + 1) * tq)
                ocw_ref[rows_u, hh * HEAD_DIM:(hh + 1) * HEAD_DIM] = (
                    gates[:, 3 * hh:3 * hh + 1] * o_cs[u][rows] + gates[:, 3 * hh + 2:3 * hh + 3] * o_ws[u][rows])


def _cover_matrix(n_rows, n_cmp, n_sel):
    c = np.arange(n_rows)[:, None] * CMP_STRIDE
    b = np.arange(LANES)[None, :] * SEL_BLOCK
    m = (c < b + SEL_BLOCK) & (c + CMP_BLOCK > b) & (np.arange(n_rows)[:, None] < n_cmp) & (np.arange(LANES)[None, :] < n_sel)
    return jnp.asarray(m.astype(np.float32), dtype=BF16)


def _expand_matrix(n_keys):
    m = (np.arange(n_keys)[None, :] // SEL_BLOCK) == np.arange(LANES)[:, None]
    return jnp.asarray(m.astype(np.float32), dtype=BF16)


SEL_MASK_BIG = 2.0 ** 100


def _nsa_sel_kernel(q_ref, k_ref, v_ref, selm_ref, sm_ref, ocw_ref, expand_ref, o_ref, s_scr, kt_scr):
    hh = pl.program_id(1) * NSA_HPG + pl.program_id(2)
    blk = FOX_BLK

    @pl.when(pl.program_id(2) == 0)
    def _():
        _transpose_rows(k_ref, kt_scr.at[0:HEAD_DIM])
        kt_scr[HEAD_DIM:, :] = expand_ref[...]

    causal = lax.broadcasted_iota(jnp.int32, (blk, blk), 1) <= lax.broadcasted_iota(jnp.int32, (blk, 1), 0)
    zeros = jnp.zeros((blk, LANES), F32)
    lane = lax.broadcasted_iota(jnp.int32, (blk, LANES), 1)
    for qb in range(q_ref.shape[0] // blk):
        rows = slice(qb * blk, (qb + 1) * blk)
        unsel = ((selm_ref[rows, :].astype(F32) - 1.0) * SEL_MASK_BIG).astype(BF16)
        q = jnp.concatenate([q_ref[rows, :], unsel], axis=1)
        mx = jnp.full((blk, LANES), -jnp.inf, F32)
        for kt in range(qb + 1):
            keys = slice(kt * blk, (kt + 1) * blk)
            s = _dot(q, kt_scr[:, keys])
            if kt == qb:
                s = jnp.where(causal, s, NEG_INF)
            s_scr[:, keys] = s
            mx = _lane_tile_max(mx, s)
        m = jnp.max(mx, axis=-1, keepdims=True)
        carry = (zeros, zeros)
        for kt in range(qb + 1):
            keys = slice(kt * blk, (kt + 1) * blk)
            carry = _exp_accumulate(carry, s_scr[:, keys], m, v_ref[keys, :])
        ls, acc = carry
        gate = jnp.sum(jnp.where(lane == 3 * hh + 1, sm_ref[rows, :], 0.0), axis=-1, keepdims=True)
        out = ocw_ref[rows, :] + gate * (acc / jnp.sum(ls, axis=-1, keepdims=True))
        o_ref[rows, :] = out.astype(BF16)


def _nsa_prompt(qn, sm, ckv, kvs_b, kvw_b, b_n, t_n):
    blk = NSA_TQ * NSA_NSUB
    nq = t_n // blk
    n_cmp = t_n // CMP_STRIDE - CMP_RATIO + 1
    n_sel = t_n // SEL_BLOCK
    n_ck = ckv.shape[2]
    n_g = N_NSA_GROUPS
    cover = _cover_matrix(n_ck, n_cmp, n_sel)
    expand = _expand_matrix(t_n)
    selm, ocw = pl.pallas_call(
        functools.partial(_nsa_front_kernel, n_cmp=n_cmp, n_sel=n_sel),
        grid=(b_n, nq),
        in_specs=[
            pl.BlockSpec((blk, NSA_Q_W), lambda b, i: (b * nq + i, 0)),
            pl.BlockSpec((blk, LANES), lambda b, i: (b * nq + i, 0)),
            pl.BlockSpec((1, 2 * n_g, n_ck, HEAD_DIM), lambda b, i: (b, 0, 0, 0)),
            pl.BlockSpec((t_n, KV_W), lambda b, i: (b, 0)),
            pl.BlockSpec((n_ck, LANES), lambda b, i: (0, 0)),
        ],
        out_specs=(pl.BlockSpec((blk, n_g * LANES), lambda b, i: (b * nq + i, 0)),
                   pl.BlockSpec((blk, NSA_Q_W), lambda b, i: (b * nq + i, 0))),
        out_shape=(jax.ShapeDtypeStruct((b_n * t_n, n_g * LANES), BF16),
                   jax.ShapeDtypeStruct((b_n * t_n, NSA_Q_W), F32)),
        scratch_shapes=[pltpu.VMEM((n_g, HEAD_DIM, t_n), BF16)],
        compiler_params=_cparams(("arbitrary", "arbitrary")),
        name="nsa_front",
    )(qn, sm, ckv, kvw_b, cover)

    hpg = NSA_HPG
    return pl.pallas_call(
        _nsa_sel_kernel,
        grid=(b_n, n_g, hpg),
        in_specs=[
            pl.BlockSpec((t_n, HEAD_DIM), lambda b, g, h: (b, g * hpg + h)),
            pl.BlockSpec((t_n, HEAD_DIM), lambda b, g, h: (b, g)),
            pl.BlockSpec((t_n, HEAD_DIM), lambda b, g, h: (b, n_g + g)),
            pl.BlockSpec((t_n, LANES), lambda b, g, h: (b, g)),
            pl.BlockSpec((t_n, LANES), lambda b, g, h: (b, 0)),
            pl.BlockSpec((t_n, HEAD_DIM), lambda b, g, h: (b, g * hpg + h)),
            pl.BlockSpec((LANES, t_n), lambda b, g, h: (0, 0)),
        ],
        out_specs=pl.BlockSpec((t_n, HEAD_DIM), lambda b, g, h: (b, g * hpg + h)),
        out_shape=jax.ShapeDtypeStruct((b_n * t_n, NSA_Q_W), BF16),
        scratch_shapes=[pltpu.VMEM((FOX_BLK, t_n), F32), pltpu.VMEM((2 * HEAD_DIM, t_n), BF16)],
        compiler_params=_cparams(("arbitrary", "arbitrary", "arbitrary")),
        name="nsa_sel",
    )(qn, kvs_b, kvs_b, selm, sm, ocw, expand)


FOX_BLK = 512


def _fox_prompt_kernel(q_ref, k_ref, v_ref, frow_ref, o_ref, s_scr, kt_scr):
    h = pl.program_id(1)
    blk = FOX_BLK
    assert q_ref.shape[0] % blk == 0
    _transpose_rows(k_ref, kt_scr)

    causal = lax.broadcasted_iota(jnp.int32, (blk, blk), 1) <= lax.broadcasted_iota(jnp.int32, (blk, 1), 0)
    zeros = jnp.zeros((blk, LANES), F32)
    for qb in range(q_ref.shape[0] // blk):
        q = q_ref[qb * blk:(qb + 1) * blk, :]
        mx = jnp.full((blk, LANES), -jnp.inf, F32)
        for kt in range(qb + 1):
            keys = slice(kt * blk, (kt + 1) * blk)
            s = _dot(q, kt_scr[:, keys]) - frow_ref[0, pl.ds(h, 1), keys]
            if kt == qb:
                s = jnp.where(causal, s, NEG_INF)
            s_scr[:, keys] = s
            mx = _lane_tile_max(mx, s)
        m = jnp.max(mx, axis=-1, keepdims=True)
        carry = (zeros, zeros)
        for kt in range(qb + 1):
            keys = slice(kt * blk, (kt + 1) * blk)
            carry = _exp_accumulate(carry, s_scr[:, keys], m, v_ref[keys, :])
        ls, acc = carry
        o_ref[qb * blk:(qb + 1) * blk, :] = (acc / jnp.sum(ls, axis=-1, keepdims=True)).astype(BF16)


def _fox_prompt(qf, kvf_b, frow, b_n, t_n):
    nh = N_FOX_HEADS
    return pl.pallas_call(
        _fox_prompt_kernel,
        grid=(b_n, nh),
        in_specs=[
            pl.BlockSpec((t_n, HEAD_DIM), lambda b, h: (b, h)),
            pl.BlockSpec((t_n, HEAD_DIM), lambda b, h: (b, h)),
            pl.BlockSpec((t_n, HEAD_DIM), lambda b, h: (b, nh + h)),
            pl.BlockSpec((1, nh, t_n), lambda b, h: (b, 0, 0)),
        ],
        out_specs=pl.BlockSpec((t_n, HEAD_DIM), lambda b, h: (b, h)),
        out_shape=jax.ShapeDtypeStruct((b_n * t_n, FOX_W), BF16),
        scratch_shapes=[pltpu.VMEM((FOX_BLK, t_n), F32), pltpu.VMEM((HEAD_DIM, t_n), BF16)],
        compiler_params=_cparams(("arbitrary", "arbitrary")),
        name="fox_prompt",
    )(qf, kvf_b, kvf_b, frow)


FOX_SAMPLE_CH = 512
TOK_PAD = 8
NEW_PAD = 128
NSA_SAMPLE_SUB = 1


def _nsa_sample_kernel(pt_ref, *refs, n_pages, page, n_buf, n_tok, n_sub):
    del pt_ref
    per_seq = 2 * n_pages + 1
    kvs_new_ref, kvw_new_ref, q_ref, sm_ref = refs[n_sub * per_seq:n_sub * per_seq + 4]
    for sq in range(n_sub):
        seq_refs = refs[sq * per_seq:(sq + 1) * per_seq]
        _nsa_sample_one(seq_refs[:n_pages], seq_refs[n_pages:2 * n_pages], seq_refs[2 * n_pages],
                        kvs_new_ref[sq], kvw_new_ref[sq], q_ref.at[sq], sm_ref[sq],
                        *refs[n_sub * per_seq + 4:-1], refs[-1].at[sq],
                        n_pages=n_pages, page=page, n_buf=n_buf, n_tok=n_tok)


def _nsa_sample_one(cmp_pages, sel_pages, win_ref, kvs_new, kvw_new, q_ref, gates, w1k_ref, w1v_ref, pek_ref, pev_ref,
                    w2k_ref, w2v_ref, cover_ref, expand_ref, o_ref, *, n_pages, page, n_buf, n_tok):
    past = n_pages * page
    n_slab = 2 * N_NSA_GROUPS
    chunks_per_page = page // CMP_STRIDE
    n_chunk = n_pages * chunks_per_page
    n_cmp = (past + n_tok + CMP_STRIDE - 1) // CMP_STRIDE - CMP_RATIO + 1
    n_sel = (past + n_tok + SEL_BLOCK - 1) // SEL_BLOCK
    tp = TOK_PAD
    tpos = past + lax.broadcasted_iota(jnp.int32, (tp, 1), 0)

    rows_pc = CMP_STRIDE * n_slab
    swapped = [jnp.swapaxes(cmp_pages[p][...].reshape(chunks_per_page, rows_pc, HEAD_DIM), 0, 1)
               for p in range(n_pages)]

    def compress(slab, w1_ref, pe_ref, w2_ref):
        cols = []
        for i in range(CMP_STRIDE):
            cols.append(jnp.concatenate([swapped[p][i * n_slab + slab] for p in range(n_pages)], axis=0))
        xc = jnp.concatenate(cols, axis=1).astype(BF16)
        return _compress_tail(xc, w1_ref[...], pe_ref[...], w2_ref[...]).astype(BF16)

    qs, o_cs, scores = [], [], []
    for g in range(N_NSA_GROUPS):
        q = jnp.concatenate([q_ref[:, (g * NSA_HPG + h) * HEAD_DIM:(g * NSA_HPG + h + 1) * HEAD_DIM]
                             for h in range(NSA_HPG)], axis=0)
        ck = compress(g, w1k_ref, pek_ref, w2k_ref)
        cv = compress(N_NSA_GROUPS + g, w1v_ref, pev_ref, w2v_ref)
        s_c = _dot_nt(q, ck)
        cidx = lax.broadcasted_iota(jnp.int32, (tp, n_chunk), 1)
        mc = jnp.where((cidx * CMP_STRIDE + CMP_BLOCK - 1 <= tpos) & (cidx < n_cmp), 1.0, 0.0)
        p_c = _masked_softmax(s_c, _tile_rows(mc, NSA_HPG) > 0.5)
        p_cb = p_c.astype(BF16)
        imp4 = _dot(p_cb, cover_ref[...])
        imp = imp4[0:tp] + imp4[tp:2 * tp] + imp4[2 * tp:3 * tp] + imp4[3 * tp:4 * tp]
        qs.append(q)
        o_cs.append(_dot(p_cb, cv))
        scores.append(_sel_scores(imp, tpos, n_sel))

    score_all = jnp.concatenate(scores + [jnp.zeros((LANES - N_NSA_GROUPS * tp, LANES), F32)], axis=0)
    selm_all = _topk_mask(score_all, min(SEL_TOPK, n_sel), n_sel)

    def sel_rows(slab):
        return jnp.concatenate([sel_pages[p][pl.ds(slab, page, stride=n_slab), :] for p in range(n_pages)], axis=0)

    def with_new(cached, new):
        pad = jnp.zeros((NEW_PAD - new.shape[0], new.shape[1]), new.dtype)
        return jnp.concatenate([cached, new, pad], axis=0).astype(BF16)

    for g in range(N_NSA_GROUPS):
        q, o_c = qs[g], o_cs[g]
        selm = selm_all[g * tp:(g + 1) * tp].astype(BF16)
        kc = slice(g * HEAD_DIM, (g + 1) * HEAD_DIM)
        vc = slice((N_NSA_GROUPS + g) * HEAD_DIM, (N_NSA_GROUPS + g + 1) * HEAD_DIM)

        n_keys = past + NEW_PAD
        s_s = _dot_nt(q, with_new(sel_rows(g), kvs_new[:, kc]))
        selx = _dot(selm, expand_ref[...])
        kpos = lax.broadcasted_iota(jnp.int32, (tp, n_keys), 1)
        ms = jnp.where((selx > 0.5) & (kpos <= tpos) & (kpos < past + n_tok), 1.0, 0.0)
        p_s = _masked_softmax(s_s, _tile_rows(ms, NSA_HPG) > 0.5)
        o_s = _dot(p_s.astype(BF16), with_new(sel_rows(N_NSA_GROUPS + g), kvs_new[:, vc]))

        s_w = _dot_nt(q, with_new(win_ref[pl.ds(g, n_buf, stride=n_slab), :], kvw_new[:, kc]))
        wpos = past - n_buf + lax.broadcasted_iota(jnp.int32, (tp, n_buf + NEW_PAD), 1)
        mw = jnp.where((wpos <= tpos) & (wpos > tpos - WINDOW) & (wpos < past + n_tok), 1.0, 0.0)
        p_w = _masked_softmax(s_w, _tile_rows(mw, NSA_HPG) > 0.5)
        o_w = _dot(p_w.astype(BF16), with_new(win_ref[pl.ds(N_NSA_GROUPS + g, n_buf, stride=n_slab), :],
                                              kvw_new[:, vc]))

        for h in range(NSA_HPG):
            hh = g * NSA_HPG + h
            rows = slice(h * tp, (h + 1) * tp)
            out = (gates[:, 3 * hh:3 * hh + 1] * o_c[rows] + gates[:, 3 * hh + 1:3 * hh + 2] * o_s[rows]
                   + gates[:, 3 * hh + 2:3 * hh + 3] * o_w[rows])
            o_ref[:, hh * HEAD_DIM:(hh + 1) * HEAD_DIM] = out.astype(BF16)


def _pad_tokens(a, n_seq, n_tok):
    a = a.reshape(n_seq, n_tok, a.shape[-1])
    return jnp.pad(a, ((0, 0), (0, TOK_PAD - n_tok), (0, 0)))


def _nsa_sample(page_table, cache_cmp, cache_sel, win_buf, kvs_new, kvw_new, qn, sm, cmp_w, n_tok):
    n_seq, n_pages = page_table.shape
    page = cache_cmp.shape[1]
    n_slab = 2 * N_NSA_GROUPS
    n_buf = win_buf.shape[1]
    past = n_pages * page
    cmp2 = cache_cmp.reshape(-1, HEAD_DIM)
    sel2 = cache_sel.reshape(-1, HEAD_DIM)
    win2 = win_buf.reshape(-1, HEAD_DIM)
    n_chunk = past // CMP_STRIDE
    n_cmp = (past + n_tok + CMP_STRIDE - 1) // CMP_STRIDE - CMP_RATIO + 1
    n_sel = (past + n_tok + SEL_BLOCK - 1) // SEL_BLOCK
    cover = _cover_matrix(n_chunk, min(n_cmp, n_chunk), n_sel)
    expand = _expand_matrix(past + NEW_PAD)
    w1k, pek, w2k, w1v, pev, w2v = cmp_w

    n_sub = NSA_SAMPLE_SUB
    assert n_seq % n_sub == 0

    def page_spec(sq, p):
        return pl.BlockSpec((page * n_slab, HEAD_DIM), lambda b, pt, sq=sq, p=p: (pt[b * n_sub + sq, p], 0))

    const2 = lambda b, pt: (0, 0)
    seq3 = lambda b, pt: (b, 0, 0)
    per_seq = []
    for sq in range(n_sub):
        per_seq += ([page_spec(sq, p) for p in range(n_pages)] + [page_spec(sq, p) for p in range(n_pages)]
                    + [pl.BlockSpec((n_buf * n_slab, HEAD_DIM), lambda b, pt, sq=sq: (b * n_sub + sq, 0))])
    in_specs = (per_seq + [
        pl.BlockSpec((n_sub, TOK_PAD, KV_W), seq3),
        pl.BlockSpec((n_sub, TOK_PAD, KV_W), seq3),
        pl.BlockSpec((n_sub, TOK_PAD, NSA_Q_W), seq3),
        pl.BlockSpec((n_sub, TOK_PAD, LANES), seq3),
        pl.BlockSpec(w1k.shape, const2),
        pl.BlockSpec(w1v.shape, const2),
        pl.BlockSpec(pek.shape, const2),
        pl.BlockSpec(pev.shape, const2),
        pl.BlockSpec(w2k.shape, const2),
        pl.BlockSpec(w2v.shape, const2),
        pl.BlockSpec(cover.shape, const2),
        pl.BlockSpec(expand.shape, const2),
    ])
    kern = functools.partial(_nsa_sample_kernel, n_pages=n_pages, page=page, n_buf=n_buf, n_tok=n_tok, n_sub=n_sub)
    grid_spec = pltpu.PrefetchScalarGridSpec(
        num_scalar_prefetch=1, grid=(n_seq // n_sub,), in_specs=in_specs,
        out_specs=pl.BlockSpec((n_sub, TOK_PAD, NSA_Q_W), seq3))
    return pl.pallas_call(
        kern,
        grid_spec=grid_spec,
        out_shape=jax.ShapeDtypeStruct((n_seq, TOK_PAD, NSA_Q_W), BF16),
        compiler_params=_cparams(("arbitrary",)),
        name="nsa_sample",
    )(page_table, *(([cmp2] * n_pages + [sel2] * n_pages + [win2]) * n_sub),
      _pad_tokens(kvs_new, n_seq, n_tok), _pad_tokens(kvw_new, n_seq, n_tok),
      _pad_tokens(qn, n_seq, n_tok), _pad_tokens(sm, n_seq, n_tok),
      w1k, w1v, pek, pev, w2k, w2v, cover, expand)


def _fox_sample_kernel(pt_ref, *refs, n_pages, page, n_tok):
    kv_pages = refs[:n_pages]
    lf_pages = refs[n_pages:2 * n_pages]
    k_new_ref, v_new_ref, q_ref, lfn_ref, o_ref, mask_scr, s_scr, kb_scr = refs[2 * n_pages:]
    del pt_ref
    nh = N_FOX_HEADS
    rows_pp = 2 * nh
    page_rows = page * rows_pp
    n_chunk = page_rows // LANES
    n_q = nh * n_tok
    assert rows_pp == 16 and LANES % rows_pp == 0 and n_q <= LANES

    @pl.when(pl.program_id(0) == 0)
    def _():
        qrow = lax.broadcasted_iota(jnp.int32, mask_scr.shape, 0)
        lane = lax.broadcasted_iota(jnp.int32, mask_scr.shape, 1)
        mask_scr[...] = jnp.where((lane & (rows_pp - 1)) == (qrow & (nh - 1)), 0.0, NEG_INF)

    x = jnp.concatenate([lf_pages[p][0] for p in range(n_pages)], axis=0)
    n_r = x.shape[0]
    la = lax.broadcasted_iota(jnp.int32, (LANES, LANES), 0)
    lb = lax.broadcasted_iota(jnp.int32, (LANES, LANES), 1)
    same = (la & (rows_pp - 1)) == (lb & (rows_pp - 1))
    u_in = jnp.where(same & (jnp.right_shift(la, 4) <= jnp.right_shift(lb, 4)), 1.0, 0.0).astype(BF16)
    u_all = jnp.where(same, 1.0, 0.0).astype(BF16)
    xh, xm, xl = _split3(x)
    within = _dot(xh, u_in) + _dot(xm, u_in) + _dot(xl, u_in)
    tot = _dot(xh, u_all) + _dot(xm, u_all) + _dot(xl, u_all)
    ra = lax.broadcasted_iota(jnp.int32, (n_r, n_r), 0)
    rb = lax.broadcasted_iota(jnp.int32, (n_r, n_r), 1)
    before = jnp.where(rb < ra, 1.0, 0.0).astype(BF16)
    th, tm_, tl = _split3(tot)
    offs = _dot(before, th) + _dot(before, tm_) + _dot(before, tl)
    f_end = offs[n_r - 1:n_r, :] + tot[n_r - 1:n_r, :]
    bias = (f_end - (within + offs)) * LOG2E

    q_all = q_ref[0]

    ch = mask_scr.shape[1]
    lt = ch // LANES
    steps = [(p, c) for p in range(n_pages) for c in range(page_rows // ch)]
    mx = jnp.full((n_q, LANES), -jnp.inf, F32)
    for p, c in steps:
        k_b = kv_pages[p][pl.ds(c * ch, ch), :].astype(BF16)
        kb_scr[p * page_rows + c * ch:p * page_rows + (c + 1) * ch, :] = k_b
        r0 = p * n_chunk + c * lt
        brow = jnp.concatenate([bias[r0 + i:r0 + i + 1, :] for i in range(lt)], axis=1)
        s = _dot_nt(q_all, k_b) + brow + mask_scr[...]
        s_scr[:, p * page_rows + c * ch:p * page_rows + (c + 1) * ch] = s
        mx = _lane_tile_max(mx, s)

    pad = jnp.zeros((LANES - n_q, HEAD_DIM), F32)
    k_new = jnp.concatenate([k_new_ref[0], pad], axis=0).astype(BF16)
    v_new = jnp.concatenate([v_new_ref[0], pad], axis=0).astype(BF16)
    g_in = jnp.where(((la & (nh - 1)) == (lb & (nh - 1))) & (la <= lb), 1.0, 0.0).astype(BF16)
    nh_, nm_, nl_ = _split3(lfn_ref[0])
    c_new = (_dot(nh_, g_in) + _dot(nm_, g_in) + _dot(nl_, g_in))[0:1, :] * LOG2E
    qrow = lax.broadcasted_iota(jnp.int32, (n_q, LANES), 0)
    lane = lax.broadcasted_iota(jnp.int32, (n_q, LANES), 1)
    ok = ((lane & (nh - 1)) == (qrow & (nh - 1))) & (lane <= qrow)
    s_new = jnp.where(ok, _dot_nt(q_all, k_new) - c_new, NEG_INF)
    m = jnp.max(jnp.maximum(mx, s_new), axis=-1, keepdims=True)

    carry = _exp_accumulate((jnp.zeros((n_q, LANES), F32), jnp.zeros((n_q, HEAD_DIM), F32)), s_new, m, v_new)
    for p, c in steps:
        rows = slice(p * page_rows + c * ch, p * page_rows + (c + 1) * ch)
        carry = _exp_accumulate(carry, s_scr[:, rows], m, kb_scr[rows, :], lane_shift=nh)
    ls, acc = carry
    o_ref[0] = (acc / jnp.sum(ls, axis=-1, keepdims=True)).astype(BF16)


def _fox_sample(page_table, cache_fox, cache_logf, kvf_new, qf, sm, n_tok):
    n_seq, n_pages = page_table.shape
    page = cache_fox.shape[1]
    nh = N_FOX_HEADS
    kv2 = cache_fox.reshape(-1, HEAD_DIM)
    rows_pp = 2 * nh
    n_chunk = page * rows_pp // LANES
    lf_c = jnp.pad(cache_logf, ((0, 0), (0, 0), (0, rows_pp - nh))).reshape(-1, n_chunk, LANES)
    n_q = n_tok * nh
    lfn = sm[:, LOGF_COL0:LOGF_COL0 + nh].reshape(n_seq, 1, n_q)
    lfn = jnp.pad(lfn, ((0, 0), (0, 7), (0, LANES - n_q)))
    kv_new = kvf_new.reshape(n_seq, n_tok, 2, nh, HEAD_DIM)
    k_new = kv_new[:, :, 0].reshape(n_seq, n_q, HEAD_DIM)
    v_new = kv_new[:, :, 1].reshape(n_seq, n_q, HEAD_DIM)
    q3 = qf.reshape(n_seq, n_q, HEAD_DIM)

    seq3 = lambda b, pt: (b, 0, 0)
    in_specs = ([pl.BlockSpec((page * rows_pp, HEAD_DIM), lambda b, pt, p=p: (pt[b, p], 0)) for p in range(n_pages)]
                + [pl.BlockSpec((1, n_chunk, LANES), lambda b, pt, p=p: (pt[b, p], 0, 0)) for p in range(n_pages)]
                + [pl.BlockSpec((1, n_q, HEAD_DIM), seq3),
                   pl.BlockSpec((1, n_q, HEAD_DIM), seq3),
                   pl.BlockSpec((1, n_q, HEAD_DIM), seq3),
                   pl.BlockSpec((1, 8, LANES), seq3)])
    kern = functools.partial(_fox_sample_kernel, n_pages=n_pages, page=page, n_tok=n_tok)
    n_rows = n_pages * page * rows_pp
    grid_spec = pltpu.PrefetchScalarGridSpec(
        num_scalar_prefetch=1, grid=(n_seq,), in_specs=in_specs,
        out_specs=pl.BlockSpec((1, n_q, HEAD_DIM), seq3),
        scratch_shapes=[pltpu.VMEM((n_q, FOX_SAMPLE_CH), F32),
                        pltpu.VMEM((n_q, n_rows), F32),
                        pltpu.VMEM((n_rows, HEAD_DIM), BF16)])
    out = pl.pallas_call(
        kern,
        grid_spec=grid_spec,
        out_shape=jax.ShapeDtypeStruct((n_seq, n_q, HEAD_DIM), BF16),
        compiler_params=_cparams(("arbitrary",)),
        name="fox_sample",
    )(page_table, *([kv2] * n_pages), *([lf_c] * n_pages), k_new, v_new, q3, lfn)
    return out.reshape(n_seq * n_tok, nh * HEAD_DIM)


def _postmix_kernel(on_ref, of_ref, gm0_ref, gm1_ref, x_ref, wn_ref, wf_ref, wo_ref, g_ref, y_ref):
    a = _dot(on_ref[...], wn_ref[...])
    b = _dot(of_ref[...], wf_ref[...])
    merged = gm0_ref[...] * a + gm1_ref[...] * b
    z = _dot(merged.astype(BF16), wo_ref[...])
    y_ref[...] = x_ref[...] + _rms(z, g_ref[...])


def _postmix(o_n, o_f, gm, x2, wn, wf, wo, g):
    n, d = x2.shape
    tm = 256
    row = lambda i: (i, 0)
    const = lambda i: (0, 0)
    return pl.pallas_call(
        _postmix_kernel,
        grid=(n // tm,),
        in_specs=[
            pl.BlockSpec((tm, NSA_Q_W), row),
            pl.BlockSpec((tm, FOX_W), row),
            pl.BlockSpec((tm, d), lambda i: (i, 0)),
            pl.BlockSpec((tm, d), lambda i: (i, 1)),
            pl.BlockSpec((tm, d), row),
            pl.BlockSpec(wn.shape, const),
            pl.BlockSpec(wf.shape, const),
            pl.BlockSpec(wo.shape, const),
            pl.BlockSpec((1, d), const),
        ],
        out_specs=pl.BlockSpec((tm, d), row),
        out_shape=jax.ShapeDtypeStruct((n, d), F32),
        compiler_params=_cparams(("arbitrary",)),
        name="postmix",
    )(o_n, o_f, gm, gm, x2, wn, wf, wo, g)


FFN_TM = 1024
FFN_TF = 512
HALO = 16


def _ffn_kernel(*refs, seq_tiles, n_tok):
    if n_tok is None:
        (x_ref, xh_ref, g_ref, wg_ref, wu_ref, wd_ref, wc_ref, bc_ref, gp_ref,
         y_ref, gt_ref, h_scr, hh_scr) = refs
    else:
        (x_ref, s0_ref, s1_ref, g_ref, wg_ref, wu_ref, wd_ref, wc_ref, bc_ref, gp_ref,
         y_ref, gt_ref, h_scr) = refs
    i = pl.program_id(0)
    f = pl.program_id(1)
    tm = x_ref.shape[0]

    @pl.when(f == 0)
    def _():
        h_scr[...] = _rms(x_ref[...], g_ref[...]).astype(BF16)
        y_ref[...] = jnp.zeros_like(y_ref)
        if n_tok is None:
            hh_scr[...] = _rms(xh_ref[...], g_ref[...]).astype(BF16)

    h2 = h_scr[...]
    gate = _dot(h2, wg_ref[...])
    up = _dot(h2, wu_ref[...])
    row = lax.broadcasted_iota(jnp.int32, gate.shape, 0)
    r1 = pltpu.roll(gate, 1, axis=0)
    r2 = pltpu.roll(gate, 2, axis=0)
    if n_tok is None:
        first = (i % seq_tiles) == 0
        gh = jnp.where(first, 0.0, _dot(hh_scr[...], wg_ref[...]))
        p1 = gh[HALO - 1:HALO, :]
        p2 = gh[HALO - 2:HALO - 1, :]
        g1 = jnp.where(row == 0, p1, r1)
        g2 = jnp.where(row == 0, p2, jnp.where(row == 1, p1, r2))
        gt_ref[...] = gate[tm - 8:tm, :]
    else:
        assert n_tok & (n_tok - 1) == 0
        rt = row & (n_tok - 1)
        g1 = jnp.where(rt == 0, s1_ref[...], r1)
        g2 = jnp.where(rt == 0, s0_ref[...], jnp.where(rt == 1, s1_ref[...], r2))
        gt_ref[...] = gate
    wc = wc_ref[...]
    gc = bc_ref[...] + wc[0:1, :] * g2 + wc[1:2, :] * g1 + wc[2:3, :] * gate
    act = jax.nn.gelu(gc, approximate=True) * up
    y_ref[...] += _dot(act.astype(BF16), wd_ref[...])

    @pl.when(f == pl.num_programs(1) - 1)
    def _():
        y_ref[...] = x_ref[...] + _rms(y_ref[...], gp_ref[...])


def _ffn(x2, g_pre, w_up_b, w_down_b, w_conv, b_conv, g_post, *, seq_len=None, state=None):
    n, d = x2.shape
    d_ff = w_down_b.shape[0]
    tf = FFN_TF
    nf = d_ff // tf
    tm = min(FFN_TM, n)
    common_w = [
        pl.BlockSpec((1, d), lambda i, f: (0, 0)),
        pl.BlockSpec((d, tf), lambda i, f: (0, f)),
        pl.BlockSpec((d, tf), lambda i, f: (0, nf + f)),
        pl.BlockSpec((tf, d), lambda i, f: (f, 0)),
        pl.BlockSpec((CONV_WIDTH, tf), lambda i, f: (0, f)),
        pl.BlockSpec((1, tf), lambda i, f: (0, f)),
        pl.BlockSpec((1, d), lambda i, f: (0, 0)),
    ]
    w_args = (g_pre, w_up_b, w_up_b, w_down_b, w_conv, b_conv, g_post)
    row = lambda i, f: (i, 0)
    once = pl.Buffered(1)
    if state is None:
        seq_tiles = seq_len // tm
        halo_blocks = tm // HALO
        in_specs = [pl.BlockSpec((tm, d), row, pipeline_mode=once),
                    pl.BlockSpec((HALO, d), lambda i, f: (jnp.maximum(i * halo_blocks - 1, 0), 0))] + common_w
        args = (x2, x2) + w_args
        gt_shape = jax.ShapeDtypeStruct((n // tm * 8, d_ff), F32)
        gt_spec = pl.BlockSpec((8, tf), lambda i, f: (i, f))
        scratch = [pltpu.VMEM((tm, d), BF16), pltpu.VMEM((HALO, d), BF16)]
        kern = functools.partial(_ffn_kernel, seq_tiles=seq_tiles, n_tok=None)
    else:
        n_tok = n // state.shape[0]
        s0 = jnp.repeat(state[:, 0], n_tok, axis=0)
        s1 = jnp.repeat(state[:, 1], n_tok, axis=0)
        in_specs = [pl.BlockSpec((tm, d), row, pipeline_mode=once),
                    pl.BlockSpec((tm, tf), lambda i, f: (i, f)),
                    pl.BlockSpec((tm, tf), lambda i, f: (i, f))] + common_w
        args = (x2, s0, s1) + w_args
        gt_shape = jax.ShapeDtypeStruct((n, d_ff), F32)
        gt_spec = pl.BlockSpec((tm, tf), lambda i, f: (i, f))
        scratch = [pltpu.VMEM((tm, d), BF16)]
        kern = functools.partial(_ffn_kernel, seq_tiles=None, n_tok=n_tok)
    return pl.pallas_call(
        kern,
        grid=(n // tm, nf),
        in_specs=in_specs,
        out_specs=(pl.BlockSpec((tm, d), row), gt_spec),
        out_shape=(jax.ShapeDtypeStruct((n, d), F32), gt_shape),
        scratch_shapes=scratch,
        compiler_params=_cparams(("arbitrary", "arbitrary")),
        name="ffn",
    )(*args)


def _rope_tables(pos):
    half = HEAD_DIM // 2
    inv_freq = ROPE_THETA ** (-jnp.arange(half, dtype=F32) / half)
    ang = pos.astype(F32)[:, None] * inv_freq[None, :]
    cos, sin = jnp.cos(ang), jnp.sin(ang)
    return jnp.concatenate([cos, cos], axis=-1), jnp.concatenate([-sin, sin], axis=-1)


def _cmp_weights(w1, pe, w2):
    w1r = w1.reshape(CMP_RATIO, CMP_STRIDE * HEAD_DIM, HEAD_DIM)
    w1cat = jnp.concatenate([w1r[r] for r in range(CMP_RATIO)], axis=1).astype(BF16)
    pe8 = jnp.pad(pe.reshape(CMP_RATIO, CMP_STRIDE * HEAD_DIM), ((0, 8 - CMP_RATIO), (0, 0)))
    return w1cat, pe8, w2.astype(BF16)


def kernel(x_prompt, x_sample, cache_nsa_cmp_kv, cache_nsa_sel_kv, cache_nsa_win_kv, cache_fox_kv, cache_fox_logf, state_ffn_conv, page_table, g_pre_mix, w_in, b_fgt, w_cmp_k1, pe_cmp_k, w_cmp_k2, w_cmp_v1, pe_cmp_v, w_cmp_v2, w_nsa_o, w_fox_o, w_out, g_post_mix, g_pre_ffn, w_up, w_conv, b_conv, w_down, g_post_ffn):
    b_p, t_p, d = x_prompt.shape
    b_s, t_s, _ = x_sample.shape
    depth = w_in.shape[0]
    page = cache_nsa_cmp_kv.shape[2]
    past = page_table.shape[1] * page
    g_n, n_h = N_NSA_GROUPS, N_FOX_HEADS

    cos_p, sin_p = _rope_tables(jnp.tile(jnp.arange(t_p), b_p))
    cos_s, sin_s = _rope_tables(jnp.tile(past + jnp.arange(t_s), b_s))

    y_p = x_prompt.reshape(b_p * t_p, d)
    y_s = x_sample.reshape(b_s * t_s, d)
    outs = {k: [] for k in ('cmp_p', 'cmp_s', 'sel_p', 'sel_s', 'win_p', 'win_s',
                            'fox_p', 'fox_s', 'lf_p', 'lf_s', 'conv_p', 'conv_s')}
    o_q = NSA_Q_W
    o_g = o_q + 3 * KV_W
    o_f = o_g + N_GATE_COLS
    o_ff = o_f + 3 * FOX_W
    o_m = o_ff + n_h
    for l in range(depth):
        w = w_in[l]
        w_main = jnp.concatenate([w[:, :o_g], w[:, o_f:o_ff], w[:, o_m:]], axis=1).astype(BF16)
        w_small = jnp.concatenate([w[:, o_g:o_f], w[:, o_ff:o_m],
                                   jnp.zeros((d, LANES - N_GATE_COLS - n_h), F32)], axis=1).astype(BF16)
        bf_row = jnp.zeros((1, LANES), F32).at[0, LOGF_COL0:LOGF_COL0 + n_h].set(b_fgt[l])
        g1 = g_pre_mix[l][None, :]
        cmp_k = _cmp_weights(w_cmp_k1[l], pe_cmp_k[l], w_cmp_k2[l])
        cmp_v = _cmp_weights(w_cmp_v1[l], pe_cmp_v[l], w_cmp_v2[l])
        wn, wf, wo = w_nsa_o[l].astype(BF16), w_fox_o[l].astype(BF16), w_out[l].astype(BF16)
        wu, wd = w_up[l].astype(BF16), w_down[l].astype(BF16)
        ffn_w = (g_pre_ffn[l][None, :], wu, wd, w_conv[l], b_conv[l][None, :], g_post_ffn[l][None, :])

        (qn, kvc, kvs, kvw, qf, kvf, gm, sm, kvs_b, kvw_b, kvf_b) = _project(y_p, g1, cos_p, sin_p, w_main, w_small, bf_row)
        w1cat = jnp.stack([cmp_k[0], cmp_v[0]])
        pe8 = jnp.stack([cmp_k[1], cmp_v[1]])
        w2 = jnp.stack([cmp_k[2], cmp_v[2]])
        ckv = _compress_prompt(kvc, b_p, t_p, w1cat, pe8, w2)
        frow = _fcum_prompt(sm, b_p, t_p)
        o_n = _nsa_prompt(qn, sm, ckv, kvs_b, kvw_b, b_p, t_p)
        o_fx = _fox_prompt(qf, kvf_b, frow, b_p, t_p)
        y1 = _postmix(o_n, o_fx, gm, y_p, wn, wf, wo, g_post_mix[l][None, :])
        y_p, gt = _ffn(y1, *ffn_w, seq_len=t_p)
        n_win = min(WINDOW, t_p)
        outs['cmp_p'].append(kvc.reshape(b_p, t_p, 2, g_n, HEAD_DIM))
        outs['sel_p'].append(kvs.reshape(b_p, t_p, 2, g_n, HEAD_DIM))
        outs['win_p'].append(kvw.reshape(b_p, t_p, 2, g_n, HEAD_DIM)[:, t_p - n_win:])
        outs['fox_p'].append(kvf.reshape(b_p, t_p, 2, n_h, HEAD_DIM))
        outs['lf_p'].append(sm[:, LOGF_COL0:LOGF_COL0 + n_h].reshape(b_p, t_p, n_h))
        tiles_per_seq = t_p // FFN_TM
        gt = gt.reshape(b_p, tiles_per_seq, 8, -1)
        outs['conv_p'].append(gt[:, -1, 8 - (CONV_WIDTH - 1):])

        (qn, kvc, kvs, kvw, qf, kvf, gm, sm, _, _, _) = _project(y_s, g1, cos_s, sin_s, w_main, w_small, bf_row)
        win_buf = cache_nsa_win_kv[l]
        o_n = _nsa_sample(page_table, cache_nsa_cmp_kv[l], cache_nsa_sel_kv[l], win_buf, kvs, kvw, qn, sm,
                          cmp_k + cmp_v, t_s)
        o_fx = _fox_sample(page_table, cache_fox_kv[l], cache_fox_logf[l], kvf, qf, sm, t_s)
        o_n = o_n[:, :t_s].reshape(b_s * t_s, -1)
        y1 = _postmix(o_n, o_fx, gm, y_s, wn, wf, wo, g_post_mix[l][None, :])
        y_s, gt = _ffn(y1, *ffn_w, state=state_ffn_conv[l])
        kw_new = kvw.reshape(b_s, t_s, 2, g_n, HEAD_DIM)
        n_win = min(WINDOW, win_buf.shape[1] + t_s)
        outs['cmp_s'].append(kvc.reshape(b_s, t_s, 2, g_n, HEAD_DIM))
        outs['sel_s'].append(kvs.reshape(b_s, t_s, 2, g_n, HEAD_DIM))
        outs['win_s'].append(jnp.concatenate([win_buf, kw_new], axis=1)[:, -n_win:])
        outs['fox_s'].append(kvf.reshape(b_s, t_s, 2, n_h, HEAD_DIM))
        outs['lf_s'].append(sm[:, LOGF_COL0:LOGF_COL0 + n_h].reshape(b_s, t_s, n_h))
        gfull = jnp.concatenate([state_ffn_conv[l], gt.reshape(b_s, t_s, -1)], axis=1)
        outs['conv_s'].append(gfull[:, t_s:])

    st = {k: jnp.stack(v) for k, v in outs.items()}
    return (y_p.reshape(b_p, t_p, d), y_s.reshape(b_s, t_s, d),
            st['cmp_p'], st['cmp_s'], st['sel_p'], st['sel_s'], st['win_p'], st['win_s'],
            st['fox_p'], st['fox_s'], st['lf_p'], st['lf_s'], st['conv_p'], st['conv_s'])
```

```python
import functools

import numpy as np
import jax
import jax.numpy as jnp
from jax import lax
from jax.experimental import pallas as pl
from jax.experimental.pallas import tpu as pltpu

F32 = jnp.float32
BF16 = jnp.bfloat16

HEAD_DIM = 128
N_NSA_HEADS = 8
N_NSA_GROUPS = 2
NSA_HPG = N_NSA_HEADS // N_NSA_GROUPS
N_FOX_HEADS = 8
CMP_BLOCK = 32
CMP_STRIDE = 16
CMP_RATIO = CMP_BLOCK // CMP_STRIDE
SEL_BLOCK = 64
SEL_TOPK = 16
N_LOCAL_BLOCKS = 2
WINDOW = 512
CONV_WIDTH = 3
ROPE_THETA = 10000.0
RMS_EPS = 1e-6
FORCE_BONUS = 1e4
NEG_INF = -1e30
LOG2E = 1.4426950408889634
QK_SCALE = HEAD_DIM ** -0.5 * LOG2E

N_GATE_COLS = N_NSA_HEADS * 3
LOGF_COL0 = N_GATE_COLS
LANES = 128
VMEM_LIMIT = 56 * 1024 * 1024

PROJ_TN = 512
PROJ_TM = 1024
KV_W = 2 * N_NSA_GROUPS * HEAD_DIM
NSA_Q_W = N_NSA_HEADS * HEAD_DIM
FOX_W = N_FOX_HEADS * HEAD_DIM


def _cparams(sem):
    return pltpu.CompilerParams(dimension_semantics=sem, vmem_limit_bytes=VMEM_LIMIT)


def _dot(a, b):
    return jnp.dot(a, b, preferred_element_type=F32)


def _dot_nt(a, b):
    return lax.dot_general(a, b, (((1,), (1,)), ((), ())), preferred_element_type=F32)


def _rms(x, g):
    return x * lax.rsqrt(jnp.mean(x * x, axis=-1, keepdims=True) + RMS_EPS) * g


def _masked_softmax(s, mask):
    sm = jnp.where(mask, s, NEG_INF)
    m = jnp.max(sm, axis=-1, keepdims=True)
    e = jnp.where(mask, jnp.exp2(sm - m), 0.0)
    l = jnp.sum(e, axis=-1, keepdims=True)
    return e / jnp.where(l > 0.0, l, 1.0)


def _transpose_rows(src_ref, dst_ref, cols=slice(None)):
    for c in range(src_ref.shape[0] // LANES):
        rows = slice(c * LANES, (c + 1) * LANES)
        dst_ref[:, rows] = src_ref[rows, cols].astype(F32).T.astype(BF16)


def _masked_exp(s, mask):
    sm = jnp.where(mask, s, NEG_INF)
    e = jnp.where(mask, jnp.exp2(sm - jnp.max(sm, axis=-1, keepdims=True)), 0.0)
    l = jnp.sum(e, axis=-1, keepdims=True)
    return e, jnp.where(l > 0.0, l, 1.0)


def _lane_tile_max(mx, s):
    for c in range(s.shape[1] // LANES):
        mx = jnp.maximum(mx, s[:, c * LANES:(c + 1) * LANES])
    return mx


def _exp_accumulate(carry, s, m, v, lane_shift=0):
    ls, acc = carry
    p = jnp.exp2(s - m)
    tiles = [p[:, c * LANES:(c + 1) * LANES] for c in range(s.shape[1] // LANES)]
    for t in tiles:
        ls = ls + t
    if lane_shift:
        p = jnp.concatenate([pltpu.roll(t, lane_shift, axis=1) for t in tiles], axis=1)
    return ls, acc + _dot(p.astype(BF16), v)


def _split3(x):
    hi = x.astype(BF16)
    r = x - hi.astype(F32)
    mid = r.astype(BF16)
    lo = (r - mid.astype(F32)).astype(BF16)
    return hi, mid, lo


def _topk_mask(score, k, n_sel):
    st = score.T
    nv = -(-n_sel // 8)
    slabs = [st[8 * v:8 * v + 8, :] for v in range(nv)]
    sub = lax.broadcasted_iota(jnp.int32, (8, LANES), 0)
    ranks = [jnp.zeros((8, LANES), F32) for _ in range(nv)]
    for b2 in range(n_sel):
        row = jnp.broadcast_to(st[b2:b2 + 1, :], (8, LANES))
        for v in range(nv):
            if b2 < 8 * v:
                beats = row >= slabs[v]
            elif b2 >= 8 * v + 8:
                beats = row > slabs[v]
            else:
                beats = (row > slabs[v]) | ((row == slabs[v]) & (sub > b2 - 8 * v))
            ranks[v] = ranks[v] + jnp.where(beats, 1.0, 0.0)
    sel = [jnp.where((ranks[v] < k) & (sub + 8 * v < n_sel), 1.0, 0.0) for v in range(nv)]
    sel_t = jnp.concatenate(sel + [jnp.zeros((LANES - 8 * nv, LANES), F32)], axis=0)
    return sel_t.T


def _sel_scores(imp, tpos, n_sel):
    bidx = lax.broadcasted_iota(jnp.int32, imp.shape, 1)
    cur = jnp.right_shift(tpos, 6)
    valid = bidx <= cur
    forced = (bidx == 0) | (valid & (bidx > cur - N_LOCAL_BLOCKS))
    score = jnp.where(valid, jnp.where(forced, imp + FORCE_BONUS, imp), NEG_INF)
    return jnp.where(bidx < n_sel, score, -jnp.inf)


def _proj_kernel(x_ref, g_ref, cos_ref, sin_ref, w_ref, ws_ref, bf_ref,
                 qn_ref, kvc_ref, kvs_ref, kvw_ref, qf_ref, kvf_ref, gm_ref, sm_ref,
                 kvsb_ref, kvwb_ref, kvfb_ref, h_scr):
    j = pl.program_id(1)

    @pl.when(j == 0)
    def _():
        x = x_ref[...]
        y = x * lax.rsqrt(jnp.mean(x * x, axis=-1, keepdims=True) + RMS_EPS)
        h = (y * g_ref[...]).astype(BF16)
        h_scr[...] = h
        s = _dot(h, ws_ref[...])
        lane = lax.broadcasted_iota(jnp.int32, s.shape, 1)
        z = s + bf_ref[...]
        lf = jnp.minimum(z, 0.0) - jnp.log1p(jnp.exp(-jnp.abs(z)))
        sm_ref[...] = jnp.where(lane < N_GATE_COLS, jax.nn.sigmoid(s),
                                jnp.where(lane < LOGF_COL0 + N_FOX_HEADS, lf, 0.0))

    cos = cos_ref[...]
    sin = sin_ref[...]
    half_w = PROJ_TN // 2
    halves = [slice(0, half_w), slice(half_w, PROJ_TN)]

    def mm(cols):
        return _dot(h_scr[...], w_ref[:, cols])

    def rope2(a):
        return jnp.concatenate(
            [a[:, k * HEAD_DIM:(k + 1) * HEAD_DIM] * cos
             + pltpu.roll(a[:, k * HEAD_DIM:(k + 1) * HEAD_DIM], HEAD_DIM // 2, axis=1) * sin
             for k in range(half_w // HEAD_DIM)], axis=1)

    def kv_rows(ref, bref):
        assert half_w == N_NSA_GROUPS * HEAD_DIM
        for cols, is_key in zip(halves, (True, False)):
            a = mm(cols)
            a = rope2(a) if is_key else a
            ref[:, cols] = a
            if bref is not None:
                bref[:, cols] = a.astype(BF16)

    @pl.when(j < 2)
    def _():
        for cols in halves:
            qn_ref[:, cols] = (rope2(mm(cols)) * QK_SCALE).astype(BF16)

    @pl.when(j == 2)
    def _():
        kv_rows(kvc_ref, None)

    @pl.when(j == 3)
    def _():
        kv_rows(kvs_ref, kvsb_ref)

    @pl.when(j == 4)
    def _():
        kv_rows(kvw_ref, kvwb_ref)

    @pl.when((j >= 5) & (j < 7))
    def _():
        for cols in halves:
            qf_ref[:, cols] = (mm(cols) * QK_SCALE).astype(BF16)

    @pl.when((j >= 7) & (j < 11))
    def _():
        for cols in halves:
            a = mm(cols)
            kvf_ref[:, cols] = a
            kvfb_ref[:, cols] = a.astype(BF16)

    @pl.when(j >= 11)
    def _():
        for cols in halves:
            gm_ref[:, cols] = jax.nn.sigmoid(mm(cols))


def _project(x2, g, cos2, sin2, w_main, w_small, bf_row):
    n, d = x2.shape
    tm = min(PROJ_TM, n)
    n_j = w_main.shape[1] // PROJ_TN
    tn = PROJ_TN

    def clip(lo, hi):
        return lambda i, j: (i, jnp.clip(j - lo, 0, hi - lo))

    row = lambda i, j: (i, 0)
    out_shape = (
        jax.ShapeDtypeStruct((n, NSA_Q_W), BF16),
        jax.ShapeDtypeStruct((n, KV_W), F32),
        jax.ShapeDtypeStruct((n, KV_W), F32),
        jax.ShapeDtypeStruct((n, KV_W), F32),
        jax.ShapeDtypeStruct((n, FOX_W), BF16),
        jax.ShapeDtypeStruct((n, 2 * FOX_W), F32),
        jax.ShapeDtypeStruct((n, 2 * d), F32),
        jax.ShapeDtypeStruct((n, LANES), F32),
        jax.ShapeDtypeStruct((n, KV_W), BF16),
        jax.ShapeDtypeStruct((n, KV_W), BF16),
        jax.ShapeDtypeStruct((n, 2 * FOX_W), BF16),
    )
    out_specs = (
        pl.BlockSpec((tm, tn), clip(0, 1)),
        pl.BlockSpec((tm, tn), row),
        pl.BlockSpec((tm, tn), row),
        pl.BlockSpec((tm, tn), row),
        pl.BlockSpec((tm, tn), clip(5, 6)),
        pl.BlockSpec((tm, tn), clip(7, 10)),
        pl.BlockSpec((tm, tn), clip(11, 18)),
        pl.BlockSpec((tm, LANES), row),
        pl.BlockSpec((tm, tn), row),
        pl.BlockSpec((tm, tn), row),
        pl.BlockSpec((tm, tn), clip(7, 10)),
    )
    in_specs = [
        pl.BlockSpec((tm, d), row, pipeline_mode=pl.Buffered(1)),
        pl.BlockSpec((1, d), lambda i, j: (0, 0)),
        pl.BlockSpec((tm, LANES), row),
        pl.BlockSpec((tm, LANES), row),
        pl.BlockSpec((d, tn), lambda i, j: (0, j)),
        pl.BlockSpec((d, LANES), lambda i, j: (0, 0)),
        pl.BlockSpec((1, LANES), lambda i, j: (0, 0)),
    ]
    return pl.pallas_call(
        _proj_kernel,
        grid=(n // tm, n_j),
        in_specs=in_specs,
        out_specs=out_specs,
        out_shape=out_shape,
        scratch_shapes=[pltpu.VMEM((tm, d), BF16)],
        compiler_params=_cparams(("arbitrary", "arbitrary")),
        name="proj",
    )(x2, g, cos2, sin2, w_main, w_small, bf_row)


def _compress_tail(xc, w1, pe8, w2):
    n = xc.shape[0]
    part = _dot(xc, w1)
    pp = _dot(pe8.astype(BF16), w1)
    pe_term = pp[0:1, :HEAD_DIM] + pp[1:2, HEAD_DIM:]
    hid = pe_term + part[:, :HEAD_DIM] + pltpu.roll(part[:, HEAD_DIM:], n - 1, axis=0)
    return _dot(jax.nn.gelu(hid, approximate=True).astype(BF16), w2)


def _cmp_prompt_kernel(x_ref, w1_ref, pe_ref, w2_ref, o_ref):
    n = x_ref.shape[0] // CMP_STRIDE
    xc = jnp.concatenate([x_ref[pl.ds(i, n, stride=CMP_STRIDE), :] for i in range(CMP_STRIDE)],
                         axis=1).astype(BF16)
    o_ref[0, 0] = _compress_tail(xc, w1_ref[0], pe_ref[0], w2_ref[0]).astype(BF16)


def _compress_prompt(kvc, b_n, t_n, w1cat, pe8, w2):
    n = t_n // CMP_STRIDE
    return pl.pallas_call(
        _cmp_prompt_kernel,
        grid=(b_n, 2 * N_NSA_GROUPS),
        in_specs=[
            pl.BlockSpec((t_n, HEAD_DIM), lambda b, s: (b, s)),
            pl.BlockSpec((1, CMP_STRIDE * HEAD_DIM, 2 * HEAD_DIM), lambda b, s: (s // N_NSA_GROUPS, 0, 0)),
            pl.BlockSpec((1, 8, CMP_STRIDE * HEAD_DIM), lambda b, s: (s // N_NSA_GROUPS, 0, 0)),
            pl.BlockSpec((1, HEAD_DIM, HEAD_DIM), lambda b, s: (s // N_NSA_GROUPS, 0, 0)),
        ],
        out_specs=pl.BlockSpec((1, 1, n, HEAD_DIM), lambda b, s: (b, s, 0, 0)),
        out_shape=jax.ShapeDtypeStruct((b_n, 2 * N_NSA_GROUPS, n, HEAD_DIM), BF16),
        compiler_params=_cparams(("arbitrary", "arbitrary")),
        name="cmp_prompt",
    )(kvc, w1cat, pe8, w2)


def _fcum_kernel(x_ref, frow_ref, carry_scr):
    i = pl.program_id(1)

    @pl.when(i == 0)
    def _():
        carry_scr[...] = jnp.zeros_like(carry_scr)

    x = x_ref[...]
    tb = x.shape[0]
    r = lax.broadcasted_iota(jnp.int32, (tb, tb), 0)
    c = lax.broadcasted_iota(jnp.int32, (tb, tb), 1)
    tri = jnp.where(r >= c, 1.0, 0.0).astype(BF16)
    hi, mid, lo = _split3(x)
    cs = _dot(tri, hi) + _dot(tri, mid) + _dot(tri, lo) + carry_scr[0:1, :]
    carry_scr[...] = jnp.broadcast_to(cs[tb - 1:tb, :], carry_scr.shape)
    frow_ref[0] = cs.T[LOGF_COL0:LOGF_COL0 + N_FOX_HEADS, :] * LOG2E


def _fcum_prompt(sm, b_n, t_n):
    tb = 512
    nb = t_n // tb
    return pl.pallas_call(
        _fcum_kernel,
        grid=(b_n, nb),
        in_specs=[pl.BlockSpec((tb, LANES), lambda b, i: (b * nb + i, 0))],
        out_specs=pl.BlockSpec((1, N_FOX_HEADS, tb), lambda b, i: (b, 0, i)),
        out_shape=jax.ShapeDtypeStruct((b_n, N_FOX_HEADS, t_n), F32),
        scratch_shapes=[pltpu.VMEM((8, LANES), F32)],
        compiler_params=_cparams(("arbitrary", "arbitrary")),
        name="fcum_prompt",
    )(sm)


NSA_TQ = 128
NSA_NSUB = 2


def _tile_rows(a, reps):
    return jnp.concatenate([a] * reps, axis=0)


def _nsa_front_kernel(q_ref, sm_ref, ck_ref, kw_ref, cover_ref, selm_ref, ocw_ref, kwt_scr, *, n_cmp, n_sel):
    i = pl.program_id(1)
    tq, nsub = NSA_TQ, NSA_NSUB
    t0 = i * (tq * nsub)
    n_ck = ck_ref.shape[2]
    band = WINDOW + tq
    tpos = [t0 + u * tq + lax.broadcasted_iota(jnp.int32, (tq, 1), 0) for u in range(nsub)]

    @pl.when(i == 0)
    def _():
        for g in range(N_NSA_GROUPS):
            _transpose_rows(kw_ref, kwt_scr.at[g], slice(g * HEAD_DIM, (g + 1) * HEAD_DIM))

    for g in range(N_NSA_GROUPS):
        ck = ck_ref[0, g]
        cv = ck_ref[0, N_NSA_GROUPS + g]
        o_cs, o_ws, selms = [], [], []
        for u in range(nsub):
            q = jnp.concatenate([q_ref[u * tq:(u + 1) * tq, (g * NSA_HPG + h) * HEAD_DIM:(g * NSA_HPG + h + 1) * HEAD_DIM]
                                 for h in range(NSA_HPG)], axis=0)
            s_c = _dot_nt(q, ck)
            cidx = lax.broadcasted_iota(jnp.int32, (tq, n_ck), 1)
            mc = jnp.where((cidx * CMP_STRIDE + CMP_BLOCK - 1 <= tpos[u]) & (cidx < n_cmp), 1.0, 0.0)
            e_c, l_c = _masked_exp(s_c, _tile_rows(mc, NSA_HPG) > 0.5)
            e_cb = e_c.astype(BF16)
            imp4 = _dot(e_cb, cover_ref[...]) / l_c
            imp = imp4[0:tq] + imp4[tq:2 * tq] + imp4[2 * tq:3 * tq] + imp4[3 * tq:4 * tq]
            selms.append(_topk_mask(_sel_scores(imp, tpos[u], n_sel), min(SEL_TOPK, n_sel), n_sel).astype(BF16))
            o_cs.append(_dot(e_cb, cv) / l_c)
            w0 = pl.multiple_of(jnp.maximum(t0 + u * tq - WINDOW, 0), tq)
            vw = kw_ref[pl.ds(w0, band), (N_NSA_GROUPS + g) * HEAD_DIM:(N_NSA_GROUPS + g + 1) * HEAD_DIM]
            s_w = _dot(q, kwt_scr[g, :, pl.ds(w0, band)])
            wpos = w0 + lax.broadcasted_iota(jnp.int32, (tq, band), 1)
            bw = jnp.where((wpos <= tpos[u]) & (wpos > tpos[u] - WINDOW), 0.0, NEG_INF)
            s_w = s_w + _tile_rows(bw, NSA_HPG)
            e_w = jnp.exp2(s_w - jnp.max(s_w, axis=-1, keepdims=True))
            o_ws.append(_dot(e_w.astype(BF16), vw) / jnp.sum(e_w, axis=-1, keepdims=True))

        for u in range(nsub):
            rows_u = slice(u * tq, (u + 1) * tq)
            selm_ref[rows_u, g * LANES:(g + 1) * LANES] = selms[u]
            gates = sm_ref[rows_u, :]
            for h in range(NSA_HPG):
                hh = g * NSA_HPG + h
                rows = slice(h * tq, (h + 1) * tq)
                ocw_ref[rows_u, hh * HEAD_DIM:(hh + 1) * HEAD_DIM] = (
                    gates[:, 3 * hh:3 * hh + 1] * o_cs[u][rows] + gates[:, 3 * hh + 2:3 * hh + 3] * o_ws[u][rows])


def _cover_matrix(n_rows, n_cmp, n_sel):
    c = np.arange(n_rows)[:, None] * CMP_STRIDE
    b = np.arange(LANES)[None, :] * SEL_BLOCK
    m = (c < b + SEL_BLOCK) & (c + CMP_BLOCK > b) & (np.arange(n_rows)[:, None] < n_cmp) & (np.arange(LANES)[None, :] < n_sel)
    return jnp.asarray(m.astype(np.float32), dtype=BF16)


def _expand_matrix(n_keys):
    m = (np.arange(n_keys)[None, :] // SEL_BLOCK) == np.arange(LANES)[:, None]
    return jnp.asarray(m.astype(np.float32), dtype=BF16)


SEL_MASK_BIG = 2.0 ** 100


def _nsa_sel_kernel(q_ref, k_ref, v_ref, selm_ref, sm_ref, ocw_ref, expand_ref, o_ref, s_scr, kt_scr):
    hh = pl.program_id(1) * NSA_HPG + pl.program_id(2)
    blk = FOX_BLK

    @pl.when(pl.program_id(2) == 0)
    def _():
        _transpose_rows(k_ref, kt_scr.at[0:HEAD_DIM])
        kt_scr[HEAD_DIM:, :] = expand_ref[...]

    causal = lax.broadcasted_iota(jnp.int32, (blk, blk), 1) <= lax.broadcasted_iota(jnp.int32, (blk, 1), 0)
    zeros = jnp.zeros((blk, LANES), F32)
    lane = lax.broadcasted_iota(jnp.int32, (blk, LANES), 1)
    for qb in range(q_ref.shape[0] // blk):
        rows = slice(qb * blk, (qb + 1) * blk)
        unsel = ((selm_ref[rows, :].astype(F32) - 1.0) * SEL_MASK_BIG).astype(BF16)
        q = jnp.concatenate([q_ref[rows, :], unsel], axis=1)
        mx = jnp.full((blk, LANES), -jnp.inf, F32)
        for kt in range(qb + 1):
            keys = slice(kt * blk, (kt + 1) * blk)
            s = _dot(q, kt_scr[:, keys])
            if kt == qb:
                s = jnp.where(causal, s, NEG_INF)
            s_scr[:, keys] = s
            mx = _lane_tile_max(mx, s)
        m = jnp.max(mx, axis=-1, keepdims=True)
        carry = (zeros, zeros)
        for kt in range(qb + 1):
            keys = slice(kt * blk, (kt + 1) * blk)
            carry = _exp_accumulate(carry, s_scr[:, keys], m, v_ref[keys, :])
        ls, acc = carry
        gate = jnp.sum(jnp.where(lane == 3 * hh + 1, sm_ref[rows, :], 0.0), axis=-1, keepdims=True)
        out = ocw_ref[rows, :] + gate * (acc / jnp.sum(ls, axis=-1, keepdims=True))
        o_ref[rows, :] = out.astype(BF16)


def _nsa_prompt(qn, sm, ckv, kvs_b, kvw_b, b_n, t_n):
    blk = NSA_TQ * NSA_NSUB
    nq = t_n // blk
    n_cmp = t_n // CMP_STRIDE - CMP_RATIO + 1
    n_sel = t_n // SEL_BLOCK
    n_ck = ckv.shape[2]
    n_g = N_NSA_GROUPS
    cover = _cover_matrix(n_ck, n_cmp, n_sel)
    expand = _expand_matrix(t_n)
    selm, ocw = pl.pallas_call(
        functools.partial(_nsa_front_kernel, n_cmp=n_cmp, n_sel=n_sel),
        grid=(b_n, nq),
        in_specs=[
            pl.BlockSpec((blk, NSA_Q_W), lambda b, i: (b * nq + i, 0)),
            pl.BlockSpec((blk, LANES), lambda b, i: (b * nq + i, 0)),
            pl.BlockSpec((1, 2 * n_g, n_ck, HEAD_DIM), lambda b, i: (b, 0, 0, 0)),
            pl.BlockSpec((t_n, KV_W), lambda b, i: (b, 0)),
            pl.BlockSpec((n_ck, LANES), lambda b, i: (0, 0)),
        ],
        out_specs=(pl.BlockSpec((blk, n_g * LANES), lambda b, i: (b * nq + i, 0)),
                   pl.BlockSpec((blk, NSA_Q_W), lambda b, i: (b * nq + i, 0))),
        out_shape=(jax.ShapeDtypeStruct((b_n * t_n, n_g * LANES), BF16),
                   jax.ShapeDtypeStruct((b_n * t_n, NSA_Q_W), F32)),
        scratch_shapes=[pltpu.VMEM((n_g, HEAD_DIM, t_n), BF16)],
        compiler_params=_cparams(("arbitrary", "arbitrary")),
        name="nsa_front",
    )(qn, sm, ckv, kvw_b, cover)

    hpg = NSA_HPG
    return pl.pallas_call(
        _nsa_sel_kernel,
        grid=(b_n, n_g, hpg),
        in_specs=[
            pl.BlockSpec((t_n, HEAD_DIM), lambda b, g, h: (b, g * hpg + h)),
            pl.BlockSpec((t_n, HEAD_DIM), lambda b, g, h: (b, g)),
            pl.BlockSpec((t_n, HEAD_DIM), lambda b, g, h: (b, n_g + g)),
            pl.BlockSpec((t_n, LANES), lambda b, g, h: (b, g)),
            pl.BlockSpec((t_n, LANES), lambda b, g, h: (b, 0)),
            pl.BlockSpec((t_n, HEAD_DIM), lambda b, g, h: (b, g * hpg + h)),
            pl.BlockSpec((LANES, t_n), lambda b, g, h: (0, 0)),
        ],
        out_specs=pl.BlockSpec((t_n, HEAD_DIM), lambda b, g, h: (b, g * hpg + h)),
        out_shape=jax.ShapeDtypeStruct((b_n * t_n, NSA_Q_W), BF16),
        scratch_shapes=[pltpu.VMEM((FOX_BLK, t_n), F32), pltpu.VMEM((2 * HEAD_DIM, t_n), BF16)],
        compiler_params=_cparams(("arbitrary", "arbitrary", "arbitrary")),
        name="nsa_sel",
    )(qn, kvs_b, kvs_b, selm, sm, ocw, expand)


FOX_BLK = 512


def _fox_prompt_kernel(q_ref, k_ref, v_ref, frow_ref, o_ref, s_scr, kt_scr):
    h = pl.program_id(1)
    blk = FOX_BLK
    assert q_ref.shape[0] % blk == 0
    _transpose_rows(k_ref, kt_scr)

    causal = lax.broadcasted_iota(jnp.int32, (blk, blk), 1) <= lax.broadcasted_iota(jnp.int32, (blk, 1), 0)
    zeros = jnp.zeros((blk, LANES), F32)
    for qb in range(q_ref.shape[0] // blk):
        q = q_ref[qb * blk:(qb + 1) * blk, :]
        mx = jnp.full((blk, LANES), -jnp.inf, F32)
        for kt in range(qb + 1):
            keys = slice(kt * blk, (kt + 1) * blk)
            s = _dot(q, kt_scr[:, keys]) - frow_ref[0, pl.ds(h, 1), keys]
            if kt == qb:
                s = jnp.where(causal, s, NEG_INF)
            s_scr[:, keys] = s
            mx = _lane_tile_max(mx, s)
        m = jnp.max(mx, axis=-1, keepdims=True)
        carry = (zeros, zeros)
        for kt in range(qb + 1):
            keys = slice(kt * blk, (kt + 1) * blk)
            carry = _exp_accumulate(carry, s_scr[:, keys], m, v_ref[keys, :])
        ls, acc = carry
        o_ref[qb * blk:(qb + 1) * blk, :] = (acc / jnp.sum(ls, axis=-1, keepdims=True)).astype(BF16)


def _fox_prompt(qf, kvf_b, frow, b_n, t_n):
    nh = N_FOX_HEADS
    return pl.pallas_call(
        _fox_prompt_kernel,
        grid=(b_n, nh),
        in_specs=[
            pl.BlockSpec((t_n, HEAD_DIM), lambda b, h: (b, h)),
            pl.BlockSpec((t_n, HEAD_DIM), lambda b, h: (b, h)),
            pl.BlockSpec((t_n, HEAD_DIM), lambda b, h: (b, nh + h)),
            pl.BlockSpec((1, nh, t_n), lambda b, h: (b, 0, 0)),
        ],
        out_specs=pl.BlockSpec((t_n, HEAD_DIM), lambda b, h: (b, h)),
        out_shape=jax.ShapeDtypeStruct((b_n * t_n, FOX_W), BF16),
        scratch_shapes=[pltpu.VMEM((FOX_BLK, t_n), F32), pltpu.VMEM((HEAD_DIM, t_n), BF16)],
        compiler_params=_cparams(("arbitrary", "arbitrary")),
        name="fox_prompt",
    )(qf, kvf_b, kvf_b, frow)


FOX_SAMPLE_CH = 512
TOK_PAD = 8
NEW_PAD = 128
NSA_SAMPLE_SUB = 1


def _nsa_sample_kernel(pt_ref, *refs, n_pages, page, n_buf, n_tok, n_sub):
    del pt_ref
    per_seq = 2 * n_pages + 1
    kvs_new_ref, kvw_new_ref, q_ref, sm_ref = refs[n_sub * per_seq:n_sub * per_seq + 4]
    assert n_sub == 1
    for sq in range(n_sub):
        seq_refs = refs[sq * per_seq:(sq + 1) * per_seq]
        _nsa_sample_one(seq_refs[:n_pages], seq_refs[n_pages:2 * n_pages], seq_refs[2 * n_pages],
                        kvs_new_ref[sq], kvw_new_ref[sq], q_ref.at[sq], sm_ref[sq],
                        *refs[n_sub * per_seq + 4:-2], refs[-2].at[sq], refs[-1],
                        n_pages=n_pages, page=page, n_buf=n_buf, n_tok=n_tok)


def _nsa_sample_one(cmp_pages, sel_pages, win_ref, kvs_new, kvw_new, q_ref, gates, w1k_ref, w1v_ref, pek_ref, pev_ref,
                    w2k_ref, w2v_ref, cover_ref, expand_ref, o_ref, wino_ref, *, n_pages, page, n_buf, n_tok):
    past = n_pages * page
    n_slab = 2 * N_NSA_GROUPS

    shift = n_tok * n_slab
    keep = n_buf * n_slab - shift
    assert n_buf == WINDOW and shift % 8 == 0
    wino_ref[0:keep, :] = win_ref[shift:, :]
    for t in range(n_tok):
        for s in range(n_slab):
            wino_ref[keep + t * n_slab + s:keep + t * n_slab + s + 1, :] = kvw_new[t:t + 1, s * HEAD_DIM:(s + 1) * HEAD_DIM]
    chunks_per_page = page // CMP_STRIDE
    n_chunk = n_pages * chunks_per_page
    n_cmp = (past + n_tok + CMP_STRIDE - 1) // CMP_STRIDE - CMP_RATIO + 1
    n_sel = (past + n_tok + SEL_BLOCK - 1) // SEL_BLOCK
    tp = TOK_PAD
    tpos = past + lax.broadcasted_iota(jnp.int32, (tp, 1), 0)

    rows_pc = CMP_STRIDE * n_slab
    swapped = [jnp.swapaxes(cmp_pages[p][...].reshape(chunks_per_page, rows_pc, HEAD_DIM), 0, 1)
               for p in range(n_pages)]

    def compress(slab, w1_ref, pe_ref, w2_ref):
        cols = []
        for i in range(CMP_STRIDE):
            cols.append(jnp.concatenate([swapped[p][i * n_slab + slab] for p in range(n_pages)], axis=0))
        xc = jnp.concatenate(cols, axis=1).astype(BF16)
        return _compress_tail(xc, w1_ref[...], pe_ref[...], w2_ref[...]).astype(BF16)

    qs, o_cs, scores = [], [], []
    for g in range(N_NSA_GROUPS):
        q = jnp.concatenate([q_ref[:, (g * NSA_HPG + h) * HEAD_DIM:(g * NSA_HPG + h + 1) * HEAD_DIM]
                             for h in range(NSA_HPG)], axis=0)
        ck = compress(g, w1k_ref, pek_ref, w2k_ref)
        cv = compress(N_NSA_GROUPS + g, w1v_ref, pev_ref, w2v_ref)
        s_c = _dot_nt(q, ck)
        cidx = lax.broadcasted_iota(jnp.int32, (tp, n_chunk), 1)
        mc = jnp.where((cidx * CMP_STRIDE + CMP_BLOCK - 1 <= tpos) & (cidx < n_cmp), 1.0, 0.0)
        p_c = _masked_softmax(s_c, _tile_rows(mc, NSA_HPG) > 0.5)
        p_cb = p_c.astype(BF16)
        imp4 = _dot(p_cb, cover_ref[...])
        imp = imp4[0:tp] + imp4[tp:2 * tp] + imp4[2 * tp:3 * tp] + imp4[3 * tp:4 * tp]
        qs.append(q)
        o_cs.append(_dot(p_cb, cv))
        scores.append(_sel_scores(imp, tpos, n_sel))

    score_all = jnp.concatenate(scores + [jnp.zeros((LANES - N_NSA_GROUPS * tp, LANES), F32)], axis=0)
    selm_all = _topk_mask(score_all, min(SEL_TOPK, n_sel), n_sel)

    def sel_rows(slab):
        return jnp.concatenate([sel_pages[p][pl.ds(slab, page, stride=n_slab), :] for p in range(n_pages)], axis=0)

    def with_new(cached, new):
        pad = jnp.zeros((NEW_PAD - new.shape[0], new.shape[1]), new.dtype)
        return jnp.concatenate([cached, new, pad], axis=0).astype(BF16)

    for g in range(N_NSA_GROUPS):
        q, o_c = qs[g], o_cs[g]
        selm = selm_all[g * tp:(g + 1) * tp].astype(BF16)
        kc = slice(g * HEAD_DIM, (g + 1) * HEAD_DIM)
        vc = slice((N_NSA_GROUPS + g) * HEAD_DIM, (N_NSA_GROUPS + g + 1) * HEAD_DIM)

        n_keys = past + NEW_PAD
        s_s = _dot_nt(q, with_new(sel_rows(g), kvs_new[:, kc]))
        selx = _dot(selm, expand_ref[...])
        kpos = lax.broadcasted_iota(jnp.int32, (tp, n_keys), 1)
        ms = jnp.where((selx > 0.5) & (kpos <= tpos) & (kpos < past + n_tok), 1.0, 0.0)
        p_s = _masked_softmax(s_s, _tile_rows(ms, NSA_HPG) > 0.5)
        o_s = _dot(p_s.astype(BF16), with_new(sel_rows(N_NSA_GROUPS + g), kvs_new[:, vc]))

        s_w = _dot_nt(q, with_new(win_ref[pl.ds(g, n_buf, stride=n_slab), :], kvw_new[:, kc]))
        wpos = past - n_buf + lax.broadcasted_iota(jnp.int32, (tp, n_buf + NEW_PAD), 1)
        mw = jnp.where((wpos <= tpos) & (wpos > tpos - WINDOW) & (wpos < past + n_tok), 1.0, 0.0)
        p_w = _masked_softmax(s_w, _tile_rows(mw, NSA_HPG) > 0.5)
        o_w = _dot(p_w.astype(BF16), with_new(win_ref[pl.ds(N_NSA_GROUPS + g, n_buf, stride=n_slab), :],
                                              kvw_new[:, vc]))

        for h in range(NSA_HPG):
            hh = g * NSA_HPG + h
            rows = slice(h * tp, (h + 1) * tp)
            out = (gates[:, 3 * hh:3 * hh + 1] * o_c[rows] + gates[:, 3 * hh + 1:3 * hh + 2] * o_s[rows]
                   + gates[:, 3 * hh + 2:3 * hh + 3] * o_w[rows])
            o_ref[:, hh * HEAD_DIM:(hh + 1) * HEAD_DIM] = out.astype(BF16)


def _pad_tokens(a, n_seq, n_tok):
    a = a.reshape(n_seq, n_tok, a.shape[-1])
    return jnp.pad(a, ((0, 0), (0, TOK_PAD - n_tok), (0, 0)))


def _nsa_sample(page_table, cache_cmp, cache_sel, win_buf, kvs_new, kvw_new, qn, sm, cmp_w, n_tok):
    n_seq, n_pages = page_table.shape
    page = cache_cmp.shape[1]
    n_slab = 2 * N_NSA_GROUPS
    n_buf = win_buf.shape[1]
    past = n_pages * page
    cmp2 = cache_cmp.reshape(-1, HEAD_DIM)
    sel2 = cache_sel.reshape(-1, HEAD_DIM)
    win2 = win_buf.reshape(-1, HEAD_DIM)
    n_chunk = past // CMP_STRIDE
    n_cmp = (past + n_tok + CMP_STRIDE - 1) // CMP_STRIDE - CMP_RATIO + 1
    n_sel = (past + n_tok + SEL_BLOCK - 1) // SEL_BLOCK
    cover = _cover_matrix(n_chunk, min(n_cmp, n_chunk), n_sel)
    expand = _expand_matrix(past + NEW_PAD)
    w1k, pek, w2k, w1v, pev, w2v = cmp_w

    n_sub = NSA_SAMPLE_SUB
    assert n_seq % n_sub == 0

    def page_spec(sq, p):
        return pl.BlockSpec((page * n_slab, HEAD_DIM), lambda b, pt, sq=sq, p=p: (pt[b * n_sub + sq, p], 0))

    const2 = lambda b, pt: (0, 0)
    seq3 = lambda b, pt: (b, 0, 0)
    per_seq = []
    for sq in range(n_sub):
        per_seq += ([page_spec(sq, p) for p in range(n_pages)] + [page_spec(sq, p) for p in range(n_pages)]
                    + [pl.BlockSpec((n_buf * n_slab, HEAD_DIM), lambda b, pt, sq=sq: (b * n_sub + sq, 0))])
    in_specs = (per_seq + [
        pl.BlockSpec((n_sub, TOK_PAD, KV_W), seq3),
        pl.BlockSpec((n_sub, TOK_PAD, KV_W), seq3),
        pl.BlockSpec((n_sub, TOK_PAD, NSA_Q_W), seq3),
        pl.BlockSpec((n_sub, TOK_PAD, LANES), seq3),
        pl.BlockSpec(w1k.shape, const2),
        pl.BlockSpec(w1v.shape, const2),
        pl.BlockSpec(pek.shape, const2),
        pl.BlockSpec(pev.shape, const2),
        pl.BlockSpec(w2k.shape, const2),
        pl.BlockSpec(w2v.shape, const2),
        pl.BlockSpec(cover.shape, const2),
        pl.BlockSpec(expand.shape, const2),
    ])
    kern = functools.partial(_nsa_sample_kernel, n_pages=n_pages, page=page, n_buf=n_buf, n_tok=n_tok, n_sub=n_sub)
    grid_spec = pltpu.PrefetchScalarGridSpec(
        num_scalar_prefetch=1, grid=(n_seq // n_sub,), in_specs=in_specs,
        out_specs=(pl.BlockSpec((n_sub, TOK_PAD, NSA_Q_W), seq3),
                   pl.BlockSpec((n_buf * n_slab, HEAD_DIM), lambda b, pt: (b, 0))))
    return pl.pallas_call(
        kern,
        grid_spec=grid_spec,
        out_shape=(jax.ShapeDtypeStruct((n_seq, TOK_PAD, NSA_Q_W), BF16),
                   jax.ShapeDtypeStruct(win2.shape, F32)),
        compiler_params=_cparams(("arbitrary",)),
        name="nsa_sample",
    )(page_table, *(([cmp2] * n_pages + [sel2] * n_pages + [win2]) * n_sub),
      _pad_tokens(kvs_new, n_seq, n_tok), _pad_tokens(kvw_new, n_seq, n_tok),
      _pad_tokens(qn, n_seq, n_tok), _pad_tokens(sm, n_seq, n_tok),
      w1k, w1v, pek, pev, w2k, w2v, cover, expand)


def _fox_sample_kernel(pt_ref, *refs, n_pages, page, n_tok):
    kv_pages = refs[:n_pages]
    lf_pages = refs[n_pages:2 * n_pages]
    k_new_ref, v_new_ref, q_ref, lfn_ref, o_ref, mask_scr, s_scr, kb_scr = refs[2 * n_pages:]
    del pt_ref
    nh = N_FOX_HEADS
    rows_pp = 2 * nh
    page_rows = page * rows_pp
    n_chunk = page_rows // LANES
    n_q = nh * n_tok
    assert rows_pp == 16 and LANES % rows_pp == 0 and n_q <= LANES

    @pl.when(pl.program_id(0) == 0)
    def _():
        qrow = lax.broadcasted_iota(jnp.int32, mask_scr.shape, 0)
        lane = lax.broadcasted_iota(jnp.int32, mask_scr.shape, 1)
        mask_scr[...] = jnp.where((lane & (rows_pp - 1)) == (qrow & (nh - 1)), 0.0, NEG_INF)

    x = jnp.concatenate([lf_pages[p][0] for p in range(n_pages)], axis=0)
    n_r = x.shape[0]
    la = lax.broadcasted_iota(jnp.int32, (LANES, LANES), 0)
    lb = lax.broadcasted_iota(jnp.int32, (LANES, LANES), 1)
    same = (la & (rows_pp - 1)) == (lb & (rows_pp - 1))
    u_in = jnp.where(same & (jnp.right_shift(la, 4) <= jnp.right_shift(lb, 4)), 1.0, 0.0).astype(BF16)
    u_all = jnp.where(same, 1.0, 0.0).astype(BF16)
    xh, xm, xl = _split3(x)
    within = _dot(xh, u_in) + _dot(xm, u_in) + _dot(xl, u_in)
    tot = _dot(xh, u_all) + _dot(xm, u_all) + _dot(xl, u_all)
    ra = lax.broadcasted_iota(jnp.int32, (n_r, n_r), 0)
    rb = lax.broadcasted_iota(jnp.int32, (n_r, n_r), 1)
    before = jnp.where(rb < ra, 1.0, 0.0).astype(BF16)
    th, tm_, tl = _split3(tot)
    offs = _dot(before, th) + _dot(before, tm_) + _dot(before, tl)
    f_end = offs[n_r - 1:n_r, :] + tot[n_r - 1:n_r, :]
    bias = (f_end - (within + offs)) * LOG2E

    q_all = q_ref[0]

    ch = mask_scr.shape[1]
    lt = ch // LANES
    steps = [(p, c) for p in range(n_pages) for c in range(page_rows // ch)]
    mx = jnp.full((n_q, LANES), -jnp.inf, F32)
    for p, c in steps:
        k_b = kv_pages[p][pl.ds(c * ch, ch), :].astype(BF16)
        kb_scr[p * page_rows + c * ch:p * page_rows + (c + 1) * ch, :] = k_b
        r0 = p * n_chunk + c * lt
        brow = jnp.concatenate([bias[r0 + i:r0 + i + 1, :] for i in range(lt)], axis=1)
        s = _dot_nt(q_all, k_b) + brow + mask_scr[...]
        s_scr[:, p * page_rows + c * ch:p * page_rows + (c + 1) * ch] = s
        mx = _lane_tile_max(mx, s)

    pad = jnp.zeros((LANES - n_q, HEAD_DIM), F32)
    k_new = jnp.concatenate([k_new_ref[0], pad], axis=0).astype(BF16)
    v_new = jnp.concatenate([v_new_ref[0], pad], axis=0).astype(BF16)
    g_in = jnp.where(((la & (nh - 1)) == (lb & (nh - 1))) & (la <= lb), 1.0, 0.0).astype(BF16)
    nh_, nm_, nl_ = _split3(lfn_ref[0])
    c_new = (_dot(nh_, g_in) + _dot(nm_, g_in) + _dot(nl_, g_in))[0:1, :] * LOG2E
    qrow = lax.broadcasted_iota(jnp.int32, (n_q, LANES), 0)
    lane = lax.broadcasted_iota(jnp.int32, (n_q, LANES), 1)
    ok = ((lane & (nh - 1)) == (qrow & (nh - 1))) & (lane <= qrow)
    s_new = jnp.where(ok, _dot_nt(q_all, k_new) - c_new, NEG_INF)
    m = jnp.max(jnp.maximum(mx, s_new), axis=-1, keepdims=True)

    carry = _exp_accumulate((jnp.zeros((n_q, LANES), F32), jnp.zeros((n_q, HEAD_DIM), F32)), s_new, m, v_new)
    for p, c in steps:
        rows = slice(p * page_rows + c * ch, p * page_rows + (c + 1) * ch)
        carry = _exp_accumulate(carry, s_scr[:, rows], m, kb_scr[rows, :], lane_shift=nh)
    ls, acc = carry
    o_ref[0] = (acc / jnp.sum(ls, axis=-1, keepdims=True)).astype(BF16)


def _fox_sample(page_table, cache_fox, cache_logf, kvf_new, qf, sm, n_tok):
    n_seq, n_pages = page_table.shape
    page = cache_fox.shape[1]
    nh = N_FOX_HEADS
    kv2 = cache_fox.reshape(-1, HEAD_DIM)
    rows_pp = 2 * nh
    n_chunk = page * rows_pp // LANES
    lf_c = jnp.pad(cache_logf, ((0, 0), (0, 0), (0, rows_pp - nh))).reshape(-1, n_chunk, LANES)
    n_q = n_tok * nh
    lfn = sm[:, LOGF_COL0:LOGF_COL0 + nh].reshape(n_seq, 1, n_q)
    lfn = jnp.pad(lfn, ((0, 0), (0, 7), (0, LANES - n_q)))
    kv_new = kvf_new.reshape(n_seq, n_tok, 2, nh, HEAD_DIM)
    k_new = kv_new[:, :, 0].reshape(n_seq, n_q, HEAD_DIM)
    v_new = kv_new[:, :, 1].reshape(n_seq, n_q, HEAD_DIM)
    q3 = qf.reshape(n_seq, n_q, HEAD_DIM)

    seq3 = lambda b, pt: (b, 0, 0)
    in_specs = ([pl.BlockSpec((page * rows_pp, HEAD_DIM), lambda b, pt, p=p: (pt[b, p], 0)) for p in range(n_pages)]
                + [pl.BlockSpec((1, n_chunk, LANES), lambda b, pt, p=p: (pt[b, p], 0, 0)) for p in range(n_pages)]
                + [pl.BlockSpec((1, n_q, HEAD_DIM), seq3),
                   pl.BlockSpec((1, n_q, HEAD_DIM), seq3),
                   pl.BlockSpec((1, n_q, HEAD_DIM), seq3),
                   pl.BlockSpec((1, 8, LANES), seq3)])
    kern = functools.partial(_fox_sample_kernel, n_pages=n_pages, page=page, n_tok=n_tok)
    n_rows = n_pages * page * rows_pp
    grid_spec = pltpu.PrefetchScalarGridSpec(
        num_scalar_prefetch=1, grid=(n_seq,), in_specs=in_specs,
        out_specs=pl.BlockSpec((1, n_q, HEAD_DIM), seq3),
        scratch_shapes=[pltpu.VMEM((n_q, FOX_SAMPLE_CH), F32),
                        pltpu.VMEM((n_q, n_rows), F32),
                        pltpu.VMEM((n_rows, HEAD_DIM), BF16)])
    out = pl.pallas_call(
        kern,
        grid_spec=grid_spec,
        out_shape=jax.ShapeDtypeStruct((n_seq, n_q, HEAD_DIM), BF16),
        compiler_params=_cparams(("arbitrary",)),
        name="fox_sample",
    )(page_table, *([kv2] * n_pages), *([lf_c] * n_pages), k_new, v_new, q3, lfn)
    return out.reshape(n_seq * n_tok, nh * HEAD_DIM)


def _postmix_kernel(on_ref, of_ref, gm0_ref, gm1_ref, x_ref, wn_ref, wf_ref, wo_ref, g_ref, y_ref):
    a = _dot(on_ref[...], wn_ref[...])
    b = _dot(of_ref[...], wf_ref[...])
    merged = gm0_ref[...] * a + gm1_ref[...] * b
    z = _dot(merged.astype(BF16), wo_ref[...])
    y_ref[...] = x_ref[...] + _rms(z, g_ref[...])


def _postmix(o_n, o_f, gm, x2, wn, wf, wo, g):
    n, d = x2.shape
    tm = 256
    row = lambda i: (i, 0)
    const = lambda i: (0, 0)
    return pl.pallas_call(
        _postmix_kernel,
        grid=(n // tm,),
        in_specs=[
            pl.BlockSpec((tm, NSA_Q_W), row),
            pl.BlockSpec((tm, FOX_W), row),
            pl.BlockSpec((tm, d), lambda i: (i, 0)),
            pl.BlockSpec((tm, d), lambda i: (i, 1)),
            pl.BlockSpec((tm, d), row),
            pl.BlockSpec(wn.shape, const),
            pl.BlockSpec(wf.shape, const),
            pl.BlockSpec(wo.shape, const),
            pl.BlockSpec((1, d), const),
        ],
        out_specs=pl.BlockSpec((tm, d), row),
        out_shape=jax.ShapeDtypeStruct((n, d), F32),
        compiler_params=_cparams(("arbitrary",)),
        name="postmix",
    )(o_n, o_f, gm, gm, x2, wn, wf, wo, g)


FFN_TM = 1024
FFN_TF = 512
HALO = 16


def _ffn_kernel(*refs, seq_tiles, n_tok):
    if n_tok is None:
        (x_ref, xh_ref, g_ref, wg_ref, wu_ref, wd_ref, wc_ref, bc_ref, gp_ref,
         y_ref, gt_ref, h_scr, hh_scr, acc_scr) = refs
    else:
        (x_ref, s0_ref, s1_ref, g_ref, wg_ref, wu_ref, wd_ref, wc_ref, bc_ref, gp_ref,
         y_ref, gt_ref, h_scr, acc_scr) = refs
    i = pl.program_id(0)
    f = pl.program_id(1)
    tm = x_ref.shape[0]

    @pl.when(f == 0)
    def _():
        h_scr[...] = _rms(x_ref[...], g_ref[...]).astype(BF16)
        acc_scr[...] = jnp.zeros_like(acc_scr)
        if n_tok is None:
            hh_scr[...] = _rms(xh_ref[...], g_ref[...]).astype(BF16)

    h2 = h_scr[...]
    gate = _dot(h2, wg_ref[...])
    up = _dot(h2, wu_ref[...])
    row = lax.broadcasted_iota(jnp.int32, gate.shape, 0)
    r1 = pltpu.roll(gate, 1, axis=0)
    r2 = pltpu.roll(gate, 2, axis=0)
    if n_tok is None:
        first = (i % seq_tiles) == 0
        gh = jnp.where(first, 0.0, _dot(hh_scr[...], wg_ref[...]))
        p1 = gh[HALO - 1:HALO, :]
        p2 = gh[HALO - 2:HALO - 1, :]
        g1 = jnp.where(row == 0, p1, r1)
        g2 = jnp.where(row == 0, p2, jnp.where(row == 1, p1, r2))
        gt_ref[...] = gate[tm - 8:tm, :]
    else:
        assert n_tok & (n_tok - 1) == 0
        rt = row & (n_tok - 1)
        g1 = jnp.where(rt == 0, s1_ref[...], r1)
        g2 = jnp.where(rt == 0, s0_ref[...], jnp.where(rt == 1, s1_ref[...], r2))
        gt_ref[...] = gate
    wc = wc_ref[...]
    gc = bc_ref[...] + wc[0:1, :] * g2 + wc[1:2, :] * g1 + wc[2:3, :] * gate
    act = jax.nn.gelu(gc, approximate=True) * up
    acc_scr[...] += _dot(act.astype(BF16), wd_ref[...])

    @pl.when(f == pl.num_programs(1) - 1)
    def _():
        y_ref[...] = x_ref[...] + _rms(acc_scr[...], gp_ref[...])


def _ffn(x2, g_pre, w_up_b, w_down_b, w_conv, b_conv, g_post, *, seq_len=None, state=None):
    n, d = x2.shape
    d_ff = w_down_b.shape[0]
    tf = FFN_TF
    nf = d_ff // tf
    tm = min(FFN_TM, n)
    common_w = [
        pl.BlockSpec((1, d), lambda i, f: (0, 0)),
        pl.BlockSpec((d, tf), lambda i, f: (0, f)),
        pl.BlockSpec((d, tf), lambda i, f: (0, nf + f)),
        pl.BlockSpec((tf, d), lambda i, f: (f, 0)),
        pl.BlockSpec((CONV_WIDTH, tf), lambda i, f: (0, f)),
        pl.BlockSpec((1, tf), lambda i, f: (0, f)),
        pl.BlockSpec((1, d), lambda i, f: (0, 0)),
    ]
    w_args = (g_pre, w_up_b, w_up_b, w_down_b, w_conv, b_conv, g_post)
    row = lambda i, f: (i, 0)
    once = pl.Buffered(1)
    if state is None:
        seq_tiles = seq_len // tm
        halo_blocks = tm // HALO
        in_specs = [pl.BlockSpec((tm, d), row, pipeline_mode=once),
                    pl.BlockSpec((HALO, d), lambda i, f: (jnp.maximum(i * halo_blocks - 1, 0), 0))] + common_w
        args = (x2, x2) + w_args
        gt_shape = jax.ShapeDtypeStruct((n // tm * 8, d_ff), F32)
        gt_spec = pl.BlockSpec((8, tf), lambda i, f: (i, f))
        scratch = [pltpu.VMEM((tm, d), BF16), pltpu.VMEM((HALO, d), BF16), pltpu.VMEM((tm, d), F32)]
        kern = functools.partial(_ffn_kernel, seq_tiles=seq_tiles, n_tok=None)
    else:
        n_tok = n // state.shape[0]
        s0 = jnp.repeat(state[:, 0], n_tok, axis=0)
        s1 = jnp.repeat(state[:, 1], n_tok, axis=0)
        in_specs = [pl.BlockSpec((tm, d), row, pipeline_mode=once),
                    pl.BlockSpec((tm, tf), lambda i, f: (i, f)),
                    pl.BlockSpec((tm, tf), lambda i, f: (i, f))] + common_w
        args = (x2, s0, s1) + w_args
        gt_shape = jax.ShapeDtypeStruct((n, d_ff), F32)
        gt_spec = pl.BlockSpec((tm, tf), lambda i, f: (i, f))
        scratch = [pltpu.VMEM((tm, d), BF16), pltpu.VMEM((tm, d), F32)]
        kern = functools.partial(_ffn_kernel, seq_tiles=None, n_tok=n_tok)
    return pl.pallas_call(
        kern,
        grid=(n // tm, nf),
        in_specs=in_specs,
        out_specs=(pl.BlockSpec((tm, d), row, pipeline_mode=once), gt_spec),
        out_shape=(jax.ShapeDtypeStruct((n, d), F32), gt_shape),
        scratch_shapes=scratch,
        compiler_params=_cparams(("arbitrary", "arbitrary")),
        name="ffn",
    )(*args)


def _rope_tables(pos):
    half = HEAD_DIM // 2
    inv_freq = ROPE_THETA ** (-jnp.arange(half, dtype=F32) / half)
    ang = pos.astype(F32)[:, None] * inv_freq[None, :]
    cos, sin = jnp.cos(ang), jnp.sin(ang)
    return jnp.concatenate([cos, cos], axis=-1), jnp.concatenate([-sin, sin], axis=-1)


def _cmp_weights(w1, pe, w2):
    w1r = w1.reshape(CMP_RATIO, CMP_STRIDE * HEAD_DIM, HEAD_DIM)
    w1cat = jnp.concatenate([w1r[r] for r in range(CMP_RATIO)], axis=1).astype(BF16)
    pe8 = jnp.pad(pe.reshape(CMP_RATIO, CMP_STRIDE * HEAD_DIM), ((0, 8 - CMP_RATIO), (0, 0)))
    return w1cat, pe8, w2.astype(BF16)


def kernel(x_prompt, x_sample, cache_nsa_cmp_kv, cache_nsa_sel_kv, cache_nsa_win_kv, cache_fox_kv, cache_fox_logf, state_ffn_conv, page_table, g_pre_mix, w_in, b_fgt, w_cmp_k1, pe_cmp_k, w_cmp_k2, w_cmp_v1, pe_cmp_v, w_cmp_v2, w_nsa_o, w_fox_o, w_out, g_post_mix, g_pre_ffn, w_up, w_conv, b_conv, w_down, g_post_ffn):
    b_p, t_p, d = x_prompt.shape
    b_s, t_s, _ = x_sample.shape
    depth = w_in.shape[0]
    page = cache_nsa_cmp_kv.shape[2]
    past = page_table.shape[1] * page
    g_n, n_h = N_NSA_GROUPS, N_FOX_HEADS

    cos_p, sin_p = _rope_tables(jnp.tile(jnp.arange(t_p), b_p))
    cos_s, sin_s = _rope_tables(jnp.tile(past + jnp.arange(t_s), b_s))

    y_p = x_prompt.reshape(b_p * t_p, d)
    y_s = x_sample.reshape(b_s * t_s, d)
    outs = {k: [] for k in ('cmp_p', 'cmp_s', 'sel_p', 'sel_s', 'win_p', 'win_s',
                            'fox_p', 'fox_s', 'lf_p', 'lf_s', 'conv_p', 'conv_s')}
    o_q = NSA_Q_W
    o_g = o_q + 3 * KV_W
    o_f = o_g + N_GATE_COLS
    o_ff = o_f + 3 * FOX_W
    o_m = o_ff + n_h
    for l in range(depth):
        w = w_in[l]
        w_main = jnp.concatenate([w[:, :o_g], w[:, o_f:o_ff], w[:, o_m:]], axis=1).astype(BF16)
        w_small = jnp.concatenate([w[:, o_g:o_f], w[:, o_ff:o_m],
                                   jnp.zeros((d, LANES - N_GATE_COLS - n_h), F32)], axis=1).astype(BF16)
        bf_row = jnp.zeros((1, LANES), F32).at[0, LOGF_COL0:LOGF_COL0 + n_h].set(b_fgt[l])
        g1 = g_pre_mix[l][None, :]
        cmp_k = _cmp_weights(w_cmp_k1[l], pe_cmp_k[l], w_cmp_k2[l])
        cmp_v = _cmp_weights(w_cmp_v1[l], pe_cmp_v[l], w_cmp_v2[l])
        wn, wf, wo = w_nsa_o[l].astype(BF16), w_fox_o[l].astype(BF16), w_out[l].astype(BF16)
        wu, wd = w_up[l].astype(BF16), w_down[l].astype(BF16)
        ffn_w = (g_pre_ffn[l][None, :], wu, wd, w_conv[l], b_conv[l][None, :], g_post_ffn[l][None, :])

        (qn, kvc, kvs, kvw, qf, kvf, gm, sm, kvs_b, kvw_b, kvf_b) = _project(y_p, g1, cos_p, sin_p, w_main, w_small, bf_row)
        w1cat = jnp.stack([cmp_k[0], cmp_v[0]])
        pe8 = jnp.stack([cmp_k[1], cmp_v[1]])
        w2 = jnp.stack([cmp_k[2], cmp_v[2]])
        ckv = _compress_prompt(kvc, b_p, t_p, w1cat, pe8, w2)
        frow = _fcum_prompt(sm, b_p, t_p)
        o_n = _nsa_prompt(qn, sm, ckv, kvs_b, kvw_b, b_p, t_p)
        o_fx = _fox_prompt(qf, kvf_b, frow, b_p, t_p)
        y1 = _postmix(o_n, o_fx, gm, y_p, wn, wf, wo, g_post_mix[l][None, :])
        y_p, gt = _ffn(y1, *ffn_w, seq_len=t_p)
        n_win = min(WINDOW, t_p)
        outs['cmp_p'].append(kvc.reshape(b_p, t_p, 2, g_n, HEAD_DIM))
        outs['sel_p'].append(kvs.reshape(b_p, t_p, 2, g_n, HEAD_DIM))
        outs['win_p'].append(kvw.reshape(b_p, t_p, 2, g_n, HEAD_DIM)[:, t_p - n_win:])
        outs['fox_p'].append(kvf.reshape(b_p, t_p, 2, n_h, HEAD_DIM))
        outs['lf_p'].append(sm[:, LOGF_COL0:LOGF_COL0 + n_h].reshape(b_p, t_p, n_h))
        tiles_per_seq = t_p // FFN_TM
        gt = gt.reshape(b_p, tiles_per_seq, 8, -1)
        outs['conv_p'].append(gt[:, -1, 8 - (CONV_WIDTH - 1):])

        (qn, kvc, kvs, kvw, qf, kvf, gm, sm, _, _, _) = _project(y_s, g1, cos_s, sin_s, w_main, w_small, bf_row)
        win_buf = cache_nsa_win_kv[l]
        o_n, win_new = _nsa_sample(page_table, cache_nsa_cmp_kv[l], cache_nsa_sel_kv[l], win_buf, kvs, kvw, qn, sm,
                                   cmp_k + cmp_v, t_s)
        o_fx = _fox_sample(page_table, cache_fox_kv[l], cache_fox_logf[l], kvf, qf, sm, t_s)
        o_n = o_n[:, :t_s].reshape(b_s * t_s, -1)
        y1 = _postmix(o_n, o_fx, gm, y_s, wn, wf, wo, g_post_mix[l][None, :])
        y_s, gt = _ffn(y1, *ffn_w, state=state_ffn_conv[l])
        outs['cmp_s'].append(kvc.reshape(b_s, t_s, 2, g_n, HEAD_DIM))
        outs['sel_s'].append(kvs.reshape(b_s, t_s, 2, g_n, HEAD_DIM))
        outs['win_s'].append(win_new.reshape(win_buf.shape))
        outs['fox_s'].append(kvf.reshape(b_s, t_s, 2, n_h, HEAD_DIM))
        outs['lf_s'].append(sm[:, LOGF_COL0:LOGF_COL0 + n_h].reshape(b_s, t_s, n_h))
        gfull = jnp.concatenate([state_ffn_conv[l], gt.reshape(b_s, t_s, -1)], axis=1)
        outs['conv_s'].append(gfull[:, t_s:])

    st = {k: jnp.stack(v) for k, v in outs.items()}
    return (y_p.reshape(b_p, t_p, d), y_s.reshape(b_s, t_s, d),
            st['cmp_p'], st['cmp_s'], st['sel_p'], st['sel_s'], st['win_p'], st['win_s'],
            st['fox_p'], st['fox_s'], st['lf_p'], st['lf_s'], st['conv_p'], st['conv_s'])
```

```python
import functools

import numpy as np
import jax
import jax.numpy as jnp
from jax import lax
from jax.experimental import pallas as pl
from jax.experimental.pallas import tpu as pltpu

F32 = jnp.float32
BF16 = jnp.bfloat16

HEAD_DIM = 128
N_NSA_HEADS = 8
N_NSA_GROUPS = 2
NSA_HPG = N_NSA_HEADS // N_NSA_GROUPS
N_FOX_HEADS = 8
CMP_BLOCK = 32
CMP_STRIDE = 16
CMP_RATIO = CMP_BLOCK // CMP_STRIDE
SEL_BLOCK = 64
SEL_TOPK = 16
N_LOCAL_BLOCKS = 2
WINDOW = 512
CONV_WIDTH = 3
ROPE_THETA = 10000.0
RMS_EPS = 1e-6
FORCE_BONUS = 1e4
NEG_INF = -1e30
LOG2E = 1.4426950408889634
QK_SCALE = HEAD_DIM ** -0.5 * LOG2E

N_GATE_COLS = N_NSA_HEADS * 3
LOGF_COL0 = N_GATE_COLS
LANES = 128
VMEM_LIMIT = 56 * 1024 * 1024

PROJ_TN = 512
PROJ_TM = 1024
KV_W = 2 * N_NSA_GROUPS * HEAD_DIM
NSA_Q_W = N_NSA_HEADS * HEAD_DIM
FOX_W = N_FOX_HEADS * HEAD_DIM


def _cparams(sem):
    return pltpu.CompilerParams(dimension_semantics=sem, vmem_limit_bytes=VMEM_LIMIT)


def _dot(a, b):
    return jnp.dot(a, b, preferred_element_type=F32)


def _dot_nt(a, b):
    return lax.dot_general(a, b, (((1,), (1,)), ((), ())), preferred_element_type=F32)


def _rms(x, g):
    return x * lax.rsqrt(jnp.mean(x * x, axis=-1, keepdims=True) + RMS_EPS) * g


def _masked_softmax(s, mask):
    sm = jnp.where(mask, s, NEG_INF)
    m = jnp.max(sm, axis=-1, keepdims=True)
    e = jnp.where(mask, jnp.exp2(sm - m), 0.0)
    l = jnp.sum(e, axis=-1, keepdims=True)
    return e / jnp.where(l > 0.0, l, 1.0)


def _transpose_rows(src_ref, dst_ref, cols=slice(None)):
    for c in range(src_ref.shape[0] // LANES):
        rows = slice(c * LANES, (c + 1) * LANES)
        dst_ref[:, rows] = src_ref[rows, cols].astype(F32).T.astype(BF16)


def _masked_exp(s, mask):
    sm = jnp.where(mask, s, NEG_INF)
    e = jnp.where(mask, jnp.exp2(sm - jnp.max(sm, axis=-1, keepdims=True)), 0.0)
    l = jnp.sum(e, axis=-1, keepdims=True)
    return e, jnp.where(l > 0.0, l, 1.0)


def _lane_tile_max(mx, s):
    for c in range(s.shape[1] // LANES):
        mx = jnp.maximum(mx, s[:, c * LANES:(c + 1) * LANES])
    return mx


def _exp_accumulate(carry, s, m, v, lane_shift=0):
    ls, acc = carry
    p = jnp.exp2(s - m)
    tiles = [p[:, c * LANES:(c + 1) * LANES] for c in range(s.shape[1] // LANES)]
    for t in tiles:
        ls = ls + t
    if lane_shift:
        p = jnp.concatenate([pltpu.roll(t, lane_shift, axis=1) for t in tiles], axis=1)
    return ls, acc + _dot(p.astype(BF16), v)


def _split3(x):
    hi = x.astype(BF16)
    r = x - hi.astype(F32)
    mid = r.astype(BF16)
    lo = (r - mid.astype(F32)).astype(BF16)
    return hi, mid, lo


def _topk_mask(score, k, n_sel):
    st = score.T
    nv = -(-n_sel // 8)
    slabs = [st[8 * v:8 * v + 8, :] for v in range(nv)]
    sub = lax.broadcasted_iota(jnp.int32, (8, LANES), 0)
    ranks = [jnp.zeros((8, LANES), F32) for _ in range(nv)]
    for b2 in range(n_sel):
        row = jnp.broadcast_to(st[b2:b2 + 1, :], (8, LANES))
        for v in range(nv):
            if b2 < 8 * v:
                beats = row >= slabs[v]
            elif b2 >= 8 * v + 8:
                beats = row > slabs[v]
            else:
                beats = (row > slabs[v]) | ((row == slabs[v]) & (sub > b2 - 8 * v))
            ranks[v] = ranks[v] + jnp.where(beats, 1.0, 0.0)
    sel = [jnp.where((ranks[v] < k) & (sub + 8 * v < n_sel), 1.0, 0.0) for v in range(nv)]
    sel_t = jnp.concatenate(sel + [jnp.zeros((LANES - 8 * nv, LANES), F32)], axis=0)
    return sel_t.T


def _sel_scores(imp, tpos, n_sel):
    bidx = lax.broadcasted_iota(jnp.int32, imp.shape, 1)
    cur = jnp.right_shift(tpos, 6)
    valid = bidx <= cur
    forced = (bidx == 0) | (valid & (bidx > cur - N_LOCAL_BLOCKS))
    score = jnp.where(valid, jnp.where(forced, imp + FORCE_BONUS, imp), NEG_INF)
    return jnp.where(bidx < n_sel, score, -jnp.inf)


def _proj_kernel(x_ref, g_ref, cos_ref, sin_ref, w_ref, ws_ref, bf_ref,
                 qn_ref, kvc_ref, kvs_ref, kvw_ref, qf_ref, kvf_ref, gm_ref, sm_ref,
                 kvsb_ref, kvwb_ref, kvfb_ref, h_scr):
    j = pl.program_id(1)

    @pl.when(j == 0)
    def _():
        x = x_ref[...]
        y = x * lax.rsqrt(jnp.mean(x * x, axis=-1, keepdims=True) + RMS_EPS)
        h = (y * g_ref[...]).astype(BF16)
        h_scr[...] = h
        s = _dot(h, ws_ref[...])
        lane = lax.broadcasted_iota(jnp.int32, s.shape, 1)
        z = s + bf_ref[...]
        lf = jnp.minimum(z, 0.0) - jnp.log1p(jnp.exp(-jnp.abs(z)))
        sm_ref[...] = jnp.where(lane < N_GATE_COLS, jax.nn.sigmoid(s),
                                jnp.where(lane < LOGF_COL0 + N_FOX_HEADS, lf, 0.0))

    cos = cos_ref[...]
    sin = sin_ref[...]
    half_w = PROJ_TN // 2
    halves = [slice(0, half_w), slice(half_w, PROJ_TN)]

    def mm(cols):
        return _dot(h_scr[...], w_ref[:, cols])

    def rope2(a):
        return jnp.concatenate(
            [a[:, k * HEAD_DIM:(k + 1) * HEAD_DIM] * cos
             + pltpu.roll(a[:, k * HEAD_DIM:(k + 1) * HEAD_DIM], HEAD_DIM // 2, axis=1) * sin
             for k in range(half_w // HEAD_DIM)], axis=1)

    def kv_rows(ref, bref):
        assert half_w == N_NSA_GROUPS * HEAD_DIM
        for cols, is_key in zip(halves, (True, False)):
            a = mm(cols)
            a = rope2(a) if is_key else a
            ref[:, cols] = a
            if bref is not None:
                bref[:, cols] = a.astype(BF16)

    @pl.when(j < 2)
    def _():
        for cols in halves:
            qn_ref[:, cols] = (rope2(mm(cols)) * QK_SCALE).astype(BF16)

    @pl.when(j == 2)
    def _():
        kv_rows(kvc_ref, None)

    @pl.when(j == 3)
    def _():
        kv_rows(kvs_ref, kvsb_ref)

    @pl.when(j == 4)
    def _():
        kv_rows(kvw_ref, kvwb_ref)

    @pl.when((j >= 5) & (j < 7))
    def _():
        for cols in halves:
            qf_ref[:, cols] = (mm(cols) * QK_SCALE).astype(BF16)

    @pl.when((j >= 7) & (j < 11))
    def _():
        for cols in halves:
            a = mm(cols)
            kvf_ref[:, cols] = a
            kvfb_ref[:, cols] = a.astype(BF16)

    @pl.when(j >= 11)
    def _():
        for cols in halves:
            gm_ref[:, cols] = jax.nn.sigmoid(mm(cols))


def _project(x2, g, cos2, sin2, w_main, w_small, bf_row):
    n, d = x2.shape
    tm = min(PROJ_TM, n)
    n_j = w_main.shape[1] // PROJ_TN
    tn = PROJ_TN

    def clip(lo, hi):
        return lambda i, j: (i, jnp.clip(j - lo, 0, hi - lo))

    row = lambda i, j: (i, 0)
    out_shape = (
        jax.ShapeDtypeStruct((n, NSA_Q_W), BF16),
        jax.ShapeDtypeStruct((n, KV_W), F32),
        jax.ShapeDtypeStruct((n, KV_W), F32),
        jax.ShapeDtypeStruct((n, KV_W), F32),
        jax.ShapeDtypeStruct((n, FOX_W), BF16),
        jax.ShapeDtypeStruct((n, 2 * FOX_W), F32),
        jax.ShapeDtypeStruct((n, 2 * d), F32),
        jax.ShapeDtypeStruct((n, LANES), F32),
        jax.ShapeDtypeStruct((n, KV_W), BF16),
        jax.ShapeDtypeStruct((n, KV_W), BF16),
        jax.ShapeDtypeStruct((n, 2 * FOX_W), BF16),
    )
    out_specs = (
        pl.BlockSpec((tm, tn), clip(0, 1)),
        pl.BlockSpec((tm, tn), row),
        pl.BlockSpec((tm, tn), row),
        pl.BlockSpec((tm, tn), row),
        pl.BlockSpec((tm, tn), clip(5, 6)),
        pl.BlockSpec((tm, tn), clip(7, 10)),
        pl.BlockSpec((tm, tn), clip(11, 18)),
        pl.BlockSpec((tm, LANES), row),
        pl.BlockSpec((tm, tn), row),
        pl.BlockSpec((tm, tn), row),
        pl.BlockSpec((tm, tn), clip(7, 10)),
    )
    in_specs = [
        pl.BlockSpec((tm, d), row, pipeline_mode=pl.Buffered(1)),
        pl.BlockSpec((1, d), lambda i, j: (0, 0)),
        pl.BlockSpec((tm, LANES), row),
        pl.BlockSpec((tm, LANES), row),
        pl.BlockSpec((d, tn), lambda i, j: (0, j)),
        pl.BlockSpec((d, LANES), lambda i, j: (0, 0)),
        pl.BlockSpec((1, LANES), lambda i, j: (0, 0)),
    ]
    return pl.pallas_call(
        _proj_kernel,
        grid=(n // tm, n_j),
        in_specs=in_specs,
        out_specs=out_specs,
        out_shape=out_shape,
        scratch_shapes=[pltpu.VMEM((tm, d), BF16)],
        compiler_params=_cparams(("arbitrary", "arbitrary")),
        name="proj",
    )(x2, g, cos2, sin2, w_main, w_small, bf_row)


def _compress_tail(xc, w1, pe8, w2):
    n = xc.shape[0]
    part = _dot(xc, w1)
    pp = _dot(pe8.astype(BF16), w1)
    pe_term = pp[0:1, :HEAD_DIM] + pp[1:2, HEAD_DIM:]
    hid = pe_term + part[:, :HEAD_DIM] + pltpu.roll(part[:, HEAD_DIM:], n - 1, axis=0)
    return _dot(jax.nn.gelu(hid, approximate=True).astype(BF16), w2)


def _cmp_prompt_kernel(x_ref, w1_ref, pe_ref, w2_ref, o_ref):
    n = x_ref.shape[0] // CMP_STRIDE
    xc = jnp.concatenate([x_ref[pl.ds(i, n, stride=CMP_STRIDE), :] for i in range(CMP_STRIDE)],
                         axis=1).astype(BF16)
    o_ref[0, 0] = _compress_tail(xc, w1_ref[0], pe_ref[0], w2_ref[0]).astype(BF16)


def _compress_prompt(kvc, b_n, t_n, w1cat, pe8, w2):
    n = t_n // CMP_STRIDE
    return pl.pallas_call(
        _cmp_prompt_kernel,
        grid=(b_n, 2 * N_NSA_GROUPS),
        in_specs=[
            pl.BlockSpec((t_n, HEAD_DIM), lambda b, s: (b, s)),
            pl.BlockSpec((1, CMP_STRIDE * HEAD_DIM, 2 * HEAD_DIM), lambda b, s: (s // N_NSA_GROUPS, 0, 0)),
            pl.BlockSpec((1, 8, CMP_STRIDE * HEAD_DIM), lambda b, s: (s // N_NSA_GROUPS, 0, 0)),
            pl.BlockSpec((1, HEAD_DIM, HEAD_DIM), lambda b, s: (s // N_NSA_GROUPS, 0, 0)),
        ],
        out_specs=pl.BlockSpec((1, 1, n, HEAD_DIM), lambda b, s: (b, s, 0, 0)),
        out_shape=jax.ShapeDtypeStruct((b_n, 2 * N_NSA_GROUPS, n, HEAD_DIM), BF16),
        compiler_params=_cparams(("arbitrary", "arbitrary")),
        name="cmp_prompt",
    )(kvc, w1cat, pe8, w2)


def _fcum_kernel(x_ref, frow_ref, carry_scr):
    i = pl.program_id(1)

    @pl.when(i == 0)
    def _():
        carry_scr[...] = jnp.zeros_like(carry_scr)

    x = x_ref[...]
    tb = x.shape[0]
    r = lax.broadcasted_iota(jnp.int32, (tb, tb), 0)
    c = lax.broadcasted_iota(jnp.int32, (tb, tb), 1)
    tri = jnp.where(r >= c, 1.0, 0.0).astype(BF16)
    hi, mid, lo = _split3(x)
    cs = _dot(tri, hi) + _dot(tri, mid) + _dot(tri, lo) + carry_scr[0:1, :]
    carry_scr[...] = jnp.broadcast_to(cs[tb - 1:tb, :], carry_scr.shape)
    frow_ref[0] = cs.T[LOGF_COL0:LOGF_COL0 + N_FOX_HEADS, :] * LOG2E


def _fcum_prompt(sm, b_n, t_n):
    tb = 512
    nb = t_n // tb
    return pl.pallas_call(
        _fcum_kernel,
        grid=(b_n, nb),
        in_specs=[pl.BlockSpec((tb, LANES), lambda b, i: (b * nb + i, 0))],
        out_specs=pl.BlockSpec((1, N_FOX_HEADS, tb), lambda b, i: (b, 0, i)),
        out_shape=jax.ShapeDtypeStruct((b_n, N_FOX_HEADS, t_n), F32),
        scratch_shapes=[pltpu.VMEM((8, LANES), F32)],
        compiler_params=_cparams(("arbitrary", "arbitrary")),
        name="fcum_prompt",
    )(sm)


NSA_TQ = 128
NSA_NSUB = 2


def _tile_rows(a, reps):
    return jnp.concatenate([a] * reps, axis=0)


def _nsa_front_kernel(q_ref, sm_ref, ck_ref, kw_ref, cover_ref, selm_ref, ocw_ref, kwt_scr, *, n_cmp, n_sel):
    i = pl.program_id(1)
    tq, nsub = NSA_TQ, NSA_NSUB
    t0 = i * (tq * nsub)
    n_ck = ck_ref.shape[2]
    band = WINDOW + tq
    tpos = [t0 + u * tq + lax.broadcasted_iota(jnp.int32, (tq, 1), 0) for u in range(nsub)]

    @pl.when(i == 0)
    def _():
        for g in range(N_NSA_GROUPS):
            _transpose_rows(kw_ref, kwt_scr.at[g], slice(g * HEAD_DIM, (g + 1) * HEAD_DIM))

    for g in range(N_NSA_GROUPS):
        ck = ck_ref[0, g]
        cv = ck_ref[0, N_NSA_GROUPS + g]
        o_cs, o_ws, selms = [], [], []
        for u in range(nsub):
            q = jnp.concatenate([q_ref[u * tq:(u + 1) * tq, (g * NSA_HPG + h) * HEAD_DIM:(g * NSA_HPG + h + 1) * HEAD_DIM]
                                 for h in range(NSA_HPG)], axis=0)
            s_c = _dot_nt(q, ck)
            cidx = lax.broadcasted_iota(jnp.int32, (tq, n_ck), 1)
            mc = jnp.where((cidx * CMP_STRIDE + CMP_BLOCK - 1 <= tpos[u]) & (cidx < n_cmp), 1.0, 0.0)
            e_c, l_c = _masked_exp(s_c, _tile_rows(mc, NSA_HPG) > 0.5)
            e_cb = e_c.astype(BF16)
            imp4 = _dot(e_cb, cover_ref[...]) / l_c
            imp = imp4[0:tq] + imp4[tq:2 * tq] + imp4[2 * tq:3 * tq] + imp4[3 * tq:4 * tq]
            selms.append(_topk_mask(_sel_scores(imp, tpos[u], n_sel), min(SEL_TOPK, n_sel), n_sel).astype(BF16))
            o_cs.append(_dot(e_cb, cv) / l_c)
            w0 = pl.multiple_of(jnp.maximum(t0 + u * tq - WINDOW, 0), tq)
            vw = kw_ref[pl.ds(w0, band), (N_NSA_GROUPS + g) * HEAD_DIM:(N_NSA_GROUPS + g + 1) * HEAD_DIM]
            s_w = _dot(q, kwt_scr[g, :, pl.ds(w0, band)])
            wpos = w0 + lax.broadcasted_iota(jnp.int32, (tq, band), 1)
            bw = jnp.where((wpos <= tpos[u]) & (wpos > tpos[u] - WINDOW), 0.0, NEG_INF)
            s_w = s_w + _tile_rows(bw, NSA_HPG)
            e_w = jnp.exp2(s_w - jnp.max(s_w, axis=-1, keepdims=True))
            o_ws.append(_dot(e_w.astype(BF16), vw) / jnp.sum(e_w, axis=-1, keepdims=True))

        for u in range(nsub):
            rows_u = slice(u * tq, (u + 1) * tq)
            selm_ref[rows_u, g * LANES:(g + 1) * LANES] = selms[u]
            gates = sm_ref[rows_u, :]
            for h in range(NSA_HPG):
                hh = g * NSA_HPG + h
                rows = slice(h * tq, (h + 1) * tq)
                ocw_ref[rows_u, hh * HEAD_DIM:(hh + 1) * HEAD_DIM] = (
                    gates[:, 3 * hh:3 * hh + 1] * o_cs[u][rows] + gates[:, 3 * hh + 2:3 * hh + 3] * o_ws[u][rows])


def _cover_matrix(n_rows, n_cmp, n_sel):
    c = np.arange(n_rows)[:, None] * CMP_STRIDE
    b = np.arange(LANES)[None, :] * SEL_BLOCK
    m = (c < b + SEL_BLOCK) & (c + CMP_BLOCK > b) & (np.arange(n_rows)[:, None] < n_cmp) & (np.arange(LANES)[None, :] < n_sel)
    return jnp.asarray(m.astype(np.float32), dtype=BF16)


def _expand_matrix(n_keys):
    m = (np.arange(n_keys)[None, :] // SEL_BLOCK) == np.arange(LANES)[:, None]
    return jnp.asarray(m.astype(np.float32), dtype=BF16)


SEL_MASK_BIG = 2.0 ** 100


def _nsa_sel_kernel(q_ref, k_ref, v_ref, selm_ref, sm_ref, ocw_ref, expand_ref, o_ref, s_scr, kt_scr):
    hh = pl.program_id(1) * NSA_HPG + pl.program_id(2)
    blk = FOX_BLK

    @pl.when(pl.program_id(2) == 0)
    def _():
        _transpose_rows(k_ref, kt_scr.at[0:HEAD_DIM])
        kt_scr[HEAD_DIM:, :] = expand_ref[...]

    causal = lax.broadcasted_iota(jnp.int32, (blk, blk), 1) <= lax.broadcasted_iota(jnp.int32, (blk, 1), 0)
    zeros = jnp.zeros((blk, LANES), F32)
    lane = lax.broadcasted_iota(jnp.int32, (blk, LANES), 1)
    for qb in range(q_ref.shape[0] // blk):
        rows = slice(qb * blk, (qb + 1) * blk)
        unsel = ((selm_ref[rows, :].astype(F32) - 1.0) * SEL_MASK_BIG).astype(BF16)
        q = jnp.concatenate([q_ref[rows, :], unsel], axis=1)
        mx = jnp.full((blk, LANES), -jnp.inf, F32)
        for kt in range(qb + 1):
            keys = slice(kt * blk, (kt + 1) * blk)
            s = _dot(q, kt_scr[:, keys])
            if kt == qb:
                s = jnp.where(causal, s, NEG_INF)
            s_scr[:, keys] = s
            mx = _lane_tile_max(mx, s)
        m = jnp.max(mx, axis=-1, keepdims=True)
        carry = (zeros, zeros)
        for kt in range(qb + 1):
            keys = slice(kt * blk, (kt + 1) * blk)
            carry = _exp_accumulate(carry, s_scr[:, keys], m, v_ref[keys, :])
        ls, acc = carry
        gate = jnp.sum(jnp.where(lane == 3 * hh + 1, sm_ref[rows, :], 0.0), axis=-1, keepdims=True)
        out = ocw_ref[rows, :] + gate * (acc / jnp.sum(ls, axis=-1, keepdims=True))
        o_ref[rows, :] = out.astype(BF16)


def _nsa_prompt(qn, sm, ckv, kvs_b, kvw_b, b_n, t_n):
    blk = NSA_TQ * NSA_NSUB
    nq = t_n // blk
    n_cmp = t_n // CMP_STRIDE - CMP_RATIO + 1
    n_sel = t_n // SEL_BLOCK
    n_ck = ckv.shape[2]
    n_g = N_NSA_GROUPS
    cover = _cover_matrix(n_ck, n_cmp, n_sel)
    expand = _expand_matrix(t_n)
    selm, ocw = pl.pallas_call(
        functools.partial(_nsa_front_kernel, n_cmp=n_cmp, n_sel=n_sel),
        grid=(b_n, nq),
        in_specs=[
            pl.BlockSpec((blk, NSA_Q_W), lambda b, i: (b * nq + i, 0)),
            pl.BlockSpec((blk, LANES), lambda b, i: (b * nq + i, 0)),
            pl.BlockSpec((1, 2 * n_g, n_ck, HEAD_DIM), lambda b, i: (b, 0, 0, 0)),
            pl.BlockSpec((t_n, KV_W), lambda b, i: (b, 0)),
            pl.BlockSpec((n_ck, LANES), lambda b, i: (0, 0)),
        ],
        out_specs=(pl.BlockSpec((blk, n_g * LANES), lambda b, i: (b * nq + i, 0)),
                   pl.BlockSpec((blk, NSA_Q_W), lambda b, i: (b * nq + i, 0))),
        out_shape=(jax.ShapeDtypeStruct((b_n * t_n, n_g * LANES), BF16),
                   jax.ShapeDtypeStruct((b_n * t_n, NSA_Q_W), F32)),
        scratch_shapes=[pltpu.VMEM((n_g, HEAD_DIM, t_n), BF16)],
        compiler_params=_cparams(("arbitrary", "arbitrary")),
        name="nsa_front",
    )(qn, sm, ckv, kvw_b, cover)

    hpg = NSA_HPG
    return pl.pallas_call(
        _nsa_sel_kernel,
        grid=(b_n, n_g, hpg),
        in_specs=[
            pl.BlockSpec((t_n, HEAD_DIM), lambda b, g, h: (b, g * hpg + h)),
            pl.BlockSpec((t_n, HEAD_DIM), lambda b, g, h: (b, g)),
            pl.BlockSpec((t_n, HEAD_DIM), lambda b, g, h: (b, n_g + g)),
            pl.BlockSpec((t_n, LANES), lambda b, g, h: (b, g)),
            pl.BlockSpec((t_n, LANES), lambda b, g, h: (b, 0)),
            pl.BlockSpec((t_n, HEAD_DIM), lambda b, g, h: (b, g * hpg + h)),
            pl.BlockSpec((LANES, t_n), lambda b, g, h: (0, 0)),
        ],
        out_specs=pl.BlockSpec((t_n, HEAD_DIM), lambda b, g, h: (b, g * hpg + h)),
        out_shape=jax.ShapeDtypeStruct((b_n * t_n, NSA_Q_W), BF16),
        scratch_shapes=[pltpu.VMEM((FOX_BLK, t_n), F32), pltpu.VMEM((2 * HEAD_DIM, t_n), BF16)],
        compiler_params=_cparams(("arbitrary", "arbitrary", "arbitrary")),
        name="nsa_sel",
    )(qn, kvs_b, kvs_b, selm, sm, ocw, expand)


FOX_BLK = 512


def _fox_prompt_kernel(q_ref, k_ref, v_ref, frow_ref, o_ref, s_scr, kt_scr):
    h = pl.program_id(1)
    blk = FOX_BLK
    assert q_ref.shape[0] % blk == 0
    _transpose_rows(k_ref, kt_scr)

    causal = lax.broadcasted_iota(jnp.int32, (blk, blk), 1) <= lax.broadcasted_iota(jnp.int32, (blk, 1), 0)
    zeros = jnp.zeros((blk, LANES), F32)
    for qb in range(q_ref.shape[0] // blk):
        q = q_ref[qb * blk:(qb + 1) * blk, :]
        mx = jnp.full((blk, LANES), -jnp.inf, F32)
        for kt in range(qb + 1):
            keys = slice(kt * blk, (kt + 1) * blk)
            s = _dot(q, kt_scr[:, keys]) - frow_ref[0, pl.ds(h, 1), keys]
            if kt == qb:
                s = jnp.where(causal, s, NEG_INF)
            s_scr[:, keys] = s
            mx = _lane_tile_max(mx, s)
        m = jnp.max(mx, axis=-1, keepdims=True)
        carry = (zeros, zeros)
        for kt in range(qb + 1):
            keys = slice(kt * blk, (kt + 1) * blk)
            carry = _exp_accumulate(carry, s_scr[:, keys], m, v_ref[keys, :])
        ls, acc = carry
        o_ref[qb * blk:(qb + 1) * blk, :] = (acc / jnp.sum(ls, axis=-1, keepdims=True)).astype(BF16)


def _fox_prompt(qf, kvf_b, frow, b_n, t_n):
    nh = N_FOX_HEADS
    return pl.pallas_call(
        _fox_prompt_kernel,
        grid=(b_n, nh),
        in_specs=[
            pl.BlockSpec((t_n, HEAD_DIM), lambda b, h: (b, h)),
            pl.BlockSpec((t_n, HEAD_DIM), lambda b, h: (b, h)),
            pl.BlockSpec((t_n, HEAD_DIM), lambda b, h: (b, nh + h)),
            pl.BlockSpec((1, nh, t_n), lambda b, h: (b, 0, 0)),
        ],
        out_specs=pl.BlockSpec((t_n, HEAD_DIM), lambda b, h: (b, h)),
        out_shape=jax.ShapeDtypeStruct((b_n * t_n, FOX_W), BF16),
        scratch_shapes=[pltpu.VMEM((FOX_BLK, t_n), F32), pltpu.VMEM((HEAD_DIM, t_n), BF16)],
        compiler_params=_cparams(("arbitrary", "arbitrary")),
        name="fox_prompt",
    )(qf, kvf_b, kvf_b, frow)


FOX_SAMPLE_CH = 512
TOK_PAD = 8
NEW_PAD = 128
NSA_SAMPLE_SUB = 1


def _nsa_sample_kernel(pt_ref, *refs, n_pages, page, n_buf, n_tok, n_sub):
    del pt_ref
    per_seq = 2 * n_pages + 1
    kvs_new_ref, kvw_new_ref, q_ref, sm_ref = refs[n_sub * per_seq:n_sub * per_seq + 4]
    assert n_sub == 1
    for sq in range(n_sub):
        seq_refs = refs[sq * per_seq:(sq + 1) * per_seq]
        _nsa_sample_one(seq_refs[:n_pages], seq_refs[n_pages:2 * n_pages], seq_refs[2 * n_pages],
                        kvs_new_ref[sq], kvw_new_ref[sq], q_ref.at[sq], sm_ref[sq],
                        *refs[n_sub * per_seq + 4:-2], refs[-2].at[sq], refs[-1],
                        n_pages=n_pages, page=page, n_buf=n_buf, n_tok=n_tok)


def _nsa_sample_one(cmp_pages, sel_pages, win_ref, kvs_new, kvw_new, q_ref, gates, w1k_ref, w1v_ref, pek_ref, pev_ref,
                    w2k_ref, w2v_ref, cover_ref, expand_ref, o_ref, wino_ref, *, n_pages, page, n_buf, n_tok):
    past = n_pages * page
    n_slab = 2 * N_NSA_GROUPS

    shift = n_tok * n_slab
    keep = n_buf * n_slab - shift
    assert n_buf == WINDOW and shift % 8 == 0
    wino_ref[0:keep, :] = win_ref[shift:, :]
    for t in range(n_tok):
        for s in range(n_slab):
            wino_ref[keep + t * n_slab + s:keep + t * n_slab + s + 1, :] = kvw_new[t:t + 1, s * HEAD_DIM:(s + 1) * HEAD_DIM]
    chunks_per_page = page // CMP_STRIDE
    n_chunk = n_pages * chunks_per_page
    n_cmp = (past + n_tok + CMP_STRIDE - 1) // CMP_STRIDE - CMP_RATIO + 1
    n_sel = (past + n_tok + SEL_BLOCK - 1) // SEL_BLOCK
    tp = TOK_PAD
    tpos = past + lax.broadcasted_iota(jnp.int32, (tp, 1), 0)

    rows_pc = CMP_STRIDE * n_slab
    swapped = [jnp.swapaxes(cmp_pages[p][...].reshape(chunks_per_page, rows_pc, HEAD_DIM), 0, 1)
               for p in range(n_pages)]

    def compress(slab, w1_ref, pe_ref, w2_ref):
        cols = []
        for i in range(CMP_STRIDE):
            cols.append(jnp.concatenate([swapped[p][i * n_slab + slab] for p in range(n_pages)], axis=0))
        xc = jnp.concatenate(cols, axis=1).astype(BF16)
        return _compress_tail(xc, w1_ref[...], pe_ref[...], w2_ref[...]).astype(BF16)

    qs, o_cs, scores = [], [], []
    for g in range(N_NSA_GROUPS):
        q = jnp.concatenate([q_ref[:, (g * NSA_HPG + h) * HEAD_DIM:(g * NSA_HPG + h + 1) * HEAD_DIM]
                             for h in range(NSA_HPG)], axis=0)
        ck = compress(g, w1k_ref, pek_ref, w2k_ref)
        cv = compress(N_NSA_GROUPS + g, w1v_ref, pev_ref, w2v_ref)
        s_c = _dot_nt(q, ck)
        cidx = lax.broadcasted_iota(jnp.int32, (tp, n_chunk), 1)
        mc = jnp.where((cidx * CMP_STRIDE + CMP_BLOCK - 1 <= tpos) & (cidx < n_cmp), 1.0, 0.0)
        p_c = _masked_softmax(s_c, _tile_rows(mc, NSA_HPG) > 0.5)
        p_cb = p_c.astype(BF16)
        imp4 = _dot(p_cb, cover_ref[...])
        imp = imp4[0:tp] + imp4[tp:2 * tp] + imp4[2 * tp:3 * tp] + imp4[3 * tp:4 * tp]
        qs.append(q)
        o_cs.append(_dot(p_cb, cv))
        scores.append(_sel_scores(imp, tpos, n_sel))

    score_all = jnp.concatenate(scores + [jnp.zeros((LANES - N_NSA_GROUPS * tp, LANES), F32)], axis=0)
    selm_all = _topk_mask(score_all, min(SEL_TOPK, n_sel), n_sel)

    def sel_rows(slab):
        return jnp.concatenate([sel_pages[p][pl.ds(slab, page, stride=n_slab), :] for p in range(n_pages)], axis=0)

    def with_new(cached, new):
        pad = jnp.zeros((NEW_PAD - new.shape[0], new.shape[1]), new.dtype)
        return jnp.concatenate([cached, new, pad], axis=0).astype(BF16)

    for g in range(N_NSA_GROUPS):
        q, o_c = qs[g], o_cs[g]
        selm = selm_all[g * tp:(g + 1) * tp].astype(BF16)
        kc = slice(g * HEAD_DIM, (g + 1) * HEAD_DIM)
        vc = slice((N_NSA_GROUPS + g) * HEAD_DIM, (N_NSA_GROUPS + g + 1) * HEAD_DIM)

        n_keys = past + NEW_PAD
        s_s = _dot_nt(q, with_new(sel_rows(g), kvs_new[:, kc]))
        selx = _dot(selm, expand_ref[...])
        kpos = lax.broadcasted_iota(jnp.int32, (tp, n_keys), 1)
        ms = jnp.where((selx > 0.5) & (kpos <= tpos) & (kpos < past + n_tok), 1.0, 0.0)
        p_s = _masked_softmax(s_s, _tile_rows(ms, NSA_HPG) > 0.5)
        o_s = _dot(p_s.astype(BF16), with_new(sel_rows(N_NSA_GROUPS + g), kvs_new[:, vc]))

        s_w = _dot_nt(q, with_new(win_ref[pl.ds(g, n_buf, stride=n_slab), :], kvw_new[:, kc]))
        wpos = past - n_buf + lax.broadcasted_iota(jnp.int32, (tp, n_buf + NEW_PAD), 1)
        mw = jnp.where((wpos <= tpos) & (wpos > tpos - WINDOW) & (wpos < past + n_tok), 1.0, 0.0)
        p_w = _masked_softmax(s_w, _tile_rows(mw, NSA_HPG) > 0.5)
        o_w = _dot(p_w.astype(BF16), with_new(win_ref[pl.ds(N_NSA_GROUPS + g, n_buf, stride=n_slab), :],
                                              kvw_new[:, vc]))

        for h in range(NSA_HPG):
            hh = g * NSA_HPG + h
            rows = slice(h * tp, (h + 1) * tp)
            out = (gates[:, 3 * hh:3 * hh + 1] * o_c[rows] + gates[:, 3 * hh + 1:3 * hh + 2] * o_s[rows]
                   + gates[:, 3 * hh + 2:3 * hh + 3] * o_w[rows])
            o_ref[:, hh * HEAD_DIM:(hh + 1) * HEAD_DIM] = out.astype(BF16)


def _pad_tokens(a, n_seq, n_tok):
    a = a.reshape(n_seq, n_tok, a.shape[-1])
    return jnp.pad(a, ((0, 0), (0, TOK_PAD - n_tok), (0, 0)))


def _nsa_sample(page_table, cache_cmp, cache_sel, win_buf, kvs_new, kvw_new, qn, sm, cmp_w, n_tok):
    n_seq, n_pages = page_table.shape
    page = cache_cmp.shape[1]
    n_slab = 2 * N_NSA_GROUPS
    n_buf = win_buf.shape[1]
    past = n_pages * page
    cmp2 = cache_cmp.reshape(-1, HEAD_DIM)
    sel2 = cache_sel.reshape(-1, HEAD_DIM)
    win2 = win_buf.reshape(-1, HEAD_DIM)
    n_chunk = past // CMP_STRIDE
    n_cmp = (past + n_tok + CMP_STRIDE - 1) // CMP_STRIDE - CMP_RATIO + 1
    n_sel = (past + n_tok + SEL_BLOCK - 1) // SEL_BLOCK
    cover = _cover_matrix(n_chunk, min(n_cmp, n_chunk), n_sel)
    expand = _expand_matrix(past + NEW_PAD)
    w1k, pek, w2k, w1v, pev, w2v = cmp_w

    n_sub = NSA_SAMPLE_SUB
    assert n_seq % n_sub == 0

    def page_spec(sq, p):
        return pl.BlockSpec((page * n_slab, HEAD_DIM), lambda b, pt, sq=sq, p=p: (pt[b * n_sub + sq, p], 0))

    const2 = lambda b, pt: (0, 0)
    seq3 = lambda b, pt: (b, 0, 0)
    per_seq = []
    for sq in range(n_sub):
        per_seq += ([page_spec(sq, p) for p in range(n_pages)] + [page_spec(sq, p) for p in range(n_pages)]
                    + [pl.BlockSpec((n_buf * n_slab, HEAD_DIM), lambda b, pt, sq=sq: (b * n_sub + sq, 0))])
    in_specs = (per_seq + [
        pl.BlockSpec((n_sub, TOK_PAD, KV_W), seq3),
        pl.BlockSpec((n_sub, TOK_PAD, KV_W), seq3),
        pl.BlockSpec((n_sub, TOK_PAD, NSA_Q_W), seq3),
        pl.BlockSpec((n_sub, TOK_PAD, LANES), seq3),
        pl.BlockSpec(w1k.shape, const2),
        pl.BlockSpec(w1v.shape, const2),
        pl.BlockSpec(pek.shape, const2),
        pl.BlockSpec(pev.shape, const2),
        pl.BlockSpec(w2k.shape, const2),
        pl.BlockSpec(w2v.shape, const2),
        pl.BlockSpec(cover.shape, const2),
        pl.BlockSpec(expand.shape, const2),
    ])
    kern = functools.partial(_nsa_sample_kernel, n_pages=n_pages, page=page, n_buf=n_buf, n_tok=n_tok, n_sub=n_sub)
    grid_spec = pltpu.PrefetchScalarGridSpec(
        num_scalar_prefetch=1, grid=(n_seq // n_sub,), in_specs=in_specs,
        out_specs=(pl.BlockSpec((n_sub, TOK_PAD, NSA_Q_W), seq3),
                   pl.BlockSpec((n_buf * n_slab, HEAD_DIM), lambda b, pt: (b, 0))))
    return pl.pallas_call(
        kern,
        grid_spec=grid_spec,
        out_shape=(jax.ShapeDtypeStruct((n_seq, TOK_PAD, NSA_Q_W), BF16),
                   jax.ShapeDtypeStruct(win2.shape, F32)),
        compiler_params=_cparams(("arbitrary",)),
        name="nsa_sample",
    )(page_table, *(([cmp2] * n_pages + [sel2] * n_pages + [win2]) * n_sub),
      _pad_tokens(kvs_new, n_seq, n_tok), _pad_tokens(kvw_new, n_seq, n_tok),
      _pad_tokens(qn, n_seq, n_tok), _pad_tokens(sm, n_seq, n_tok),
      w1k, w1v, pek, pev, w2k, w2v, cover, expand)


def _fox_sample_kernel(pt_ref, *refs, n_pages, page, n_tok):
    kv_pages = refs[:n_pages]
    lf_pages = refs[n_pages:2 * n_pages]
    k_new_ref, v_new_ref, q_ref, lfn_ref, o_ref, mask_scr, s_scr, kb_scr = refs[2 * n_pages:]
    del pt_ref
    nh = N_FOX_HEADS
    rows_pp = 2 * nh
    page_rows = page * rows_pp
    n_chunk = page_rows // LANES
    n_q = nh * n_tok
    assert rows_pp == 16 and LANES % rows_pp == 0 and n_q <= LANES

    @pl.when(pl.program_id(0) == 0)
    def _():
        qrow = lax.broadcasted_iota(jnp.int32, mask_scr.shape, 0)
        lane = lax.broadcasted_iota(jnp.int32, mask_scr.shape, 1)
        mask_scr[...] = jnp.where((lane & (rows_pp - 1)) == (qrow & (nh - 1)), 0.0, NEG_INF)

    x = jnp.concatenate([lf_pages[p][0] for p in range(n_pages)], axis=0)
    n_r = x.shape[0]
    la = lax.broadcasted_iota(jnp.int32, (LANES, LANES), 0)
    lb = lax.broadcasted_iota(jnp.int32, (LANES, LANES), 1)
    same = (la & (rows_pp - 1)) == (lb & (rows_pp - 1))
    u_in = jnp.where(same & (jnp.right_shift(la, 4) <= jnp.right_shift(lb, 4)), 1.0, 0.0).astype(BF16)
    u_all = jnp.where(same, 1.0, 0.0).astype(BF16)
    xh, xm, xl = _split3(x)
    within = _dot(xh, u_in) + _dot(xm, u_in) + _dot(xl, u_in)
    tot = _dot(xh, u_all) + _dot(xm, u_all) + _dot(xl, u_all)
    ra = lax.broadcasted_iota(jnp.int32, (n_r, n_r), 0)
    rb = lax.broadcasted_iota(jnp.int32, (n_r, n_r), 1)
    before = jnp.where(rb < ra, 1.0, 0.0).astype(BF16)
    th, tm_, tl = _split3(tot)
    offs = _dot(before, th) + _dot(before, tm_) + _dot(before, tl)
    f_end = offs[n_r - 1:n_r, :] + tot[n_r - 1:n_r, :]
    bias = (f_end - (within + offs)) * LOG2E

    q_all = q_ref[0]

    ch = mask_scr.shape[1]
    lt = ch // LANES
    steps = [(p, c) for p in range(n_pages) for c in range(page_rows // ch)]
    mx = jnp.full((n_q, LANES), -jnp.inf, F32)
    for p, c in steps:
        k_b = kv_pages[p][pl.ds(c * ch, ch), :].astype(BF16)
        kb_scr[p * page_rows + c * ch:p * page_rows + (c + 1) * ch, :] = k_b
        r0 = p * n_chunk + c * lt
        brow = jnp.concatenate([bias[r0 + i:r0 + i + 1, :] for i in range(lt)], axis=1)
        s = _dot_nt(q_all, k_b) + brow + mask_scr[...]
        s_scr[:, p * page_rows + c * ch:p * page_rows + (c + 1) * ch] = s
        mx = _lane_tile_max(mx, s)

    pad = jnp.zeros((LANES - n_q, HEAD_DIM), F32)
    k_new = jnp.concatenate([k_new_ref[0], pad], axis=0).astype(BF16)
    v_new = jnp.concatenate([v_new_ref[0], pad], axis=0).astype(BF16)
    g_in = jnp.where(((la & (nh - 1)) == (lb & (nh - 1))) & (la <= lb), 1.0, 0.0).astype(BF16)
    nh_, nm_, nl_ = _split3(lfn_ref[0])
    c_new = (_dot(nh_, g_in) + _dot(nm_, g_in) + _dot(nl_, g_in))[0:1, :] * LOG2E
    qrow = lax.broadcasted_iota(jnp.int32, (n_q, LANES), 0)
    lane = lax.broadcasted_iota(jnp.int32, (n_q, LANES), 1)
    ok = ((lane & (nh - 1)) == (qrow & (nh - 1))) & (lane <= qrow)
    s_new = jnp.where(ok, _dot_nt(q_all, k_new) - c_new, NEG_INF)
    m = jnp.max(jnp.maximum(mx, s_new), axis=-1, keepdims=True)

    carry = _exp_accumulate((jnp.zeros((n_q, LANES), F32), jnp.zeros((n_q, HEAD_DIM), F32)), s_new, m, v_new)
    for p, c in steps:
        rows = slice(p * page_rows + c * ch, p * page_rows + (c + 1) * ch)
        carry = _exp_accumulate(carry, s_scr[:, rows], m, kb_scr[rows, :], lane_shift=nh)
    ls, acc = carry
    o_ref[0] = (acc / jnp.sum(ls, axis=-1, keepdims=True)).astype(BF16)


def _fox_sample(page_table, cache_fox, cache_logf, kvf_new, qf, sm, n_tok):
    n_seq, n_pages = page_table.shape
    page = cache_fox.shape[1]
    nh = N_FOX_HEADS
    kv2 = cache_fox.reshape(-1, HEAD_DIM)
    rows_pp = 2 * nh
    n_chunk = page * rows_pp // LANES
    lf_c = jnp.pad(cache_logf, ((0, 0), (0, 0), (0, rows_pp - nh))).reshape(-1, n_chunk, LANES)
    n_q = n_tok * nh
    lfn = sm[:, LOGF_COL0:LOGF_COL0 + nh].reshape(n_seq, 1, n_q)
    lfn = jnp.pad(lfn, ((0, 0), (0, 7), (0, LANES - n_q)))
    kv_new = kvf_new.reshape(n_seq, n_tok, 2, nh, HEAD_DIM)
    k_new = kv_new[:, :, 0].reshape(n_seq, n_q, HEAD_DIM)
    v_new = kv_new[:, :, 1].reshape(n_seq, n_q, HEAD_DIM)
    q3 = qf.reshape(n_seq, n_q, HEAD_DIM)

    seq3 = lambda b, pt: (b, 0, 0)
    in_specs = ([pl.BlockSpec((page * rows_pp, HEAD_DIM), lambda b, pt, p=p: (pt[b, p], 0)) for p in range(n_pages)]
                + [pl.BlockSpec((1, n_chunk, LANES), lambda b, pt, p=p: (pt[b, p], 0, 0)) for p in range(n_pages)]
                + [pl.BlockSpec((1, n_q, HEAD_DIM), seq3),
                   pl.BlockSpec((1, n_q, HEAD_DIM), seq3),
                   pl.BlockSpec((1, n_q, HEAD_DIM), seq3),
                   pl.BlockSpec((1, 8, LANES), seq3)])
    kern = functools.partial(_fox_sample_kernel, n_pages=n_pages, page=page, n_tok=n_tok)
    n_rows = n_pages * page * rows_pp
    grid_spec = pltpu.PrefetchScalarGridSpec(
        num_scalar_prefetch=1, grid=(n_seq,), in_specs=in_specs,
        out_specs=pl.BlockSpec((1, n_q, HEAD_DIM), seq3),
        scratch_shapes=[pltpu.VMEM((n_q, FOX_SAMPLE_CH), F32),
                        pltpu.VMEM((n_q, n_rows), F32),
                        pltpu.VMEM((n_rows, HEAD_DIM), BF16)])
    out = pl.pallas_call(
        kern,
        grid_spec=grid_spec,
        out_shape=jax.ShapeDtypeStruct((n_seq, n_q, HEAD_DIM), BF16),
        compiler_params=_cparams(("arbitrary",)),
        name="fox_sample",
    )(page_table, *([kv2] * n_pages), *([lf_c] * n_pages), k_new, v_new, q3, lfn)
    return out.reshape(n_seq * n_tok, nh * HEAD_DIM)


def _postmix_kernel(on_ref, of_ref, gm0_ref, gm1_ref, x_ref, wn_ref, wf_ref, wo_ref, g_ref, y_ref):
    a = _dot(on_ref[...], wn_ref[...])
    b = _dot(of_ref[...], wf_ref[...])
    merged = gm0_ref[...] * a + gm1_ref[...] * b
    z = _dot(merged.astype(BF16), wo_ref[...])
    y_ref[...] = x_ref[...] + _rms(z, g_ref[...])


def _postmix(o_n, o_f, gm, x2, wn, wf, wo, g):
    n, d = x2.shape
    tm = 256
    row = lambda i: (i, 0)
    const = lambda i: (0, 0)
    return pl.pallas_call(
        _postmix_kernel,
        grid=(n // tm,),
        in_specs=[
            pl.BlockSpec((tm, NSA_Q_W), row),
            pl.BlockSpec((tm, FOX_W), row),
            pl.BlockSpec((tm, d), lambda i: (i, 0)),
            pl.BlockSpec((tm, d), lambda i: (i, 1)),
            pl.BlockSpec((tm, d), row),
            pl.BlockSpec(wn.shape, const),
            pl.BlockSpec(wf.shape, const),
            pl.BlockSpec(wo.shape, const),
            pl.BlockSpec((1, d), const),
        ],
        out_specs=pl.BlockSpec((tm, d), row),
        out_shape=jax.ShapeDtypeStruct((n, d), F32),
        compiler_params=_cparams(("arbitrary",)),
        name="postmix",
    )(o_n, o_f, gm, gm, x2, wn, wf, wo, g)


FFN_TM = 1024
FFN_TF = 512
HALO = 16


def _ffn_kernel(*refs, seq_tiles, n_tok):
    if n_tok is None:
        (x_ref, xh_ref, g_ref, wg_ref, wu_ref, wd_ref, wc_ref, bc_ref, gp_ref,
         y_ref, gt_ref, h_scr, hh_scr, acc_scr) = refs
    else:
        (x_ref, s0_ref, s1_ref, g_ref, wg_ref, wu_ref, wd_ref, wc_ref, bc_ref, gp_ref,
         y_ref, gt_ref, h_scr, acc_scr) = refs
    i = pl.program_id(0)
    f = pl.program_id(1)
    tm = x_ref.shape[0]

    @pl.when(f == 0)
    def _():
        h_scr[...] = _rms(x_ref[...], g_ref[...]).astype(BF16)
        acc_scr[...] = jnp.zeros_like(acc_scr)
        if n_tok is None:
            hh_scr[...] = _rms(xh_ref[...], g_ref[...]).astype(BF16)

    h2 = h_scr[...]
    gate = _dot(h2, wg_ref[...])
    up = _dot(h2, wu_ref[...])
    row = lax.broadcasted_iota(jnp.int32, gate.shape, 0)
    r1 = pltpu.roll(gate, 1, axis=0)
    r2 = pltpu.roll(gate, 2, axis=0)
    if n_tok is None:
        first = (i % seq_tiles) == 0
        gh = jnp.where(first, 0.0, _dot(hh_scr[...], wg_ref[...]))
        p1 = gh[HALO - 1:HALO, :]
        p2 = gh[HALO - 2:HALO - 1, :]
        g1 = jnp.where(row == 0, p1, r1)
        g2 = jnp.where(row == 0, p2, jnp.where(row == 1, p1, r2))
        gt_ref[...] = gate[tm - 8:tm, :]
    else:
        assert n_tok & (n_tok - 1) == 0
        rt = row & (n_tok - 1)
        g1 = jnp.where(rt == 0, s1_ref[...], r1)
        g2 = jnp.where(rt == 0, s0_ref[...], jnp.where(rt == 1, s1_ref[...], r2))
        gt_ref[...] = gate
    wc = wc_ref[...]
    gc = bc_ref[...] + wc[0:1, :] * g2 + wc[1:2, :] * g1 + wc[2:3, :] * gate
    act = jax.nn.gelu(gc, approximate=True) * up
    acc_scr[...] += _dot(act.astype(BF16), wd_ref[...])

    @pl.when(f == pl.num_programs(1) - 1)
    def _():
        y_ref[...] = x_ref[...] + _rms(acc_scr[...], gp_ref[...])


def _ffn(x2, g_pre, w_up_b, w_down_b, w_conv, b_conv, g_post, *, seq_len=None, state=None):
    n, d = x2.shape
    d_ff = w_down_b.shape[0]
    tf = FFN_TF
    nf = d_ff // tf
    tm = min(FFN_TM, n)
    common_w = [
        pl.BlockSpec((1, d), lambda i, f: (0, 0)),
        pl.BlockSpec((d, tf), lambda i, f: (0, f)),
        pl.BlockSpec((d, tf), lambda i, f: (0, nf + f)),
        pl.BlockSpec((tf, d), lambda i, f: (f, 0)),
        pl.BlockSpec((CONV_WIDTH, tf), lambda i, f: (0, f)),
        pl.BlockSpec((1, tf), lambda i, f: (0, f)),
        pl.BlockSpec((1, d), lambda i, f: (0, 0)),
    ]
    w_args = (g_pre, w_up_b, w_up_b, w_down_b, w_conv, b_conv, g_post)
    row = lambda i, f: (i, 0)
    once = pl.Buffered(1)
    if state is None:
        seq_tiles = seq_len // tm
        halo_blocks = tm // HALO
        in_specs = [pl.BlockSpec((tm, d), row, pipeline_mode=once),
                    pl.BlockSpec((HALO, d), lambda i, f: (jnp.maximum(i * halo_blocks - 1, 0), 0))] + common_w
        args = (x2, x2) + w_args
        gt_shape = jax.ShapeDtypeStruct((n // tm * 8, d_ff), F32)
        gt_spec = pl.BlockSpec((8, tf), lambda i, f: (i, f))
        scratch = [pltpu.VMEM((tm, d), BF16), pltpu.VMEM((HALO, d), BF16), pltpu.VMEM((tm, d), F32)]
        kern = functools.partial(_ffn_kernel, seq_tiles=seq_tiles, n_tok=None)
    else:
        n_tok = n // state.shape[0]
        s0 = jnp.repeat(state[:, 0], n_tok, axis=0)
        s1 = jnp.repeat(state[:, 1], n_tok, axis=0)
        in_specs = [pl.BlockSpec((tm, d), row, pipeline_mode=once),
                    pl.BlockSpec((tm, tf), lambda i, f: (i, f)),
                    pl.BlockSpec((tm, tf), lambda i, f: (i, f))] + common_w
        args = (x2, s0, s1) + w_args
        gt_shape = jax.ShapeDtypeStruct((n, d_ff), F32)
        gt_spec = pl.BlockSpec((tm, tf), lambda i, f: (i, f))
        scratch = [pltpu.VMEM((tm, d), BF16), pltpu.VMEM((tm, d), F32)]
        kern = functools.partial(_ffn_kernel, seq_tiles=None, n_tok=n_tok)
    return pl.pallas_call(
        kern,
        grid=(n // tm, nf),
        in_specs=in_specs,
        out_specs=(pl.BlockSpec((tm, d), row, pipeline_mode=once), gt_spec),
        out_shape=(jax.ShapeDtypeStruct((n, d), F32), gt_shape),
        scratch_shapes=scratch,
        compiler_params=_cparams(("arbitrary", "arbitrary")),
        name="ffn",
    )(*args)


def _rope_tables(pos):
    half = HEAD_DIM // 2
    inv_freq = ROPE_THETA ** (-np.arange(half, dtype=np.float64) / half)
    ang = np.asarray(pos, np.float64)[:, None] * inv_freq[None, :]
    cos, sin = np.cos(ang), np.sin(ang)
    return (jnp.asarray(np.concatenate([cos, cos], axis=-1), F32),
            jnp.asarray(np.concatenate([-sin, sin], axis=-1), F32))


def _cmp_weights(w1, pe, w2):
    w1r = w1.reshape(CMP_RATIO, CMP_STRIDE * HEAD_DIM, HEAD_DIM)
    w1cat = jnp.concatenate([w1r[r] for r in range(CMP_RATIO)], axis=1).astype(BF16)
    pe8 = jnp.pad(pe.reshape(CMP_RATIO, CMP_STRIDE * HEAD_DIM), ((0, 8 - CMP_RATIO), (0, 0)))
    return w1cat, pe8, w2.astype(BF16)


def kernel(x_prompt, x_sample, cache_nsa_cmp_kv, cache_nsa_sel_kv, cache_nsa_win_kv, cache_fox_kv, cache_fox_logf, state_ffn_conv, page_table, g_pre_mix, w_in, b_fgt, w_cmp_k1, pe_cmp_k, w_cmp_k2, w_cmp_v1, pe_cmp_v, w_cmp_v2, w_nsa_o, w_fox_o, w_out, g_post_mix, g_pre_ffn, w_up, w_conv, b_conv, w_down, g_post_ffn):
    b_p, t_p, d = x_prompt.shape
    b_s, t_s, _ = x_sample.shape
    depth = w_in.shape[0]
    page = cache_nsa_cmp_kv.shape[2]
    past = page_table.shape[1] * page
    g_n, n_h = N_NSA_GROUPS, N_FOX_HEADS

    cos_p, sin_p = _rope_tables(np.tile(np.arange(t_p), b_p))
    cos_s, sin_s = _rope_tables(np.tile(past + np.arange(t_s), b_s))

    y_p = x_prompt.reshape(b_p * t_p, d)
    y_s = x_sample.reshape(b_s * t_s, d)
    outs = {k: [] for k in ('cmp_p', 'cmp_s', 'sel_p', 'sel_s', 'win_p', 'win_s',
                            'fox_p', 'fox_s', 'lf_p', 'lf_s', 'conv_p', 'conv_s')}
    o_q = NSA_Q_W
    o_g = o_q + 3 * KV_W
    o_f = o_g + N_GATE_COLS
    o_ff = o_f + 3 * FOX_W
    o_m = o_ff + n_h
    for l in range(depth):
        w = w_in[l]
        w_main = jnp.concatenate([w[:, :o_g], w[:, o_f:o_ff], w[:, o_m:]], axis=1).astype(BF16)
        w_small = jnp.concatenate([w[:, o_g:o_f], w[:, o_ff:o_m],
                                   jnp.zeros((d, LANES - N_GATE_COLS - n_h), F32)], axis=1).astype(BF16)
        bf_row = jnp.zeros((1, LANES), F32).at[0, LOGF_COL0:LOGF_COL0 + n_h].set(b_fgt[l])
        g1 = g_pre_mix[l][None, :]
        cmp_k = _cmp_weights(w_cmp_k1[l], pe_cmp_k[l], w_cmp_k2[l])
        cmp_v = _cmp_weights(w_cmp_v1[l], pe_cmp_v[l], w_cmp_v2[l])
        wn, wf, wo = w_nsa_o[l].astype(BF16), w_fox_o[l].astype(BF16), w_out[l].astype(BF16)
        wu, wd = w_up[l].astype(BF16), w_down[l].astype(BF16)
        ffn_w = (g_pre_ffn[l][None, :], wu, wd, w_conv[l], b_conv[l][None, :], g_post_ffn[l][None, :])

        (qn, kvc, kvs, kvw, qf, kvf, gm, sm, kvs_b, kvw_b, kvf_b) = _project(y_p, g1, cos_p, sin_p, w_main, w_small, bf_row)
        w1cat = jnp.stack([cmp_k[0], cmp_v[0]])
        pe8 = jnp.stack([cmp_k[1], cmp_v[1]])
        w2 = jnp.stack([cmp_k[2], cmp_v[2]])
        ckv = _compress_prompt(kvc, b_p, t_p, w1cat, pe8, w2)
        frow = _fcum_prompt(sm, b_p, t_p)
        o_n = _nsa_prompt(qn, sm, ckv, kvs_b, kvw_b, b_p, t_p)
        o_fx = _fox_prompt(qf, kvf_b, frow, b_p, t_p)
        y1 = _postmix(o_n, o_fx, gm, y_p, wn, wf, wo, g_post_mix[l][None, :])
        y_p, gt = _ffn(y1, *ffn_w, seq_len=t_p)
        n_win = min(WINDOW, t_p)
        outs['cmp_p'].append(kvc.reshape(b_p, t_p, 2, g_n, HEAD_DIM))
        outs['sel_p'].append(kvs.reshape(b_p, t_p, 2, g_n, HEAD_DIM))
        outs['win_p'].append(kvw.reshape(b_p, t_p, 2, g_n, HEAD_DIM)[:, t_p - n_win:])
        outs['fox_p'].append(kvf.reshape(b_p, t_p, 2, n_h, HEAD_DIM))
        outs['lf_p'].append(sm[:, LOGF_COL0:LOGF_COL0 + n_h].reshape(b_p, t_p, n_h))
        tiles_per_seq = t_p // FFN_TM
        gt = gt.reshape(b_p, tiles_per_seq, 8, -1)
        outs['conv_p'].append(gt[:, -1, 8 - (CONV_WIDTH - 1):])

        (qn, kvc, kvs, kvw, qf, kvf, gm, sm, _, _, _) = _project(y_s, g1, cos_s, sin_s, w_main, w_small, bf_row)
        win_buf = cache_nsa_win_kv[l]
        o_n, win_new = _nsa_sample(page_table, cache_nsa_cmp_kv[l], cache_nsa_sel_kv[l], win_buf, kvs, kvw, qn, sm,
                                   cmp_k + cmp_v, t_s)
        o_fx = _fox_sample(page_table, cache_fox_kv[l], cache_fox_logf[l], kvf, qf, sm, t_s)
        o_n = o_n[:, :t_s].reshape(b_s * t_s, -1)
        y1 = _postmix(o_n, o_fx, gm, y_s, wn, wf, wo, g_post_mix[l][None, :])
        y_s, gt = _ffn(y1, *ffn_w, state=state_ffn_conv[l])
        outs['cmp_s'].append(kvc.reshape(b_s, t_s, 2, g_n, HEAD_DIM))
        outs['sel_s'].append(kvs.reshape(b_s, t_s, 2, g_n, HEAD_DIM))
        outs['win_s'].append(win_new.reshape(win_buf.shape))
        outs['fox_s'].append(kvf.reshape(b_s, t_s, 2, n_h, HEAD_DIM))
        outs['lf_s'].append(sm[:, LOGF_COL0:LOGF_COL0 + n_h].reshape(b_s, t_s, n_h))
        gfull = jnp.concatenate([state_ffn_conv[l], gt.reshape(b_s, t_s, -1)], axis=1)
        outs['conv_s'].append(gfull[:, t_s:])

    st = {k: jnp.stack(v) for k, v in outs.items()}
    return (y_p.reshape(b_p, t_p, d), y_s.reshape(b_s, t_s, d),
            st['cmp_p'], st['cmp_s'], st['sel_p'], st['sel_s'], st['win_p'], st['win_s'],
            st['fox_p'], st['fox_s'], st['lf_p'], st['lf_s'], st['conv_p'], st['conv_s'])
```
